```python
import math
import jax, jax.numpy as jnp
from jax import lax
import numpy as np

D_MODEL = 1024
BATCH = 32
SEQ = 256
DEPTH = 2
DEC_BATCH = 2
DEC_SEQ = 2048
PAST_LEN = 512

GRID_W = 64
N_AH_LAYERS = (DEPTH + 1) // 2
N_ML_LAYERS = DEPTH // 2
A_HEADS = 4
A_KV_HEADS = 2
A_GROUP = A_HEADS // A_KV_HEADS
A_HEAD_DIM = 128
A_Q = A_HEADS * A_HEAD_DIM
A_KV = A_KV_HEADS * A_HEAD_DIM
Q_BLOCK = 128
ROPE_THETA = 10000.0
HY_CH = D_MODEL // 2
HY_BANDS = 8
HY_EMB = 1 + 2 * HY_BANDS
HY_W = 64
HY_TARGET = 1e-2
HY_FAST_PCT = 0.3
HY_SLOW_PCT = 1.5
HY_MAX_DECAY = math.log(HY_TARGET) / HY_FAST_PCT
HY_MIN_DECAY = math.log(HY_TARGET) / HY_SLOW_PCT
AH_IN = A_Q + 2 * A_KV + 3 * HY_CH
AH_OUT = A_Q + HY_CH
ML_HEADS = 8
ML_HEAD_DIM = D_MODEL // ML_HEADS
ML_W = ML_HEADS * ML_HEAD_DIM
ML_IN = 4 * ML_W + 4 * ML_HEADS
ML_CHUNK = 64
D_FF = 2816
NORM_EPS = 1e-6
NEG_BIG = -1e30

kernel_name = 'hybrid_diffusion_step'


def rms_norm(x, g):
    xf = x.astype(jnp.float32)
    y = xf * lax.rsqrt(jnp.mean(xf * xf, axis=-1, keepdims=True) + NORM_EPS)
    return (y * g.astype(jnp.float32)).astype(x.dtype)


def modulate(x, g, shift, scale):
    return rms_norm(x, g) * (1.0 + scale) + shift


def dwconv3(x, w, b):
    L = x.shape[1]
    xp = jnp.pad(x, ((0, 0), (1, 1), (0, 0)))
    return xp[:, :L] * w[0] + xp[:, 1:L + 1] * w[1] + xp[:, 2:] * w[2] + b


def axial_rope(L):
    rows = L // GRID_W
    row = jnp.repeat(jnp.arange(rows, dtype=jnp.float32), GRID_W)
    col = jnp.tile(jnp.arange(GRID_W, dtype=jnp.float32), rows)
    n_freq = A_HEAD_DIM // 4
    inv = ROPE_THETA ** (-jnp.arange(n_freq, dtype=jnp.float32) / n_freq)
    ang = jnp.concatenate([row[:, None] * inv, col[:, None] * inv], axis=-1)
    return jnp.cos(ang), jnp.sin(ang)


def apply_rope(x, cos, sin):
    half = x.shape[-1] // 2
    xf = x.astype(jnp.float32)
    x1, x2 = xf[..., :half], xf[..., half:]
    c, s = cos[None, :, None, :], sin[None, :, None, :]
    return jnp.concatenate([x1 * c - x2 * s, x1 * s + x2 * c], axis=-1).astype(x.dtype)


def block_attention(q, k, v):
    B, Lq = q.shape[0], q.shape[1]
    nb = Lq // Q_BLOCK
    qb = q.astype(jnp.float32).reshape(B, nb, Q_BLOCK, A_KV_HEADS, A_GROUP, A_HEAD_DIM)
    qb = jnp.moveaxis(qb, 1, 0)
    kf, vf = k.astype(jnp.float32), v.astype(jnp.float32)
    scale = A_HEAD_DIM ** -0.5

    def one_block(qi):
        s = jnp.einsum('bqkgd,bskd->bkgqs', qi, kf) * scale
        p = jax.nn.softmax(s, axis=-1)
        return jnp.einsum('bkgqs,bskd->bqkgd', p, vf)

    o = lax.map(one_block, qb)
    return jnp.moveaxis(o, 0, 1).reshape(B, Lq, A_Q).astype(q.dtype)


def hyena_filters(L, w1, b1, w2, b2, w3, b3, sin_freq):
    t = jnp.linspace(0.0, 1.0, L, dtype=jnp.float32)
    bands = jnp.arange(1, HY_BANDS + 1, dtype=jnp.float32)
    ang = 2.0 * jnp.pi * t[:, None] * bands
    z = jnp.concatenate([t[:, None], jnp.cos(ang), jnp.sin(ang)], axis=-1)
    h = jnp.sin(sin_freq[0] * (z @ w1 + b1))
    h = jnp.sin(sin_freq[1] * (h @ w2 + b2))
    filt = (h @ w3 + b3).astype(jnp.float32)
    deltas = jnp.abs(jnp.linspace(HY_MIN_DECAY, HY_MAX_DECAY, HY_CH, dtype=jnp.float32))
    window = jnp.exp(-t[:, None] * deltas)
    filt = filt.reshape(L, 2, HY_CH) * window[:, None, :]
    return filt[:, 0], filt[:, 1]


def long_conv(v, h):
    L = v.shape[1]
    n = 2 * L
    V = jnp.fft.rfft(v, n=n, axis=1)
    Hf = jnp.fft.rfft(h, n=n, axis=0)
    return jnp.fft.irfft(V * Hf[None], n=n, axis=1)[:, :L]


def hyena_mixer(u, conv_w, conv_b, w1, b1, w2, b2, w3, b3, sin_freq, skip):
    L = u.shape[1]
    uc = dwconv3(u, conv_w, conv_b)
    x0, x1, v = uc[..., :HY_CH], uc[..., HY_CH:2 * HY_CH], uc[..., 2 * HY_CH:]
    v = (v * x1).astype(jnp.float32)
    h_f, h_b = hyena_filters(L, w1, b1, w2, b2, w3, b3, sin_freq)
    y = long_conv(v, h_f) + jnp.flip(long_conv(jnp.flip(v, axis=1), h_b), axis=1) + skip * v
    return (y * x0).astype(u.dtype)


def attn_hyena_mixer(x, lat, w_in, w_out, q_norm, k_norm, hy):
    B, L, _ = x.shape
    proj = x @ w_in
    q = rms_norm(proj[..., :A_Q].reshape(B, L, A_HEADS, A_HEAD_DIM), q_norm)
    k = rms_norm(proj[..., A_Q:A_Q + A_KV].reshape(B, L, A_KV_HEADS, A_HEAD_DIM), k_norm)
    v = proj[..., A_Q + A_KV:A_Q + 2 * A_KV].reshape(B, L, A_KV_HEADS, A_HEAD_DIM)
    u = proj[..., A_Q + 2 * A_KV:]
    if lat is None:
        k_all, v_all = k, v
    else:
        (cos, sin), ctx_k, ctx_v = lat
        q = apply_rope(q, cos, sin)
        k_all = jnp.concatenate([apply_rope(k, cos, sin), ctx_k.astype(k.dtype)], axis=1)
        v_all = jnp.concatenate([v, ctx_v.astype(v.dtype)], axis=1)
    attn = block_attention(q, k_all, v_all)
    hyo = hyena_mixer(u, *hy)
    out = jnp.concatenate([attn, hyo], axis=-1) @ w_out
    return out, k, v


def mlstm_chunked(q, k, v, ig, fg, C0, n0, m0):
    B, H, L, dh = q.shape
    nc = L // ML_CHUNK

    def chunks(a):
        return jnp.moveaxis(a.reshape(B, H, nc, ML_CHUNK, *a.shape[3:]), 2, 0)

    lower = jnp.tril(jnp.ones((ML_CHUNK, ML_CHUNK), dtype=bool))

    def step(carry, inp):
        C, n, m = carry
        qc, kc, vc, ic, lfc = inp
        b = jnp.cumsum(lfc, axis=-1)
        dmat = jnp.where(lower, b[..., :, None] - b[..., None, :] + ic[..., None, :], NEG_BIG)
        inter = b + m[..., None]
        mt = jnp.maximum(inter, jnp.max(dmat, axis=-1))
        w_intra = jnp.exp(dmat - mt[..., None])
        w_inter = jnp.exp(inter - mt)
        s = jnp.einsum('bhtd,bhsd->bhts', qc, kc) * w_intra
        num = w_inter[..., None] * jnp.einsum('bhvk,bhtk->bhtv', C, qc) + jnp.einsum('bhts,bhsv->bhtv', s, vc)
        den = w_inter * jnp.einsum('bhk,bhtk->bht', n, qc) + jnp.sum(s, axis=-1)
        h = num / jnp.maximum(jnp.abs(den), jnp.exp(-mt))[..., None]
        m_new = mt[..., -1]
        w_state = jnp.exp(b[..., -1] + m - m_new)
        w_tok = jnp.exp(b[..., -1:] - b + ic - m_new[..., None])
        C_new = w_state[..., None, None] * C + jnp.einsum('bhs,bhsv,bhsk->bhvk', w_tok, vc, kc)
        n_new = w_state[..., None] * n + jnp.einsum('bhs,bhsk->bhk', w_tok, kc)
        return (C_new, n_new, m_new), h

    (C, n, m), h = lax.scan(step, (C0, n0, m0),
                            (chunks(q), chunks(k), chunks(v), chunks(ig), chunks(jax.nn.log_sigmoid(fg))))
    h = jnp.moveaxis(h, 0, 2).reshape(B, H, L, dh)
    return h, (C, n, m)


def mlstm_mixer(x, C0, n0, m0, w_in, b_gates, conv_w, conv_b, head_norm, w_out):
    B, L, _ = x.shape
    proj = x @ w_in
    qk = jax.nn.silu(dwconv3(proj[..., :2 * ML_W], conv_w, conv_b))
    v = proj[..., 2 * ML_W:3 * ML_W]
    o = proj[..., 3 * ML_W:4 * ML_W]
    gates = (proj[..., 4 * ML_W:] + b_gates).astype(jnp.float32)
    gates = gates.reshape(B, L, 4, ML_HEADS).transpose(2, 0, 3, 1)

    def heads(a):
        return a.reshape(B, L, ML_HEADS, ML_HEAD_DIM).transpose(0, 2, 1, 3).astype(jnp.float32)

    q = heads(qk[..., :ML_W])
    k = heads(qk[..., ML_W:]) * (ML_HEAD_DIM ** -0.5)
    vh = heads(v)
    C0, n0, m0 = C0.astype(jnp.float32), n0.astype(jnp.float32), m0.astype(jnp.float32)
    h_f, (Cf, nf, mf) = mlstm_chunked(q, k, vh, gates[0], gates[2], C0[:, 0], n0[:, 0], m0[:, 0])

    def rev(a):
        return jnp.flip(a, axis=2)

    h_b, (Cb, nb, mb) = mlstm_chunked(rev(q), rev(k), rev(vh), rev(gates[1]), rev(gates[3]),
                                      C0[:, 1], n0[:, 1], m0[:, 1])
    h = (h_f + rev(h_b)).transpose(0, 2, 1, 3)
    h = rms_norm(h, head_norm.reshape(ML_HEADS, ML_HEAD_DIM)).reshape(B, L, ML_W).astype(x.dtype)
    out = (h * jax.nn.sigmoid(o)) @ w_out
    state = (jnp.stack([Cf, Cb], axis=1), jnp.stack([nf, nb], axis=1), jnp.stack([mf, mb], axis=1))
    return out, state


def conv_ffn(x, w_up, conv_w, conv_b, w_down):
    h = dwconv3(x @ w_up, conv_w, conv_b)
    return (jax.nn.gelu(h[..., :D_FF], approximate=False) * h[..., D_FF:]) @ w_down


def setup_inputs(seed: int = 0) -> dict:
    key = jax.random.key(seed)
    keys = iter(jax.random.split(key, 64))

    def nrm(shape, scale):
        return scale * jax.random.normal(next(keys), shape, jnp.float32)

    def gain(shape):
        return 1.0 + nrm(shape, 0.05)

    LA, LM = N_AH_LAYERS, N_ML_LAYERS
    inp = {}
    inp['x_prompt'] = nrm((BATCH, SEQ, D_MODEL), 1.0)
    inp['x_sample'] = nrm((DEC_BATCH, DEC_SEQ, D_MODEL), 1.0)
    inp['cache_attn_k'] = nrm((DEC_BATCH, LA, PAST_LEN, A_KV_HEADS, A_HEAD_DIM), 1.0)
    inp['cache_attn_v'] = nrm((DEC_BATCH, LA, PAST_LEN, A_KV_HEADS, A_HEAD_DIM), 1.0)
    inp['state_mlstm_C'] = nrm((DEC_BATCH, LM, 2, ML_HEADS, ML_HEAD_DIM, ML_HEAD_DIM), 0.3)
    inp['state_mlstm_n'] = nrm((DEC_BATCH, LM, 2, ML_HEADS, ML_HEAD_DIM), 0.3)
    inp['state_mlstm_m'] = jax.random.uniform(next(keys), (DEC_BATCH, LM, 2, ML_HEADS), jnp.float32, 0.0, 3.0)
    inp['c'] = nrm((DEC_BATCH, D_MODEL), 1.0)
    inp['c_ctx'] = nrm((D_MODEL,), 1.0)
    inp['w_mod'] = nrm((DEPTH, D_MODEL, 6 * D_MODEL), 0.5 * D_MODEL ** -0.5)
    inp['b_mod'] = nrm((DEPTH, 6 * D_MODEL), 0.1)
    inp['norm_mix_pre'] = gain((DEPTH, D_MODEL))
    inp['norm_mix_post'] = gain((DEPTH, D_MODEL))
    inp['norm_ffn_pre'] = gain((DEPTH, D_MODEL))
    inp['norm_ffn_post'] = gain((DEPTH, D_MODEL))
    inp['ffn_w_up'] = nrm((DEPTH, D_MODEL, 2 * D_FF), D_MODEL ** -0.5)
    inp['ffn_conv_w'] = nrm((DEPTH, 3, 2 * D_FF), 0.5)
    inp['ffn_conv_b'] = nrm((DEPTH, 2 * D_FF), 0.02)
    inp['ffn_w_down'] = nrm((DEPTH, D_FF, D_MODEL), D_FF ** -0.5)
    inp['ah_w_in'] = nrm((LA, D_MODEL, AH_IN), D_MODEL ** -0.5)
    inp['ah_w_out'] = nrm((LA, AH_OUT, D_MODEL), AH_OUT ** -0.5)
    inp['attn_q_norm'] = gain((LA, A_HEAD_DIM))
    inp['attn_k_norm'] = gain((LA, A_HEAD_DIM))
    inp['hy_conv_w'] = nrm((LA, 3, 3 * HY_CH), 0.5)
    inp['hy_conv_b'] = nrm((LA, 3 * HY_CH), 0.02)
    inp['hy_w1'] = nrm((LA, HY_EMB, HY_W), HY_EMB ** -0.5)
    inp['hy_b1'] = nrm((LA, HY_W), 0.1)
    inp['hy_w2'] = nrm((LA, HY_W, HY_W), HY_W ** -0.5)
    inp['hy_b2'] = nrm((LA, HY_W), 0.1)
    inp['hy_w3'] = nrm((LA, HY_W, 2 * HY_CH), 0.1 * HY_W ** -0.5)
    inp['hy_b3'] = nrm((LA, 2 * HY_CH), 0.01)
    inp['hy_sin_freq'] = gain((LA, 2, HY_W))
    inp['hy_skip'] = nrm((LA, HY_CH), 0.5)
    inp['ml_w_in'] = nrm((LM, D_MODEL, ML_IN), D_MODEL ** -0.5)
    b_i = nrm((LM, 2 * ML_HEADS), 0.1)
    b_f = jnp.tile(jnp.linspace(3.0, 6.0, ML_HEADS, dtype=jnp.float32), 2)[None] + nrm((LM, 2 * ML_HEADS), 0.1)
    inp['ml_b_gates'] = jnp.concatenate([b_i, b_f], axis=-1)
    inp['ml_conv_w'] = nrm((LM, 3, 2 * ML_W), 0.5)
    inp['ml_conv_b'] = nrm((LM, 2 * ML_W), 0.02)
    inp['ml_head_norm'] = gain((LM, ML_W))
    inp['ml_w_out'] = nrm((LM, ML_W, D_MODEL), ML_W ** -0.5)
    return inp


def reference(x_prompt, x_sample, cache_attn_k, cache_attn_v, state_mlstm_C, state_mlstm_n, state_mlstm_m,
              c, c_ctx, w_mod, b_mod, norm_mix_pre, norm_mix_post, norm_ffn_pre, norm_ffn_post,
              ffn_w_up, ffn_conv_w, ffn_conv_b, ffn_w_down,
              ah_w_in, ah_w_out, attn_q_norm, attn_k_norm,
              hy_conv_w, hy_conv_b, hy_w1, hy_b1, hy_w2, hy_b2, hy_w3, hy_b3, hy_sin_freq, hy_skip,
              ml_w_in, ml_b_gates, ml_conv_w, ml_conv_b, ml_head_norm, ml_w_out):
    xp, xs = x_prompt, x_sample
    B = xp.shape[0]
    rope = axial_rope(xs.shape[1])
    ks, vs, Cs, ns, ms = [], [], [], [], []
    for l in range(DEPTH):
        j = l // 2
        mod_p = jnp.split(jax.nn.silu(c_ctx) @ w_mod[l] + b_mod[l], 6, axis=-1)
        mod_s = [m[:, None, :] for m in jnp.split(jax.nn.silu(c) @ w_mod[l] + b_mod[l], 6, axis=-1)]
        hp = modulate(xp, norm_mix_pre[l], mod_p[0], mod_p[1])
        hs = modulate(xs, norm_mix_pre[l], mod_s[0], mod_s[1])
        if l % 2 == 0:
            hy = (hy_conv_w[j], hy_conv_b[j], hy_w1[j], hy_b1[j], hy_w2[j], hy_b2[j],
                  hy_w3[j], hy_b3[j], hy_sin_freq[j], hy_skip[j])
            out_p, k_ctx, v_ctx = attn_hyena_mixer(hp, None, ah_w_in[j], ah_w_out[j],
                                                   attn_q_norm[j], attn_k_norm[j], hy)
            out_s, _, _ = attn_hyena_mixer(hs, (rope, cache_attn_k[:, j], cache_attn_v[:, j]),
                                           ah_w_in[j], ah_w_out[j], attn_q_norm[j], attn_k_norm[j], hy)
            ks.append(k_ctx)
            vs.append(v_ctx)
        else:
            zC = jnp.zeros((B, 2, ML_HEADS, ML_HEAD_DIM, ML_HEAD_DIM), jnp.float32)
            zn = jnp.zeros((B, 2, ML_HEADS, ML_HEAD_DIM), jnp.float32)
            zm = jnp.zeros((B, 2, ML_HEADS), jnp.float32)
            out_p, st = mlstm_mixer(hp, zC, zn, zm, ml_w_in[j], ml_b_gates[j], ml_conv_w[j], ml_conv_b[j],
                                    ml_head_norm[j], ml_w_out[j])
            out_s, _ = mlstm_mixer(hs, state_mlstm_C[:, j], state_mlstm_n[:, j], state_mlstm_m[:, j],
                                   ml_w_in[j], ml_b_gates[j], ml_conv_w[j], ml_conv_b[j],
                                   ml_head_norm[j], ml_w_out[j])
            Cs.append(st[0].astype(xp.dtype))
            ns.append(st[1].astype(xp.dtype))
            ms.append(st[2].astype(xp.dtype))
        xp = xp + mod_p[2] * rms_norm(out_p, norm_mix_post[l])
        xs = xs + mod_s[2] * rms_norm(out_s, norm_mix_post[l])
        hp = modulate(xp, norm_ffn_pre[l], mod_p[3], mod_p[4])
        hs = modulate(xs, norm_ffn_pre[l], mod_s[3], mod_s[4])
        xp = xp + mod_p[5] * rms_norm(conv_ffn(hp, ffn_w_up[l], ffn_conv_w[l], ffn_conv_b[l], ffn_w_down[l]),
                                      norm_ffn_post[l])
        xs = xs + mod_s[5] * rms_norm(conv_ffn(hs, ffn_w_up[l], ffn_conv_w[l], ffn_conv_b[l], ffn_w_down[l]),
                                      norm_ffn_post[l])
    new_attn_k = jnp.stack(ks, axis=1)
    new_attn_v = jnp.stack(vs, axis=1)
    new_mlstm_C = jnp.stack(Cs, axis=1)
    new_mlstm_n = jnp.stack(ns, axis=1)
    new_mlstm_m = jnp.stack(ms, axis=1)
    return (xp, xs, new_attn_k, new_attn_v, new_mlstm_C, new_mlstm_n, new_mlstm_m)
```

```python
import functools
import math

import numpy as np
import jax
import jax.numpy as jnp
from jax import lax
from jax.experimental import pallas as pl
from jax.experimental.pallas import tpu as pltpu

F32 = jnp.float32
BF16 = jnp.bfloat16

D_MODEL = 1024
DEPTH = 2
GRID_W = 64
A_HEADS = 4
A_KV_HEADS = 2
A_HEAD_DIM = 128
A_Q = A_HEADS * A_HEAD_DIM
A_KV = A_KV_HEADS * A_HEAD_DIM
ROPE_THETA = 10000.0
HY_CH = D_MODEL // 2
HY_BANDS = 8
HY_EMB = 1 + 2 * HY_BANDS
HY_W = 64
HY_TARGET = 1e-2
HY_FAST_PCT = 0.3
HY_SLOW_PCT = 1.5
HY_MAX_DECAY = math.log(HY_TARGET) / HY_FAST_PCT
HY_MIN_DECAY = math.log(HY_TARGET) / HY_SLOW_PCT
AH_IN = A_Q + 2 * A_KV + 3 * HY_CH
ML_HEADS = 8
ML_HEAD_DIM = D_MODEL // ML_HEADS
ML_W = ML_HEADS * ML_HEAD_DIM
D_FF = 2816
NORM_EPS = 1e-6
NEG_BIG = -1e30

LANES = 128
VMEM_LIMIT = 56 * 1024 * 1024
COL_TILE = 256
ML_SCAN_CHUNK = 128
INV_SQRT2 = 1.0 / math.sqrt(2.0)


def _cparams(*sem):
    return pltpu.CompilerParams(dimension_semantics=sem, vmem_limit_bytes=VMEM_LIMIT)


def _rms(x, gain):
    return x * lax.rsqrt(jnp.mean(x * x, axis=-1, keepdims=True) + NORM_EPS) * gain


def _modulated(x, gain, shift, scale):
    return _rms(x, gain) * (1.0 + scale) + shift


def _dot(a, b):
    return jnp.dot(a, b, preferred_element_type=F32)


def _dot_nt(a, b):
    return lax.dot_general(a, b, (((1,), (1,)), ((), ())), preferred_element_type=F32)


def _dot_tn(a, b):
    return lax.dot_general(a, b, (((0,), (0,)), ((), ())), preferred_element_type=F32)


def _split3(x):
    hi = x.astype(BF16)
    r1 = x - hi.astype(F32)
    mid = r1.astype(BF16)
    lo = (r1 - mid.astype(F32)).astype(BF16)
    return hi, mid, lo


def _dot_f32(a, b):
    a0, a1, a2 = _split3(a)
    b0, b1, b2 = _split3(b)
    return (_dot(a0, b0) + (_dot(a0, b1) + _dot(a1, b0))
            + (_dot(a0, b2) + _dot(a1, b1) + _dot(a2, b0)))


def _dwconv3(y, conv_ref, seq_len):
    rows, cols = y.shape
    pos = lax.broadcasted_iota(jnp.int32, (rows, LANES), 0) % seq_len
    first = pos == 0
    last = pos == seq_len - 1
    outs = []
    for c0 in range(0, cols, LANES):
        yc = y[:, c0:c0 + LANES]
        prev = jnp.where(first, 0.0, pltpu.roll(yc, 1, 0))
        nxt = jnp.where(last, 0.0, pltpu.roll(yc, rows - 1, 0))
        outs.append(prev * conv_ref[0:1, c0:c0 + LANES] + yc * conv_ref[1:2, c0:c0 + LANES]
                    + nxt * conv_ref[2:3, c0:c0 + LANES] + conv_ref[3:4, c0:c0 + LANES])
    return outs[0] if len(outs) == 1 else jnp.concatenate(outs, axis=1)


def _mod_kernel(c_ref, w_ref, b_ref, o_ref):
    a = c_ref[...]
    a = a * jax.nn.sigmoid(a)
    o_ref[0] = _dot(a.astype(BF16), w_ref[0].astype(BF16)) + b_ref[0]


def _mod_call(cc, w_mod, b_mod):
    tn = 1536
    n = 6 * D_MODEL
    return pl.pallas_call(
        _mod_kernel,
        grid=(DEPTH, n // tn),
        in_specs=[pl.BlockSpec((8, D_MODEL), lambda l, j: (0, 0)),
                  pl.BlockSpec((1, D_MODEL, tn), lambda l, j: (l, 0, j)),
                  pl.BlockSpec((1, 1, tn), lambda l, j: (l, 0, j))],
        out_specs=pl.BlockSpec((1, 8, tn), lambda l, j: (l, 0, j)),
        out_shape=jax.ShapeDtypeStruct((DEPTH, 8, n), F32),
        compiler_params=_cparams("arbitrary", "arbitrary"),
        name="mod",
    )(cc, w_mod, b_mod.reshape(DEPTH, 1, n))


def _ah_inproj_kernel(*refs, seq_len, rope):
    if rope:
        x_ref, mod_ref, gain_ref, w_ref, qkn_ref, conv_ref, cc_ref, ss_ref, o_ref, xn_ref = refs
    else:
        x_ref, mod_ref, gain_ref, w_ref, qkn_ref, conv_ref, o_ref, xn_ref = refs
    j = pl.program_id(1)

    @pl.when(j == 0)
    def _():
        xn_ref[...] = _modulated(x_ref[...], gain_ref[0:1, :], mod_ref[0, 0:1, :],
                                 mod_ref[0, 1:2, :]).astype(BF16)

    y = _dot(xn_ref[...], w_ref[...])

    def head(yh, g):
        yh = _rms(yh, g)
        if rope:
            yh = yh * cc_ref[...] + pltpu.roll(yh, A_HEAD_DIM // 2, 1) * ss_ref[...]
        return yh

    heads_per_tile = COL_TILE // A_HEAD_DIM

    @pl.when(j < A_Q // COL_TILE)
    def _():
        for h in range(heads_per_tile):
            sl = slice(h * A_HEAD_DIM, (h + 1) * A_HEAD_DIM)
            o_ref[:, sl] = head(y[:, sl], qkn_ref[0:1, :]) * (A_HEAD_DIM ** -0.5)

    @pl.when(j == A_Q // COL_TILE)
    def _():
        for h in range(heads_per_tile):
            sl = slice(h * A_HEAD_DIM, (h + 1) * A_HEAD_DIM)
            o_ref[:, sl] = head(y[:, sl], qkn_ref[1:2, :])

    @pl.when(j == (A_Q + A_KV) // COL_TILE)
    def _():
        o_ref[...] = y

    @pl.when(j >= (A_Q + 2 * A_KV) // COL_TILE)
    def _():
        o_ref[...] = _dwconv3(y, conv_ref, seq_len)


def _ah_inproj_call(x, mod, gains, w, qkn, conv, rope_tabs, *, tm, seq_len, rows_per_mod):
    rows = x.shape[0]
    rope = rope_tabs is not None
    in_specs = [pl.BlockSpec((tm, D_MODEL), lambda i, j: (i, 0)),
                pl.BlockSpec((1, 8, D_MODEL), lambda i, j: (i * tm // rows_per_mod, 0, 0)),
                pl.BlockSpec((8, D_MODEL), lambda i, j: (0, 0)),
                pl.BlockSpec((D_MODEL, COL_TILE), lambda i, j: (0, j)),
                pl.BlockSpec((8, A_HEAD_DIM), lambda i, j: (0, 0)),
                pl.BlockSpec((8, COL_TILE), lambda i, j: (0, j))]
    args = [x, mod, gains, w, qkn, conv]
    if rope:
        assert tm == seq_len
        in_specs += [pl.BlockSpec((tm, A_HEAD_DIM), lambda i, j: (0, 0))] * 2
        args += list(rope_tabs)
    return pl.pallas_call(
        functools.partial(_ah_inproj_kernel, seq_len=seq_len, rope=rope),
        grid=(rows // tm, AH_IN // COL_TILE),
        in_specs=in_specs,
        out_specs=pl.BlockSpec((tm, COL_TILE), lambda i, j: (i, j)),
        out_shape=jax.ShapeDtypeStruct((rows, AH_IN), F32),
        scratch_shapes=[pltpu.VMEM((tm, D_MODEL), BF16)],
        compiler_params=_cparams("arbitrary", "arbitrary"),
        name="ah_inproj",
    )(*args)


def _attn_kernel(*refs, ctx):
    if ctx:
        q_ref, k_ref, v_ref, kc_ref, vc_ref, o_ref = refs
    else:
        q_ref, k_ref, v_ref, o_ref = refs
    k = k_ref[0].astype(BF16)
    v = v_ref[0].astype(BF16)
    if ctx:
        kc = kc_ref[0].astype(BF16)
        vc = vc_ref[0].astype(BF16)
    for g in range(A_HEADS // A_KV_HEADS):
        sl = slice(g * A_HEAD_DIM, (g + 1) * A_HEAD_DIM)
        q = q_ref[0, :, sl].astype(BF16)
        s = _dot_nt(q, k)
        m = jnp.max(s, axis=-1, keepdims=True)
        if ctx:
            sc = _dot_nt(q, kc)
            m = jnp.maximum(m, jnp.max(sc, axis=-1, keepdims=True))
        p = jnp.exp(s - m)
        den = jnp.sum(p, axis=-1, keepdims=True)
        o = _dot(p.astype(BF16), v)
        if ctx:
            pc = jnp.exp(sc - m)
            den = den + jnp.sum(pc, axis=-1, keepdims=True)
            o = o + _dot(pc.astype(BF16), vc)
        o_ref[0, :, sl] = o / den


def _attn_call(proj, ctx_k, ctx_v, *, batch, seq_len, tq):
    proj3 = proj.reshape(batch, seq_len, AH_IN)
    ctx = ctx_k is not None
    qw = A_Q // A_KV_HEADS
    k_blk0 = A_Q // A_HEAD_DIM
    v_blk0 = (A_Q + A_KV) // A_HEAD_DIM
    in_specs = [pl.BlockSpec((1, tq, qw), lambda b, h, i: (b, i, h)),
                pl.BlockSpec((1, seq_len, A_HEAD_DIM), lambda b, h, i: (b, 0, k_blk0 + h)),
                pl.BlockSpec((1, seq_len, A_HEAD_DIM), lambda b, h, i: (b, 0, v_blk0 + h))]
    args = [proj3, proj3, proj3]
    if ctx:
        past = ctx_k.shape[1]
        in_specs += [pl.BlockSpec((1, past, A_HEAD_DIM), lambda b, h, i: (b, 0, h))] * 2
        args += [ctx_k, ctx_v]
    return pl.pallas_call(
        functools.partial(_attn_kernel, ctx=ctx),
        grid=(batch, A_KV_HEADS, seq_len // tq),
        in_specs=in_specs,
        out_specs=pl.BlockSpec((1, tq, qw), lambda b, h, i: (b, i, h)),
        out_shape=jax.ShapeDtypeStruct((batch, seq_len, A_Q), F32),
        compiler_params=_cparams("arbitrary", "arbitrary", "arbitrary"),
        name="attn",
    )(*args)


def _hy_filter_kernel(z_ref, w1_ref, b1_ref, w2_ref, b2_ref, w3_ref, b3_ref, sf_ref, dl_ref,
                      cm_ref, sm_ref, gr_ref, gi_ref, hs_ref, hd_ref):
    @pl.when(pl.program_id(0) == 0)
    def _():
        z = z_ref[...]
        h = jnp.sin(sf_ref[0:1, :] * (_dot_f32(z, w1_ref[...]) + b1_ref[...]))
        h = jnp.sin(sf_ref[1:2, :] * (_dot_f32(h, w2_ref[...]) + b2_ref[...]))
        filt = _dot_f32(h, w3_ref[...]) + b3_ref[...]
        win = jnp.exp(-z[:, 0:1] * dl_ref[...])
        hf = filt[:, :HY_CH] * win
        hb = filt[:, HY_CH:] * win
        hs_ref[...] = (hf + hb).astype(BF16)
        hd_ref[...] = (hb - hf).astype(BF16)

    gr_ref[...] = _dot(cm_ref[...], hs_ref[...])
    gi_ref[...] = _dot(sm_ref[...], hd_ref[...])


def _hy_filter_call(z, w1, b1, w2, b2, w3, b3, sf, deltas, cm, sm, *, seq_len, tk):
    full = lambda a: pl.BlockSpec(a.shape, lambda k: (0,) * a.ndim)
    small = [z, w1, b1, w2, b2, w3, b3, sf, deltas]
    return pl.pallas_call(
        _hy_filter_kernel,
        grid=(seq_len // tk,),
        in_specs=[full(a) for a in small] + [pl.BlockSpec((tk, seq_len), lambda k: (k, 0))] * 2,
        out_specs=[pl.BlockSpec((tk, HY_CH), lambda k: (k, 0))] * 2,
        out_shape=[jax.ShapeDtypeStruct((seq_len, HY_CH), F32)] * 2,
        scratch_shapes=[pltpu.VMEM((seq_len, HY_CH), BF16)] * 2,
        compiler_params=_cparams("arbitrary"),
        name="hy_filter",
    )(*small, cm, sm)


def _hy_conv_kernel(x0_ref, x1_ref, v_ref, skip_ref, cm_ref, sm_ref, cmt_ref, smt_ref, gr_ref, gi_ref,
                    o_ref, vv_ref, *, seq_len):
    kb = pl.program_id(2)

    @pl.when(kb == 0)
    def _():
        vv_ref[...] = (v_ref[0] * x1_ref[0]).astype(BF16)
        o_ref[...] = jnp.zeros_like(o_ref)

    vv = vv_ref[...]
    vr = _dot(cm_ref[...], vv)
    wi = _dot(sm_ref[...], vv)
    gr = gr_ref[...]
    gi = gi_ref[...]
    pr = gr * vr + gi * wi
    qi = gr * wi - gi * vr
    o_ref[0] += _dot(cmt_ref[...], pr.astype(BF16)) + _dot(smt_ref[...], qi.astype(BF16))

    @pl.when(kb == pl.num_programs(2) - 1)
    def _():
        vvf = v_ref[0] * x1_ref[0]
        o_ref[0] = (o_ref[0] * (1.0 / seq_len) + skip_ref[...] * vvf) * x0_ref[0]


def _hy_conv_call(proj, skip, dft, gr, gi, *, batch, seq_len, tc, tk):
    proj3 = proj.reshape(batch, seq_len, AH_IN)
    cm, sm, cmt, smt = dft
    u0 = (A_Q + 2 * A_KV) // tc
    nch = HY_CH // tc
    return pl.pallas_call(
        functools.partial(_hy_conv_kernel, seq_len=seq_len),
        grid=(batch, nch, seq_len // tk),
        in_specs=[pl.BlockSpec((1, seq_len, tc), lambda b, c, k: (b, 0, u0 + c)),
                  pl.BlockSpec((1, seq_len, tc), lambda b, c, k: (b, 0, u0 + nch + c)),
                  pl.BlockSpec((1, seq_len, tc), lambda b, c, k: (b, 0, u0 + 2 * nch + c)),
                  pl.BlockSpec((1, tc), lambda b, c, k: (0, c)),
                  pl.BlockSpec((tk, seq_len), lambda b, c, k: (k, 0)),
                  pl.BlockSpec((tk, seq_len), lambda b, c, k: (k, 0)),
                  pl.BlockSpec((seq_len, tk), lambda b, c, k: (0, k)),
                  pl.BlockSpec((seq_len, tk), lambda b, c, k: (0, k)),
                  pl.BlockSpec((tk, tc), lambda b, c, k: (k, c)),
                  pl.BlockSpec((tk, tc), lambda b, c, k: (k, c))],
        out_specs=pl.BlockSpec((1, seq_len, tc), lambda b, c, k: (b, 0, c)),
        out_shape=jax.ShapeDtypeStruct((batch, seq_len, HY_CH), F32),
        scratch_shapes=[pltpu.VMEM((seq_len, tc), BF16)],
        compiler_params=_cparams("arbitrary", "arbitrary", "arbitrary"),
        name="hy_conv",
    )(proj3, proj3, proj3, skip, cm, sm, cmt, smt, gr, gi)


def _ah_out_kernel(x_ref, a_ref, h_ref, w_ref, mod_ref, gain_ref, o_ref):
    out = _dot(a_ref[...].astype(BF16), w_ref[0:A_Q, :]) + _dot(h_ref[...].astype(BF16), w_ref[A_Q:, :])
    o_ref[...] = x_ref[...] + mod_ref[0, 2:3, :] * _rms(out, gain_ref[1:2, :])


def _ah_out_call(x, attn, hyo, w, mod, gains, *, tm, rows_per_mod):
    rows = x.shape[0]
    return pl.pallas_call(
        _ah_out_kernel,
        grid=(rows // tm,),
        in_specs=[pl.BlockSpec((tm, D_MODEL), lambda i: (i, 0)),
                  pl.BlockSpec((tm, A_Q), lambda i: (i, 0)),
                  pl.BlockSpec((tm, HY_CH), lambda i: (i, 0)),
                  pl.BlockSpec((A_Q + HY_CH, D_MODEL), lambda i: (0, 0)),
                  pl.BlockSpec((1, 8, D_MODEL), lambda i: (i * tm // rows_per_mod, 0, 0)),
                  pl.BlockSpec((8, D_MODEL), lambda i: (0, 0))],
        out_specs=pl.BlockSpec((tm, D_MODEL), lambda i: (i, 0)),
        out_shape=jax.ShapeDtypeStruct((rows, D_MODEL), F32),
        compiler_params=_cparams("arbitrary"),
        name="ah_out",
    )(x, attn, hyo, w, mod, gains)


def _ffn_kernel(x_ref, mod_ref, gain_ref, wg_ref, wl_ref, cg_ref, cl_ref, wd_ref, o_ref, xn_ref, *, seq_len):
    j = pl.program_id(1)

    @pl.when(j == 0)
    def _():
        xn_ref[...] = _modulated(x_ref[...], gain_ref[2:3, :], mod_ref[0, 3:4, :],
                                 mod_ref[0, 4:5, :]).astype(BF16)
        o_ref[...] = jnp.zeros_like(o_ref)

    xn = xn_ref[...]
    hg = _dwconv3(_dot(xn, wg_ref[...]), cg_ref, seq_len)
    hl = _dwconv3(_dot(xn, wl_ref[...]), cl_ref, seq_len)
    act = (0.5 * hg * (1.0 + lax.erf(hg * INV_SQRT2))) * hl
    o_ref[...] += _dot(act.astype(BF16), wd_ref[...])

    @pl.when(j == pl.num_programs(1) - 1)
    def _():
        o_ref[...] = x_ref[...] + mod_ref[0, 5:6, :] * _rms(o_ref[...], gain_ref[3:4, :])


def _ffn_call(x, mod, gains, w_up, conv, w_down, *, tm, seq_len, rows_per_mod):
    rows = x.shape[0]
    nj = D_FF // COL_TILE
    return pl.pallas_call(
        functools.partial(_ffn_kernel, seq_len=seq_len),
        grid=(rows // tm, nj),
        in_specs=[pl.BlockSpec((tm, D_MODEL), lambda i, j: (i, 0)),
                  pl.BlockSpec((1, 8, D_MODEL), lambda i, j: (i * tm // rows_per_mod, 0, 0)),
                  pl.BlockSpec((8, D_MODEL), lambda i, j: (0, 0)),
                  pl.BlockSpec((D_MODEL, COL_TILE), lambda i, j: (0, j)),
                  pl.BlockSpec((D_MODEL, COL_TILE), lambda i, j: (0, nj + j)),
                  pl.BlockSpec((8, COL_TILE), lambda i, j: (0, j)),
                  pl.BlockSpec((8, COL_TILE), lambda i, j: (0, nj + j)),
                  pl.BlockSpec((COL_TILE, D_MODEL), lambda i, j: (j, 0))],
        out_specs=pl.BlockSpec((tm, D_MODEL), lambda i, j: (i, 0)),
        out_shape=jax.ShapeDtypeStruct((rows, D_MODEL), F32),
        scratch_shapes=[pltpu.VMEM((tm, D_MODEL), BF16)],
        compiler_params=_cparams("arbitrary", "arbitrary"),
        name="ffn",
    )(x, mod, gains, w_up, w_up, conv, conv, w_down)


def _ml_inproj_kernel(x_ref, mod_ref, gain_ref, w_ref, wg_ref, bg_ref, conv_ref, o_ref, g_ref, xn_ref, *, seq_len):
    j = pl.program_id(1)

    @pl.when(j == 0)
    def _():
        xn = _modulated(x_ref[...], gain_ref[0:1, :], mod_ref[0, 0:1, :], mod_ref[0, 1:2, :]).astype(BF16)
        xn_ref[...] = xn
        g_ref[...] = _dot(xn, wg_ref[...]) + bg_ref[...]

    y = _dot(xn_ref[...], w_ref[...])

    def conv_silu(scale):
        z = _dwconv3(y, conv_ref, seq_len)
        return z * jax.nn.sigmoid(z) * scale

    @pl.when(j < ML_W // COL_TILE)
    def _():
        o_ref[...] = conv_silu(1.0)

    @pl.when(jnp.logical_and(j >= ML_W // COL_TILE, j < 2 * ML_W // COL_TILE))
    def _():
        o_ref[...] = conv_silu(ML_HEAD_DIM ** -0.5)

    @pl.when(j >= 2 * ML_W // COL_TILE)
    def _():
        o_ref[...] = y


def _ml_inproj_call(x, mod, gains, w, wg, bg, conv, *, tm, seq_len, rows_per_mod):
    rows = x.shape[0]
    n = 4 * ML_W
    return pl.pallas_call(
        functools.partial(_ml_inproj_kernel, seq_len=seq_len),
        grid=(rows // tm, n // COL_TILE),
        in_specs=[pl.BlockSpec((tm, D_MODEL), lambda i, j: (i, 0)),
                  pl.BlockSpec((1, 8, D_MODEL), lambda i, j: (i * tm // rows_per_mod, 0, 0)),
                  pl.BlockSpec((8, D_MODEL), lambda i, j: (0, 0)),
                  pl.BlockSpec((D_MODEL, COL_TILE), lambda i, j: (0, j)),
                  pl.BlockSpec((D_MODEL, LANES), lambda i, j: (0, 0)),
                  pl.BlockSpec((1, LANES), lambda i, j: (0, 0)),
                  pl.BlockSpec((8, COL_TILE), lambda i, j: (0, j))],
        out_specs=[pl.BlockSpec((tm, COL_TILE), lambda i, j: (i, j)),
                   pl.BlockSpec((tm, LANES), lambda i, j: (i, 0))],
        out_shape=[jax.ShapeDtypeStruct((rows, n), F32), jax.ShapeDtypeStruct((rows, LANES), F32)],
        scratch_shapes=[pltpu.VMEM((tm, D_MODEL), BF16)],
        compiler_params=_cparams("arbitrary", "arbitrary"),
        name="ml_inproj",
    )(x, mod, gains, w, wg, bg, conv)


def _log_sigmoid(x):
    return jnp.minimum(x, 0.0) - jnp.log1p(jnp.exp(-jnp.abs(x)))


def _ml_scan_kernel(*refs, chunk, has_init):
    if has_init:
        (qf, kf, vf, qb, kb, vb, gf, gb, gtf, gtb, c0_ref, n0_ref, m0_ref,
         hf_ref, hb_ref, c_ref, n_ref, m_ref) = refs
    else:
        qf, kf, vf, qb, kb, vb, gf, gb, gtf, gtb, hf_ref, hb_ref, c_ref, n_ref, m_ref = refs

    @pl.when(pl.program_id(1) == 0)
    def _():
        if has_init:
            c_ref[...] = c0_ref[...]
            n_ref[...] = n0_ref[...]
            m_ref[...] = m0_ref[...]
        else:
            c_ref[...] = jnp.zeros_like(c_ref)
            n_ref[...] = jnp.zeros_like(n_ref)
            m_ref[...] = jnp.zeros_like(m_ref)

    row = lax.broadcasted_iota(jnp.int32, (chunk, chunk), 0)
    col = lax.broadcasted_iota(jnp.int32, (chunk, chunk), 1)
    lower = row >= col
    upper = col >= row
    lower_f = jnp.where(lower, 1.0, 0.0).astype(BF16)
    upper_f = jnp.where(upper, 1.0, 0.0).astype(BF16)

    def tri_dot_l(tri, x):
        x0, x1, x2 = _split3(x)
        return _dot(tri, x0) + (_dot(tri, x1) + _dot(tri, x2))

    def tri_dot_r(x, tri):
        x0, x1, x2 = _split3(x)
        return _dot(x0, tri) + (_dot(x1, tri) + _dot(x2, tri))

    nh = ML_HEADS
    for d, (q_ref, k_ref, v_ref, g_ref, gt_ref, h_ref) in enumerate(
            ((qf, kf, vf, gf, gtf, hf_ref), (qb, kb, vb, gb, gtb, hb_ref))):
        gc = g_ref[0]
        gt = gt_ref[0, 0]
        ic = gc[:, nh * d:nh * (d + 1)]
        lfc = _log_sigmoid(gc[:, nh * (2 + d):nh * (3 + d)])
        ir = gt[nh * d:nh * (d + 1), :]
        lfr = _log_sigmoid(gt[nh * (2 + d):nh * (3 + d), :])
        if d == 0:
            bc = tri_dot_l(lower_f, lfc)
            br = tri_dot_r(lfr, upper_f)
            mask, last = lower, chunk - 1
        else:
            bc = tri_dot_l(upper_f, lfc)
            br = tri_dot_r(lfr, lower_f)
            mask, last = upper, 0
        rr = ir - br
        for h in range(nh):
            sl = slice(h * ML_HEAD_DIM, (h + 1) * ML_HEAD_DIM)
            q = q_ref[0, :, sl]
            k = k_ref[0, :, sl]
            v = v_ref[0, :, sl]
            q16 = q.astype(BF16)
            k16 = k.astype(BF16)
            b_col = bc[:, h:h + 1]
            dm = jnp.where(mask, b_col + rr[h:h + 1, :], NEG_BIG)
            m_prev = m_ref[0, d, h:h + 1, 0:1]
            inter = b_col + m_prev
            mt = jnp.maximum(inter, jnp.max(dm, axis=-1, keepdims=True))
            w_intra = jnp.exp(dm - mt)
            w_inter = jnp.exp(inter - mt)
            s = _dot_nt(q16, k16) * w_intra
            c_old = c_ref[0, d, h]
            n_old = n_ref[0, d, h:h + 1, :]
            num = w_inter * _dot_nt(q16, c_old.astype(BF16)) + _dot(s.astype(BF16), v.astype(BF16))
            den = (w_inter * jnp.sum(q * n_old, axis=-1, keepdims=True)
                   + jnp.sum(s, axis=-1, keepdims=True))
            h_ref[0, :, sl] = num / jnp.maximum(jnp.abs(den), jnp.exp(-mt))
            m_new = mt[last:last + 1, :]
            b_last = b_col[last:last + 1, :]
            w_state = jnp.exp(b_last + m_prev - m_new)
            w_tok = jnp.exp(b_last - b_col + ic[:, h:h + 1] - m_new)
            c_ref[0, d, h] = w_state * c_old + _dot_tn((w_tok * v).astype(BF16), k16)
            n_ref[0, d, h:h + 1, :] = w_state * n_old + jnp.sum(w_tok * k, axis=0, keepdims=True)
            m_ref[0, d, h:h + 1, :] = jnp.broadcast_to(m_new, (1, LANES))


def _ml_scan_call(proj, gates, init, *, batch, seq_len, chunk):
    nc = seq_len // chunk
    proj3 = proj.reshape(batch, seq_len, 4 * ML_W)
    g3 = gates.reshape(batch, seq_len, LANES)
    gt4 = g3[:, :, :4 * ML_HEADS].reshape(batch, nc, chunk, 4 * ML_HEADS).transpose(0, 1, 3, 2)
    has_init = init is not None

    def qkv(which, rev):
        if rev:
            return pl.BlockSpec((1, chunk, ML_W), lambda b, i: (b, nc - 1 - i, which))
        return pl.BlockSpec((1, chunk, ML_W), lambda b, i: (b, i, which))

    in_specs = [qkv(0, False), qkv(1, False), qkv(2, False), qkv(0, True), qkv(1, True), qkv(2, True),
                pl.BlockSpec((1, chunk, LANES), lambda b, i: (b, i, 0)),
                pl.BlockSpec((1, chunk, LANES), lambda b, i: (b, nc - 1 - i, 0)),
                pl.BlockSpec((1, 1, 4 * ML_HEADS, chunk), lambda b, i: (b, i, 0, 0)),
                pl.BlockSpec((1, 1, 4 * ML_HEADS, chunk), lambda b, i: (b, nc - 1 - i, 0, 0))]
    args = [proj3] * 6 + [g3, g3, gt4, gt4]
    c_spec = pl.BlockSpec((1, 2, ML_HEADS, ML_HEAD_DIM, ML_HEAD_DIM), lambda b, i: (b, 0, 0, 0, 0))
    n_spec = pl.BlockSpec((1, 2, ML_HEADS, ML_HEAD_DIM), lambda b, i: (b, 0, 0, 0))
    if has_init:
        in_specs += [c_spec, n_spec, n_spec]
        args += list(init)
    return pl.pallas_call(
        functools.partial(_ml_scan_kernel, chunk=chunk, has_init=has_init),
        grid=(batch, nc),
        in_specs=in_specs,
        out_specs=[pl.BlockSpec((1, chunk, ML_W), lambda b, i: (b, i, 0)),
                   pl.BlockSpec((1, chunk, ML_W), lambda b, i: (b, nc - 1 - i, 0)),
                   c_spec, n_spec, n_spec],
        out_shape=[jax.ShapeDtypeStruct((batch, seq_len, ML_W), F32),
                   jax.ShapeDtypeStruct((batch, seq_len, ML_W), F32),
                   jax.ShapeDtypeStruct((batch, 2, ML_HEADS, ML_HEAD_DIM, ML_HEAD_DIM), F32),
                   jax.ShapeDtypeStruct((batch, 2, ML_HEADS, ML_HEAD_DIM), F32),
                   jax.ShapeDtypeStruct((batch, 2, ML_HEADS, LANES), F32)],
        compiler_params=_cparams("arbitrary", "arbitrary"),
        name="ml_scan",
    )(*args)


def _ml_out_kernel(x_ref, hf_ref, hb_ref, og_ref, hn_ref, w_ref, mod_ref, gain_ref, o_ref):
    h = hf_ref[...] + hb_ref[...]
    parts = []
    for hd in range(ML_HEADS):
        sl = slice(hd * ML_HEAD_DIM, (hd + 1) * ML_HEAD_DIM)
        parts.append(_rms(h[:, sl], hn_ref[0:1, sl]))
    a = jnp.concatenate(parts, axis=1) * jax.nn.sigmoid(og_ref[...])
    out = _dot(a.astype(BF16), w_ref[...])
    o_ref[...] = x_ref[...] + mod_ref[0, 2:3, :] * _rms(out, gain_ref[1:2, :])


def _ml_out_call(x, hf, hb, proj, head_norm, w, mod, gains, *, tm, rows_per_mod):
    rows = x.shape[0]
    return pl.pallas_call(
        _ml_out_kernel,
        grid=(rows // tm,),
        in_specs=[pl.BlockSpec((tm, D_MODEL), lambda i: (i, 0)),
                  pl.BlockSpec((tm, ML_W), lambda i: (i, 0)),
                  pl.BlockSpec((tm, ML_W), lambda i: (i, 0)),
                  pl.BlockSpec((tm, ML_W), lambda i: (i, 3)),
                  pl.BlockSpec((1, ML_W), lambda i: (0, 0)),
                  pl.BlockSpec((ML_W, D_MODEL), lambda i: (0, 0)),
                  pl.BlockSpec((1, 8, D_MODEL), lambda i: (i * tm // rows_per_mod, 0, 0)),
                  pl.BlockSpec((8, D_MODEL), lambda i: (0, 0))],
        out_specs=pl.BlockSpec((tm, D_MODEL), lambda i: (i, 0)),
        out_shape=jax.ShapeDtypeStruct((rows, D_MODEL), F32),
        compiler_params=_cparams("arbitrary"),
        name="ml_out",
    )(x, hf, hb, proj, head_norm, w, mod, gains)


def _dft_tables(seq_len):
    n4 = 4 * seq_len
    lo = 64
    hi = seq_len // lo
    ar = np.arange(seq_len, dtype=np.int64)

    def cs(idx):
        ang = 2.0 * np.pi * (idx % n4).astype(np.float64) / n4
        return jnp.asarray(np.cos(ang), F32), jnp.asarray(np.sin(ang), F32)

    def expand(a, b):
        (ac, as_), (bc, bs) = a, b
        c = ac[:, :, None] * bc[:, None, :] - as_[:, :, None] * bs[:, None, :]
        s = as_[:, :, None] * bc[:, None, :] + ac[:, :, None] * bs[:, None, :]
        return c.reshape(seq_len, seq_len).astype(BF16), s.reshape(seq_len, seq_len).astype(BF16)

    odd = 2 * ar + 1
    cm, sm = expand(cs(odd[:, None] * (lo * np.arange(hi))[None, :]), cs(odd[:, None] * np.arange(lo)[None, :]))
    cmt, smt = expand(cs(ar[:, None] * (2 * lo * np.arange(hi))[None, :]),
                      cs(ar[:, None] * (2 * np.arange(lo) + 1)[None, :]))
    return cm, sm, cmt, smt


def _hy_features(seq_len):
    t = np.linspace(0.0, 1.0, seq_len, dtype=np.float32)
    bands = np.arange(1, HY_BANDS + 1, dtype=np.float32)
    ang = (np.float32(2.0 * np.pi) * t[:, None]) * bands
    z = np.concatenate([t[:, None], np.cos(ang), np.sin(ang)], axis=-1).astype(np.float32)
    z = np.pad(z, ((0, 0), (0, 32 - HY_EMB)))
    deltas = np.abs(np.linspace(HY_MIN_DECAY, HY_MAX_DECAY, HY_CH, dtype=np.float32))[None, :]
    return jnp.asarray(z), jnp.asarray(deltas)


def _rope_tables(seq_len):
    rows = seq_len // GRID_W
    row = np.repeat(np.arange(rows, dtype=np.float32), GRID_W)
    col = np.tile(np.arange(GRID_W, dtype=np.float32), rows)
    n_freq = A_HEAD_DIM // 4
    inv = (np.float32(ROPE_THETA) ** (-np.arange(n_freq, dtype=np.float32) / n_freq)).astype(np.float32)
    ang = np.concatenate([row[:, None] * inv, col[:, None] * inv], axis=-1).astype(np.float32)
    cos, sin = np.cos(ang), np.sin(ang)
    return (jnp.asarray(np.concatenate([cos, cos], axis=-1), F32),
            jnp.asarray(np.concatenate([-sin, sin], axis=-1), F32))


def _pad_rows(a, rows=8):
    return jnp.pad(a, ((0, rows - a.shape[0]), (0, 0)))


def kernel(x_prompt, x_sample, cache_attn_k, cache_attn_v, state_mlstm_C, state_mlstm_n, state_mlstm_m, c, c_ctx, w_mod, b_mod, norm_mix_pre, norm_mix_post, norm_ffn_pre, norm_ffn_post, ffn_w_up, ffn_conv_w, ffn_conv_b, ffn_w_down, ah_w_in, ah_w_out, attn_q_norm, attn_k_norm, hy_conv_w, hy_conv_b, hy_w1, hy_b1, hy_w2, hy_b2, hy_w3, hy_b3, hy_sin_freq, hy_skip, ml_w_in, ml_b_gates, ml_conv_w, ml_conv_b, ml_head_norm, ml_w_out):
    bp, lp, _ = x_prompt.shape
    bs, ls, _ = x_sample.shape
    past = cache_attn_k.shape[2]
    xp = x_prompt.reshape(bp * lp, D_MODEL)
    xs = x_sample.reshape(bs * ls, D_MODEL)
    groups = {
        "p": dict(batch=bp, seq_len=lp, tm=1024, rows_per_mod=bp * lp),
        "s": dict(batch=bs, seq_len=ls, tm=ls, rows_per_mod=ls),
    }

    cc = jnp.concatenate([c, c_ctx[None, :], jnp.zeros((8 - bs - 1, D_MODEL), F32)], axis=0)
    mod_all = _mod_call(cc, w_mod, b_mod).reshape(DEPTH, 8, 6, D_MODEL)
    mod_all = jnp.pad(mod_all, ((0, 0), (0, 0), (0, 2), (0, 0)))
    rope = _rope_tables(ls)

    new_k = new_v = new_c = new_n = new_m = None
    for l in range(DEPTH):
        j = l // 2
        mods = {"s": mod_all[l, :bs], "p": mod_all[l, bs:bs + 1]}
        gains = _pad_rows(jnp.stack([norm_mix_pre[l], norm_mix_post[l], norm_ffn_pre[l], norm_ffn_post[l]]))
        xin = {"p": xp, "s": xs}
        xmid = {}
        if l % 2 == 0:
            w_in = ah_w_in[j].astype(BF16)
            w_out = ah_w_out[j].astype(BF16)
            qkn = _pad_rows(jnp.stack([attn_q_norm[j], attn_k_norm[j]]))
            conv = jnp.pad(_pad_rows(jnp.concatenate([hy_conv_w[j], hy_conv_b[j][None, :]], axis=0)),
                           ((0, 0), (A_Q + 2 * A_KV, 0)))
            w1 = jnp.pad(hy_w1[j], ((0, 32 - HY_EMB), (0, 0)))
            sf = _pad_rows(hy_sin_freq[j])
            for g, cfg in groups.items():
                batch, seq_len, tm, rpm = cfg["batch"], cfg["seq_len"], cfg["tm"], cfg["rows_per_mod"]
                proj = _ah_inproj_call(xin[g], mods[g], gains, w_in, qkn, conv, rope if g == "s" else None,
                                       tm=tm, seq_len=seq_len, rows_per_mod=rpm)
                if g == "s":
                    ctx_k = cache_attn_k[:, j].reshape(bs, past, A_KV)
                    ctx_v = cache_attn_v[:, j].reshape(bs, past, A_KV)
                else:
                    ctx_k = ctx_v = None
                    new_k = proj[:, A_Q:A_Q + A_KV].reshape(bp, 1, lp, A_KV_HEADS, A_HEAD_DIM)
                    new_v = proj[:, A_Q + A_KV:A_Q + 2 * A_KV].reshape(bp, 1, lp, A_KV_HEADS, A_HEAD_DIM)
                attn = _attn_call(proj, ctx_k, ctx_v, batch=batch, seq_len=seq_len, tq=256)
                z, deltas = _hy_features(seq_len)
                dft = _dft_tables(seq_len)
                tk = 256
                gr, gi = _hy_filter_call(z, w1, hy_b1[j][None, :], hy_w2[j], hy_b2[j][None, :], hy_w3[j],
                                         hy_b3[j][None, :], sf, deltas, dft[0], dft[1], seq_len=seq_len, tk=tk)
                hyo = _hy_conv_call(proj, hy_skip[j][None, :], dft, gr, gi, batch=batch, seq_len=seq_len,
                                    tc=256, tk=tk)
                xmid[g] = _ah_out_call(xin[g], attn.reshape(batch * seq_len, A_Q),
                                       hyo.reshape(batch * seq_len, HY_CH), w_out, mods[g], gains,
                                       tm=512, rows_per_mod=rpm)
        else:
            w_in = ml_w_in[j][:, :4 * ML_W].astype(BF16)
            w_g = jnp.pad(ml_w_in[j][:, 4 * ML_W:], ((0, 0), (0, LANES - 4 * ML_HEADS))).astype(BF16)
            b_g = jnp.pad(ml_b_gates[j], (0, LANES - 4 * ML_HEADS))[None, :]
            w_out = ml_w_out[j].astype(BF16)
            conv = jnp.pad(_pad_rows(jnp.concatenate([ml_conv_w[j], ml_conv_b[j][None, :]], axis=0)),
                           ((0, 0), (0, 2 * ML_W)))
            for g, cfg in groups.items():
                batch, seq_len, tm, rpm = cfg["batch"], cfg["seq_len"], cfg["tm"], cfg["rows_per_mod"]
                proj, gates = _ml_inproj_call(xin[g], mods[g], gains, w_in, w_g, b_g, conv,
                                              tm=tm, seq_len=seq_len, rows_per_mod=rpm)
                if g == "s":
                    init = (state_mlstm_C[:, j], state_mlstm_n[:, j],
                            jnp.broadcast_to(state_mlstm_m[:, j][..., None], (bs, 2, ML_HEADS, LANES)))
                else:
                    init = None
                hf, hb, c_new, n_new, m_new = _ml_scan_call(proj, gates, init, batch=batch, seq_len=seq_len,
                                                            chunk=ML_SCAN_CHUNK)
                if g == "p":
                    new_c, new_n, new_m = c_new[:, None], n_new[:, None], m_new[:, None, :, :, 0]
                xmid[g] = _ml_out_call(xin[g], hf.reshape(batch * seq_len, ML_W), hb.reshape(batch * seq_len, ML_W),
                                       proj, ml_head_norm[j][None, :], w_out, mods[g], gains,
                                       tm=512, rows_per_mod=rpm)
        w_up = ffn_w_up[l].astype(BF16)
        w_down = ffn_w_down[l].astype(BF16)
        conv = _pad_rows(jnp.concatenate([ffn_conv_w[l], ffn_conv_b[l][None, :]], axis=0))
        xout = {}
        for g, cfg in groups.items():
            xout[g] = _ffn_call(xmid[g], mods[g], gains, w_up, conv, w_down, tm=cfg["tm"],
                                seq_len=cfg["seq_len"], rows_per_mod=cfg["rows_per_mod"])
        xp, xs = xout["p"], xout["s"]

    return (xp.reshape(bp, lp, D_MODEL), xs.reshape(bs, ls, D_MODEL), new_k, new_v, new_c, new_n, new_m)
```

```python
import functools
import math

import numpy as np
import jax
import jax.numpy as jnp
from jax import lax
from jax.experimental import pallas as pl
from jax.experimental.pallas import tpu as pltpu

F32 = jnp.float32
BF16 = jnp.bfloat16

D_MODEL = 1024
DEPTH = 2
GRID_W = 64
A_HEADS = 4
A_KV_HEADS = 2
A_HEAD_DIM = 128
A_Q = A_HEADS * A_HEAD_DIM
A_KV = A_KV_HEADS * A_HEAD_DIM
ROPE_THETA = 10000.0
HY_CH = D_MODEL // 2
HY_BANDS = 8
HY_EMB = 1 + 2 * HY_BANDS
HY_W = 64
HY_TARGET = 1e-2
HY_FAST_PCT = 0.3
HY_SLOW_PCT = 1.5
HY_MAX_DECAY = math.log(HY_TARGET) / HY_FAST_PCT
HY_MIN_DECAY = math.log(HY_TARGET) / HY_SLOW_PCT
AH_IN = A_Q + 2 * A_KV + 3 * HY_CH
ML_HEADS = 8
ML_HEAD_DIM = D_MODEL // ML_HEADS
ML_W = ML_HEADS * ML_HEAD_DIM
D_FF = 2816
NORM_EPS = 1e-6
NEG_BIG = -1e30

LANES = 128
VMEM_LIMIT = 56 * 1024 * 1024
COL_TILE = 256
ML_SCAN_CHUNK = 128
INV_SQRT2 = 1.0 / math.sqrt(2.0)


def _cparams(*sem):
    return pltpu.CompilerParams(dimension_semantics=sem, vmem_limit_bytes=VMEM_LIMIT)


def _rms(x, gain):
    return x * lax.rsqrt(jnp.mean(x * x, axis=-1, keepdims=True) + NORM_EPS) * gain


def _modulated(x, gain, shift, scale):
    return _rms(x, gain) * (1.0 + scale) + shift


def _dot(a, b):
    return jnp.dot(a, b, preferred_element_type=F32)


def _dot_nt(a, b):
    return lax.dot_general(a, b, (((1,), (1,)), ((), ())), preferred_element_type=F32)


def _dot_tn(a, b):
    return lax.dot_general(a, b, (((0,), (0,)), ((), ())), preferred_element_type=F32)


def _split3(x):
    hi = x.astype(BF16)
    r1 = x - hi.astype(F32)
    mid = r1.astype(BF16)
    lo = (r1 - mid.astype(F32)).astype(BF16)
    return hi, mid, lo


def _dot_f32(a, b):
    a0, a1, a2 = _split3(a)
    b0, b1, b2 = _split3(b)
    return (_dot(a0, b0) + (_dot(a0, b1) + _dot(a1, b0))
            + (_dot(a0, b2) + _dot(a1, b1) + _dot(a2, b0)))


def _dwconv3(y, conv_ref, seq_len):
    rows, cols = y.shape
    pos = lax.broadcasted_iota(jnp.int32, (rows, LANES), 0) % seq_len
    first = pos == 0
    last = pos == seq_len - 1
    outs = []
    for c0 in range(0, cols, LANES):
        yc = y[:, c0:c0 + LANES]
        prev = jnp.where(first, 0.0, pltpu.roll(yc, 1, 0))
        nxt = jnp.where(last, 0.0, pltpu.roll(yc, rows - 1, 0))
        outs.append(prev * conv_ref[0:1, c0:c0 + LANES] + yc * conv_ref[1:2, c0:c0 + LANES]
                    + nxt * conv_ref[2:3, c0:c0 + LANES] + conv_ref[3:4, c0:c0 + LANES])
    return outs[0] if len(outs) == 1 else jnp.concatenate(outs, axis=1)


def _mod_kernel(c_ref, w_ref, b_ref, o_ref):
    a = c_ref[...]
    a = a * jax.nn.sigmoid(a)
    o_ref[0] = _dot(a.astype(BF16), w_ref[0].astype(BF16)) + b_ref[0]


def _mod_call(cc, w_mod, b_mod):
    tn = 1536
    n = 6 * D_MODEL
    return pl.pallas_call(
        _mod_kernel,
        grid=(DEPTH, n // tn),
        in_specs=[pl.BlockSpec((8, D_MODEL), lambda l, j: (0, 0)),
                  pl.BlockSpec((1, D_MODEL, tn), lambda l, j: (l, 0, j)),
                  pl.BlockSpec((1, 1, tn), lambda l, j: (l, 0, j))],
        out_specs=pl.BlockSpec((1, 8, tn), lambda l, j: (l, 0, j)),
        out_shape=jax.ShapeDtypeStruct((DEPTH, 8, n), F32),
        compiler_params=_cparams("arbitrary", "arbitrary"),
        name="mod",
    )(cc, w_mod, b_mod.reshape(DEPTH, 1, n))


def _ah_inproj_kernel(*refs, seq_len, rope):
    if rope:
        (x_ref, mod_ref, gain_ref, w_ref, qkn_ref, conv_ref, cc_ref, ss_ref,
         q_ref, k_ref, v_ref, u_ref, xn_ref) = refs
        kf_ref = vf_ref = None
    else:
        (x_ref, mod_ref, gain_ref, w_ref, qkn_ref, conv_ref,
         q_ref, k_ref, v_ref, u_ref, kf_ref, vf_ref, xn_ref) = refs
    j = pl.program_id(1)

    @pl.when(j == 0)
    def _():
        xn_ref[...] = _modulated(x_ref[...], gain_ref[0:1, :], mod_ref[0, 0:1, :],
                                 mod_ref[0, 1:2, :]).astype(BF16)

    y = _dot(xn_ref[...], w_ref[...])

    def head(yh, g):
        yh = _rms(yh, g)
        if rope:
            yh = yh * cc_ref[...] + pltpu.roll(yh, A_HEAD_DIM // 2, 1) * ss_ref[...]
        return yh

    heads_per_tile = COL_TILE // A_HEAD_DIM

    @pl.when(j < A_Q // COL_TILE)
    def _():
        for h in range(heads_per_tile):
            sl = slice(h * A_HEAD_DIM, (h + 1) * A_HEAD_DIM)
            q_ref[:, sl] = (head(y[:, sl], qkn_ref[0:1, :]) * (A_HEAD_DIM ** -0.5)).astype(BF16)

    @pl.when(j == A_Q // COL_TILE)
    def _():
        for h in range(heads_per_tile):
            sl = slice(h * A_HEAD_DIM, (h + 1) * A_HEAD_DIM)
            kh = head(y[:, sl], qkn_ref[1:2, :])
            k_ref[:, sl] = kh.astype(BF16)
            if kf_ref is not None:
                kf_ref[:, sl] = kh

    @pl.when(j == (A_Q + A_KV) // COL_TILE)
    def _():
        v_ref[...] = y.astype(BF16)
        if vf_ref is not None:
            vf_ref[...] = y

    @pl.when(j >= (A_Q + 2 * A_KV) // COL_TILE)
    def _():
        u_ref[...] = _dwconv3(y, conv_ref, seq_len).astype(BF16)


def _ah_inproj_call(x, mod, gains, w, qkn, conv, rope_tabs, *, tm, seq_len, rows_per_mod):
    rows = x.shape[0]
    rope = rope_tabs is not None
    q_tiles = A_Q // COL_TILE
    u0 = (A_Q + 2 * A_KV) // COL_TILE
    u_tiles = 3 * HY_CH // COL_TILE
    assert A_KV == COL_TILE
    out_specs = [pl.BlockSpec((tm, COL_TILE), lambda i, j: (i, jnp.minimum(j, q_tiles - 1))),
                 pl.BlockSpec((tm, COL_TILE), lambda i, j: (i, 0)),
                 pl.BlockSpec((tm, COL_TILE), lambda i, j: (i, 0)),
                 pl.BlockSpec((tm, COL_TILE), lambda i, j: (i, jnp.clip(j - u0, 0, u_tiles - 1)))]
    out_shape = [jax.ShapeDtypeStruct((rows, A_Q), BF16), jax.ShapeDtypeStruct((rows, A_KV), BF16),
                 jax.ShapeDtypeStruct((rows, A_KV), BF16), jax.ShapeDtypeStruct((rows, 3 * HY_CH), BF16)]
    if not rope:
        out_specs += [pl.BlockSpec((tm, COL_TILE), lambda i, j: (i, 0))] * 2
        out_shape += [jax.ShapeDtypeStruct((rows, A_KV), F32)] * 2
    in_specs = [pl.BlockSpec((tm, D_MODEL), lambda i, j: (i, 0)),
                pl.BlockSpec((1, 8, D_MODEL), lambda i, j: (i * tm // rows_per_mod, 0, 0)),
                pl.BlockSpec((8, D_MODEL), lambda i, j: (0, 0)),
                pl.BlockSpec((D_MODEL, COL_TILE), lambda i, j: (0, j)),
                pl.BlockSpec((8, A_HEAD_DIM), lambda i, j: (0, 0)),
                pl.BlockSpec((8, COL_TILE), lambda i, j: (0, j))]
    args = [x, mod, gains, w, qkn, conv]
    if rope:
        assert tm == seq_len
        in_specs += [pl.BlockSpec((tm, A_HEAD_DIM), lambda i, j: (0, 0))] * 2
        args += list(rope_tabs)
    return pl.pallas_call(
        functools.partial(_ah_inproj_kernel, seq_len=seq_len, rope=rope),
        grid=(rows // tm, AH_IN // COL_TILE),
        in_specs=in_specs,
        out_specs=out_specs,
        out_shape=out_shape,
        scratch_shapes=[pltpu.VMEM((tm, D_MODEL), BF16)],
        compiler_params=_cparams("arbitrary", "arbitrary"),
        name="ah_inproj",
    )(*args)


def _attn_kernel(*refs, ctx):
    if ctx:
        q_ref, k_ref, v_ref, kc_ref, vc_ref, o_ref = refs
    else:
        q_ref, k_ref, v_ref, o_ref = refs
    k = k_ref[0]
    v = v_ref[0]
    if ctx:
        kc = kc_ref[0]
        vc = vc_ref[0]
    for g in range(A_HEADS // A_KV_HEADS):
        sl = slice(g * A_HEAD_DIM, (g + 1) * A_HEAD_DIM)
        q = q_ref[0, :, sl]
        s = _dot_nt(q, k)
        m = jnp.max(s, axis=-1, keepdims=True)
        if ctx:
            sc = _dot_nt(q, kc)
            m = jnp.maximum(m, jnp.max(sc, axis=-1, keepdims=True))
        p = jnp.exp(s - m)
        den = jnp.sum(p, axis=-1, keepdims=True)
        o = _dot(p.astype(BF16), v)
        if ctx:
            pc = jnp.exp(sc - m)
            den = den + jnp.sum(pc, axis=-1, keepdims=True)
            o = o + _dot(pc.astype(BF16), vc)
        o_ref[0, :, sl] = (o / den).astype(BF16)


def _attn_call(q, k, v, ctx_k, ctx_v, *, batch, seq_len, tq):
    ctx = ctx_k is not None
    qw = A_Q // A_KV_HEADS
    in_specs = [pl.BlockSpec((1, tq, qw), lambda b, h, i: (b, i, h)),
                pl.BlockSpec((1, seq_len, A_HEAD_DIM), lambda b, h, i: (b, 0, h)),
                pl.BlockSpec((1, seq_len, A_HEAD_DIM), lambda b, h, i: (b, 0, h))]
    args = [q.reshape(batch, seq_len, A_Q), k.reshape(batch, seq_len, A_KV), v.reshape(batch, seq_len, A_KV)]
    if ctx:
        past = ctx_k.shape[1]
        in_specs += [pl.BlockSpec((1, past, A_HEAD_DIM), lambda b, h, i: (b, 0, h))] * 2
        args += [ctx_k, ctx_v]
    return pl.pallas_call(
        functools.partial(_attn_kernel, ctx=ctx),
        grid=(batch, A_KV_HEADS, seq_len // tq),
        in_specs=in_specs,
        out_specs=pl.BlockSpec((1, tq, qw), lambda b, h, i: (b, i, h)),
        out_shape=jax.ShapeDtypeStruct((batch, seq_len, A_Q), BF16),
        compiler_params=_cparams("arbitrary", "arbitrary", "arbitrary"),
        name="attn",
    )(*args)


def _hy_filter_kernel(z_ref, w1_ref, b1_ref, w2_ref, b2_ref, w3_ref, b3_ref, sf_ref, dl_ref,
                      cm_ref, sm_ref, gr_ref, gi_ref, hs_ref, hd_ref):
    @pl.when(pl.program_id(0) == 0)
    def _():
        z = z_ref[...]
        h = jnp.sin(sf_ref[0:1, :] * (_dot_f32(z, w1_ref[...]) + b1_ref[...]))
        h = jnp.sin(sf_ref[1:2, :] * (_dot_f32(h, w2_ref[...]) + b2_ref[...]))
        filt = _dot_f32(h, w3_ref[...]) + b3_ref[...]
        win = jnp.exp(-z[:, 0:1] * dl_ref[...])
        hf = filt[:, :HY_CH] * win
        hb = filt[:, HY_CH:] * win
        hs_ref[...] = (hf + hb).astype(BF16)
        hd_ref[...] = (hb - hf).astype(BF16)

    gr_ref[...] = _dot(cm_ref[...], hs_ref[...])
    gi_ref[...] = _dot(sm_ref[...], hd_ref[...])


def _hy_filter_call(z, w1, b1, w2, b2, w3, b3, sf, deltas, cm, sm, *, seq_len, tk):
    full = lambda a: pl.BlockSpec(a.shape, lambda k: (0,) * a.ndim)
    small = [z, w1, b1, w2, b2, w3, b3, sf, deltas]
    return pl.pallas_call(
        _hy_filter_kernel,
        grid=(seq_len // tk,),
        in_specs=[full(a) for a in small] + [pl.BlockSpec((tk, seq_len), lambda k: (k, 0))] * 2,
        out_specs=[pl.BlockSpec((tk, HY_CH), lambda k: (k, 0))] * 2,
        out_shape=[jax.ShapeDtypeStruct((seq_len, HY_CH), F32)] * 2,
        scratch_shapes=[pltpu.VMEM((seq_len, HY_CH), BF16)] * 2,
        compiler_params=_cparams("arbitrary"),
        name="hy_filter",
    )(*small, cm, sm)


def _hy_conv_kernel(x0_ref, x1_ref, v_ref, skip_ref, cm_ref, sm_ref, cmt_ref, smt_ref, gr_ref, gi_ref,
                    o_ref, vv_ref, acc_ref, *, seq_len):
    kb = pl.program_id(2)

    @pl.when(kb == 0)
    def _():
        vv_ref[...] = (v_ref[0].astype(F32) * x1_ref[0].astype(F32)).astype(BF16)
        acc_ref[...] = jnp.zeros_like(acc_ref)

    vv = vv_ref[...]
    vr = _dot(cm_ref[...], vv)
    wi = _dot(sm_ref[...], vv)
    gr = gr_ref[...]
    gi = gi_ref[...]
    pr = gr * vr + gi * wi
    qi = gr * wi - gi * vr
    acc_ref[...] += _dot(cmt_ref[...], pr.astype(BF16)) + _dot(smt_ref[...], qi.astype(BF16))

    @pl.when(kb == pl.num_programs(2) - 1)
    def _():
        vvf = v_ref[0].astype(F32) * x1_ref[0].astype(F32)
        o_ref[0] = ((acc_ref[...] * (1.0 / seq_len) + skip_ref[...] * vvf) * x0_ref[0].astype(F32)).astype(BF16)


def _hy_conv_call(u, skip, dft, gr, gi, *, batch, seq_len, tc, tk):
    proj3 = u.reshape(batch, seq_len, 3 * HY_CH)
    cm, sm, cmt, smt = dft
    nch = HY_CH // tc
    return pl.pallas_call(
        functools.partial(_hy_conv_kernel, seq_len=seq_len),
        grid=(batch, nch, seq_len // tk),
        in_specs=[pl.BlockSpec((1, seq_len, tc), lambda b, c, k: (b, 0, c)),
                  pl.BlockSpec((1, seq_len, tc), lambda b, c, k: (b, 0, nch + c)),
                  pl.BlockSpec((1, seq_len, tc), lambda b, c, k: (b, 0, 2 * nch + c)),
                  pl.BlockSpec((1, tc), lambda b, c, k: (0, c)),
                  pl.BlockSpec((tk, seq_len), lambda b, c, k: (k, 0)),
                  pl.BlockSpec((tk, seq_len), lambda b, c, k: (k, 0)),
                  pl.BlockSpec((seq_len, tk), lambda b, c, k: (0, k)),
                  pl.BlockSpec((seq_len, tk), lambda b, c, k: (0, k)),
                  pl.BlockSpec((tk, tc), lambda b, c, k: (k, c)),
                  pl.BlockSpec((tk, tc), lambda b, c, k: (k, c))],
        out_specs=pl.BlockSpec((1, seq_len, tc), lambda b, c, k: (b, 0, c)),
        out_shape=jax.ShapeDtypeStruct((batch, seq_len, HY_CH), BF16),
        scratch_shapes=[pltpu.VMEM((seq_len, tc), BF16), pltpu.VMEM((seq_len, tc), F32)],
        compiler_params=_cparams("arbitrary", "arbitrary", "arbitrary"),
        name="hy_conv",
    )(proj3, proj3, proj3, skip, cm, sm, cmt, smt, gr, gi)


def _ah_out_kernel(x_ref, a_ref, h_ref, w_ref, mod_ref, gain_ref, o_ref):
    out = _dot(a_ref[...], w_ref[0:A_Q, :]) + _dot(h_ref[...], w_ref[A_Q:, :])
    o_ref[...] = x_ref[...] + mod_ref[0, 2:3, :] * _rms(out, gain_ref[1:2, :])


def _ah_out_call(x, attn, hyo, w, mod, gains, *, tm, rows_per_mod):
    rows = x.shape[0]
    return pl.pallas_call(
        _ah_out_kernel,
        grid=(rows // tm,),
        in_specs=[pl.BlockSpec((tm, D_MODEL), lambda i: (i, 0)),
                  pl.BlockSpec((tm, A_Q), lambda i: (i, 0)),
                  pl.BlockSpec((tm, HY_CH), lambda i: (i, 0)),
                  pl.BlockSpec((A_Q + HY_CH, D_MODEL), lambda i: (0, 0)),
                  pl.BlockSpec((1, 8, D_MODEL), lambda i: (i * tm // rows_per_mod, 0, 0)),
                  pl.BlockSpec((8, D_MODEL), lambda i: (0, 0))],
        out_specs=pl.BlockSpec((tm, D_MODEL), lambda i: (i, 0)),
        out_shape=jax.ShapeDtypeStruct((rows, D_MODEL), F32),
        compiler_params=_cparams("arbitrary"),
        name="ah_out",
    )(x, attn, hyo, w, mod, gains)


def _ffn_kernel(x_ref, mod_ref, gain_ref, wg_ref, wl_ref, cg_ref, cl_ref, wd_ref, o_ref, xn_ref, *, seq_len):
    j = pl.program_id(1)

    @pl.when(j == 0)
    def _():
        xn_ref[...] = _modulated(x_ref[...], gain_ref[2:3, :], mod_ref[0, 3:4, :],
                                 mod_ref[0, 4:5, :]).astype(BF16)
        o_ref[...] = jnp.zeros_like(o_ref)

    xn = xn_ref[...]
    hg = _dwconv3(_dot(xn, wg_ref[...]), cg_ref, seq_len)
    hl = _dwconv3(_dot(xn, wl_ref[...]), cl_ref, seq_len)
    act = (0.5 * hg * (1.0 + lax.erf(hg * INV_SQRT2))) * hl
    o_ref[...] += _dot(act.astype(BF16), wd_ref[...])

    @pl.when(j == pl.num_programs(1) - 1)
    def _():
        o_ref[...] = x_ref[...] + mod_ref[0, 5:6, :] * _rms(o_ref[...], gain_ref[3:4, :])


def _ffn_call(x, mod, gains, w_up, conv, w_down, *, tm, seq_len, rows_per_mod):
    rows = x.shape[0]
    nj = D_FF // COL_TILE
    return pl.pallas_call(
        functools.partial(_ffn_kernel, seq_len=seq_len),
        grid=(rows // tm, nj),
        in_specs=[pl.BlockSpec((tm, D_MODEL), lambda i, j: (i, 0)),
                  pl.BlockSpec((1, 8, D_MODEL), lambda i, j: (i * tm // rows_per_mod, 0, 0)),
                  pl.BlockSpec((8, D_MODEL), lambda i, j: (0, 0)),
                  pl.BlockSpec((D_MODEL, COL_TILE), lambda i, j: (0, j)),
                  pl.BlockSpec((D_MODEL, COL_TILE), lambda i, j: (0, nj + j)),
                  pl.BlockSpec((8, COL_TILE), lambda i, j: (0, j)),
                  pl.BlockSpec((8, COL_TILE), lambda i, j: (0, nj + j)),
                  pl.BlockSpec((COL_TILE, D_MODEL), lambda i, j: (j, 0))],
        out_specs=pl.BlockSpec((tm, D_MODEL), lambda i, j: (i, 0)),
        out_shape=jax.ShapeDtypeStruct((rows, D_MODEL), F32),
        scratch_shapes=[pltpu.VMEM((tm, D_MODEL), BF16)],
        compiler_params=_cparams("arbitrary", "arbitrary"),
        name="ffn",
    )(x, mod, gains, w_up, w_up, conv, conv, w_down)


def _ml_inproj_kernel(x_ref, mod_ref, gain_ref, w_ref, wg_ref, bg_ref, conv_ref,
                      q_ref, kt_ref, v_ref, o_ref, g_ref, xn_ref, *, seq_len):
    j = pl.program_id(1)
    tiles = ML_W // COL_TILE

    @pl.when(j == 0)
    def _():
        xn = _modulated(x_ref[...], gain_ref[0:1, :], mod_ref[0, 0:1, :], mod_ref[0, 1:2, :]).astype(BF16)
        xn_ref[...] = xn
        g_ref[...] = (_dot(xn, wg_ref[...]) + bg_ref[...]).T

    y = _dot(xn_ref[...], w_ref[...])

    def conv_silu():
        z = _dwconv3(y, conv_ref, seq_len)
        return z * jax.nn.sigmoid(z)

    @pl.when(j < tiles)
    def _():
        q_ref[...] = conv_silu().astype(BF16)

    @pl.when(jnp.logical_and(j >= tiles, j < 2 * tiles))
    def _():
        kt_ref[...] = (conv_silu() * (ML_HEAD_DIM ** -0.5)).T.astype(BF16)

    @pl.when(jnp.logical_and(j >= 2 * tiles, j < 3 * tiles))
    def _():
        v_ref[...] = y.astype(BF16)

    @pl.when(j >= 3 * tiles)
    def _():
        o_ref[...] = y


def _ml_inproj_call(x, mod, gains, w, wg, bg, conv, *, tm, seq_len, rows_per_mod):
    rows = x.shape[0]
    n = 4 * ML_W
    tiles = ML_W // COL_TILE

    def section(s):
        return lambda j: jnp.clip(j - s * tiles, 0, tiles - 1)

    return pl.pallas_call(
        functools.partial(_ml_inproj_kernel, seq_len=seq_len),
        grid=(rows // tm, n // COL_TILE),
        in_specs=[pl.BlockSpec((tm, D_MODEL), lambda i, j: (i, 0)),
                  pl.BlockSpec((1, 8, D_MODEL), lambda i, j: (i * tm // rows_per_mod, 0, 0)),
                  pl.BlockSpec((8, D_MODEL), lambda i, j: (0, 0)),
                  pl.BlockSpec((D_MODEL, COL_TILE), lambda i, j: (0, j)),
                  pl.BlockSpec((D_MODEL, LANES), lambda i, j: (0, 0)),
                  pl.BlockSpec((1, LANES), lambda i, j: (0, 0)),
                  pl.BlockSpec((8, COL_TILE), lambda i, j: (0, j))],
        out_specs=[pl.BlockSpec((tm, COL_TILE), lambda i, j: (i, section(0)(j))),
                   pl.BlockSpec((COL_TILE, tm), lambda i, j: (section(1)(j), i)),
                   pl.BlockSpec((tm, COL_TILE), lambda i, j: (i, section(2)(j))),
                   pl.BlockSpec((tm, COL_TILE), lambda i, j: (i, section(3)(j))),
                   pl.BlockSpec((LANES, tm), lambda i, j: (0, i))],
        out_shape=[jax.ShapeDtypeStruct((rows, ML_W), BF16), jax.ShapeDtypeStruct((ML_W, rows), BF16),
                   jax.ShapeDtypeStruct((rows, ML_W), BF16), jax.ShapeDtypeStruct((rows, ML_W), F32),
                   jax.ShapeDtypeStruct((LANES, rows), F32)],
        scratch_shapes=[pltpu.VMEM((tm, D_MODEL), BF16)],
        compiler_params=_cparams("arbitrary", "arbitrary"),
        name="ml_inproj",
    )(x, mod, gains, w, wg, bg, conv)


def _log_sigmoid(x):
    return jnp.minimum(x, 0.0) - jnp.log1p(jnp.exp(-jnp.abs(x)))


def _lane_scan(x, op, reverse, fill):
    n = x.shape[1]
    lane = lax.broadcasted_iota(jnp.int32, x.shape, 1)
    sh = 1
    while sh < n:
        if reverse:
            x = op(x, jnp.where(lane < n - sh, pltpu.roll(x, n - sh, 1), fill))
        else:
            x = op(x, jnp.where(lane >= sh, pltpu.roll(x, sh, 1), fill))
        sh *= 2
    return x


def _bcast_selectors(chunk):
    sel = np.zeros((ML_HEADS, LANES, chunk + LANES), np.float32)
    for h in range(ML_HEADS):
        for g in range(3):
            sel[h, g * ML_HEADS + h, :chunk] = 1.0
            sel[h, (3 + g) * ML_HEADS + h, chunk:] = 1.0
    return jnp.asarray(sel, BF16)


def _ml_scan_kernel(*refs, chunk, has_init):
    if has_init:
        (qf, ktf, vf, qb, ktb, vb, colf, colb, rowf, rowb, sel_ref, c0_ref, n0_ref,
         hf_ref, hb_ref, c_ref, n_ref, s_sc) = refs
    else:
        qf, ktf, vf, qb, ktb, vb, colf, colb, rowf, rowb, sel_ref, hf_ref, hb_ref, c_ref, n_ref, s_sc = refs
    nh, dh = ML_HEADS, ML_HEAD_DIM
    step = pl.program_id(1)

    @pl.when(step == 0)
    def _():
        if has_init:
            for d in range(2):
                for h in range(nh):
                    s_sc[d, h, :, 0:dh] = c0_ref[0, d, h].T
                    s_sc[d, h, :, dh:2 * dh] = jnp.broadcast_to(n0_ref[0, d, h:h + 1, :], (dh, dh)).T
        else:
            s_sc[...] = jnp.zeros_like(s_sc)

    row = lax.broadcasted_iota(jnp.int32, (chunk, chunk), 0)
    col = lax.broadcasted_iota(jnp.int32, (chunk, chunk), 1)
    ones_cols = jnp.ones((chunk, dh), BF16)

    for d, (q_ref, kt_ref, v_ref, col_ref, row_ref, h_ref) in enumerate(
            ((qf, ktf, vf, colf, rowf, hf_ref), (qb, ktb, vb, colb, rowb, hb_ref))):
        reverse = d == 1
        cols = col_ref[0, 0, 0]
        r = row_ref[0, 0, 0, 0:nh, :]
        w_tok = row_ref[0, 0, 0, nh:2 * nh, :]
        w_state = row_ref[0, 0, 0, 2 * nh:3 * nh, :]
        m_prev = row_ref[0, 0, 0, 3 * nh:4 * nh, :]
        mask = (col >= row) if reverse else (col <= row)
        for h in range(nh):
            sl = slice(h * dh, (h + 1) * dh)
            q16 = q_ref[0, :, sl]
            kt16 = kt_ref[sl, :]
            vaug = jnp.concatenate([v_ref[0, :, sl], ones_cols], axis=1)
            bc = _dot(cols, sel_ref[h])
            p_bc = bc[:, :chunk]
            w = jnp.exp(jnp.where(mask, p_bc + r[h:h + 1, :], NEG_BIG))
            w_inter = jnp.exp(p_bc[:, :dh] + m_prev[h:h + 1, 0:dh])
            s16 = (_dot(q16, kt16) * w).astype(BF16)
            s_old = s_sc[d, h]
            tot = _dot(s16, vaug) + jnp.concatenate([w_inter, w_inter], axis=1) * _dot(q16, s_old.astype(BF16))
            h_ref[0, :, sl] = tot[:, :dh] / jnp.maximum(jnp.abs(tot[:, dh:]), jnp.exp(bc[:, chunk:]))
            kts = (kt16.astype(F32) * w_tok[h:h + 1, :]).astype(BF16)
            ws = w_state[h:h + 1, 0:dh]
            s_sc[d, h] = jnp.concatenate([ws, ws], axis=1) * s_old + _dot(kts, vaug)

    @pl.when(step == pl.num_programs(1) - 1)
    def _():
        for d in range(2):
            for h in range(nh):
                s_fin = s_sc[d, h]
                c_ref[0, d, h] = s_fin[:, 0:dh].T
                n_ref[0, d, h:h + 1, :] = s_fin[:, dh:2 * dh].T[0:1, :]


def _ml_gate_kernel(*refs, nc, chunk, bb, has_init):
    if has_init:
        gt_ref, m0_ref, cols_ref, rows_ref, m_ref = refs
    else:
        gt_ref, cols_ref, rows_ref, m_ref = refs
    nh = ML_HEADS
    pad = jnp.zeros((LANES - 6 * nh, chunk), F32)
    for d in range(2):
        reverse = d == 1
        last = 0 if reverse else chunk - 1
        for bi in range(bb):
            m = m0_ref[bi, d] if has_init else jnp.zeros((nh, chunk), F32)
            for c in (range(nc - 1, -1, -1) if reverse else range(nc)):
                c0 = (bi * nc + c) * chunk
                gt = gt_ref[0:4 * nh, c0:c0 + chunk]
                b = _lane_scan(_log_sigmoid(gt[nh * (2 + d):nh * (3 + d), :]), jnp.add, reverse, 0.0)
                r = gt[nh * d:nh * (d + 1), :] - b
                mx = jnp.maximum(m, _lane_scan(r, jnp.maximum, reverse, -jnp.inf))
                mt = b + mx
                m_new = jnp.broadcast_to(mt[:, last:last + 1], (nh, chunk))
                b_last = jnp.broadcast_to(b[:, last:last + 1], (nh, chunk))
                rows_ref[bi, c, d, 0:nh, :] = r
                rows_ref[bi, c, d, nh:2 * nh, :] = jnp.exp(r + b_last - m_new)
                rows_ref[bi, c, d, 2 * nh:3 * nh, :] = jnp.exp(b_last + m - m_new)
                rows_ref[bi, c, d, 3 * nh:4 * nh, :] = m
                stack = []
                for x in (-mx, -mt):
                    stack += [t.astype(F32) for t in _split3(x)]
                stack.append(pad)
                cols_ref[bi, c, d] = jnp.concatenate(stack, axis=0).T.astype(BF16)
                m = m_new
            m_ref[bi, d] = m[:, 0:LANES]


def _ml_gate_call(gates, m0, *, batch, seq_len, chunk):
    nc = seq_len // chunk
    bb = max(1, 16 // nc)
    nh = ML_HEADS
    has_init = m0 is not None
    in_specs = [pl.BlockSpec((LANES, bb * seq_len), lambda i: (0, i))]
    args = [gates]
    if has_init:
        in_specs.append(pl.BlockSpec((bb, 2, nh, chunk), lambda i: (i, 0, 0, 0)))
        args.append(jnp.broadcast_to(m0[..., None], m0.shape + (chunk,)))
    return pl.pallas_call(
        functools.partial(_ml_gate_kernel, nc=nc, chunk=chunk, bb=bb, has_init=has_init),
        grid=(batch // bb,),
        in_specs=in_specs,
        out_specs=[pl.BlockSpec((bb, nc, 2, chunk, LANES), lambda i: (i, 0, 0, 0, 0)),
                   pl.BlockSpec((bb, nc, 2, 4 * nh, chunk), lambda i: (i, 0, 0, 0, 0)),
                   pl.BlockSpec((bb, 2, nh, LANES), lambda i: (i, 0, 0, 0))],
        out_shape=[jax.ShapeDtypeStruct((batch, nc, 2, chunk, LANES), BF16),
                   jax.ShapeDtypeStruct((batch, nc, 2, 4 * nh, chunk), F32),
                   jax.ShapeDtypeStruct((batch, 2, nh, LANES), F32)],
        compiler_params=_cparams("arbitrary"),
        name="ml_gate",
    )(*args)


def _ml_scan_call(q, kt, v, cols, rows, init, *, batch, seq_len, chunk):
    nc = seq_len // chunk
    nh = ML_HEADS
    q3 = q.reshape(batch, seq_len, ML_W)
    v3 = v.reshape(batch, seq_len, ML_W)
    sel = _bcast_selectors(chunk)
    has_init = init is not None

    fwd3 = lambda b, i: (b, i, 0)
    bwd3 = lambda b, i: (b, nc - 1 - i, 0)
    in_specs = [pl.BlockSpec((1, chunk, ML_W), fwd3),
                pl.BlockSpec((ML_W, chunk), lambda b, i: (0, b * nc + i)),
                pl.BlockSpec((1, chunk, ML_W), fwd3),
                pl.BlockSpec((1, chunk, ML_W), bwd3),
                pl.BlockSpec((ML_W, chunk), lambda b, i: (0, b * nc + nc - 1 - i)),
                pl.BlockSpec((1, chunk, ML_W), bwd3),
                pl.BlockSpec((1, 1, 1, chunk, LANES), lambda b, i: (b, i, 0, 0, 0)),
                pl.BlockSpec((1, 1, 1, chunk, LANES), lambda b, i: (b, nc - 1 - i, 1, 0, 0)),
                pl.BlockSpec((1, 1, 1, 4 * nh, chunk), lambda b, i: (b, i, 0, 0, 0)),
                pl.BlockSpec((1, 1, 1, 4 * nh, chunk), lambda b, i: (b, nc - 1 - i, 1, 0, 0)),
                pl.BlockSpec(sel.shape, lambda b, i: (0, 0, 0))]
    args = [q3, kt, v3, q3, kt, v3, cols, cols, rows, rows, sel]
    c_spec = pl.BlockSpec((1, 2, nh, ML_HEAD_DIM, ML_HEAD_DIM), lambda b, i: (b, 0, 0, 0, 0))
    n_spec = pl.BlockSpec((1, 2, nh, ML_HEAD_DIM), lambda b, i: (b, 0, 0, 0))
    if has_init:
        in_specs += [c_spec, n_spec]
        args += list(init)
    return pl.pallas_call(
        functools.partial(_ml_scan_kernel, chunk=chunk, has_init=has_init),
        grid=(batch, nc),
        in_specs=in_specs,
        out_specs=[pl.BlockSpec((1, chunk, ML_W), fwd3),
                   pl.BlockSpec((1, chunk, ML_W), bwd3),
                   c_spec, n_spec],
        out_shape=[jax.ShapeDtypeStruct((batch, seq_len, ML_W), F32),
                   jax.ShapeDtypeStruct((batch, seq_len, ML_W), F32),
                   jax.ShapeDtypeStruct((batch, 2, nh, ML_HEAD_DIM, ML_HEAD_DIM), F32),
                   jax.ShapeDtypeStruct((batch, 2, nh, ML_HEAD_DIM), F32)],
        scratch_shapes=[pltpu.VMEM((2, nh, ML_HEAD_DIM, 2 * ML_HEAD_DIM), F32)],
        compiler_params=_cparams("arbitrary", "arbitrary"),
        name="ml_scan",
    )(*args)


def _ml_out_kernel(x_ref, hf_ref, hb_ref, og_ref, hn_ref, w_ref, mod_ref, gain_ref, o_ref):
    h = hf_ref[...] + hb_ref[...]
    parts = []
    for hd in range(ML_HEADS):
        sl = slice(hd * ML_HEAD_DIM, (hd + 1) * ML_HEAD_DIM)
        parts.append(_rms(h[:, sl], hn_ref[0:1, sl]))
    a = jnp.concatenate(parts, axis=1) * jax.nn.sigmoid(og_ref[...])
    out = _dot(a.astype(BF16), w_ref[...])
    o_ref[...] = x_ref[...] + mod_ref[0, 2:3, :] * _rms(out, gain_ref[1:2, :])


def _ml_out_call(x, hf, hb, og, head_norm, w, mod, gains, *, tm, rows_per_mod):
    rows = x.shape[0]
    return pl.pallas_call(
        _ml_out_kernel,
        grid=(rows // tm,),
        in_specs=[pl.BlockSpec((tm, D_MODEL), lambda i: (i, 0)),
                  pl.BlockSpec((tm, ML_W), lambda i: (i, 0)),
                  pl.BlockSpec((tm, ML_W), lambda i: (i, 0)),
                  pl.BlockSpec((tm, ML_W), lambda i: (i, 0)),
                  pl.BlockSpec((1, ML_W), lambda i: (0, 0)),
                  pl.BlockSpec((ML_W, D_MODEL), lambda i: (0, 0)),
                  pl.BlockSpec((1, 8, D_MODEL), lambda i: (i * tm // rows_per_mod, 0, 0)),
                  pl.BlockSpec((8, D_MODEL), lambda i: (0, 0))],
        out_specs=pl.BlockSpec((tm, D_MODEL), lambda i: (i, 0)),
        out_shape=jax.ShapeDtypeStruct((rows, D_MODEL), F32),
        compiler_params=_cparams("arbitrary"),
        name="ml_out",
    )(x, hf, hb, og, head_norm, w, mod, gains)


@functools.lru_cache(maxsize=None)
def _dft_tables_np(seq_len):
    n4 = 4 * seq_len
    ar = np.arange(seq_len, dtype=np.int64)
    idx = ((2 * ar + 1)[:, None] * ar[None, :]) % n4
    ang = (2.0 * np.pi / n4) * idx.astype(np.float64)
    cm = np.cos(ang).astype(np.float32)
    sm = np.sin(ang).astype(np.float32)
    return tuple(np.ascontiguousarray(a) for a in (cm, sm, cm.T, sm.T))


def _dft_tables(seq_len):
    return tuple(jnp.asarray(a).astype(BF16) for a in _dft_tables_np(seq_len))


def _hy_features(seq_len):
    t = np.linspace(0.0, 1.0, seq_len, dtype=np.float32)
    bands = np.arange(1, HY_BANDS + 1, dtype=np.float32)
    ang = (np.float32(2.0 * np.pi) * t[:, None]) * bands
    z = np.concatenate([t[:, None], np.cos(ang), np.sin(ang)], axis=-1).astype(np.float32)
    z = np.pad(z, ((0, 0), (0, 32 - HY_EMB)))
    deltas = np.abs(np.linspace(HY_MIN_DECAY, HY_MAX_DECAY, HY_CH, dtype=np.float32))[None, :]
    return jnp.asarray(z), jnp.asarray(deltas)


def _rope_tables(seq_len):
    rows = seq_len // GRID_W
    row = np.repeat(np.arange(rows, dtype=np.float32), GRID_W)
    col = np.tile(np.arange(GRID_W, dtype=np.float32), rows)
    n_freq = A_HEAD_DIM // 4
    inv = (np.float32(ROPE_THETA) ** (-np.arange(n_freq, dtype=np.float32) / n_freq)).astype(np.float32)
    ang = np.concatenate([row[:, None] * inv, col[:, None] * inv], axis=-1).astype(np.float32)
    cos, sin = np.cos(ang), np.sin(ang)
    return (jnp.asarray(np.concatenate([cos, cos], axis=-1), F32),
            jnp.asarray(np.concatenate([-sin, sin], axis=-1), F32))


def _pad_rows(a, rows=8):
    return jnp.pad(a, ((0, rows - a.shape[0]), (0, 0)))


def kernel(x_prompt, x_sample, cache_attn_k, cache_attn_v, state_mlstm_C, state_mlstm_n, state_mlstm_m, c, c_ctx, w_mod, b_mod, norm_mix_pre, norm_mix_post, norm_ffn_pre, norm_ffn_post, ffn_w_up, ffn_conv_w, ffn_conv_b, ffn_w_down, ah_w_in, ah_w_out, attn_q_norm, attn_k_norm, hy_conv_w, hy_conv_b, hy_w1, hy_b1, hy_w2, hy_b2, hy_w3, hy_b3, hy_sin_freq, hy_skip, ml_w_in, ml_b_gates, ml_conv_w, ml_conv_b, ml_head_norm, ml_w_out):
    bp, lp, _ = x_prompt.shape
    bs, ls, _ = x_sample.shape
    past = cache_attn_k.shape[2]
    xp = x_prompt.reshape(bp * lp, D_MODEL)
    xs = x_sample.reshape(bs * ls, D_MODEL)
    groups = {
        "p": dict(batch=bp, seq_len=lp, tm=1024, rows_per_mod=bp * lp),
        "s": dict(batch=bs, seq_len=ls, tm=ls, rows_per_mod=ls),
    }

    cc = jnp.concatenate([c, c_ctx[None, :], jnp.zeros((8 - bs - 1, D_MODEL), F32)], axis=0)
    mod_all = _mod_call(cc, w_mod, b_mod).reshape(DEPTH, 8, 6, D_MODEL)
    mod_all = jnp.pad(mod_all, ((0, 0), (0, 0), (0, 2), (0, 0)))
    rope = _rope_tables(ls)

    new_k = new_v = new_c = new_n = new_m = None
    for l in range(DEPTH):
        j = l // 2
        mods = {"s": mod_all[l, :bs], "p": mod_all[l, bs:bs + 1]}
        gains = _pad_rows(jnp.stack([norm_mix_pre[l], norm_mix_post[l], norm_ffn_pre[l], norm_ffn_post[l]]))
        xin = {"p": xp, "s": xs}
        xmid = {}
        if l % 2 == 0:
            w_in = ah_w_in[j].astype(BF16)
            w_out = ah_w_out[j].astype(BF16)
            qkn = _pad_rows(jnp.stack([attn_q_norm[j], attn_k_norm[j]]))
            conv = jnp.pad(_pad_rows(jnp.concatenate([hy_conv_w[j], hy_conv_b[j][None, :]], axis=0)),
                           ((0, 0), (A_Q + 2 * A_KV, 0)))
            w1 = jnp.pad(hy_w1[j], ((0, 32 - HY_EMB), (0, 0)))
            sf = _pad_rows(hy_sin_freq[j])
            for g, cfg in groups.items():
                batch, seq_len, tm, rpm = cfg["batch"], cfg["seq_len"], cfg["tm"], cfg["rows_per_mod"]
                outs = _ah_inproj_call(xin[g], mods[g], gains, w_in, qkn, conv, rope if g == "s" else None,
                                       tm=tm, seq_len=seq_len, rows_per_mod=rpm)
                q, k, v, u = outs[:4]
                if g == "s":
                    ctx_k = cache_attn_k[:, j].reshape(bs, past, A_KV).astype(BF16)
                    ctx_v = cache_attn_v[:, j].reshape(bs, past, A_KV).astype(BF16)
                else:
                    ctx_k = ctx_v = None
                    new_k = outs[4].reshape(bp, 1, lp, A_KV_HEADS, A_HEAD_DIM)
                    new_v = outs[5].reshape(bp, 1, lp, A_KV_HEADS, A_HEAD_DIM)
                attn = _attn_call(q, k, v, ctx_k, ctx_v, batch=batch, seq_len=seq_len, tq=256)
                z, deltas = _hy_features(seq_len)
                dft = _dft_tables(seq_len)
                tk = 256
                gr, gi = _hy_filter_call(z, w1, hy_b1[j][None, :], hy_w2[j], hy_b2[j][None, :], hy_w3[j],
                                         hy_b3[j][None, :], sf, deltas, dft[0], dft[1], seq_len=seq_len, tk=tk)
                hyo = _hy_conv_call(u, hy_skip[j][None, :], dft, gr, gi, batch=batch, seq_len=seq_len,
                                    tc=256, tk=tk)
                xmid[g] = _ah_out_call(xin[g], attn.reshape(batch * seq_len, A_Q),
                                       hyo.reshape(batch * seq_len, HY_CH), w_out, mods[g], gains,
                                       tm=512, rows_per_mod=rpm)
        else:
            w_in = ml_w_in[j][:, :4 * ML_W].astype(BF16)
            w_g = jnp.pad(ml_w_in[j][:, 4 * ML_W:], ((0, 0), (0, LANES - 4 * ML_HEADS))).astype(BF16)
            b_g = jnp.pad(ml_b_gates[j], (0, LANES - 4 * ML_HEADS))[None, :]
            w_out = ml_w_out[j].astype(BF16)
            conv = jnp.pad(_pad_rows(jnp.concatenate([ml_conv_w[j], ml_conv_b[j][None, :]], axis=0)),
                           ((0, 0), (0, 2 * ML_W)))
            for g, cfg in groups.items():
                batch, seq_len, tm, rpm = cfg["batch"], cfg["seq_len"], cfg["tm"], cfg["rows_per_mod"]
                q, kt, v, og, gates = _ml_inproj_call(xin[g], mods[g], gains, w_in, w_g, b_g, conv,
                                                      tm=tm, seq_len=seq_len, rows_per_mod=rpm)
                if g == "s":
                    init, m0 = (state_mlstm_C[:, j], state_mlstm_n[:, j]), state_mlstm_m[:, j]
                else:
                    init = m0 = None
                cols, rows, m_new = _ml_gate_call(gates, m0, batch=batch, seq_len=seq_len, chunk=ML_SCAN_CHUNK)
                hf, hb, c_new, n_new = _ml_scan_call(q, kt, v, cols, rows, init, batch=batch, seq_len=seq_len,
                                                     chunk=ML_SCAN_CHUNK)
                if g == "p":
                    new_c, new_n, new_m = c_new[:, None], n_new[:, None], m_new[:, None, :, :, 0]
                xmid[g] = _ml_out_call(xin[g], hf.reshape(batch * seq_len, ML_W), hb.reshape(batch * seq_len, ML_W),
                                       og, ml_head_norm[j][None, :], w_out, mods[g], gains,
                                       tm=512, rows_per_mod=rpm)
        w_up = ffn_w_up[l].astype(BF16)
        w_down = ffn_w_down[l].astype(BF16)
        conv = _pad_rows(jnp.concatenate([ffn_conv_w[l], ffn_conv_b[l][None, :]], axis=0))
        xout = {}
        for g, cfg in groups.items():
            xout[g] = _ffn_call(xmid[g], mods[g], gains, w_up, conv, w_down, tm=cfg["tm"],
                                seq_len=cfg["seq_len"], rows_per_mod=cfg["rows_per_mod"])
        xp, xs = xout["p"], xout["s"]

    return (xp.reshape(bp, lp, D_MODEL), xs.reshape(bs, ls, D_MODEL), new_k, new_v, new_c, new_n, new_m)
```

```python
import functools
import math

import numpy as np
import jax
import jax.numpy as jnp
from jax import lax
from jax.experimental import pallas as pl
from jax.experimental.pallas import tpu as pltpu

F32 = jnp.float32
BF16 = jnp.bfloat16

D_MODEL = 1024
DEPTH = 2
GRID_W = 64
A_HEADS = 4
A_KV_HEADS = 2
A_HEAD_DIM = 128
A_Q = A_HEADS * A_HEAD_DIM
A_KV = A_KV_HEADS * A_HEAD_DIM
ROPE_THETA = 10000.0
HY_CH = D_MODEL // 2
HY_BANDS = 8
HY_EMB = 1 + 2 * HY_BANDS
HY_W = 64
HY_TARGET = 1e-2
HY_FAST_PCT = 0.3
HY_SLOW_PCT = 1.5
HY_MAX_DECAY = math.log(HY_TARGET) / HY_FAST_PCT
HY_MIN_DECAY = math.log(HY_TARGET) / HY_SLOW_PCT
AH_IN = A_Q + 2 * A_KV + 3 * HY_CH
ML_HEADS = 8
ML_HEAD_DIM = D_MODEL // ML_HEADS
ML_W = ML_HEADS * ML_HEAD_DIM
D_FF = 2816
NORM_EPS = 1e-6
NEG_BIG = -1e30

LANES = 128
VMEM_LIMIT = 56 * 1024 * 1024
COL_TILE = 256
ML_SCAN_CHUNK = 128
INV_SQRT2 = 1.0 / math.sqrt(2.0)


def _cparams(*sem):
    return pltpu.CompilerParams(dimension_semantics=sem, vmem_limit_bytes=VMEM_LIMIT)


def _rms(x, gain):
    return x * lax.rsqrt(jnp.mean(x * x, axis=-1, keepdims=True) + NORM_EPS) * gain


def _modulated(x, gain, shift, scale):
    return _rms(x, gain) * (1.0 + scale) + shift


def _dot(a, b):
    return jnp.dot(a, b, preferred_element_type=F32)


def _dot_nt(a, b):
    return lax.dot_general(a, b, (((1,), (1,)), ((), ())), preferred_element_type=F32)


def _dot_tn(a, b):
    return lax.dot_general(a, b, (((0,), (0,)), ((), ())), preferred_element_type=F32)


def _split3(x):
    hi = x.astype(BF16)
    r1 = x - hi.astype(F32)
    mid = r1.astype(BF16)
    lo = (r1 - mid.astype(F32)).astype(BF16)
    return hi, mid, lo


def _dot_f32(a, b):
    a0, a1, a2 = _split3(a)
    b0, b1, b2 = _split3(b)
    return (_dot(a0, b0) + (_dot(a0, b1) + _dot(a1, b0))
            + (_dot(a0, b2) + _dot(a1, b1) + _dot(a2, b0)))


def _dwconv3(y, conv_ref, seq_len):
    rows, cols = y.shape
    pos = lax.broadcasted_iota(jnp.int32, (rows, LANES), 0) % seq_len
    first = pos == 0
    last = pos == seq_len - 1
    outs = []
    for c0 in range(0, cols, LANES):
        yc = y[:, c0:c0 + LANES]
        prev = jnp.where(first, 0.0, pltpu.roll(yc, 1, 0))
        nxt = jnp.where(last, 0.0, pltpu.roll(yc, rows - 1, 0))
        outs.append(prev * conv_ref[0:1, c0:c0 + LANES] + yc * conv_ref[1:2, c0:c0 + LANES]
                    + nxt * conv_ref[2:3, c0:c0 + LANES] + conv_ref[3:4, c0:c0 + LANES])
    return outs[0] if len(outs) == 1 else jnp.concatenate(outs, axis=1)


def _mod_kernel(c_ref, w_ref, b_ref, o_ref):
    a = c_ref[...]
    a = a * jax.nn.sigmoid(a)
    o_ref[0] = _dot(a.astype(BF16), w_ref[0].astype(BF16)) + b_ref[0]


def _mod_call(cc, w_mod, b_mod):
    tn = 1536
    n = 6 * D_MODEL
    return pl.pallas_call(
        _mod_kernel,
        grid=(DEPTH, n // tn),
        in_specs=[pl.BlockSpec((8, D_MODEL), lambda l, j: (0, 0)),
                  pl.BlockSpec((1, D_MODEL, tn), lambda l, j: (l, 0, j)),
                  pl.BlockSpec((1, 1, tn), lambda l, j: (l, 0, j))],
        out_specs=pl.BlockSpec((1, 8, tn), lambda l, j: (l, 0, j)),
        out_shape=jax.ShapeDtypeStruct((DEPTH, 8, n), F32),
        compiler_params=_cparams("arbitrary", "arbitrary"),
        name="mod",
    )(cc, w_mod, b_mod.reshape(DEPTH, 1, n))


def _ah_inproj_kernel(*refs, seq_len, rope):
    if rope:
        (x_ref, mod_ref, gain_ref, w_ref, qkn_ref, conv_ref, cc_ref, ss_ref,
         q_ref, k_ref, v_ref, u_ref, xn_ref) = refs
        kf_ref = vf_ref = None
    else:
        (x_ref, mod_ref, gain_ref, w_ref, qkn_ref, conv_ref,
         q_ref, k_ref, v_ref, u_ref, kf_ref, vf_ref, xn_ref) = refs
    j = pl.program_id(1)

    @pl.when(j == 0)
    def _():
        xn_ref[...] = _modulated(x_ref[...], gain_ref[0:1, :], mod_ref[0, 0:1, :],
                                 mod_ref[0, 1:2, :]).astype(BF16)

    y = _dot(xn_ref[...], w_ref[...])

    def head(yh, g):
        yh = _rms(yh, g)
        if rope:
            yh = yh * cc_ref[...] + pltpu.roll(yh, A_HEAD_DIM // 2, 1) * ss_ref[...]
        return yh

    heads_per_tile = COL_TILE // A_HEAD_DIM

    @pl.when(j < A_Q // COL_TILE)
    def _():
        for h in range(heads_per_tile):
            sl = slice(h * A_HEAD_DIM, (h + 1) * A_HEAD_DIM)
            q_ref[:, sl] = (head(y[:, sl], qkn_ref[0:1, :]) * (A_HEAD_DIM ** -0.5)).astype(BF16)

    @pl.when(j == A_Q // COL_TILE)
    def _():
        for h in range(heads_per_tile):
            sl = slice(h * A_HEAD_DIM, (h + 1) * A_HEAD_DIM)
            kh = head(y[:, sl], qkn_ref[1:2, :])
            k_ref[:, sl] = kh.astype(BF16)
            if kf_ref is not None:
                kf_ref[:, sl] = kh

    @pl.when(j == (A_Q + A_KV) // COL_TILE)
    def _():
        v_ref[...] = y.astype(BF16)
        if vf_ref is not None:
            vf_ref[...] = y

    @pl.when(j >= (A_Q + 2 * A_KV) // COL_TILE)
    def _():
        u_ref[...] = _dwconv3(y, conv_ref, seq_len).astype(BF16)


def _ah_inproj_call(x, mod, gains, w, qkn, conv, rope_tabs, *, tm, seq_len, rows_per_mod):
    rows = x.shape[0]
    rope = rope_tabs is not None
    q_tiles = A_Q // COL_TILE
    u0 = (A_Q + 2 * A_KV) // COL_TILE
    u_tiles = 3 * HY_CH // COL_TILE
    assert A_KV == COL_TILE
    out_specs = [pl.BlockSpec((tm, COL_TILE), lambda i, j: (i, jnp.minimum(j, q_tiles - 1))),
                 pl.BlockSpec((tm, COL_TILE), lambda i, j: (i, 0)),
                 pl.BlockSpec((tm, COL_TILE), lambda i, j: (i, 0)),
                 pl.BlockSpec((tm, COL_TILE), lambda i, j: (i, jnp.clip(j - u0, 0, u_tiles - 1)))]
    out_shape = [jax.ShapeDtypeStruct((rows, A_Q), BF16), jax.ShapeDtypeStruct((rows, A_KV), BF16),
                 jax.ShapeDtypeStruct((rows, A_KV), BF16), jax.ShapeDtypeStruct((rows, 3 * HY_CH), BF16)]
    if not rope:
        out_specs += [pl.BlockSpec((tm, COL_TILE), lambda i, j: (i, 0))] * 2
        out_shape += [jax.ShapeDtypeStruct((rows, A_KV), F32)] * 2
    in_specs = [pl.BlockSpec((tm, D_MODEL), lambda i, j: (i, 0)),
                pl.BlockSpec((1, 8, D_MODEL), lambda i, j: (i * tm // rows_per_mod, 0, 0)),
                pl.BlockSpec((8, D_MODEL), lambda i, j: (0, 0)),
                pl.BlockSpec((D_MODEL, COL_TILE), lambda i, j: (0, j)),
                pl.BlockSpec((8, A_HEAD_DIM), lambda i, j: (0, 0)),
                pl.BlockSpec((8, COL_TILE), lambda i, j: (0, j))]
    args = [x, mod, gains, w, qkn, conv]
    if rope:
        assert tm == seq_len
        in_specs += [pl.BlockSpec((tm, A_HEAD_DIM), lambda i, j: (0, 0))] * 2
        args += list(rope_tabs)
    return pl.pallas_call(
        functools.partial(_ah_inproj_kernel, seq_len=seq_len, rope=rope),
        grid=(rows // tm, AH_IN // COL_TILE),
        in_specs=in_specs,
        out_specs=out_specs,
        out_shape=out_shape,
        scratch_shapes=[pltpu.VMEM((tm, D_MODEL), BF16)],
        compiler_params=_cparams("arbitrary", "arbitrary"),
        name="ah_inproj",
    )(*args)


def _attn_kernel(*refs, ctx):
    if ctx:
        q_ref, k_ref, v_ref, kc_ref, vc_ref, o_ref = refs
    else:
        q_ref, k_ref, v_ref, o_ref = refs
    k = k_ref[0]
    v = v_ref[0]
    if ctx:
        kc = kc_ref[0]
        vc = vc_ref[0]
    for g in range(A_HEADS // A_KV_HEADS):
        sl = slice(g * A_HEAD_DIM, (g + 1) * A_HEAD_DIM)
        q = q_ref[0, :, sl]
        s = _dot_nt(q, k)
        m = jnp.max(s, axis=-1, keepdims=True)
        if ctx:
            sc = _dot_nt(q, kc)
            m = jnp.maximum(m, jnp.max(sc, axis=-1, keepdims=True))
        p = jnp.exp(s - m)
        den = jnp.sum(p, axis=-1, keepdims=True)
        o = _dot(p.astype(BF16), v)
        if ctx:
            pc = jnp.exp(sc - m)
            den = den + jnp.sum(pc, axis=-1, keepdims=True)
            o = o + _dot(pc.astype(BF16), vc)
        o_ref[0, :, sl] = (o / den).astype(BF16)


def _attn_call(q, k, v, ctx_k, ctx_v, *, batch, seq_len, tq):
    ctx = ctx_k is not None
    qw = A_Q // A_KV_HEADS
    in_specs = [pl.BlockSpec((1, tq, qw), lambda b, h, i: (b, i, h)),
                pl.BlockSpec((1, seq_len, A_HEAD_DIM), lambda b, h, i: (b, 0, h)),
                pl.BlockSpec((1, seq_len, A_HEAD_DIM), lambda b, h, i: (b, 0, h))]
    args = [q.reshape(batch, seq_len, A_Q), k.reshape(batch, seq_len, A_KV), v.reshape(batch, seq_len, A_KV)]
    if ctx:
        past = ctx_k.shape[1]
        in_specs += [pl.BlockSpec((1, past, A_HEAD_DIM), lambda b, h, i: (b, 0, h))] * 2
        args += [ctx_k, ctx_v]
    return pl.pallas_call(
        functools.partial(_attn_kernel, ctx=ctx),
        grid=(batch, A_KV_HEADS, seq_len // tq),
        in_specs=in_specs,
        out_specs=pl.BlockSpec((1, tq, qw), lambda b, h, i: (b, i, h)),
        out_shape=jax.ShapeDtypeStruct((batch, seq_len, A_Q), BF16),
        compiler_params=_cparams("arbitrary", "arbitrary", "arbitrary"),
        name="attn",
    )(*args)


def _hy_filter_kernel(z_ref, w1_ref, b1_ref, w2_ref, b2_ref, w3_ref, b3_ref, sf_ref, dl_ref,
                      cm_ref, sm_ref, gr_ref, gi_ref, hs_ref, hd_ref):
    @pl.when(pl.program_id(0) == 0)
    def _():
        z = z_ref[...]
        h = jnp.sin(sf_ref[0:1, :] * (_dot_f32(z, w1_ref[...]) + b1_ref[...]))
        h = jnp.sin(sf_ref[1:2, :] * (_dot_f32(h, w2_ref[...]) + b2_ref[...]))
        filt = _dot_f32(h, w3_ref[...]) + b3_ref[...]
        win = jnp.exp(-z[:, 0:1] * dl_ref[...])
        hf = filt[:, :HY_CH] * win
        hb = filt[:, HY_CH:] * win
        hs_ref[...] = (hf + hb).astype(BF16)
        hd_ref[...] = (hb - hf).astype(BF16)

    gr_ref[...] = _dot(cm_ref[...], hs_ref[...])
    gi_ref[...] = _dot(sm_ref[...], hd_ref[...])


def _hy_filter_call(z, w1, b1, w2, b2, w3, b3, sf, deltas, cm, sm, *, seq_len, tk):
    full = lambda a: pl.BlockSpec(a.shape, lambda k: (0,) * a.ndim)
    small = [z, w1, b1, w2, b2, w3, b3, sf, deltas]
    return pl.pallas_call(
        _hy_filter_kernel,
        grid=(seq_len // tk,),
        in_specs=[full(a) for a in small] + [pl.BlockSpec((tk, seq_len), lambda k: (k, 0))] * 2,
        out_specs=[pl.BlockSpec((tk, HY_CH), lambda k: (k, 0))] * 2,
        out_shape=[jax.ShapeDtypeStruct((seq_len, HY_CH), F32)] * 2,
        scratch_shapes=[pltpu.VMEM((seq_len, HY_CH), BF16)] * 2,
        compiler_params=_cparams("arbitrary"),
        name="hy_filter",
    )(*small, cm, sm)


def _hy_conv_kernel(x0_ref, x1_ref, v_ref, skip_ref, cm_ref, sm_ref, cmt_ref, smt_ref, gr_ref, gi_ref,
                    o_ref, vv_ref, acc_ref, *, seq_len):
    kb = pl.program_id(2)

    @pl.when(kb == 0)
    def _():
        vv_ref[...] = (v_ref[0].astype(F32) * x1_ref[0].astype(F32)).astype(BF16)
        acc_ref[...] = jnp.zeros_like(acc_ref)

    vv = vv_ref[...]
    vr = _dot(cm_ref[...], vv)
    wi = _dot(sm_ref[...], vv)
    gr = gr_ref[...]
    gi = gi_ref[...]
    pr = gr * vr + gi * wi
    qi = gr * wi - gi * vr
    acc_ref[...] += _dot(cmt_ref[...], pr.astype(BF16)) + _dot(smt_ref[...], qi.astype(BF16))

    @pl.when(kb == pl.num_programs(2) - 1)
    def _():
        vvf = v_ref[0].astype(F32) * x1_ref[0].astype(F32)
        o_ref[0] = ((acc_ref[...] * (1.0 / seq_len) + skip_ref[...] * vvf) * x0_ref[0].astype(F32)).astype(BF16)


def _hy_conv_call(u, skip, dft, gr, gi, *, batch, seq_len, tc, tk):
    proj3 = u.reshape(batch, seq_len, 3 * HY_CH)
    cm, sm, cmt, smt = dft
    nch = HY_CH // tc
    return pl.pallas_call(
        functools.partial(_hy_conv_kernel, seq_len=seq_len),
        grid=(batch, nch, seq_len // tk),
        in_specs=[pl.BlockSpec((1, seq_len, tc), lambda b, c, k: (b, 0, c)),
                  pl.BlockSpec((1, seq_len, tc), lambda b, c, k: (b, 0, nch + c)),
                  pl.BlockSpec((1, seq_len, tc), lambda b, c, k: (b, 0, 2 * nch + c)),
                  pl.BlockSpec((1, tc), lambda b, c, k: (0, c)),
                  pl.BlockSpec((tk, seq_len), lambda b, c, k: (k, 0)),
                  pl.BlockSpec((tk, seq_len), lambda b, c, k: (k, 0)),
                  pl.BlockSpec((seq_len, tk), lambda b, c, k: (0, k)),
                  pl.BlockSpec((seq_len, tk), lambda b, c, k: (0, k)),
                  pl.BlockSpec((tk, tc), lambda b, c, k: (k, c)),
                  pl.BlockSpec((tk, tc), lambda b, c, k: (k, c))],
        out_specs=pl.BlockSpec((1, seq_len, tc), lambda b, c, k: (b, 0, c)),
        out_shape=jax.ShapeDtypeStruct((batch, seq_len, HY_CH), BF16),
        scratch_shapes=[pltpu.VMEM((seq_len, tc), BF16), pltpu.VMEM((seq_len, tc), F32)],
        compiler_params=_cparams("arbitrary", "arbitrary", "arbitrary"),
        name="hy_conv",
    )(proj3, proj3, proj3, skip, cm, sm, cmt, smt, gr, gi)


def _ah_out_kernel(x_ref, a_ref, h_ref, w_ref, mod_ref, gain_ref, o_ref):
    out = _dot(a_ref[...], w_ref[0:A_Q, :]) + _dot(h_ref[...], w_ref[A_Q:, :])
    o_ref[...] = x_ref[...] + mod_ref[0, 2:3, :] * _rms(out, gain_ref[1:2, :])


def _ah_out_call(x, attn, hyo, w, mod, gains, *, tm, rows_per_mod):
    rows = x.shape[0]
    return pl.pallas_call(
        _ah_out_kernel,
        grid=(rows // tm,),
        in_specs=[pl.BlockSpec((tm, D_MODEL), lambda i: (i, 0)),
                  pl.BlockSpec((tm, A_Q), lambda i: (i, 0)),
                  pl.BlockSpec((tm, HY_CH), lambda i: (i, 0)),
                  pl.BlockSpec((A_Q + HY_CH, D_MODEL), lambda i: (0, 0)),
                  pl.BlockSpec((1, 8, D_MODEL), lambda i: (i * tm // rows_per_mod, 0, 0)),
                  pl.BlockSpec((8, D_MODEL), lambda i: (0, 0))],
        out_specs=pl.BlockSpec((tm, D_MODEL), lambda i: (i, 0)),
        out_shape=jax.ShapeDtypeStruct((rows, D_MODEL), F32),
        compiler_params=_cparams("arbitrary"),
        name="ah_out",
    )(x, attn, hyo, w, mod, gains)


def _ffn_kernel(x_ref, mod_ref, gain_ref, wg_ref, wl_ref, cg_ref, cl_ref, wd_ref, o_ref, xn_ref, *, seq_len):
    j = pl.program_id(1)

    @pl.when(j == 0)
    def _():
        xn_ref[...] = _modulated(x_ref[...], gain_ref[2:3, :], mod_ref[0, 3:4, :],
                                 mod_ref[0, 4:5, :]).astype(BF16)
        o_ref[...] = jnp.zeros_like(o_ref)

    xn = xn_ref[...]
    hg = _dwconv3(_dot(xn, wg_ref[0]), cg_ref, seq_len)
    hl = _dwconv3(_dot(xn, wl_ref[0]), cl_ref, seq_len)
    act = (0.5 * hg * (1.0 + lax.erf(hg * INV_SQRT2))) * hl
    o_ref[...] += _dot(act.astype(BF16), wd_ref[0])

    @pl.when(j == pl.num_programs(1) - 1)
    def _():
        o_ref[...] = x_ref[...] + mod_ref[0, 5:6, :] * _rms(o_ref[...], gain_ref[3:4, :])


def _ffn_call(x, mod, gains, w_up, conv, w_down, layer, *, tm, tf, seq_len, rows_per_mod):
    rows = x.shape[0]
    nj = D_FF // tf
    return pl.pallas_call(
        functools.partial(_ffn_kernel, seq_len=seq_len),
        grid=(rows // tm, nj),
        in_specs=[pl.BlockSpec((tm, D_MODEL), lambda i, j: (i, 0)),
                  pl.BlockSpec((1, 8, D_MODEL), lambda i, j: (i * tm // rows_per_mod, 0, 0)),
                  pl.BlockSpec((8, D_MODEL), lambda i, j: (0, 0)),
                  pl.BlockSpec((1, D_MODEL, tf), lambda i, j: (layer, 0, j)),
                  pl.BlockSpec((1, D_MODEL, tf), lambda i, j: (layer, 0, nj + j)),
                  pl.BlockSpec((8, tf), lambda i, j: (0, j)),
                  pl.BlockSpec((8, tf), lambda i, j: (0, nj + j)),
                  pl.BlockSpec((1, tf, D_MODEL), lambda i, j: (layer, j, 0))],
        out_specs=pl.BlockSpec((tm, D_MODEL), lambda i, j: (i, 0)),
        out_shape=jax.ShapeDtypeStruct((rows, D_MODEL), F32),
        scratch_shapes=[pltpu.VMEM((tm, D_MODEL), BF16)],
        compiler_params=_cparams("arbitrary", "arbitrary"),
        name="ffn",
    )(x, mod, gains, w_up, w_up, conv, conv, w_down)


def _ml_inproj_kernel(x_ref, mod_ref, gain_ref, w_ref, wg_ref, bg_ref, conv_ref,
                      q_ref, kt_ref, v_ref, o_ref, g_ref, xn_ref, *, seq_len, cw):
    j = pl.program_id(1)
    tiles = ML_W // cw

    @pl.when(j == 0)
    def _():
        xn = _modulated(x_ref[...], gain_ref[0:1, :], mod_ref[0, 0:1, :], mod_ref[0, 1:2, :]).astype(BF16)
        xn_ref[...] = xn
        g_ref[...] = (_dot(xn, wg_ref[...]) + bg_ref[...]).T

    y = _dot(xn_ref[...], w_ref[...])

    def conv_silu():
        z = _dwconv3(y, conv_ref, seq_len)
        return z * jax.nn.sigmoid(z)

    @pl.when(j < tiles)
    def _():
        q_ref[...] = conv_silu().astype(BF16)

    @pl.when(jnp.logical_and(j >= tiles, j < 2 * tiles))
    def _():
        kt_ref[...] = (conv_silu() * (ML_HEAD_DIM ** -0.5)).T.astype(BF16)

    @pl.when(jnp.logical_and(j >= 2 * tiles, j < 3 * tiles))
    def _():
        v_ref[...] = y.astype(BF16)

    @pl.when(j >= 3 * tiles)
    def _():
        o_ref[...] = y


def _ml_inproj_call(x, mod, gains, w, wg, bg, conv, *, tm, cw, seq_len, rows_per_mod):
    rows = x.shape[0]
    n = 4 * ML_W
    tiles = ML_W // cw

    def section(s):
        return lambda j: jnp.clip(j - s * tiles, 0, tiles - 1)

    return pl.pallas_call(
        functools.partial(_ml_inproj_kernel, seq_len=seq_len, cw=cw),
        grid=(rows // tm, n // cw),
        in_specs=[pl.BlockSpec((tm, D_MODEL), lambda i, j: (i, 0)),
                  pl.BlockSpec((1, 8, D_MODEL), lambda i, j: (i * tm // rows_per_mod, 0, 0)),
                  pl.BlockSpec((8, D_MODEL), lambda i, j: (0, 0)),
                  pl.BlockSpec((D_MODEL, cw), lambda i, j: (0, j)),
                  pl.BlockSpec((D_MODEL, LANES), lambda i, j: (0, 0)),
                  pl.BlockSpec((1, LANES), lambda i, j: (0, 0)),
                  pl.BlockSpec((8, cw), lambda i, j: (0, j))],
        out_specs=[pl.BlockSpec((tm, cw), lambda i, j: (i, section(0)(j))),
                   pl.BlockSpec((cw, tm), lambda i, j: (section(1)(j), i)),
                   pl.BlockSpec((tm, cw), lambda i, j: (i, section(2)(j))),
                   pl.BlockSpec((tm, cw), lambda i, j: (i, section(3)(j))),
                   pl.BlockSpec((LANES, tm), lambda i, j: (0, i))],
        out_shape=[jax.ShapeDtypeStruct((rows, ML_W), BF16), jax.ShapeDtypeStruct((ML_W, rows), BF16),
                   jax.ShapeDtypeStruct((rows, ML_W), BF16), jax.ShapeDtypeStruct((rows, ML_W), F32),
                   jax.ShapeDtypeStruct((LANES, rows), F32)],
        scratch_shapes=[pltpu.VMEM((tm, D_MODEL), BF16)],
        compiler_params=_cparams("arbitrary", "arbitrary"),
        name="ml_inproj",
    )(x, mod, gains, w, wg, bg, conv)


def _log_sigmoid(x):
    return jnp.minimum(x, 0.0) - jnp.log1p(jnp.exp(-jnp.abs(x)))


def _lane_scan(x, op, reverse, fill):
    n = x.shape[1]
    lane = lax.broadcasted_iota(jnp.int32, x.shape, 1)
    sh = 1
    while sh < n:
        if reverse:
            x = op(x, jnp.where(lane < n - sh, pltpu.roll(x, n - sh, 1), fill))
        else:
            x = op(x, jnp.where(lane >= sh, pltpu.roll(x, sh, 1), fill))
        sh *= 2
    return x


def _bcast_selectors(chunk):
    sel = np.zeros((ML_HEADS, LANES, chunk + LANES), np.float32)
    for h in range(ML_HEADS):
        for g in range(3):
            sel[h, g * ML_HEADS + h, :chunk] = 1.0
            sel[h, (3 + g) * ML_HEADS + h, chunk:] = 1.0
    return jnp.asarray(sel, BF16)


def _ml_scan_kernel(*refs, chunk, has_init):
    if has_init:
        (qf, ktf, vf, qb, ktb, vb, colf, colb, rowf, rowb, sel_ref, c0_ref, n0_ref,
         hf_ref, hb_ref, c_ref, n_ref, s_sc, p16_sc, e_sc) = refs
    else:
        (qf, ktf, vf, qb, ktb, vb, colf, colb, rowf, rowb, sel_ref,
         hf_ref, hb_ref, c_ref, n_ref, s_sc, p16_sc, e_sc) = refs
    nh, dh = ML_HEADS, ML_HEAD_DIM
    step = pl.program_id(1)

    @pl.when(step == 0)
    def _():
        if has_init:
            for d in range(2):
                for h in range(nh):
                    s_sc[d, h, :, 0:dh] = c0_ref[0, d, h].T
                    s_sc[d, h, :, dh:2 * dh] = jnp.broadcast_to(n0_ref[0, d, h:h + 1, :], (dh, dh)).T
        else:
            s_sc[...] = jnp.zeros_like(s_sc)

    row = lax.broadcasted_iota(jnp.int32, (chunk, chunk), 0)
    col = lax.broadcasted_iota(jnp.int32, (chunk, chunk), 1)
    ones_cols = jnp.ones((chunk, dh), BF16)

    for d, (q_ref, kt_ref, v_ref, col_ref, row_ref, h_ref) in enumerate(
            ((qf, ktf, vf, colf, rowf, hf_ref), (qb, ktb, vb, colb, rowb, hb_ref))):
        reverse = d == 1
        cols = col_ref[0, 0, 0]
        r = row_ref[0, 0, 0, 0:nh, :]
        w_tok = row_ref[0, 0, 0, nh:2 * nh, :]
        w_state = row_ref[0, 0, 0, 2 * nh:3 * nh, :]
        m_prev = row_ref[0, 0, 0, 3 * nh:4 * nh, :]
        mask = (col >= row) if reverse else (col <= row)
        for h in range(nh):
            sl = slice(h * dh, (h + 1) * dh)
            bc = _dot(cols, sel_ref[h])
            p_bc = bc[:, :chunk]
            w = jnp.exp(jnp.where(mask, p_bc + r[h:h + 1, :], NEG_BIG))
            p16_sc[d, h] = (_dot(q_ref[0, :, sl], kt_ref[sl, :]) * w).astype(BF16)
            e_sc[d, h, :, 0:dh] = jnp.exp(p_bc[:, :dh] + m_prev[h:h + 1, 0:dh])
            e_sc[d, h, :, dh:2 * dh] = jnp.exp(bc[:, chunk:])
        for h in range(nh):
            sl = slice(h * dh, (h + 1) * dh)
            q16 = q_ref[0, :, sl]
            kt16 = kt_ref[sl, :]
            vaug = jnp.concatenate([v_ref[0, :, sl], ones_cols], axis=1)
            s_old = s_sc[d, h]
            w_inter = e_sc[d, h, :, 0:dh]
            tot = (_dot(p16_sc[d, h], vaug)
                   + jnp.concatenate([w_inter, w_inter], axis=1) * _dot(q16, s_old.astype(BF16)))
            h_ref[0, :, sl] = tot[:, :dh] / jnp.maximum(jnp.abs(tot[:, dh:]), e_sc[d, h, :, dh:2 * dh])
            kts = (kt16.astype(F32) * w_tok[h:h + 1, :]).astype(BF16)
            ws = w_state[h:h + 1, 0:dh]
            s_sc[d, h] = jnp.concatenate([ws, ws], axis=1) * s_old + _dot(kts, vaug)

    @pl.when(step == pl.num_programs(1) - 1)
    def _():
        for d in range(2):
            for h in range(nh):
                s_fin = s_sc[d, h]
                c_ref[0, d, h] = s_fin[:, 0:dh].T
                n_ref[0, d, h:h + 1, :] = s_fin[:, dh:2 * dh].T[0:1, :]


def _ml_gate_kernel(*refs, nc, chunk, bb, has_init):
    if has_init:
        gt_ref, m0_ref, cols_ref, rows_ref, m_ref = refs
    else:
        gt_ref, cols_ref, rows_ref, m_ref = refs
    nh = ML_HEADS
    pad = jnp.zeros((LANES - 6 * nh, chunk), F32)
    for d in range(2):
        reverse = d == 1
        last = 0 if reverse else chunk - 1
        for bi in range(bb):
            m = m0_ref[bi, d] if has_init else jnp.zeros((nh, chunk), F32)
            for c in (range(nc - 1, -1, -1) if reverse else range(nc)):
                c0 = (bi * nc + c) * chunk
                gt = gt_ref[0:4 * nh, c0:c0 + chunk]
                b = _lane_scan(_log_sigmoid(gt[nh * (2 + d):nh * (3 + d), :]), jnp.add, reverse, 0.0)
                r = gt[nh * d:nh * (d + 1), :] - b
                mx = jnp.maximum(m, _lane_scan(r, jnp.maximum, reverse, -jnp.inf))
                mt = b + mx
                m_new = jnp.broadcast_to(mt[:, last:last + 1], (nh, chunk))
                b_last = jnp.broadcast_to(b[:, last:last + 1], (nh, chunk))
                rows_ref[bi, c, d, 0:nh, :] = r
                rows_ref[bi, c, d, nh:2 * nh, :] = jnp.exp(r + b_last - m_new)
                rows_ref[bi, c, d, 2 * nh:3 * nh, :] = jnp.exp(b_last + m - m_new)
                rows_ref[bi, c, d, 3 * nh:4 * nh, :] = m
                stack = []
                for x in (-mx, -mt):
                    stack += [t.astype(F32) for t in _split3(x)]
                stack.append(pad)
                cols_ref[bi, c, d] = jnp.concatenate(stack, axis=0).T.astype(BF16)
                m = m_new
            m_ref[bi, d] = m[:, 0:LANES]


def _ml_gate_call(gates, m0, *, batch, seq_len, chunk):
    nc = seq_len // chunk
    bb = max(1, 16 // nc)
    nh = ML_HEADS
    has_init = m0 is not None
    in_specs = [pl.BlockSpec((LANES, bb * seq_len), lambda i: (0, i))]
    args = [gates]
    if has_init:
        in_specs.append(pl.BlockSpec((bb, 2, nh, chunk), lambda i: (i, 0, 0, 0)))
        args.append(jnp.broadcast_to(m0[..., None], m0.shape + (chunk,)))
    return pl.pallas_call(
        functools.partial(_ml_gate_kernel, nc=nc, chunk=chunk, bb=bb, has_init=has_init),
        grid=(batch // bb,),
        in_specs=in_specs,
        out_specs=[pl.BlockSpec((bb, nc, 2, chunk, LANES), lambda i: (i, 0, 0, 0, 0)),
                   pl.BlockSpec((bb, nc, 2, 4 * nh, chunk), lambda i: (i, 0, 0, 0, 0)),
                   pl.BlockSpec((bb, 2, nh, LANES), lambda i: (i, 0, 0, 0))],
        out_shape=[jax.ShapeDtypeStruct((batch, nc, 2, chunk, LANES), BF16),
                   jax.ShapeDtypeStruct((batch, nc, 2, 4 * nh, chunk), F32),
                   jax.ShapeDtypeStruct((batch, 2, nh, LANES), F32)],
        compiler_params=_cparams("arbitrary"),
        name="ml_gate",
    )(*args)


def _ml_scan_call(q, kt, v, cols, rows, init, *, batch, seq_len, chunk):
    nc = seq_len // chunk
    nh = ML_HEADS
    q3 = q.reshape(batch, seq_len, ML_W)
    v3 = v.reshape(batch, seq_len, ML_W)
    sel = _bcast_selectors(chunk)
    has_init = init is not None

    fwd3 = lambda b, i: (b, i, 0)
    bwd3 = lambda b, i: (b, nc - 1 - i, 0)
    in_specs = [pl.BlockSpec((1, chunk, ML_W), fwd3),
                pl.BlockSpec((ML_W, chunk), lambda b, i: (0, b * nc + i)),
                pl.BlockSpec((1, chunk, ML_W), fwd3),
                pl.BlockSpec((1, chunk, ML_W), bwd3),
                pl.BlockSpec((ML_W, chunk), lambda b, i: (0, b * nc + nc - 1 - i)),
                pl.BlockSpec((1, chunk, ML_W), bwd3),
                pl.BlockSpec((1, 1, 1, chunk, LANES), lambda b, i: (b, i, 0, 0, 0)),
                pl.BlockSpec((1, 1, 1, chunk, LANES), lambda b, i: (b, nc - 1 - i, 1, 0, 0)),
                pl.BlockSpec((1, 1, 1, 4 * nh, chunk), lambda b, i: (b, i, 0, 0, 0)),
                pl.BlockSpec((1, 1, 1, 4 * nh, chunk), lambda b, i: (b, nc - 1 - i, 1, 0, 0)),
                pl.BlockSpec(sel.shape, lambda b, i: (0, 0, 0))]
    args = [q3, kt, v3, q3, kt, v3, cols, cols, rows, rows, sel]
    c_spec = pl.BlockSpec((1, 2, nh, ML_HEAD_DIM, ML_HEAD_DIM), lambda b, i: (b, 0, 0, 0, 0))
    n_spec = pl.BlockSpec((1, 2, nh, ML_HEAD_DIM), lambda b, i: (b, 0, 0, 0))
    if has_init:
        in_specs += [c_spec, n_spec]
        args += list(init)
    return pl.pallas_call(
        functools.partial(_ml_scan_kernel, chunk=chunk, has_init=has_init),
        grid=(batch, nc),
        in_specs=in_specs,
        out_specs=[pl.BlockSpec((1, chunk, ML_W), fwd3),
                   pl.BlockSpec((1, chunk, ML_W), bwd3),
                   c_spec, n_spec],
        out_shape=[jax.ShapeDtypeStruct((batch, seq_len, ML_W), F32),
                   jax.ShapeDtypeStruct((batch, seq_len, ML_W), F32),
                   jax.ShapeDtypeStruct((batch, 2, nh, ML_HEAD_DIM, ML_HEAD_DIM), F32),
                   jax.ShapeDtypeStruct((batch, 2, nh, ML_HEAD_DIM), F32)],
        scratch_shapes=[pltpu.VMEM((2, nh, ML_HEAD_DIM, 2 * ML_HEAD_DIM), F32),
                        pltpu.VMEM((2, nh, chunk, chunk), BF16),
                        pltpu.VMEM((2, nh, chunk, 2 * ML_HEAD_DIM), F32)],
        compiler_params=_cparams("arbitrary", "arbitrary"),
        name="ml_scan",
    )(*args)


def _ml_out_kernel(x_ref, hf_ref, hb_ref, og_ref, hn_ref, w_ref, mod_ref, gain_ref, o_ref):
    h = hf_ref[...] + hb_ref[...]
    parts = []
    for hd in range(ML_HEADS):
        sl = slice(hd * ML_HEAD_DIM, (hd + 1) * ML_HEAD_DIM)
        parts.append(_rms(h[:, sl], hn_ref[0:1, sl]))
    a = jnp.concatenate(parts, axis=1) * jax.nn.sigmoid(og_ref[...])
    out = _dot(a.astype(BF16), w_ref[...])
    o_ref[...] = x_ref[...] + mod_ref[0, 2:3, :] * _rms(out, gain_ref[1:2, :])


def _ml_out_call(x, hf, hb, og, head_norm, w, mod, gains, *, tm, rows_per_mod):
    rows = x.shape[0]
    return pl.pallas_call(
        _ml_out_kernel,
        grid=(rows // tm,),
        in_specs=[pl.BlockSpec((tm, D_MODEL), lambda i: (i, 0)),
                  pl.BlockSpec((tm, ML_W), lambda i: (i, 0)),
                  pl.BlockSpec((tm, ML_W), lambda i: (i, 0)),
                  pl.BlockSpec((tm, ML_W), lambda i: (i, 0)),
                  pl.BlockSpec((1, ML_W), lambda i: (0, 0)),
                  pl.BlockSpec((ML_W, D_MODEL), lambda i: (0, 0)),
                  pl.BlockSpec((1, 8, D_MODEL), lambda i: (i * tm // rows_per_mod, 0, 0)),
                  pl.BlockSpec((8, D_MODEL), lambda i: (0, 0))],
        out_specs=pl.BlockSpec((tm, D_MODEL), lambda i: (i, 0)),
        out_shape=jax.ShapeDtypeStruct((rows, D_MODEL), F32),
        compiler_params=_cparams("arbitrary"),
        name="ml_out",
    )(x, hf, hb, og, head_norm, w, mod, gains)


@functools.lru_cache(maxsize=None)
def _dft_tables_np(seq_len):
    n4 = 4 * seq_len
    ar = np.arange(seq_len, dtype=np.int64)
    idx = ((2 * ar + 1)[:, None] * ar[None, :]) % n4
    ang = (2.0 * np.pi / n4) * idx.astype(np.float64)
    cm = np.cos(ang).astype(np.float32)
    sm = np.sin(ang).astype(np.float32)
    return tuple(np.ascontiguousarray(a) for a in (cm, sm, cm.T, sm.T))


def _dft_tables(seq_len):
    return tuple(jnp.asarray(a).astype(BF16) for a in _dft_tables_np(seq_len))


def _hy_features(seq_len):
    t = np.linspace(0.0, 1.0, seq_len, dtype=np.float32)
    bands = np.arange(1, HY_BANDS + 1, dtype=np.float32)
    ang = (np.float32(2.0 * np.pi) * t[:, None]) * bands
    z = np.concatenate([t[:, None], np.cos(ang), np.sin(ang)], axis=-1).astype(np.float32)
    z = np.pad(z, ((0, 0), (0, 32 - HY_EMB)))
    deltas = np.abs(np.linspace(HY_MIN_DECAY, HY_MAX_DECAY, HY_CH, dtype=np.float32))[None, :]
    return jnp.asarray(z), jnp.asarray(deltas)


def _rope_tables(seq_len):
    rows = seq_len // GRID_W
    row = np.repeat(np.arange(rows, dtype=np.float32), GRID_W)
    col = np.tile(np.arange(GRID_W, dtype=np.float32), rows)
    n_freq = A_HEAD_DIM // 4
    inv = (np.float32(ROPE_THETA) ** (-np.arange(n_freq, dtype=np.float32) / n_freq)).astype(np.float32)
    ang = np.concatenate([row[:, None] * inv, col[:, None] * inv], axis=-1).astype(np.float32)
    cos, sin = np.cos(ang), np.sin(ang)
    return (jnp.asarray(np.concatenate([cos, cos], axis=-1), F32),
            jnp.asarray(np.concatenate([-sin, sin], axis=-1), F32))


def _pad_rows(a, rows=8):
    return jnp.pad(a, ((0, rows - a.shape[0]), (0, 0)))


def kernel(x_prompt, x_sample, cache_attn_k, cache_attn_v, state_mlstm_C, state_mlstm_n, state_mlstm_m, c, c_ctx, w_mod, b_mod, norm_mix_pre, norm_mix_post, norm_ffn_pre, norm_ffn_post, ffn_w_up, ffn_conv_w, ffn_conv_b, ffn_w_down, ah_w_in, ah_w_out, attn_q_norm, attn_k_norm, hy_conv_w, hy_conv_b, hy_w1, hy_b1, hy_w2, hy_b2, hy_w3, hy_b3, hy_sin_freq, hy_skip, ml_w_in, ml_b_gates, ml_conv_w, ml_conv_b, ml_head_norm, ml_w_out):
    bp, lp, _ = x_prompt.shape
    bs, ls, _ = x_sample.shape
    past = cache_attn_k.shape[2]
    xp = x_prompt.reshape(bp * lp, D_MODEL)
    xs = x_sample.reshape(bs * ls, D_MODEL)
    groups = {
        "p": dict(batch=bp, seq_len=lp, tm=1024, rows_per_mod=bp * lp, ffn_tm=512, ffn_tf=D_FF // 2, ml_cw=512),
        "s": dict(batch=bs, seq_len=ls, tm=ls, rows_per_mod=ls, ffn_tm=ls, ffn_tf=COL_TILE, ml_cw=COL_TILE),
    }
    ffn_up16 = ffn_w_up.astype(BF16)
    ffn_down16 = ffn_w_down.astype(BF16)

    cc = jnp.concatenate([c, c_ctx[None, :], jnp.zeros((8 - bs - 1, D_MODEL), F32)], axis=0)
    mod_all = _mod_call(cc, w_mod, b_mod).reshape(DEPTH, 8, 6, D_MODEL)
    mod_all = jnp.pad(mod_all, ((0, 0), (0, 0), (0, 2), (0, 0)))
    rope = _rope_tables(ls)

    new_k = new_v = new_c = new_n = new_m = None
    for l in range(DEPTH):
        j = l // 2
        mods = {"s": mod_all[l, :bs], "p": mod_all[l, bs:bs + 1]}
        gains = _pad_rows(jnp.stack([norm_mix_pre[l], norm_mix_post[l], norm_ffn_pre[l], norm_ffn_post[l]]))
        xin = {"p": xp, "s": xs}
        xmid = {}
        if l % 2 == 0:
            w_in = ah_w_in[j].astype(BF16)
            w_out = ah_w_out[j].astype(BF16)
            qkn = _pad_rows(jnp.stack([attn_q_norm[j], attn_k_norm[j]]))
            conv = jnp.pad(_pad_rows(jnp.concatenate([hy_conv_w[j], hy_conv_b[j][None, :]], axis=0)),
                           ((0, 0), (A_Q + 2 * A_KV, 0)))
            w1 = jnp.pad(hy_w1[j], ((0, 32 - HY_EMB), (0, 0)))
            sf = _pad_rows(hy_sin_freq[j])
            for g, cfg in groups.items():
                batch, seq_len, tm, rpm = cfg["batch"], cfg["seq_len"], cfg["tm"], cfg["rows_per_mod"]
                outs = _ah_inproj_call(xin[g], mods[g], gains, w_in, qkn, conv, rope if g == "s" else None,
                                       tm=tm, seq_len=seq_len, rows_per_mod=rpm)
                q, k, v, u = outs[:4]
                if g == "s":
                    ctx_k = cache_attn_k[:, j].reshape(bs, past, A_KV).astype(BF16)
                    ctx_v = cache_attn_v[:, j].reshape(bs, past, A_KV).astype(BF16)
                else:
                    ctx_k = ctx_v = None
                    new_k = outs[4].reshape(bp, 1, lp, A_KV_HEADS, A_HEAD_DIM)
                    new_v = outs[5].reshape(bp, 1, lp, A_KV_HEADS, A_HEAD_DIM)
                attn = _attn_call(q, k, v, ctx_k, ctx_v, batch=batch, seq_len=seq_len, tq=256)
                z, deltas = _hy_features(seq_len)
                dft = _dft_tables(seq_len)
                tk = 256
                gr, gi = _hy_filter_call(z, w1, hy_b1[j][None, :], hy_w2[j], hy_b2[j][None, :], hy_w3[j],
                                         hy_b3[j][None, :], sf, deltas, dft[0], dft[1], seq_len=seq_len, tk=tk)
                hyo = _hy_conv_call(u, hy_skip[j][None, :], dft, gr, gi, batch=batch, seq_len=seq_len,
                                    tc=256, tk=tk)
                xmid[g] = _ah_out_call(xin[g], attn.reshape(batch * seq_len, A_Q),
                                       hyo.reshape(batch * seq_len, HY_CH), w_out, mods[g], gains,
                                       tm=512, rows_per_mod=rpm)
        else:
            w_in = ml_w_in[j].astype(BF16)
            w_g = jnp.pad(ml_w_in[j][:, 4 * ML_W:], ((0, 0), (0, LANES - 4 * ML_HEADS))).astype(BF16)
            b_g = jnp.pad(ml_b_gates[j], (0, LANES - 4 * ML_HEADS))[None, :]
            w_out = ml_w_out[j].astype(BF16)
            conv = jnp.pad(_pad_rows(jnp.concatenate([ml_conv_w[j], ml_conv_b[j][None, :]], axis=0)),
                           ((0, 0), (0, 2 * ML_W)))
            for g, cfg in groups.items():
                batch, seq_len, tm, rpm = cfg["batch"], cfg["seq_len"], cfg["tm"], cfg["rows_per_mod"]
                q, kt, v, og, gates = _ml_inproj_call(xin[g], mods[g], gains, w_in, w_g, b_g, conv, tm=tm,
                                                      cw=cfg["ml_cw"], seq_len=seq_len, rows_per_mod=rpm)
                if g == "s":
                    init, m0 = (state_mlstm_C[:, j], state_mlstm_n[:, j]), state_mlstm_m[:, j]
                else:
                    init = m0 = None
                cols, rows, m_new = _ml_gate_call(gates, m0, batch=batch, seq_len=seq_len, chunk=ML_SCAN_CHUNK)
                hf, hb, c_new, n_new = _ml_scan_call(q, kt, v, cols, rows, init, batch=batch, seq_len=seq_len,
                                                     chunk=ML_SCAN_CHUNK)
                if g == "p":
                    new_c, new_n, new_m = c_new[:, None], n_new[:, None], m_new[:, None, :, :, 0]
                xmid[g] = _ml_out_call(xin[g], hf.reshape(batch * seq_len, ML_W), hb.reshape(batch * seq_len, ML_W),
                                       og, ml_head_norm[j][None, :], w_out, mods[g], gains,
                                       tm=512, rows_per_mod=rpm)
        conv = _pad_rows(jnp.concatenate([ffn_conv_w[l], ffn_conv_b[l][None, :]], axis=0))
        xout = {}
        for g, cfg in groups.items():
            xout[g] = _ffn_call(xmid[g], mods[g], gains, ffn_up16, conv, ffn_down16, l, tm=cfg["ffn_tm"],
                                tf=cfg["ffn_tf"], seq_len=cfg["seq_len"], rows_per_mod=cfg["rows_per_mod"])
        xp, xs = xout["p"], xout["s"]

    return (xp.reshape(bp, lp, D_MODEL), xs.reshape(bs, ls, D_MODEL), new_k, new_v, new_c, new_n, new_m)
```

```python
import functools
import math

import numpy as np
import jax
import jax.numpy as jnp
from jax import lax
from jax.experimental import pallas as pl
from jax.experimental.pallas import tpu as pltpu

F32 = jnp.float32
BF16 = jnp.bfloat16

D_MODEL = 1024
DEPTH = 2
GRID_W = 64
A_HEADS = 4
A_KV_HEADS = 2
A_HEAD_DIM = 128
A_Q = A_HEADS * A_HEAD_DIM
A_KV = A_KV_HEADS * A_HEAD_DIM
ROPE_THETA = 10000.0
HY_CH = D_MODEL // 2
HY_BANDS = 8
HY_EMB = 1 + 2 * HY_BANDS
HY_W = 64
HY_TARGET = 1e-2
HY_FAST_PCT = 0.3
HY_SLOW_PCT = 1.5
HY_MAX_DECAY = math.log(HY_TARGET) / HY_FAST_PCT
HY_MIN_DECAY = math.log(HY_TARGET) / HY_SLOW_PCT
AH_IN = A_Q + 2 * A_KV + 3 * HY_CH
ML_HEADS = 8
ML_HEAD_DIM = D_MODEL // ML_HEADS
ML_W = ML_HEADS * ML_HEAD_DIM
D_FF = 2816
NORM_EPS = 1e-6
NEG_BIG = -1e30

LANES = 128
VMEM_LIMIT = 56 * 1024 * 1024
COL_TILE = 256
ML_SCAN_CHUNK = 128
INV_SQRT2 = 1.0 / math.sqrt(2.0)


def _cparams(*sem):
    return pltpu.CompilerParams(dimension_semantics=sem, vmem_limit_bytes=VMEM_LIMIT)


def _rms(x, gain):
    return x * lax.rsqrt(jnp.mean(x * x, axis=-1, keepdims=True) + NORM_EPS) * gain


def _modulated(x, gain, shift, scale):
    return _rms(x, gain) * (1.0 + scale) + shift


def _dot(a, b):
    return jnp.dot(a, b, preferred_element_type=F32)


def _dot_nt(a, b):
    return lax.dot_general(a, b, (((1,), (1,)), ((), ())), preferred_element_type=F32)


def _dot_tn(a, b):
    return lax.dot_general(a, b, (((0,), (0,)), ((), ())), preferred_element_type=F32)


def _split3(x):
    hi = x.astype(BF16)
    r1 = x - hi.astype(F32)
    mid = r1.astype(BF16)
    lo = (r1 - mid.astype(F32)).astype(BF16)
    return hi, mid, lo


def _dot_f32(a, b):
    a0, a1, a2 = _split3(a)
    b0, b1, b2 = _split3(b)
    return (_dot(a0, b0) + (_dot(a0, b1) + _dot(a1, b0))
            + (_dot(a0, b2) + _dot(a1, b1) + _dot(a2, b0)))


def _dwconv3(y, conv_ref, seq_len, edge=None):
    rows, cols = y.shape
    pos = lax.broadcasted_iota(jnp.int32, (rows, LANES), 0)
    if edge is None:
        pos = pos % seq_len
        first = pos == 0
        last = pos == seq_len - 1
    else:
        first = pos == 0
        last = pos == rows - 1
    outs = []
    for c0 in range(0, cols, LANES):
        yc = y[:, c0:c0 + LANES]
        before = 0.0 if edge is None else edge[0][:, c0:c0 + LANES]
        after = 0.0 if edge is None else edge[1][:, c0:c0 + LANES]
        prev = jnp.where(first, before, pltpu.roll(yc, 1, 0))
        nxt = jnp.where(last, after, pltpu.roll(yc, rows - 1, 0))
        outs.append(prev * conv_ref[0:1, c0:c0 + LANES] + yc * conv_ref[1:2, c0:c0 + LANES]
                    + nxt * conv_ref[2:3, c0:c0 + LANES] + conv_ref[3:4, c0:c0 + LANES])
    return outs[0] if len(outs) == 1 else jnp.concatenate(outs, axis=1)


def _mod_kernel(c_ref, w_ref, b_ref, o_ref):
    a = c_ref[...]
    a = a * jax.nn.sigmoid(a)
    o_ref[0] = _dot(a.astype(BF16), w_ref[0].astype(BF16)) + b_ref[0]


def _mod_call(cc, w_mod, b_mod):
    tn = 1536
    n = 6 * D_MODEL
    return pl.pallas_call(
        _mod_kernel,
        grid=(DEPTH, n // tn),
        in_specs=[pl.BlockSpec((8, D_MODEL), lambda l, j: (0, 0)),
                  pl.BlockSpec((1, D_MODEL, tn), lambda l, j: (l, 0, j)),
                  pl.BlockSpec((1, 1, tn), lambda l, j: (l, 0, j))],
        out_specs=pl.BlockSpec((1, 8, tn), lambda l, j: (l, 0, j)),
        out_shape=jax.ShapeDtypeStruct((DEPTH, 8, n), F32),
        compiler_params=_cparams("arbitrary", "arbitrary"),
        name="mod",
    )(cc, w_mod, b_mod.reshape(DEPTH, 1, n))


def _ah_inproj_kernel(*refs, seq_len, rope):
    if rope:
        (x_ref, mod_ref, gain_ref, w_ref, qkn_ref, conv_ref, cc_ref, ss_ref,
         q_ref, k_ref, v_ref, u_ref, xn_ref) = refs
        kf_ref = vf_ref = None
    else:
        (x_ref, mod_ref, gain_ref, w_ref, qkn_ref, conv_ref,
         q_ref, k_ref, v_ref, u_ref, kf_ref, vf_ref, xn_ref) = refs
    j = pl.program_id(1)

    @pl.when(j == 0)
    def _():
        xn_ref[...] = _modulated(x_ref[...], gain_ref[0:1, :], mod_ref[0, 0:1, :],
                                 mod_ref[0, 1:2, :]).astype(BF16)

    y = _dot(xn_ref[...], w_ref[...])

    def head(yh, g):
        yh = _rms(yh, g)
        if rope:
            yh = yh * cc_ref[...] + pltpu.roll(yh, A_HEAD_DIM // 2, 1) * ss_ref[...]
        return yh

    heads_per_tile = COL_TILE // A_HEAD_DIM

    @pl.when(j < A_Q // COL_TILE)
    def _():
        for h in range(heads_per_tile):
            sl = slice(h * A_HEAD_DIM, (h + 1) * A_HEAD_DIM)
            q_ref[:, sl] = (head(y[:, sl], qkn_ref[0:1, :]) * (A_HEAD_DIM ** -0.5)).astype(BF16)

    @pl.when(j == A_Q // COL_TILE)
    def _():
        for h in range(heads_per_tile):
            sl = slice(h * A_HEAD_DIM, (h + 1) * A_HEAD_DIM)
            kh = head(y[:, sl], qkn_ref[1:2, :])
            k_ref[:, sl] = kh.astype(BF16)
            if kf_ref is not None:
                kf_ref[:, sl] = kh

    @pl.when(j == (A_Q + A_KV) // COL_TILE)
    def _():
        v_ref[...] = y.astype(BF16)
        if vf_ref is not None:
            vf_ref[...] = y

    @pl.when(j >= (A_Q + 2 * A_KV) // COL_TILE)
    def _():
        u_ref[...] = _dwconv3(y, conv_ref, seq_len).astype(BF16)


def _ah_inproj_call(x, mod, gains, w, qkn, conv, rope_tabs, *, tm, seq_len, rows_per_mod):
    rows = x.shape[0]
    rope = rope_tabs is not None
    q_tiles = A_Q // COL_TILE
    u0 = (A_Q + 2 * A_KV) // COL_TILE
    u_tiles = 3 * HY_CH // COL_TILE
    assert A_KV == COL_TILE
    out_specs = [pl.BlockSpec((tm, COL_TILE), lambda i, j: (i, jnp.minimum(j, q_tiles - 1))),
                 pl.BlockSpec((tm, COL_TILE), lambda i, j: (i, 0)),
                 pl.BlockSpec((tm, COL_TILE), lambda i, j: (i, 0)),
                 pl.BlockSpec((tm, COL_TILE), lambda i, j: (i, jnp.clip(j - u0, 0, u_tiles - 1)))]
    out_shape = [jax.ShapeDtypeStruct((rows, A_Q), BF16), jax.ShapeDtypeStruct((rows, A_KV), BF16),
                 jax.ShapeDtypeStruct((rows, A_KV), BF16), jax.ShapeDtypeStruct((rows, 3 * HY_CH), BF16)]
    if not rope:
        out_specs += [pl.BlockSpec((tm, COL_TILE), lambda i, j: (i, 0))] * 2
        out_shape += [jax.ShapeDtypeStruct((rows, A_KV), F32)] * 2
    in_specs = [pl.BlockSpec((tm, D_MODEL), lambda i, j: (i, 0)),
                pl.BlockSpec((1, 8, D_MODEL), lambda i, j: (i * tm // rows_per_mod, 0, 0)),
                pl.BlockSpec((8, D_MODEL), lambda i, j: (0, 0)),
                pl.BlockSpec((D_MODEL, COL_TILE), lambda i, j: (0, j)),
                pl.BlockSpec((8, A_HEAD_DIM), lambda i, j: (0, 0)),
                pl.BlockSpec((8, COL_TILE), lambda i, j: (0, j))]
    args = [x, mod, gains, w, qkn, conv]
    if rope:
        assert tm == seq_len
        in_specs += [pl.BlockSpec((tm, A_HEAD_DIM), lambda i, j: (0, 0))] * 2
        args += list(rope_tabs)
    return pl.pallas_call(
        functools.partial(_ah_inproj_kernel, seq_len=seq_len, rope=rope),
        grid=(rows // tm, AH_IN // COL_TILE),
        in_specs=in_specs,
        out_specs=out_specs,
        out_shape=out_shape,
        scratch_shapes=[pltpu.VMEM((tm, D_MODEL), BF16)],
        compiler_params=_cparams("arbitrary", "arbitrary"),
        name="ah_inproj",
    )(*args)


def _attn_kernel(*refs, ctx, bb):
    if ctx:
        q_ref, k_ref, v_ref, kc_ref, vc_ref, o_ref = refs
    else:
        q_ref, k_ref, v_ref, o_ref = refs
    dh = A_HEAD_DIM
    for bi in range(bb):
        k = k_ref[bi]
        if ctx:
            kc = kc_ref[bi]
        else:
            vaug = jnp.concatenate([v_ref[bi], jnp.ones((k.shape[0], dh), BF16)], axis=1)
        for g in range(A_HEADS // A_KV_HEADS):
            sl = slice(g * dh, (g + 1) * dh)
            q = q_ref[bi, :, sl]
            s = _dot_nt(q, k)
            m = jnp.max(s, axis=-1, keepdims=True)
            if ctx:
                sc = _dot_nt(q, kc)
                m = jnp.maximum(m, jnp.max(sc, axis=-1, keepdims=True))
            if ctx:
                p = jnp.exp(s - m)
                pc = jnp.exp(sc - m)
                den = jnp.sum(p, axis=-1, keepdims=True) + jnp.sum(pc, axis=-1, keepdims=True)
                o = _dot(p.astype(BF16), v_ref[bi]) + _dot(pc.astype(BF16), vc_ref[bi])
                o_ref[bi, :, sl] = (o / den).astype(BF16)
            else:
                o = _dot(jnp.exp(s - m).astype(BF16), vaug)
                o_ref[bi, :, sl] = (o[:, :dh] / o[:, dh:]).astype(BF16)


def _attn_call(q, k, v, ctx_k, ctx_v, *, batch, seq_len, tq, bb):
    ctx = ctx_k is not None
    qw = A_Q // A_KV_HEADS
    in_specs = [pl.BlockSpec((bb, tq, qw), lambda b, h, i: (b, i, h)),
                pl.BlockSpec((bb, seq_len, A_HEAD_DIM), lambda b, h, i: (b, 0, h)),
                pl.BlockSpec((bb, seq_len, A_HEAD_DIM), lambda b, h, i: (b, 0, h))]
    args = [q.reshape(batch, seq_len, A_Q), k.reshape(batch, seq_len, A_KV), v.reshape(batch, seq_len, A_KV)]
    if ctx:
        past = ctx_k.shape[1]
        in_specs += [pl.BlockSpec((bb, past, A_HEAD_DIM), lambda b, h, i: (b, 0, h))] * 2
        args += [ctx_k, ctx_v]
    return pl.pallas_call(
        functools.partial(_attn_kernel, ctx=ctx, bb=bb),
        grid=(batch // bb, A_KV_HEADS, seq_len // tq),
        in_specs=in_specs,
        out_specs=pl.BlockSpec((bb, tq, qw), lambda b, h, i: (b, i, h)),
        out_shape=jax.ShapeDtypeStruct((batch, seq_len, A_Q), BF16),
        compiler_params=_cparams("arbitrary", "arbitrary", "arbitrary"),
        name="attn",
    )(*args)


def _hy_filter_kernel(z_ref, w1_ref, b1_ref, w2_ref, b2_ref, w3_ref, b3_ref, sf_ref, dl_ref,
                      cm_ref, sm_ref, gr_ref, gi_ref, hs_ref, hd_ref):
    @pl.when(pl.program_id(0) == 0)
    def _():
        z = z_ref[...]
        h = jnp.sin(sf_ref[0:1, :] * (_dot_f32(z, w1_ref[...]) + b1_ref[...]))
        h = jnp.sin(sf_ref[1:2, :] * (_dot_f32(h, w2_ref[...]) + b2_ref[...]))
        filt = _dot_f32(h, w3_ref[...]) + b3_ref[...]
        win = jnp.exp(-z[:, 0:1] * dl_ref[...])
        hf = filt[:, :HY_CH] * win
        hb = filt[:, HY_CH:] * win
        hs_ref[...] = (hf + hb).astype(BF16)
        hd_ref[...] = (hb - hf).astype(BF16)

    gr_ref[...] = _dot(cm_ref[...], hs_ref[...])
    gi_ref[...] = _dot(sm_ref[...], hd_ref[...])


def _hy_filter_call(z, w1, b1, w2, b2, w3, b3, sf, deltas, cm, sm, *, seq_len, tk):
    full = lambda a: pl.BlockSpec(a.shape, lambda k: (0,) * a.ndim)
    small = [z, w1, b1, w2, b2, w3, b3, sf, deltas]
    return pl.pallas_call(
        _hy_filter_kernel,
        grid=(seq_len // tk,),
        in_specs=[full(a) for a in small] + [pl.BlockSpec((tk, seq_len), lambda k: (k, 0))] * 2,
        out_specs=[pl.BlockSpec((tk, HY_CH), lambda k: (k, 0))] * 2,
        out_shape=[jax.ShapeDtypeStruct((seq_len, HY_CH), F32)] * 2,
        scratch_shapes=[pltpu.VMEM((seq_len, HY_CH), BF16)] * 2,
        compiler_params=_cparams("arbitrary"),
        name="hy_filter",
    )(*small, cm, sm)


def _hy_conv_kernel(x0_ref, x1_ref, v_ref, skip_ref, cm_ref, sm_ref, cmt_ref, smt_ref, gr_ref, gi_ref,
                    o_ref, vv_ref, acc_ref, *, seq_len, bb):
    kb = pl.program_id(2)
    single = seq_len == cm_ref.shape[0]
    gr = gr_ref[...]
    gi = gi_ref[...]
    for bi in range(bb):
        def gated():
            return v_ref[bi].astype(F32) * x1_ref[bi].astype(F32)

        if single:
            vv = gated().astype(BF16)
        else:
            @pl.when(kb == 0)
            def _():
                vv_ref[bi] = gated().astype(BF16)
                acc_ref[bi] = jnp.zeros_like(acc_ref[bi])

            vv = vv_ref[bi]
        vr = _dot(cm_ref[...], vv)
        wi = _dot(sm_ref[...], vv)
        pr = gr * vr + gi * wi
        qi = gr * wi - gi * vr
        y = _dot(cmt_ref[...], pr.astype(BF16)) + _dot(smt_ref[...], qi.astype(BF16))

        def finish(total):
            o_ref[bi] = ((total * (1.0 / seq_len) + skip_ref[...] * gated())
                         * x0_ref[bi].astype(F32)).astype(BF16)

        if single:
            finish(y)
        else:
            acc_ref[bi] += y

            @pl.when(kb == pl.num_programs(2) - 1)
            def _():
                finish(acc_ref[bi])


def _hy_conv_call(u, skip, dft, gr, gi, *, batch, seq_len, tc, tk, bb):
    proj3 = u.reshape(batch, seq_len, 3 * HY_CH)
    cm, sm, cmt, smt = dft
    nch = HY_CH // tc
    return pl.pallas_call(
        functools.partial(_hy_conv_kernel, seq_len=seq_len, bb=bb),
        grid=(batch // bb, nch, seq_len // tk),
        in_specs=[pl.BlockSpec((bb, seq_len, tc), lambda b, c, k: (b, 0, c)),
                  pl.BlockSpec((bb, seq_len, tc), lambda b, c, k: (b, 0, nch + c)),
                  pl.BlockSpec((bb, seq_len, tc), lambda b, c, k: (b, 0, 2 * nch + c)),
                  pl.BlockSpec((1, tc), lambda b, c, k: (0, c)),
                  pl.BlockSpec((tk, seq_len), lambda b, c, k: (k, 0)),
                  pl.BlockSpec((tk, seq_len), lambda b, c, k: (k, 0)),
                  pl.BlockSpec((seq_len, tk), lambda b, c, k: (0, k)),
                  pl.BlockSpec((seq_len, tk), lambda b, c, k: (0, k)),
                  pl.BlockSpec((tk, tc), lambda b, c, k: (k, c)),
                  pl.BlockSpec((tk, tc), lambda b, c, k: (k, c))],
        out_specs=pl.BlockSpec((bb, seq_len, tc), lambda b, c, k: (b, 0, c)),
        out_shape=jax.ShapeDtypeStruct((batch, seq_len, HY_CH), BF16),
        scratch_shapes=[pltpu.VMEM((bb, seq_len, tc), BF16), pltpu.VMEM((bb, seq_len, tc), F32)],
        compiler_params=_cparams("arbitrary", "arbitrary", "arbitrary"),
        name="hy_conv",
    )(proj3, proj3, proj3, skip, cm, sm, cmt, smt, gr, gi)


def _ah_out_kernel(x_ref, a_ref, h_ref, w_ref, mod_ref, gain_ref, o_ref):
    out = _dot(a_ref[...], w_ref[0:A_Q, :]) + _dot(h_ref[...], w_ref[A_Q:, :])
    o_ref[...] = x_ref[...] + mod_ref[0, 2:3, :] * _rms(out, gain_ref[1:2, :])


def _ah_out_call(x, attn, hyo, w, mod, gains, *, tm, rows_per_mod):
    rows = x.shape[0]
    return pl.pallas_call(
        _ah_out_kernel,
        grid=(rows // tm,),
        in_specs=[pl.BlockSpec((tm, D_MODEL), lambda i: (i, 0)),
                  pl.BlockSpec((tm, A_Q), lambda i: (i, 0)),
                  pl.BlockSpec((tm, HY_CH), lambda i: (i, 0)),
                  pl.BlockSpec((A_Q + HY_CH, D_MODEL), lambda i: (0, 0)),
                  pl.BlockSpec((1, 8, D_MODEL), lambda i: (i * tm // rows_per_mod, 0, 0)),
                  pl.BlockSpec((8, D_MODEL), lambda i: (0, 0))],
        out_specs=pl.BlockSpec((tm, D_MODEL), lambda i: (i, 0)),
        out_shape=jax.ShapeDtypeStruct((rows, D_MODEL), F32),
        compiler_params=_cparams("arbitrary"),
        name="ah_out",
    )(x, attn, hyo, w, mod, gains)


def _ffn_kernel(*refs, seq_len, halo):
    if halo:
        x_ref, xb_ref, xa_ref, mod_ref, gain_ref, wg_ref, wl_ref, cg_ref, cl_ref, wd_ref, o_ref, xn_ref = refs
    else:
        x_ref, mod_ref, gain_ref, wg_ref, wl_ref, cg_ref, cl_ref, wd_ref, o_ref, xn_ref = refs
    i = pl.program_id(0)
    j = pl.program_id(1)
    tm = x_ref.shape[0]

    def modulated(x):
        return _modulated(x, gain_ref[2:3, :], mod_ref[0, 3:4, :], mod_ref[0, 4:5, :]).astype(BF16)

    @pl.when(j == 0)
    def _():
        xn_ref[0:tm, :] = modulated(x_ref[...])
        o_ref[...] = jnp.zeros_like(o_ref)
        if halo:
            xn_ref[tm:tm + 16, :] = modulated(jnp.concatenate([xb_ref[...], xa_ref[...]], axis=0))

    xn = xn_ref[...]
    yg = _dot(xn, wg_ref[0])
    yl = _dot(xn, wl_ref[0])
    edge_g = edge_l = None
    if halo:
        at_start = (i * tm) % seq_len == 0
        at_end = ((i + 1) * tm) % seq_len == 0
        edge_g = (jnp.where(at_start, 0.0, yg[tm + 7:tm + 8, :]), jnp.where(at_end, 0.0, yg[tm + 8:tm + 9, :]))
        edge_l = (jnp.where(at_start, 0.0, yl[tm + 7:tm + 8, :]), jnp.where(at_end, 0.0, yl[tm + 8:tm + 9, :]))
        yg = yg[0:tm, :]
        yl = yl[0:tm, :]
    hg = _dwconv3(yg, cg_ref, seq_len, edge_g)
    hl = _dwconv3(yl, cl_ref, seq_len, edge_l)
    act = (0.5 * hg * (1.0 + lax.erf(hg * INV_SQRT2))) * hl
    o_ref[...] += _dot(act.astype(BF16), wd_ref[0])

    @pl.when(j == pl.num_programs(1) - 1)
    def _():
        o_ref[...] = x_ref[...] + mod_ref[0, 5:6, :] * _rms(o_ref[...], gain_ref[3:4, :])


def _ffn_call(x, mod, gains, w_up, conv, w_down, layer, *, tm, tf, seq_len, rows_per_mod):
    rows = x.shape[0]
    nj = D_FF // tf
    halo = tm < seq_len
    assert (seq_len % tm == 0) if halo else (tm % seq_len == 0)
    sub = 8
    x_specs = [pl.BlockSpec((tm, D_MODEL), lambda i, j: (i, 0))]
    x_args = [x]
    scratch = [pltpu.VMEM((tm + 2 * sub if halo else tm, D_MODEL), BF16)]
    if halo:
        per = tm // sub
        x_specs += [pl.BlockSpec((sub, D_MODEL), lambda i, j: (jnp.maximum(i * per - 1, 0), 0)),
                    pl.BlockSpec((sub, D_MODEL), lambda i, j: (jnp.minimum((i + 1) * per, rows // sub - 1), 0))]
        x_args += [x, x]
    return pl.pallas_call(
        functools.partial(_ffn_kernel, seq_len=seq_len, halo=halo),
        grid=(rows // tm, nj),
        in_specs=x_specs + [
            pl.BlockSpec((1, 8, D_MODEL), lambda i, j: (i * tm // rows_per_mod, 0, 0)),
            pl.BlockSpec((8, D_MODEL), lambda i, j: (0, 0)),
            pl.BlockSpec((1, D_MODEL, tf), lambda i, j: (layer, 0, j)),
            pl.BlockSpec((1, D_MODEL, tf), lambda i, j: (layer, 0, nj + j)),
            pl.BlockSpec((8, tf), lambda i, j: (0, j)),
            pl.BlockSpec((8, tf), lambda i, j: (0, nj + j)),
            pl.BlockSpec((1, tf, D_MODEL), lambda i, j: (layer, j, 0))],
        out_specs=pl.BlockSpec((tm, D_MODEL), lambda i, j: (i, 0)),
        out_shape=jax.ShapeDtypeStruct((rows, D_MODEL), F32),
        scratch_shapes=scratch,
        compiler_params=_cparams("arbitrary", "arbitrary"),
        name="ffn",
    )(*x_args, mod, gains, w_up, w_up, conv, conv, w_down)


def _ml_inproj_kernel(x_ref, mod_ref, gain_ref, w_ref, wg_ref, bg_ref, conv_ref,
                      q_ref, kt_ref, v_ref, o_ref, g_ref, xn_ref, *, seq_len, cw):
    j = pl.program_id(1)
    tiles = ML_W // cw

    @pl.when(j == 0)
    def _():
        xn = _modulated(x_ref[...], gain_ref[0:1, :], mod_ref[0, 0:1, :], mod_ref[0, 1:2, :]).astype(BF16)
        xn_ref[...] = xn
        g_ref[...] = (_dot(xn, wg_ref[...]) + bg_ref[...]).T

    y = _dot(xn_ref[...], w_ref[...])

    def conv_silu():
        z = _dwconv3(y, conv_ref, seq_len)
        return z * jax.nn.sigmoid(z)

    @pl.when(j < tiles)
    def _():
        q_ref[...] = conv_silu().astype(BF16)

    @pl.when(jnp.logical_and(j >= tiles, j < 2 * tiles))
    def _():
        kt_ref[...] = (conv_silu() * (ML_HEAD_DIM ** -0.5)).T.astype(BF16)

    @pl.when(jnp.logical_and(j >= 2 * tiles, j < 3 * tiles))
    def _():
        v_ref[...] = y.astype(BF16)

    @pl.when(j >= 3 * tiles)
    def _():
        o_ref[...] = y.astype(BF16)


def _ml_inproj_call(x, mod, gains, w, wg, bg, conv, *, tm, cw, seq_len, rows_per_mod):
    rows = x.shape[0]
    n = 4 * ML_W
    tiles = ML_W // cw

    def section(s):
        return lambda j: jnp.clip(j - s * tiles, 0, tiles - 1)

    return pl.pallas_call(
        functools.partial(_ml_inproj_kernel, seq_len=seq_len, cw=cw),
        grid=(rows // tm, n // cw),
        in_specs=[pl.BlockSpec((tm, D_MODEL), lambda i, j: (i, 0)),
                  pl.BlockSpec((1, 8, D_MODEL), lambda i, j: (i * tm // rows_per_mod, 0, 0)),
                  pl.BlockSpec((8, D_MODEL), lambda i, j: (0, 0)),
                  pl.BlockSpec((D_MODEL, cw), lambda i, j: (0, j)),
                  pl.BlockSpec((D_MODEL, LANES), lambda i, j: (0, 0)),
                  pl.BlockSpec((1, LANES), lambda i, j: (0, 0)),
                  pl.BlockSpec((8, cw), lambda i, j: (0, j))],
        out_specs=[pl.BlockSpec((tm, cw), lambda i, j: (i, section(0)(j))),
                   pl.BlockSpec((cw, tm), lambda i, j: (section(1)(j), i)),
                   pl.BlockSpec((tm, cw), lambda i, j: (i, section(2)(j))),
                   pl.BlockSpec((tm, cw), lambda i, j: (i, section(3)(j))),
                   pl.BlockSpec((LANES, tm), lambda i, j: (0, i))],
        out_shape=[jax.ShapeDtypeStruct((rows, ML_W), BF16), jax.ShapeDtypeStruct((ML_W, rows), BF16),
                   jax.ShapeDtypeStruct((rows, ML_W), BF16), jax.ShapeDtypeStruct((rows, ML_W), BF16),
                   jax.ShapeDtypeStruct((LANES, rows), F32)],
        scratch_shapes=[pltpu.VMEM((tm, D_MODEL), BF16)],
        compiler_params=_cparams("arbitrary", "arbitrary"),
        name="ml_inproj",
    )(x, mod, gains, w, wg, bg, conv)


def _log_sigmoid(x):
    return jnp.minimum(x, 0.0) - jnp.log1p(jnp.exp(-jnp.abs(x)))


def _lane_scan(x, op, reverse, fill):
    n = x.shape[1]
    lane = lax.broadcasted_iota(jnp.int32, x.shape, 1)
    sh = 1
    while sh < n:
        if reverse:
            x = op(x, jnp.where(lane < n - sh, pltpu.roll(x, n - sh, 1), fill))
        else:
            x = op(x, jnp.where(lane >= sh, pltpu.roll(x, sh, 1), fill))
        sh *= 2
    return x


def _bcast_selectors(chunk):
    sel = np.zeros((ML_HEADS, LANES, chunk + LANES), np.float32)
    for h in range(ML_HEADS):
        for g in range(3):
            sel[h, g * ML_HEADS + h, :chunk] = 1.0
            sel[h, (3 + g) * ML_HEADS + h, chunk:] = 1.0
    return jnp.asarray(sel, BF16)


def _ml_scan_kernel(*refs, chunk, has_init):
    if has_init:
        (qf, ktf, vf, qb, ktb, vb, colf, colb, rowf, rowb, sel_ref, c0_ref, n0_ref,
         hf_ref, hb_ref, c_ref, n_ref, s_sc, p16_sc, e_sc) = refs
    else:
        (qf, ktf, vf, qb, ktb, vb, colf, colb, rowf, rowb, sel_ref,
         hf_ref, hb_ref, c_ref, n_ref, s_sc, p16_sc, e_sc) = refs
    nh, dh = ML_HEADS, ML_HEAD_DIM
    step = pl.program_id(1)

    @pl.when(step == 0)
    def _():
        if has_init:
            for d in range(2):
                for h in range(nh):
                    s_sc[d, h, :, 0:dh] = c0_ref[0, d, h].T
                    s_sc[d, h, :, dh:2 * dh] = jnp.broadcast_to(n0_ref[0, d, h:h + 1, :], (dh, dh)).T
        else:
            s_sc[...] = jnp.zeros_like(s_sc)

    row = lax.broadcasted_iota(jnp.int32, (chunk, chunk), 0)
    col = lax.broadcasted_iota(jnp.int32, (chunk, chunk), 1)
    ones_cols = jnp.ones((chunk, dh), BF16)

    for d, (q_ref, kt_ref, v_ref, col_ref, row_ref, h_ref) in enumerate(
            ((qf, ktf, vf, colf, rowf, hf_ref), (qb, ktb, vb, colb, rowb, hb_ref))):
        reverse = d == 1
        cols = col_ref[0, 0, 0]
        r = row_ref[0, 0, 0, 0:nh, :]
        w_tok = row_ref[0, 0, 0, nh:2 * nh, :]
        w_state = row_ref[0, 0, 0, 2 * nh:3 * nh, :]
        m_prev = row_ref[0, 0, 0, 3 * nh:4 * nh, :]
        mask = (col >= row) if reverse else (col <= row)
        for h in range(nh):
            sl = slice(h * dh, (h + 1) * dh)
            bc = _dot(cols, sel_ref[h])
            p_bc = bc[:, :chunk]
            w = jnp.exp(jnp.where(mask, p_bc + r[h:h + 1, :], NEG_BIG))
            p16_sc[d, h] = (_dot(q_ref[0, :, sl], kt_ref[sl, :]) * w).astype(BF16)
            e_sc[d, h, :, 0:dh] = jnp.exp(p_bc[:, :dh] + m_prev[h:h + 1, 0:dh])
            e_sc[d, h, :, dh:2 * dh] = jnp.exp(bc[:, chunk:])
        for h in range(nh):
            sl = slice(h * dh, (h + 1) * dh)
            q16 = q_ref[0, :, sl]
            kt16 = kt_ref[sl, :]
            vaug = jnp.concatenate([v_ref[0, :, sl], ones_cols], axis=1)
            s_old = s_sc[d, h]
            w_inter = e_sc[d, h, :, 0:dh]
            tot = (_dot(p16_sc[d, h], vaug)
                   + jnp.concatenate([w_inter, w_inter], axis=1) * _dot(q16, s_old.astype(BF16)))
            h_ref[0, :, sl] = (tot[:, :dh] / jnp.maximum(jnp.abs(tot[:, dh:]),
                                                         e_sc[d, h, :, dh:2 * dh])).astype(BF16)
            kts = (kt16.astype(F32) * w_tok[h:h + 1, :]).astype(BF16)
            ws = w_state[h:h + 1, 0:dh]
            s_sc[d, h] = jnp.concatenate([ws, ws], axis=1) * s_old + _dot(kts, vaug)

    @pl.when(step == pl.num_programs(1) - 1)
    def _():
        for d in range(2):
            for h in range(nh):
                s_fin = s_sc[d, h]
                c_ref[0, d, h] = s_fin[:, 0:dh].T
                n_ref[0, d, h:h + 1, :] = s_fin[:, dh:2 * dh].T[0:1, :]


def _ml_gate_kernel(*refs, nc, chunk, bb, has_init):
    if has_init:
        gt_ref, m0_ref, cols_ref, rows_ref, m_ref = refs
    else:
        gt_ref, cols_ref, rows_ref, m_ref = refs
    nh = ML_HEADS
    pad = jnp.zeros((LANES - 6 * nh, chunk), F32)
    for d in range(2):
        reverse = d == 1
        last = 0 if reverse else chunk - 1
        for bi in range(bb):
            m = m0_ref[bi, d] if has_init else jnp.zeros((nh, chunk), F32)
            for c in (range(nc - 1, -1, -1) if reverse else range(nc)):
                c0 = (bi * nc + c) * chunk
                gt = gt_ref[0:4 * nh, c0:c0 + chunk]
                b = _lane_scan(_log_sigmoid(gt[nh * (2 + d):nh * (3 + d), :]), jnp.add, reverse, 0.0)
                r = gt[nh * d:nh * (d + 1), :] - b
                mx = jnp.maximum(m, _lane_scan(r, jnp.maximum, reverse, -jnp.inf))
                mt = b + mx
                m_new = jnp.broadcast_to(mt[:, last:last + 1], (nh, chunk))
                b_last = jnp.broadcast_to(b[:, last:last + 1], (nh, chunk))
                rows_ref[bi, c, d, 0:nh, :] = r
                rows_ref[bi, c, d, nh:2 * nh, :] = jnp.exp(r + b_last - m_new)
                rows_ref[bi, c, d, 2 * nh:3 * nh, :] = jnp.exp(b_last + m - m_new)
                rows_ref[bi, c, d, 3 * nh:4 * nh, :] = m
                stack = []
                for x in (-mx, -mt):
                    stack += [t.astype(F32) for t in _split3(x)]
                stack.append(pad)
                cols_ref[bi, c, d] = jnp.concatenate(stack, axis=0).T.astype(BF16)
                m = m_new
            m_ref[bi, d] = m[:, 0:LANES]


def _ml_gate_call(gates, m0, *, batch, seq_len, chunk):
    nc = seq_len // chunk
    bb = max(1, 16 // nc)
    nh = ML_HEADS
    has_init = m0 is not None
    in_specs = [pl.BlockSpec((LANES, bb * seq_len), lambda i: (0, i))]
    args = [gates]
    if has_init:
        in_specs.append(pl.BlockSpec((bb, 2, nh, chunk), lambda i: (i, 0, 0, 0)))
        args.append(jnp.broadcast_to(m0[..., None], m0.shape + (chunk,)))
    return pl.pallas_call(
        functools.partial(_ml_gate_kernel, nc=nc, chunk=chunk, bb=bb, has_init=has_init),
        grid=(batch // bb,),
        in_specs=in_specs,
        out_specs=[pl.BlockSpec((bb, nc, 2, chunk, LANES), lambda i: (i, 0, 0, 0, 0)),
                   pl.BlockSpec((bb, nc, 2, 4 * nh, chunk), lambda i: (i, 0, 0, 0, 0)),
                   pl.BlockSpec((bb, 2, nh, LANES), lambda i: (i, 0, 0, 0))],
        out_shape=[jax.ShapeDtypeStruct((batch, nc, 2, chunk, LANES), BF16),
                   jax.ShapeDtypeStruct((batch, nc, 2, 4 * nh, chunk), F32),
                   jax.ShapeDtypeStruct((batch, 2, nh, LANES), F32)],
        compiler_params=_cparams("arbitrary"),
        name="ml_gate",
    )(*args)


def _ml_scan_call(q, kt, v, cols, rows, init, *, batch, seq_len, chunk):
    nc = seq_len // chunk
    nh = ML_HEADS
    q3 = q.reshape(batch, seq_len, ML_W)
    v3 = v.reshape(batch, seq_len, ML_W)
    sel = _bcast_selectors(chunk)
    has_init = init is not None

    fwd3 = lambda b, i: (b, i, 0)
    bwd3 = lambda b, i: (b, nc - 1 - i, 0)
    in_specs = [pl.BlockSpec((1, chunk, ML_W), fwd3),
                pl.BlockSpec((ML_W, chunk), lambda b, i: (0, b * nc + i)),
                pl.BlockSpec((1, chunk, ML_W), fwd3),
                pl.BlockSpec((1, chunk, ML_W), bwd3),
                pl.BlockSpec((ML_W, chunk), lambda b, i: (0, b * nc + nc - 1 - i)),
                pl.BlockSpec((1, chunk, ML_W), bwd3),
                pl.BlockSpec((1, 1, 1, chunk, LANES), lambda b, i: (b, i, 0, 0, 0)),
                pl.BlockSpec((1, 1, 1, chunk, LANES), lambda b, i: (b, nc - 1 - i, 1, 0, 0)),
                pl.BlockSpec((1, 1, 1, 4 * nh, chunk), lambda b, i: (b, i, 0, 0, 0)),
                pl.BlockSpec((1, 1, 1, 4 * nh, chunk), lambda b, i: (b, nc - 1 - i, 1, 0, 0)),
                pl.BlockSpec(sel.shape, lambda b, i: (0, 0, 0))]
    args = [q3, kt, v3, q3, kt, v3, cols, cols, rows, rows, sel]
    c_spec = pl.BlockSpec((1, 2, nh, ML_HEAD_DIM, ML_HEAD_DIM), lambda b, i: (b, 0, 0, 0, 0))
    n_spec = pl.BlockSpec((1, 2, nh, ML_HEAD_DIM), lambda b, i: (b, 0, 0, 0))
    if has_init:
        in_specs += [c_spec, n_spec]
        args += list(init)
    return pl.pallas_call(
        functools.partial(_ml_scan_kernel, chunk=chunk, has_init=has_init),
        grid=(batch, nc),
        in_specs=in_specs,
        out_specs=[pl.BlockSpec((1, chunk, ML_W), fwd3),
                   pl.BlockSpec((1, chunk, ML_W), bwd3),
                   c_spec, n_spec],
        out_shape=[jax.ShapeDtypeStruct((batch, seq_len, ML_W), BF16),
                   jax.ShapeDtypeStruct((batch, seq_len, ML_W), BF16),
                   jax.ShapeDtypeStruct((batch, 2, nh, ML_HEAD_DIM, ML_HEAD_DIM), F32),
                   jax.ShapeDtypeStruct((batch, 2, nh, ML_HEAD_DIM), F32)],
        scratch_shapes=[pltpu.VMEM((2, nh, ML_HEAD_DIM, 2 * ML_HEAD_DIM), F32),
                        pltpu.VMEM((2, nh, chunk, chunk), BF16),
                        pltpu.VMEM((2, nh, chunk, 2 * ML_HEAD_DIM), F32)],
        compiler_params=_cparams("arbitrary", "arbitrary"),
        name="ml_scan",
    )(*args)


def _ml_out_kernel(x_ref, hf_ref, hb_ref, og_ref, hn_ref, w_ref, mod_ref, gain_ref, o_ref):
    h = hf_ref[...].astype(F32) + hb_ref[...].astype(F32)
    parts = []
    for hd in range(ML_HEADS):
        sl = slice(hd * ML_HEAD_DIM, (hd + 1) * ML_HEAD_DIM)
        parts.append(_rms(h[:, sl], hn_ref[0:1, sl]))
    a = jnp.concatenate(parts, axis=1) * jax.nn.sigmoid(og_ref[...].astype(F32))
    out = _dot(a.astype(BF16), w_ref[...])
    o_ref[...] = x_ref[...] + mod_ref[0, 2:3, :] * _rms(out, gain_ref[1:2, :])


def _ml_out_call(x, hf, hb, og, head_norm, w, mod, gains, *, tm, rows_per_mod):
    rows = x.shape[0]
    return pl.pallas_call(
        _ml_out_kernel,
        grid=(rows // tm,),
        in_specs=[pl.BlockSpec((tm, D_MODEL), lambda i: (i, 0)),
                  pl.BlockSpec((tm, ML_W), lambda i: (i, 0)),
                  pl.BlockSpec((tm, ML_W), lambda i: (i, 0)),
                  pl.BlockSpec((tm, ML_W), lambda i: (i, 0)),
                  pl.BlockSpec((1, ML_W), lambda i: (0, 0)),
                  pl.BlockSpec((ML_W, D_MODEL), lambda i: (0, 0)),
                  pl.BlockSpec((1, 8, D_MODEL), lambda i: (i * tm // rows_per_mod, 0, 0)),
                  pl.BlockSpec((8, D_MODEL), lambda i: (0, 0))],
        out_specs=pl.BlockSpec((tm, D_MODEL), lambda i: (i, 0)),
        out_shape=jax.ShapeDtypeStruct((rows, D_MODEL), F32),
        compiler_params=_cparams("arbitrary"),
        name="ml_out",
    )(x, hf, hb, og, head_norm, w, mod, gains)


@functools.lru_cache(maxsize=None)
def _dft_tables_np(seq_len):
    n4 = 4 * seq_len
    ar = np.arange(seq_len, dtype=np.int64)
    idx = ((2 * ar + 1)[:, None] * ar[None, :]) % n4
    ang = (2.0 * np.pi / n4) * idx.astype(np.float64)
    cm = np.cos(ang).astype(np.float32)
    sm = np.sin(ang).astype(np.float32)
    return tuple(np.ascontiguousarray(a) for a in (cm, sm, cm.T, sm.T))


def _dft_tables(seq_len):
    return tuple(jnp.asarray(a).astype(BF16) for a in _dft_tables_np(seq_len))


def _hy_features(seq_len):
    t = np.linspace(0.0, 1.0, seq_len, dtype=np.float32)
    bands = np.arange(1, HY_BANDS + 1, dtype=np.float32)
    ang = (np.float32(2.0 * np.pi) * t[:, None]) * bands
    z = np.concatenate([t[:, None], np.cos(ang), np.sin(ang)], axis=-1).astype(np.float32)
    z = np.pad(z, ((0, 0), (0, 32 - HY_EMB)))
    deltas = np.abs(np.linspace(HY_MIN_DECAY, HY_MAX_DECAY, HY_CH, dtype=np.float32))[None, :]
    return jnp.asarray(z), jnp.asarray(deltas)


def _rope_tables(seq_len):
    rows = seq_len // GRID_W
    row = np.repeat(np.arange(rows, dtype=np.float32), GRID_W)
    col = np.tile(np.arange(GRID_W, dtype=np.float32), rows)
    n_freq = A_HEAD_DIM // 4
    inv = (np.float32(ROPE_THETA) ** (-np.arange(n_freq, dtype=np.float32) / n_freq)).astype(np.float32)
    ang = np.concatenate([row[:, None] * inv, col[:, None] * inv], axis=-1).astype(np.float32)
    cos, sin = np.cos(ang), np.sin(ang)
    return (jnp.asarray(np.concatenate([cos, cos], axis=-1), F32),
            jnp.asarray(np.concatenate([-sin, sin], axis=-1), F32))


def _pad_rows(a, rows=8):
    return jnp.pad(a, ((0, rows - a.shape[0]), (0, 0)))


def kernel(x_prompt, x_sample, cache_attn_k, cache_attn_v, state_mlstm_C, state_mlstm_n, state_mlstm_m, c, c_ctx, w_mod, b_mod, norm_mix_pre, norm_mix_post, norm_ffn_pre, norm_ffn_post, ffn_w_up, ffn_conv_w, ffn_conv_b, ffn_w_down, ah_w_in, ah_w_out, attn_q_norm, attn_k_norm, hy_conv_w, hy_conv_b, hy_w1, hy_b1, hy_w2, hy_b2, hy_w3, hy_b3, hy_sin_freq, hy_skip, ml_w_in, ml_b_gates, ml_conv_w, ml_conv_b, ml_head_norm, ml_w_out):
    bp, lp, _ = x_prompt.shape
    bs, ls, _ = x_sample.shape
    past = cache_attn_k.shape[2]
    xp = x_prompt.reshape(bp * lp, D_MODEL)
    xs = x_sample.reshape(bs * ls, D_MODEL)
    groups = {
        "p": dict(batch=bp, seq_len=lp, tm=1024, rows_per_mod=bp * lp, ffn_tm=512, ffn_tf=D_FF // 2, ml_cw=512,
                  seq_bb=4, hy_tc=HY_CH),
        "s": dict(batch=bs, seq_len=ls, tm=ls, rows_per_mod=ls, ffn_tm=512, ffn_tf=D_FF // 2, ml_cw=COL_TILE,
                  seq_bb=1, hy_tc=COL_TILE),
    }
    ffn_up16 = ffn_w_up.astype(BF16)
    ffn_down16 = ffn_w_down.astype(BF16)

    cc = jnp.concatenate([c, c_ctx[None, :], jnp.zeros((8 - bs - 1, D_MODEL), F32)], axis=0)
    mod_all = _mod_call(cc, w_mod, b_mod).reshape(DEPTH, 8, 6, D_MODEL)
    mod_all = jnp.pad(mod_all, ((0, 0), (0, 0), (0, 2), (0, 0)))
    rope = _rope_tables(ls)

    new_k = new_v = new_c = new_n = new_m = None
    for l in range(DEPTH):
        j = l // 2
        mods = {"s": mod_all[l, :bs], "p": mod_all[l, bs:bs + 1]}
        gains = _pad_rows(jnp.stack([norm_mix_pre[l], norm_mix_post[l], norm_ffn_pre[l], norm_ffn_post[l]]))
        xin = {"p": xp, "s": xs}
        xmid = {}
        if l % 2 == 0:
            w_in = ah_w_in[j].astype(BF16)
            w_out = ah_w_out[j].astype(BF16)
            qkn = _pad_rows(jnp.stack([attn_q_norm[j], attn_k_norm[j]]))
            conv = jnp.pad(_pad_rows(jnp.concatenate([hy_conv_w[j], hy_conv_b[j][None, :]], axis=0)),
                           ((0, 0), (A_Q + 2 * A_KV, 0)))
            w1 = jnp.pad(hy_w1[j], ((0, 32 - HY_EMB), (0, 0)))
            sf = _pad_rows(hy_sin_freq[j])
            for g, cfg in groups.items():
                batch, seq_len, tm, rpm = cfg["batch"], cfg["seq_len"], cfg["tm"], cfg["rows_per_mod"]
                outs = _ah_inproj_call(xin[g], mods[g], gains, w_in, qkn, conv, rope if g == "s" else None,
                                       tm=tm, seq_len=seq_len, rows_per_mod=rpm)
                q, k, v, u = outs[:4]
                if g == "s":
                    ctx_k = cache_attn_k[:, j].reshape(bs, past, A_KV).astype(BF16)
                    ctx_v = cache_attn_v[:, j].reshape(bs, past, A_KV).astype(BF16)
                else:
                    ctx_k = ctx_v = None
                    new_k = outs[4].reshape(bp, 1, lp, A_KV_HEADS, A_HEAD_DIM)
                    new_v = outs[5].reshape(bp, 1, lp, A_KV_HEADS, A_HEAD_DIM)
                attn = _attn_call(q, k, v, ctx_k, ctx_v, batch=batch, seq_len=seq_len, tq=256, bb=cfg["seq_bb"])
                z, deltas = _hy_features(seq_len)
                dft = _dft_tables(seq_len)
                tk = 256
                gr, gi = _hy_filter_call(z, w1, hy_b1[j][None, :], hy_w2[j], hy_b2[j][None, :], hy_w3[j],
                                         hy_b3[j][None, :], sf, deltas, dft[0], dft[1], seq_len=seq_len, tk=tk)
                hyo = _hy_conv_call(u, hy_skip[j][None, :], dft, gr, gi, batch=batch, seq_len=seq_len,
                                    tc=cfg["hy_tc"], tk=tk, bb=cfg["seq_bb"])
                xmid[g] = _ah_out_call(xin[g], attn.reshape(batch * seq_len, A_Q),
                                       hyo.reshape(batch * seq_len, HY_CH), w_out, mods[g], gains,
                                       tm=512, rows_per_mod=rpm)
        else:
            w_in = ml_w_in[j].astype(BF16)
            w_g = jnp.pad(ml_w_in[j][:, 4 * ML_W:], ((0, 0), (0, LANES - 4 * ML_HEADS))).astype(BF16)
            b_g = jnp.pad(ml_b_gates[j], (0, LANES - 4 * ML_HEADS))[None, :]
            w_out = ml_w_out[j].astype(BF16)
            conv = jnp.pad(_pad_rows(jnp.concatenate([ml_conv_w[j], ml_conv_b[j][None, :]], axis=0)),
                           ((0, 0), (0, 2 * ML_W)))
            for g, cfg in groups.items():
                batch, seq_len, tm, rpm = cfg["batch"], cfg["seq_len"], cfg["tm"], cfg["rows_per_mod"]
                q, kt, v, og, gates = _ml_inproj_call(xin[g], mods[g], gains, w_in, w_g, b_g, conv, tm=tm,
                                                      cw=cfg["ml_cw"], seq_len=seq_len, rows_per_mod=rpm)
                if g == "s":
                    init, m0 = (state_mlstm_C[:, j], state_mlstm_n[:, j]), state_mlstm_m[:, j]
                else:
                    init = m0 = None
                cols, rows, m_new = _ml_gate_call(gates, m0, batch=batch, seq_len=seq_len, chunk=ML_SCAN_CHUNK)
                hf, hb, c_new, n_new = _ml_scan_call(q, kt, v, cols, rows, init, batch=batch, seq_len=seq_len,
                                                     chunk=ML_SCAN_CHUNK)
                if g == "p":
                    new_c, new_n, new_m = c_new[:, None], n_new[:, None], m_new[:, None, :, :, 0]
                xmid[g] = _ml_out_call(xin[g], hf.reshape(batch * seq_len, ML_W), hb.reshape(batch * seq_len, ML_W),
                                       og, ml_head_norm[j][None, :], w_out, mods[g], gains,
                                       tm=512, rows_per_mod=rpm)
        conv = _pad_rows(jnp.concatenate([ffn_conv_w[l], ffn_conv_b[l][None, :]], axis=0))
        xout = {}
        for g, cfg in groups.items():
            xout[g] = _ffn_call(xmid[g], mods[g], gains, ffn_up16, conv, ffn_down16, l, tm=cfg["ffn_tm"],
                                tf=cfg["ffn_tf"], seq_len=cfg["seq_len"], rows_per_mod=cfg["rows_per_mod"])
        xp, xs = xout["p"], xout["s"]

    return (xp.reshape(bp, lp, D_MODEL), xs.reshape(bs, ls, D_MODEL), new_k, new_v, new_c, new_n, new_m)
```

```python
import functools
import math

import numpy as np
import jax
import jax.numpy as jnp
from jax import lax
from jax.experimental import pallas as pl
from jax.experimental.pallas import tpu as pltpu

F32 = jnp.float32
BF16 = jnp.bfloat16

D_MODEL = 1024
DEPTH = 2
GRID_W = 64
A_HEADS = 4
A_KV_HEADS = 2
A_HEAD_DIM = 128
A_Q = A_HEADS * A_HEAD_DIM
A_KV = A_KV_HEADS * A_HEAD_DIM
ROPE_THETA = 10000.0
HY_CH = D_MODEL // 2
HY_BANDS = 8
HY_EMB = 1 + 2 * HY_BANDS
HY_W = 64
HY_TARGET = 1e-2
HY_FAST_PCT = 0.3
HY_SLOW_PCT = 1.5
HY_MAX_DECAY = math.log(HY_TARGET) / HY_FAST_PCT
HY_MIN_DECAY = math.log(HY_TARGET) / HY_SLOW_PCT
AH_IN = A_Q + 2 * A_KV + 3 * HY_CH
ML_HEADS = 8
ML_HEAD_DIM = D_MODEL // ML_HEADS
ML_W = ML_HEADS * ML_HEAD_DIM
D_FF = 2816
NORM_EPS = 1e-6
NEG_BIG = -1e30

LANES = 128
VMEM_LIMIT = 56 * 1024 * 1024
COL_TILE = 256
ML_SCAN_CHUNK = 128
INV_SQRT2 = 1.0 / math.sqrt(2.0)


def _cparams(*sem):
    return pltpu.CompilerParams(dimension_semantics=sem, vmem_limit_bytes=VMEM_LIMIT)


def _rms(x, gain):
    return x * lax.rsqrt(jnp.mean(x * x, axis=-1, keepdims=True) + NORM_EPS) * gain


def _modulated(x, gain, shift, scale):
    return _rms(x, gain) * (1.0 + scale) + shift


def _dot(a, b):
    return jnp.dot(a, b, preferred_element_type=F32)


def _dot_nt(a, b):
    return lax.dot_general(a, b, (((1,), (1,)), ((), ())), preferred_element_type=F32)


def _dot_tn(a, b):
    return lax.dot_general(a, b, (((0,), (0,)), ((), ())), preferred_element_type=F32)


def _split3(x):
    hi = x.astype(BF16)
    r1 = x - hi.astype(F32)
    mid = r1.astype(BF16)
    lo = (r1 - mid.astype(F32)).astype(BF16)
    return hi, mid, lo


def _dot_f32(a, b):
    a0, a1, a2 = _split3(a)
    b0, b1, b2 = _split3(b)
    return (_dot(a0, b0) + (_dot(a0, b1) + _dot(a1, b0))
            + (_dot(a0, b2) + _dot(a1, b1) + _dot(a2, b0)))


def _dwconv3(y, conv_ref, seq_len, edge=None):
    rows, cols = y.shape
    pos = lax.broadcasted_iota(jnp.int32, (rows, LANES), 0)
    if edge is None:
        pos = pos % seq_len
        first = pos == 0
        last = pos == seq_len - 1
    else:
        first = pos == 0
        last = pos == rows - 1
    outs = []
    for c0 in range(0, cols, LANES):
        yc = y[:, c0:c0 + LANES]
        before = 0.0 if edge is None else edge[0][:, c0:c0 + LANES]
        after = 0.0 if edge is None else edge[1][:, c0:c0 + LANES]
        prev = jnp.where(first, before, pltpu.roll(yc, 1, 0))
        nxt = jnp.where(last, after, pltpu.roll(yc, rows - 1, 0))
        outs.append(prev * conv_ref[0:1, c0:c0 + LANES] + yc * conv_ref[1:2, c0:c0 + LANES]
                    + nxt * conv_ref[2:3, c0:c0 + LANES] + conv_ref[3:4, c0:c0 + LANES])
    return outs[0] if len(outs) == 1 else jnp.concatenate(outs, axis=1)


def _mod_kernel(c_ref, w_ref, b_ref, o_ref):
    a = c_ref[...]
    a = a * jax.nn.sigmoid(a)
    o_ref[0] = _dot(a.astype(BF16), w_ref[0].astype(BF16)) + b_ref[0]


def _mod_call(cc, w_mod, b_mod):
    tn = 1536
    n = 6 * D_MODEL
    return pl.pallas_call(
        _mod_kernel,
        grid=(DEPTH, n // tn),
        in_specs=[pl.BlockSpec((8, D_MODEL), lambda l, j: (0, 0)),
                  pl.BlockSpec((1, D_MODEL, tn), lambda l, j: (l, 0, j)),
                  pl.BlockSpec((1, 1, tn), lambda l, j: (l, 0, j))],
        out_specs=pl.BlockSpec((1, 8, tn), lambda l, j: (l, 0, j)),
        out_shape=jax.ShapeDtypeStruct((DEPTH, 8, n), F32),
        compiler_params=_cparams("arbitrary", "arbitrary"),
        name="mod",
    )(cc, w_mod, b_mod.reshape(DEPTH, 1, n))


def _ah_inproj_kernel(*refs, seq_len, rope, halo):
    it = iter(refs)
    x_ref = next(it)
    xb_ref, xa_ref = (next(it), next(it)) if halo else (None, None)
    mod_ref, gain_ref, w_ref, qkn_ref, conv_ref = (next(it) for _ in range(5))
    cc_ref, ss_ref = (next(it), next(it)) if rope else (None, None)
    q_ref, k_ref, v_ref, u_ref = (next(it) for _ in range(4))
    kf_ref, vf_ref = (None, None) if rope else (next(it), next(it))
    tm = x_ref.shape[0]
    dh = A_HEAD_DIM

    def modulated(x):
        return _modulated(x, gain_ref[0:1, :], mod_ref[0, 0:1, :], mod_ref[0, 1:2, :]).astype(BF16)

    def head(yh, g):
        yh = _rms(yh, g)
        if rope:
            yh = yh * cc_ref[...] + pltpu.roll(yh, dh // 2, 1) * ss_ref[...]
        return yh

    xn = modulated(x_ref[...])
    yq = _dot(xn, w_ref[:, 0:A_Q])
    for h in range(A_HEADS):
        sl = slice(h * dh, (h + 1) * dh)
        q_ref[:, sl] = (head(yq[:, sl], qkn_ref[0:1, :]) * (dh ** -0.5)).astype(BF16)
    ykv = _dot(xn, w_ref[:, A_Q:A_Q + 2 * A_KV])
    for h in range(A_KV_HEADS):
        sl = slice(h * dh, (h + 1) * dh)
        kh = head(ykv[:, sl], qkn_ref[1:2, :])
        k_ref[:, sl] = kh.astype(BF16)
        if kf_ref is not None:
            kf_ref[:, sl] = kh
    yv = ykv[:, A_KV:2 * A_KV]
    v_ref[...] = yv.astype(BF16)
    if vf_ref is not None:
        vf_ref[...] = yv

    xe = xn
    if halo:
        xe = jnp.concatenate([xn, modulated(jnp.concatenate([xb_ref[...], xa_ref[...]], axis=0))], axis=0)
    cw = 512
    u0 = A_Q + 2 * A_KV
    for t in range(3 * HY_CH // cw):
        cols = slice(u0 + t * cw, u0 + (t + 1) * cw)
        y = _dot(xe, w_ref[:, cols])
        edge = None
        if halo:
            y, edge = _halo_edges(y, tm, seq_len)
        u_ref[:, t * cw:(t + 1) * cw] = _dwconv3(y, conv_ref.at[:, cols], seq_len, edge).astype(BF16)


def _ah_inproj_call(x, mod, gains, w, qkn, conv, rope_tabs, *, tm, seq_len, rows_per_mod):
    rows = x.shape[0]
    rope = rope_tabs is not None
    halo = tm < seq_len
    assert (seq_len % tm == 0) if halo else (tm % seq_len == 0)
    row_blk = lambda cols: pl.BlockSpec((tm, cols), lambda i: (i, 0))
    out_specs = [row_blk(A_Q), row_blk(A_KV), row_blk(A_KV), row_blk(3 * HY_CH)]
    out_shape = [jax.ShapeDtypeStruct((rows, A_Q), BF16), jax.ShapeDtypeStruct((rows, A_KV), BF16),
                 jax.ShapeDtypeStruct((rows, A_KV), BF16), jax.ShapeDtypeStruct((rows, 3 * HY_CH), BF16)]
    if not rope:
        out_specs += [row_blk(A_KV)] * 2
        out_shape += [jax.ShapeDtypeStruct((rows, A_KV), F32)] * 2
    in_specs = [row_blk(D_MODEL)] + (_halo_specs(tm, rows) if halo else []) + [
        pl.BlockSpec((1, 8, D_MODEL), lambda i: (i * tm // rows_per_mod, 0, 0)),
        pl.BlockSpec((8, D_MODEL), lambda i: (0, 0)),
        pl.BlockSpec((D_MODEL, AH_IN), lambda i: (0, 0)),
        pl.BlockSpec((8, A_HEAD_DIM), lambda i: (0, 0)),
        pl.BlockSpec((8, AH_IN), lambda i: (0, 0))]
    args = ([x, x, x] if halo else [x]) + [mod, gains, w, qkn, conv]
    if rope:
        per_seq = max(seq_len // tm, 1)
        in_specs += [pl.BlockSpec((tm, A_HEAD_DIM), lambda i: (i % per_seq, 0))] * 2
        args += list(rope_tabs)
    return pl.pallas_call(
        functools.partial(_ah_inproj_kernel, seq_len=seq_len, rope=rope, halo=halo),
        grid=(rows // tm,),
        in_specs=in_specs,
        out_specs=out_specs,
        out_shape=out_shape,
        compiler_params=_cparams("arbitrary"),
        name="ah_inproj",
    )(*args)


def _attn_kernel(*refs, ctx, bb):
    if ctx:
        q_ref, k_ref, v_ref, kc_ref, vc_ref, o_ref = refs
    else:
        q_ref, k_ref, v_ref, o_ref = refs
    dh = A_HEAD_DIM
    for bi in range(bb):
        k = k_ref[bi]
        if ctx:
            kc = kc_ref[bi]
        else:
            vaug = jnp.concatenate([v_ref[bi], jnp.ones((k.shape[0], dh), BF16)], axis=1)
        for g in range(A_HEADS // A_KV_HEADS):
            sl = slice(g * dh, (g + 1) * dh)
            q = q_ref[bi, :, sl]
            s = _dot_nt(q, k)
            m = jnp.max(s, axis=-1, keepdims=True)
            if ctx:
                sc = _dot_nt(q, kc)
                m = jnp.maximum(m, jnp.max(sc, axis=-1, keepdims=True))
            if ctx:
                p = jnp.exp(s - m)
                pc = jnp.exp(sc - m)
                den = jnp.sum(p, axis=-1, keepdims=True) + jnp.sum(pc, axis=-1, keepdims=True)
                o = _dot(p.astype(BF16), v_ref[bi]) + _dot(pc.astype(BF16), vc_ref[bi])
                o_ref[bi, :, sl] = (o / den).astype(BF16)
            else:
                o = _dot(jnp.exp(s - m).astype(BF16), vaug)
                o_ref[bi, :, sl] = (o[:, :dh] / o[:, dh:]).astype(BF16)


def _attn_call(q, k, v, ctx_k, ctx_v, *, batch, seq_len, tq, bb):
    ctx = ctx_k is not None
    qw = A_Q // A_KV_HEADS
    in_specs = [pl.BlockSpec((bb, tq, qw), lambda b, h, i: (b, i, h)),
                pl.BlockSpec((bb, seq_len, A_HEAD_DIM), lambda b, h, i: (b, 0, h)),
                pl.BlockSpec((bb, seq_len, A_HEAD_DIM), lambda b, h, i: (b, 0, h))]
    args = [q.reshape(batch, seq_len, A_Q), k.reshape(batch, seq_len, A_KV), v.reshape(batch, seq_len, A_KV)]
    if ctx:
        past = ctx_k.shape[1]
        in_specs += [pl.BlockSpec((bb, past, A_HEAD_DIM), lambda b, h, i: (b, 0, h))] * 2
        args += [ctx_k, ctx_v]
    return pl.pallas_call(
        functools.partial(_attn_kernel, ctx=ctx, bb=bb),
        grid=(batch // bb, A_KV_HEADS, seq_len // tq),
        in_specs=in_specs,
        out_specs=pl.BlockSpec((bb, tq, qw), lambda b, h, i: (b, i, h)),
        out_shape=jax.ShapeDtypeStruct((batch, seq_len, A_Q), BF16),
        compiler_params=_cparams("arbitrary", "arbitrary", "arbitrary"),
        name="attn",
    )(*args)


def _hy_filter_kernel(z_ref, w1_ref, b1_ref, w2_ref, b2_ref, w3_ref, b3_ref, sf_ref, dl_ref,
                      cm_ref, sm_ref, gr_ref, gi_ref, hs_ref, hd_ref):
    @pl.when(pl.program_id(0) == 0)
    def _():
        z = z_ref[...]
        h = jnp.sin(sf_ref[0:1, :] * (_dot_f32(z, w1_ref[...]) + b1_ref[...]))
        h = jnp.sin(sf_ref[1:2, :] * (_dot_f32(h, w2_ref[...]) + b2_ref[...]))
        filt = _dot_f32(h, w3_ref[...]) + b3_ref[...]
        win = jnp.exp(-z[:, 0:1] * dl_ref[...])
        hf = filt[:, :HY_CH] * win
        hb = filt[:, HY_CH:] * win
        hs_ref[...] = (hf + hb).astype(BF16)
        hd_ref[...] = (hb - hf).astype(BF16)

    gr_ref[...] = _dot(cm_ref[...], hs_ref[...])
    gi_ref[...] = _dot(sm_ref[...], hd_ref[...])


def _hy_filter_call(z, w1, b1, w2, b2, w3, b3, sf, deltas, cm, sm, *, seq_len, tk):
    full = lambda a: pl.BlockSpec(a.shape, lambda k: (0,) * a.ndim)
    small = [z, w1, b1, w2, b2, w3, b3, sf, deltas]
    return pl.pallas_call(
        _hy_filter_kernel,
        grid=(seq_len // tk,),
        in_specs=[full(a) for a in small] + [pl.BlockSpec((tk, seq_len), lambda k: (k, 0))] * 2,
        out_specs=[pl.BlockSpec((tk, HY_CH), lambda k: (k, 0))] * 2,
        out_shape=[jax.ShapeDtypeStruct((seq_len, HY_CH), F32)] * 2,
        scratch_shapes=[pltpu.VMEM((seq_len, HY_CH), BF16)] * 2,
        compiler_params=_cparams("arbitrary"),
        name="hy_filter",
    )(*small, cm, sm)


def _hy_conv_kernel(x0_ref, x1_ref, v_ref, skip_ref, cm_ref, sm_ref, cmt_ref, smt_ref, gr_ref, gi_ref,
                    o_ref, vv_ref, acc_ref, *, seq_len, bb):
    kb = pl.program_id(2)
    single = seq_len == cm_ref.shape[0]
    gr = gr_ref[...]
    gi = gi_ref[...]
    for bi in range(bb):
        def gated():
            return v_ref[bi].astype(F32) * x1_ref[bi].astype(F32)

        if single:
            vv = gated().astype(BF16)
        else:
            @pl.when(kb == 0)
            def _():
                vv_ref[bi] = gated().astype(BF16)
                acc_ref[bi] = jnp.zeros_like(acc_ref[bi])

            vv = vv_ref[bi]
        vr = _dot(cm_ref[...], vv)
        wi = _dot(sm_ref[...], vv)
        pr = gr * vr + gi * wi
        qi = gr * wi - gi * vr
        y = _dot(cmt_ref[...], pr.astype(BF16)) + _dot(smt_ref[...], qi.astype(BF16))

        def finish(total):
            o_ref[bi] = ((total * (1.0 / seq_len) + skip_ref[...] * gated())
                         * x0_ref[bi].astype(F32)).astype(BF16)

        if single:
            finish(y)
        else:
            acc_ref[bi] += y

            @pl.when(kb == pl.num_programs(2) - 1)
            def _():
                finish(acc_ref[bi])


def _hy_conv_call(u, skip, dft, gr, gi, *, batch, seq_len, tc, tk, bb):
    proj3 = u.reshape(batch, seq_len, 3 * HY_CH)
    cm, sm, cmt, smt = dft
    nch = HY_CH // tc
    return pl.pallas_call(
        functools.partial(_hy_conv_kernel, seq_len=seq_len, bb=bb),
        grid=(batch // bb, nch, seq_len // tk),
        in_specs=[pl.BlockSpec((bb, seq_len, tc), lambda b, c, k: (b, 0, c)),
                  pl.BlockSpec((bb, seq_len, tc), lambda b, c, k: (b, 0, nch + c)),
                  pl.BlockSpec((bb, seq_len, tc), lambda b, c, k: (b, 0, 2 * nch + c)),
                  pl.BlockSpec((1, tc), lambda b, c, k: (0, c)),
                  pl.BlockSpec((tk, seq_len), lambda b, c, k: (k, 0)),
                  pl.BlockSpec((tk, seq_len), lambda b, c, k: (k, 0)),
                  pl.BlockSpec((seq_len, tk), lambda b, c, k: (0, k)),
                  pl.BlockSpec((seq_len, tk), lambda b, c, k: (0, k)),
                  pl.BlockSpec((tk, tc), lambda b, c, k: (k, c)),
                  pl.BlockSpec((tk, tc), lambda b, c, k: (k, c))],
        out_specs=pl.BlockSpec((bb, seq_len, tc), lambda b, c, k: (b, 0, c)),
        out_shape=jax.ShapeDtypeStruct((batch, seq_len, HY_CH), BF16),
        scratch_shapes=[pltpu.VMEM((bb, seq_len, tc), BF16), pltpu.VMEM((bb, seq_len, tc), F32)],
        compiler_params=_cparams("arbitrary", "arbitrary", "arbitrary"),
        name="hy_conv",
    )(proj3, proj3, proj3, skip, cm, sm, cmt, smt, gr, gi)


def _ah_out_kernel(x_ref, a_ref, h_ref, w_ref, mod_ref, gain_ref, o_ref):
    out = _dot(a_ref[...], w_ref[0:A_Q, :]) + _dot(h_ref[...], w_ref[A_Q:, :])
    o_ref[...] = x_ref[...] + mod_ref[0, 2:3, :] * _rms(out, gain_ref[1:2, :])


def _ah_out_call(x, attn, hyo, w, mod, gains, *, tm, rows_per_mod):
    rows = x.shape[0]
    return pl.pallas_call(
        _ah_out_kernel,
        grid=(rows // tm,),
        in_specs=[pl.BlockSpec((tm, D_MODEL), lambda i: (i, 0)),
                  pl.BlockSpec((tm, A_Q), lambda i: (i, 0)),
                  pl.BlockSpec((tm, HY_CH), lambda i: (i, 0)),
                  pl.BlockSpec((A_Q + HY_CH, D_MODEL), lambda i: (0, 0)),
                  pl.BlockSpec((1, 8, D_MODEL), lambda i: (i * tm // rows_per_mod, 0, 0)),
                  pl.BlockSpec((8, D_MODEL), lambda i: (0, 0))],
        out_specs=pl.BlockSpec((tm, D_MODEL), lambda i: (i, 0)),
        out_shape=jax.ShapeDtypeStruct((rows, D_MODEL), F32),
        compiler_params=_cparams("arbitrary"),
        name="ah_out",
    )(x, attn, hyo, w, mod, gains)


def _ffn_kernel(*refs, seq_len, halo):
    if halo:
        x_ref, xb_ref, xa_ref, mod_ref, gain_ref, wg_ref, wl_ref, cg_ref, cl_ref, wd_ref, o_ref, xn_ref = refs
    else:
        x_ref, mod_ref, gain_ref, wg_ref, wl_ref, cg_ref, cl_ref, wd_ref, o_ref, xn_ref = refs
    i = pl.program_id(0)
    j = pl.program_id(1)
    tm = x_ref.shape[0]

    def modulated(x):
        return _modulated(x, gain_ref[2:3, :], mod_ref[0, 3:4, :], mod_ref[0, 4:5, :]).astype(BF16)

    @pl.when(j == 0)
    def _():
        xn_ref[0:tm, :] = modulated(x_ref[...])
        o_ref[...] = jnp.zeros_like(o_ref)
        if halo:
            xn_ref[tm:tm + 16, :] = modulated(jnp.concatenate([xb_ref[...], xa_ref[...]], axis=0))

    xn = xn_ref[...]
    yg = _dot(xn, wg_ref[0])
    yl = _dot(xn, wl_ref[0])
    edge_g = edge_l = None
    if halo:
        yg, edge_g = _halo_edges(yg, tm, seq_len)
        yl, edge_l = _halo_edges(yl, tm, seq_len)
    hg = _dwconv3(yg, cg_ref, seq_len, edge_g)
    hl = _dwconv3(yl, cl_ref, seq_len, edge_l)
    act = (0.5 * hg * (1.0 + lax.erf(hg * INV_SQRT2))) * hl
    o_ref[...] += _dot(act.astype(BF16), wd_ref[0])

    @pl.when(j == pl.num_programs(1) - 1)
    def _():
        o_ref[...] = x_ref[...] + mod_ref[0, 5:6, :] * _rms(o_ref[...], gain_ref[3:4, :])


def _ffn_call(x, mod, gains, w_up, conv, w_down, layer, *, tm, tf, seq_len, rows_per_mod):
    rows = x.shape[0]
    nj = D_FF // tf
    halo = tm < seq_len
    assert (seq_len % tm == 0) if halo else (tm % seq_len == 0)
    x_specs = [pl.BlockSpec((tm, D_MODEL), lambda i, j: (i, 0))] + (_halo_specs(tm, rows) if halo else [])
    x_args = [x, x, x] if halo else [x]
    scratch = [pltpu.VMEM((tm + 16 if halo else tm, D_MODEL), BF16)]
    return pl.pallas_call(
        functools.partial(_ffn_kernel, seq_len=seq_len, halo=halo),
        grid=(rows // tm, nj),
        in_specs=x_specs + [
            pl.BlockSpec((1, 8, D_MODEL), lambda i, j: (i * tm // rows_per_mod, 0, 0)),
            pl.BlockSpec((8, D_MODEL), lambda i, j: (0, 0)),
            pl.BlockSpec((1, D_MODEL, tf), lambda i, j: (layer, 0, j)),
            pl.BlockSpec((1, D_MODEL, tf), lambda i, j: (layer, 0, nj + j)),
            pl.BlockSpec((8, tf), lambda i, j: (0, j)),
            pl.BlockSpec((8, tf), lambda i, j: (0, nj + j)),
            pl.BlockSpec((1, tf, D_MODEL), lambda i, j: (layer, j, 0))],
        out_specs=pl.BlockSpec((tm, D_MODEL), lambda i, j: (i, 0)),
        out_shape=jax.ShapeDtypeStruct((rows, D_MODEL), F32),
        scratch_shapes=scratch,
        compiler_params=_cparams("arbitrary", "arbitrary"),
        name="ffn",
    )(*x_args, mod, gains, w_up, w_up, conv, conv, w_down)


def _halo_specs(tm, rows):
    sub = 8
    per = tm // sub
    return [pl.BlockSpec((sub, D_MODEL), lambda i, *_: (jnp.maximum(i * per - 1, 0), 0)),
            pl.BlockSpec((sub, D_MODEL), lambda i, *_: (jnp.minimum((i + 1) * per, rows // sub - 1), 0))]


def _halo_edges(y, tm, seq_len):
    i = pl.program_id(0)
    at_start = (i * tm) % seq_len == 0
    at_end = ((i + 1) * tm) % seq_len == 0
    edge = (jnp.where(at_start, 0.0, y[tm + 7:tm + 8, :]), jnp.where(at_end, 0.0, y[tm + 8:tm + 9, :]))
    return y[0:tm, :], edge


def _ml_inproj_kernel(*refs, seq_len, cw, halo):
    if halo:
        x_ref, xb_ref, xa_ref, mod_ref, gain_ref, w_ref, wg_ref, bg_ref, conv_ref = refs[:9]
    else:
        x_ref, mod_ref, gain_ref, w_ref, wg_ref, bg_ref, conv_ref = refs[:7]
    q_ref, kt_ref, v_ref, o_ref, g_ref = refs[-5:]
    tm = x_ref.shape[0]

    def modulated(x):
        return _modulated(x, gain_ref[0:1, :], mod_ref[0, 0:1, :], mod_ref[0, 1:2, :]).astype(BF16)

    xn = modulated(x_ref[...])
    g_ref[...] = (_dot(xn, wg_ref[...]) + bg_ref[...]).T
    xe = xn
    if halo:
        xe = jnp.concatenate([xn, modulated(jnp.concatenate([xb_ref[...], xa_ref[...]], axis=0))], axis=0)

    for t in range(2 * ML_W // cw):
        cols = slice(t * cw, (t + 1) * cw)
        y = _dot(xe, w_ref[:, cols])
        edge = None
        if halo:
            y, edge = _halo_edges(y, tm, seq_len)
        z = _dwconv3(y, conv_ref.at[:, cols], seq_len, edge)
        z = z * jax.nn.sigmoid(z)
        if t * cw < ML_W:
            q_ref[:, cols] = z.astype(BF16)
        else:
            kt_ref[t * cw - ML_W:(t + 1) * cw - ML_W, :] = (z * (ML_HEAD_DIM ** -0.5)).T.astype(BF16)
    for t in range(2 * ML_W // cw):
        y = _dot(xn, w_ref[:, 2 * ML_W + t * cw:2 * ML_W + (t + 1) * cw]).astype(BF16)
        if t * cw < ML_W:
            v_ref[:, t * cw:(t + 1) * cw] = y
        else:
            o_ref[:, t * cw - ML_W:(t + 1) * cw - ML_W] = y


def _ml_inproj_call(x, mod, gains, w, wg, bg, conv, *, tm, cw, seq_len, rows_per_mod):
    rows = x.shape[0]
    n = 4 * ML_W
    halo = tm < seq_len
    assert (seq_len % tm == 0) if halo else (tm % seq_len == 0)
    x_specs = [pl.BlockSpec((tm, D_MODEL), lambda i: (i, 0))] + (_halo_specs(tm, rows) if halo else [])
    return pl.pallas_call(
        functools.partial(_ml_inproj_kernel, seq_len=seq_len, cw=cw, halo=halo),
        grid=(rows // tm,),
        in_specs=x_specs + [
            pl.BlockSpec((1, 8, D_MODEL), lambda i: (i * tm // rows_per_mod, 0, 0)),
            pl.BlockSpec((8, D_MODEL), lambda i: (0, 0)),
            pl.BlockSpec((D_MODEL, n), lambda i: (0, 0)),
            pl.BlockSpec((D_MODEL, LANES), lambda i: (0, 0)),
            pl.BlockSpec((1, LANES), lambda i: (0, 0)),
            pl.BlockSpec((8, n), lambda i: (0, 0))],
        out_specs=[pl.BlockSpec((tm, ML_W), lambda i: (i, 0)),
                   pl.BlockSpec((ML_W, tm), lambda i: (0, i)),
                   pl.BlockSpec((tm, ML_W), lambda i: (i, 0)),
                   pl.BlockSpec((tm, ML_W), lambda i: (i, 0)),
                   pl.BlockSpec((LANES, tm), lambda i: (0, i))],
        out_shape=[jax.ShapeDtypeStruct((rows, ML_W), BF16), jax.ShapeDtypeStruct((ML_W, rows), BF16),
                   jax.ShapeDtypeStruct((rows, ML_W), BF16), jax.ShapeDtypeStruct((rows, ML_W), BF16),
                   jax.ShapeDtypeStruct((LANES, rows), F32)],
        compiler_params=_cparams("arbitrary"),
        name="ml_inproj",
    )(*([x, x, x] if halo else [x]), mod, gains, w, wg, bg, conv)


def _log_sigmoid(x):
    return jnp.minimum(x, 0.0) - jnp.log1p(jnp.exp(-jnp.abs(x)))


def _lane_scan(x, op, reverse, fill):
    n = x.shape[1]
    lane = lax.broadcasted_iota(jnp.int32, x.shape, 1)
    sh = 1
    while sh < n:
        if reverse:
            x = op(x, jnp.where(lane < n - sh, pltpu.roll(x, n - sh, 1), fill))
        else:
            x = op(x, jnp.where(lane >= sh, pltpu.roll(x, sh, 1), fill))
        sh *= 2
    return x


def _bcast_selectors(chunk):
    sel = np.zeros((ML_HEADS, LANES, chunk + LANES), np.float32)
    for h in range(ML_HEADS):
        for g in range(3):
            sel[h, g * ML_HEADS + h, :chunk] = 1.0
            sel[h, (3 + g) * ML_HEADS + h, chunk:] = 1.0
    return jnp.asarray(sel, BF16)


def _ml_scan_kernel(*refs, chunk, has_init):
    if has_init:
        (qf, ktf, vf, qb, ktb, vb, colf, colb, rowf, rowb, sel_ref, c0_ref, n0_ref,
         hf_ref, hb_ref, c_ref, n_ref, s_sc, p16_sc, e_sc) = refs
    else:
        (qf, ktf, vf, qb, ktb, vb, colf, colb, rowf, rowb, sel_ref,
         hf_ref, hb_ref, c_ref, n_ref, s_sc, p16_sc, e_sc) = refs
    nh, dh = ML_HEADS, ML_HEAD_DIM
    step = pl.program_id(1)

    @pl.when(step == 0)
    def _():
        if has_init:
            for d in range(2):
                for h in range(nh):
                    s_sc[d, h, :, 0:dh] = c0_ref[0, d, h].T
                    s_sc[d, h, :, dh:2 * dh] = jnp.broadcast_to(n0_ref[0, d, h:h + 1, :], (dh, dh)).T
        else:
            s_sc[...] = jnp.zeros_like(s_sc)

    row = lax.broadcasted_iota(jnp.int32, (chunk, chunk), 0)
    col = lax.broadcasted_iota(jnp.int32, (chunk, chunk), 1)
    ones_cols = jnp.ones((chunk, dh), BF16)

    for d, (q_ref, kt_ref, v_ref, col_ref, row_ref, h_ref) in enumerate(
            ((qf, ktf, vf, colf, rowf, hf_ref), (qb, ktb, vb, colb, rowb, hb_ref))):
        reverse = d == 1
        cols = col_ref[0, 0, 0]
        r = row_ref[0, 0, 0, 0:nh, :]
        w_tok = row_ref[0, 0, 0, nh:2 * nh, :]
        w_state = row_ref[0, 0, 0, 2 * nh:3 * nh, :]
        m_prev = row_ref[0, 0, 0, 3 * nh:4 * nh, :]
        mask = (col >= row) if reverse else (col <= row)
        for h in range(nh):
            sl = slice(h * dh, (h + 1) * dh)
            bc = _dot(cols, sel_ref[h])
            p_bc = bc[:, :chunk]
            w = jnp.exp(jnp.where(mask, p_bc + r[h:h + 1, :], NEG_BIG))
            p16_sc[d, h] = (_dot(q_ref[0, :, sl], kt_ref[sl, :]) * w).astype(BF16)
            e_sc[d, h, :, 0:dh] = jnp.exp(p_bc[:, :dh] + m_prev[h:h + 1, 0:dh])
            e_sc[d, h, :, dh:2 * dh] = jnp.exp(bc[:, chunk:])
        for h in range(nh):
            sl = slice(h * dh, (h + 1) * dh)
            q16 = q_ref[0, :, sl]
            kt16 = kt_ref[sl, :]
            vaug = jnp.concatenate([v_ref[0, :, sl], ones_cols], axis=1)
            s_old = s_sc[d, h]
            w_inter = e_sc[d, h, :, 0:dh]
            tot = (_dot(p16_sc[d, h], vaug)
                   + jnp.concatenate([w_inter, w_inter], axis=1) * _dot(q16, s_old.astype(BF16)))
            h_ref[0, :, sl] = (tot[:, :dh] / jnp.maximum(jnp.abs(tot[:, dh:]),
                                                         e_sc[d, h, :, dh:2 * dh])).astype(BF16)
            kts = (kt16.astype(F32) * w_tok[h:h + 1, :]).astype(BF16)
            ws = w_state[h:h + 1, 0:dh]
            s_sc[d, h] = jnp.concatenate([ws, ws], axis=1) * s_old + _dot(kts, vaug)

    @pl.when(step == pl.num_programs(1) - 1)
    def _():
        for d in range(2):
            for h in range(nh):
                s_fin = s_sc[d, h]
                c_ref[0, d, h] = s_fin[:, 0:dh].T
                n_ref[0, d, h:h + 1, :] = s_fin[:, dh:2 * dh].T[0:1, :]


def _ml_gate_kernel(*refs, nc, chunk, bb, has_init):
    if has_init:
        gt_ref, m0_ref, cols_ref, rows_ref, m_ref = refs
    else:
        gt_ref, cols_ref, rows_ref, m_ref = refs
    nh = ML_HEADS
    pad = jnp.zeros((LANES - 6 * nh, chunk), F32)
    for d in range(2):
        reverse = d == 1
        last = 0 if reverse else chunk - 1
        for bi in range(bb):
            m = m0_ref[bi, d] if has_init else jnp.zeros((nh, chunk), F32)
            for c in (range(nc - 1, -1, -1) if reverse else range(nc)):
                c0 = (bi * nc + c) * chunk
                gt = gt_ref[0:4 * nh, c0:c0 + chunk]
                b = _lane_scan(_log_sigmoid(gt[nh * (2 + d):nh * (3 + d), :]), jnp.add, reverse, 0.0)
                r = gt[nh * d:nh * (d + 1), :] - b
                mx = jnp.maximum(m, _lane_scan(r, jnp.maximum, reverse, -jnp.inf))
                mt = b + mx
                m_new = jnp.broadcast_to(mt[:, last:last + 1], (nh, chunk))
                b_last = jnp.broadcast_to(b[:, last:last + 1], (nh, chunk))
                rows_ref[bi, c, d, 0:nh, :] = r
                rows_ref[bi, c, d, nh:2 * nh, :] = jnp.exp(r + b_last - m_new)
                rows_ref[bi, c, d, 2 * nh:3 * nh, :] = jnp.exp(b_last + m - m_new)
                rows_ref[bi, c, d, 3 * nh:4 * nh, :] = m
                stack = []
                for x in (-mx, -mt):
                    stack += [t.astype(F32) for t in _split3(x)]
                stack.append(pad)
                cols_ref[bi, c, d] = jnp.concatenate(stack, axis=0).T.astype(BF16)
                m = m_new
            m_ref[bi, d] = m[:, 0:LANES]


def _ml_gate_call(gates, m0, *, batch, seq_len, chunk):
    nc = seq_len // chunk
    bb = max(1, 16 // nc)
    nh = ML_HEADS
    has_init = m0 is not None
    in_specs = [pl.BlockSpec((LANES, bb * seq_len), lambda i: (0, i))]
    args = [gates]
    if has_init:
        in_specs.append(pl.BlockSpec((bb, 2, nh, chunk), lambda i: (i, 0, 0, 0)))
        args.append(jnp.broadcast_to(m0[..., None], m0.shape + (chunk,)))
    return pl.pallas_call(
        functools.partial(_ml_gate_kernel, nc=nc, chunk=chunk, bb=bb, has_init=has_init),
        grid=(batch // bb,),
        in_specs=in_specs,
        out_specs=[pl.BlockSpec((bb, nc, 2, chunk, LANES), lambda i: (i, 0, 0, 0, 0)),
                   pl.BlockSpec((bb, nc, 2, 4 * nh, chunk), lambda i: (i, 0, 0, 0, 0)),
                   pl.BlockSpec((bb, 2, nh, LANES), lambda i: (i, 0, 0, 0))],
        out_shape=[jax.ShapeDtypeStruct((batch, nc, 2, chunk, LANES), BF16),
                   jax.ShapeDtypeStruct((batch, nc, 2, 4 * nh, chunk), F32),
                   jax.ShapeDtypeStruct((batch, 2, nh, LANES), F32)],
        compiler_params=_cparams("arbitrary"),
        name="ml_gate",
    )(*args)


def _ml_scan_call(q, kt, v, cols, rows, init, *, batch, seq_len, chunk):
    nc = seq_len // chunk
    nh = ML_HEADS
    q3 = q.reshape(batch, seq_len, ML_W)
    v3 = v.reshape(batch, seq_len, ML_W)
    sel = _bcast_selectors(chunk)
    has_init = init is not None

    fwd3 = lambda b, i: (b, i, 0)
    bwd3 = lambda b, i: (b, nc - 1 - i, 0)
    in_specs = [pl.BlockSpec((1, chunk, ML_W), fwd3),
                pl.BlockSpec((ML_W, chunk), lambda b, i: (0, b * nc + i)),
                pl.BlockSpec((1, chunk, ML_W), fwd3),
                pl.BlockSpec((1, chunk, ML_W), bwd3),
                pl.BlockSpec((ML_W, chunk), lambda b, i: (0, b * nc + nc - 1 - i)),
                pl.BlockSpec((1, chunk, ML_W), bwd3),
                pl.BlockSpec((1, 1, 1, chunk, LANES), lambda b, i: (b, i, 0, 0, 0)),
                pl.BlockSpec((1, 1, 1, chunk, LANES), lambda b, i: (b, nc - 1 - i, 1, 0, 0)),
                pl.BlockSpec((1, 1, 1, 4 * nh, chunk), lambda b, i: (b, i, 0, 0, 0)),
                pl.BlockSpec((1, 1, 1, 4 * nh, chunk), lambda b, i: (b, nc - 1 - i, 1, 0, 0)),
                pl.BlockSpec(sel.shape, lambda b, i: (0, 0, 0))]
    args = [q3, kt, v3, q3, kt, v3, cols, cols, rows, rows, sel]
    c_spec = pl.BlockSpec((1, 2, nh, ML_HEAD_DIM, ML_HEAD_DIM), lambda b, i: (b, 0, 0, 0, 0))
    n_spec = pl.BlockSpec((1, 2, nh, ML_HEAD_DIM), lambda b, i: (b, 0, 0, 0))
    if has_init:
        in_specs += [c_spec, n_spec]
        args += list(init)
    return pl.pallas_call(
        functools.partial(_ml_scan_kernel, chunk=chunk, has_init=has_init),
        grid=(batch, nc),
        in_specs=in_specs,
        out_specs=[pl.BlockSpec((1, chunk, ML_W), fwd3),
                   pl.BlockSpec((1, chunk, ML_W), bwd3),
                   c_spec, n_spec],
        out_shape=[jax.ShapeDtypeStruct((batch, seq_len, ML_W), BF16),
                   jax.ShapeDtypeStruct((batch, seq_len, ML_W), BF16),
                   jax.ShapeDtypeStruct((batch, 2, nh, ML_HEAD_DIM, ML_HEAD_DIM), F32),
                   jax.ShapeDtypeStruct((batch, 2, nh, ML_HEAD_DIM), F32)],
        scratch_shapes=[pltpu.VMEM((2, nh, ML_HEAD_DIM, 2 * ML_HEAD_DIM), F32),
                        pltpu.VMEM((2, nh, chunk, chunk), BF16),
                        pltpu.VMEM((2, nh, chunk, 2 * ML_HEAD_DIM), F32)],
        compiler_params=_cparams("arbitrary", "arbitrary"),
        name="ml_scan",
    )(*args)


def _ml_out_kernel(x_ref, hf_ref, hb_ref, og_ref, hn_ref, w_ref, mod_ref, gain_ref, o_ref):
    h = hf_ref[...].astype(F32) + hb_ref[...].astype(F32)
    parts = []
    for hd in range(ML_HEADS):
        sl = slice(hd * ML_HEAD_DIM, (hd + 1) * ML_HEAD_DIM)
        parts.append(_rms(h[:, sl], hn_ref[0:1, sl]))
    a = jnp.concatenate(parts, axis=1) * jax.nn.sigmoid(og_ref[...].astype(F32))
    out = _dot(a.astype(BF16), w_ref[...])
    o_ref[...] = x_ref[...] + mod_ref[0, 2:3, :] * _rms(out, gain_ref[1:2, :])


def _ml_out_call(x, hf, hb, og, head_norm, w, mod, gains, *, tm, rows_per_mod):
    rows = x.shape[0]
    return pl.pallas_call(
        _ml_out_kernel,
        grid=(rows // tm,),
        in_specs=[pl.BlockSpec((tm, D_MODEL), lambda i: (i, 0)),
                  pl.BlockSpec((tm, ML_W), lambda i: (i, 0)),
                  pl.BlockSpec((tm, ML_W), lambda i: (i, 0)),
                  pl.BlockSpec((tm, ML_W), lambda i: (i, 0)),
                  pl.BlockSpec((1, ML_W), lambda i: (0, 0)),
                  pl.BlockSpec((ML_W, D_MODEL), lambda i: (0, 0)),
                  pl.BlockSpec((1, 8, D_MODEL), lambda i: (i * tm // rows_per_mod, 0, 0)),
                  pl.BlockSpec((8, D_MODEL), lambda i: (0, 0))],
        out_specs=pl.BlockSpec((tm, D_MODEL), lambda i: (i, 0)),
        out_shape=jax.ShapeDtypeStruct((rows, D_MODEL), F32),
        compiler_params=_cparams("arbitrary"),
        name="ml_out",
    )(x, hf, hb, og, head_norm, w, mod, gains)


@functools.lru_cache(maxsize=None)
def _dft_tables_np(seq_len):
    n4 = 4 * seq_len
    ar = np.arange(seq_len, dtype=np.int64)
    idx = ((2 * ar + 1)[:, None] * ar[None, :]) % n4
    ang = (2.0 * np.pi / n4) * idx.astype(np.float64)
    cm = np.cos(ang).astype(np.float32)
    sm = np.sin(ang).astype(np.float32)
    return tuple(np.ascontiguousarray(a) for a in (cm, sm, cm.T, sm.T))


def _dft_tables(seq_len):
    return tuple(jnp.asarray(a).astype(BF16) for a in _dft_tables_np(seq_len))


def _hy_features(seq_len):
    t = np.linspace(0.0, 1.0, seq_len, dtype=np.float32)
    bands = np.arange(1, HY_BANDS + 1, dtype=np.float32)
    ang = (np.float32(2.0 * np.pi) * t[:, None]) * bands
    z = np.concatenate([t[:, None], np.cos(ang), np.sin(ang)], axis=-1).astype(np.float32)
    z = np.pad(z, ((0, 0), (0, 32 - HY_EMB)))
    deltas = np.abs(np.linspace(HY_MIN_DECAY, HY_MAX_DECAY, HY_CH, dtype=np.float32))[None, :]
    return jnp.asarray(z), jnp.asarray(deltas)


def _rope_tables(seq_len):
    rows = seq_len // GRID_W
    row = np.repeat(np.arange(rows, dtype=np.float32), GRID_W)
    col = np.tile(np.arange(GRID_W, dtype=np.float32), rows)
    n_freq = A_HEAD_DIM // 4
    inv = (np.float32(ROPE_THETA) ** (-np.arange(n_freq, dtype=np.float32) / n_freq)).astype(np.float32)
    ang = np.concatenate([row[:, None] * inv, col[:, None] * inv], axis=-1).astype(np.float32)
    cos, sin = np.cos(ang), np.sin(ang)
    return (jnp.asarray(np.concatenate([cos, cos], axis=-1), F32),
            jnp.asarray(np.concatenate([-sin, sin], axis=-1), F32))


def _pad_rows(a, rows=8):
    return jnp.pad(a, ((0, rows - a.shape[0]), (0, 0)))


def kernel(x_prompt, x_sample, cache_attn_k, cache_attn_v, state_mlstm_C, state_mlstm_n, state_mlstm_m, c, c_ctx, w_mod, b_mod, norm_mix_pre, norm_mix_post, norm_ffn_pre, norm_ffn_post, ffn_w_up, ffn_conv_w, ffn_conv_b, ffn_w_down, ah_w_in, ah_w_out, attn_q_norm, attn_k_norm, hy_conv_w, hy_conv_b, hy_w1, hy_b1, hy_w2, hy_b2, hy_w3, hy_b3, hy_sin_freq, hy_skip, ml_w_in, ml_b_gates, ml_conv_w, ml_conv_b, ml_head_norm, ml_w_out):
    bp, lp, _ = x_prompt.shape
    bs, ls, _ = x_sample.shape
    past = cache_attn_k.shape[2]
    xp = x_prompt.reshape(bp * lp, D_MODEL)
    xs = x_sample.reshape(bs * ls, D_MODEL)
    groups = {
        "p": dict(batch=bp, seq_len=lp, tm=1024, rows_per_mod=bp * lp, ffn_tm=512, ffn_tf=D_FF // 2,
                  seq_bb=4, hy_tc=HY_CH, hy_tk=256),
        "s": dict(batch=bs, seq_len=ls, tm=ls, rows_per_mod=ls, ffn_tm=512, ffn_tf=D_FF // 2,
                  seq_bb=1, hy_tc=COL_TILE, hy_tk=512),
    }
    ffn_up16 = ffn_w_up.astype(BF16)
    ffn_down16 = ffn_w_down.astype(BF16)

    cc = jnp.concatenate([c, c_ctx[None, :], jnp.zeros((8 - bs - 1, D_MODEL), F32)], axis=0)
    mod_all = _mod_call(cc, w_mod, b_mod).reshape(DEPTH, 8, 6, D_MODEL)
    mod_all = jnp.pad(mod_all, ((0, 0), (0, 0), (0, 2), (0, 0)))
    rope = _rope_tables(ls)

    new_k = new_v = new_c = new_n = new_m = None
    for l in range(DEPTH):
        j = l // 2
        mods = {"s": mod_all[l, :bs], "p": mod_all[l, bs:bs + 1]}
        gains = _pad_rows(jnp.stack([norm_mix_pre[l], norm_mix_post[l], norm_ffn_pre[l], norm_ffn_post[l]]))
        xin = {"p": xp, "s": xs}
        xmid = {}
        if l % 2 == 0:
            w_in = ah_w_in[j].astype(BF16)
            w_out = ah_w_out[j].astype(BF16)
            qkn = _pad_rows(jnp.stack([attn_q_norm[j], attn_k_norm[j]]))
            conv = jnp.pad(_pad_rows(jnp.concatenate([hy_conv_w[j], hy_conv_b[j][None, :]], axis=0)),
                           ((0, 0), (A_Q + 2 * A_KV, 0)))
            w1 = jnp.pad(hy_w1[j], ((0, 32 - HY_EMB), (0, 0)))
            sf = _pad_rows(hy_sin_freq[j])
            for g, cfg in groups.items():
                batch, seq_len, tm, rpm = cfg["batch"], cfg["seq_len"], cfg["tm"], cfg["rows_per_mod"]
                outs = _ah_inproj_call(xin[g], mods[g], gains, w_in, qkn, conv, rope if g == "s" else None,
                                       tm=512, seq_len=seq_len, rows_per_mod=rpm)
                q, k, v, u = outs[:4]
                if g == "s":
                    ctx_k = cache_attn_k[:, j].reshape(bs, past, A_KV).astype(BF16)
                    ctx_v = cache_attn_v[:, j].reshape(bs, past, A_KV).astype(BF16)
                else:
                    ctx_k = ctx_v = None
                    new_k = outs[4].reshape(bp, 1, lp, A_KV_HEADS, A_HEAD_DIM)
                    new_v = outs[5].reshape(bp, 1, lp, A_KV_HEADS, A_HEAD_DIM)
                attn = _attn_call(q, k, v, ctx_k, ctx_v, batch=batch, seq_len=seq_len, tq=256, bb=cfg["seq_bb"])
                z, deltas = _hy_features(seq_len)
                dft = _dft_tables(seq_len)
                tk = cfg["hy_tk"]
                gr, gi = _hy_filter_call(z, w1, hy_b1[j][None, :], hy_w2[j], hy_b2[j][None, :], hy_w3[j],
                                         hy_b3[j][None, :], sf, deltas, dft[0], dft[1], seq_len=seq_len, tk=tk)
                hyo = _hy_conv_call(u, hy_skip[j][None, :], dft, gr, gi, batch=batch, seq_len=seq_len,
                                    tc=cfg["hy_tc"], tk=tk, bb=cfg["seq_bb"])
                xmid[g] = _ah_out_call(xin[g], attn.reshape(batch * seq_len, A_Q),
                                       hyo.reshape(batch * seq_len, HY_CH), w_out, mods[g], gains,
                                       tm=512, rows_per_mod=rpm)
        else:
            w_in = ml_w_in[j].astype(BF16)
            w_g = jnp.pad(ml_w_in[j][:, 4 * ML_W:], ((0, 0), (0, LANES - 4 * ML_HEADS))).astype(BF16)
            b_g = jnp.pad(ml_b_gates[j], (0, LANES - 4 * ML_HEADS))[None, :]
            w_out = ml_w_out[j].astype(BF16)
            conv = jnp.pad(_pad_rows(jnp.concatenate([ml_conv_w[j], ml_conv_b[j][None, :]], axis=0)),
                           ((0, 0), (0, 2 * ML_W)))
            for g, cfg in groups.items():
                batch, seq_len, tm, rpm = cfg["batch"], cfg["seq_len"], cfg["tm"], cfg["rows_per_mod"]
                q, kt, v, og, gates = _ml_inproj_call(xin[g], mods[g], gains, w_in, w_g, b_g, conv, tm=512,
                                                      cw=512, seq_len=seq_len, rows_per_mod=rpm)
                if g == "s":
                    init, m0 = (state_mlstm_C[:, j], state_mlstm_n[:, j]), state_mlstm_m[:, j]
                else:
                    init = m0 = None
                cols, rows, m_new = _ml_gate_call(gates, m0, batch=batch, seq_len=seq_len, chunk=ML_SCAN_CHUNK)
                hf, hb, c_new, n_new = _ml_scan_call(q, kt, v, cols, rows, init, batch=batch, seq_len=seq_len,
                                                     chunk=ML_SCAN_CHUNK)
                if g == "p":
                    new_c, new_n, new_m = c_new[:, None], n_new[:, None], m_new[:, None, :, :, 0]
                xmid[g] = _ml_out_call(xin[g], hf.reshape(batch * seq_len, ML_W), hb.reshape(batch * seq_len, ML_W),
                                       og, ml_head_norm[j][None, :], w_out, mods[g], gains,
                                       tm=512, rows_per_mod=rpm)
        conv = _pad_rows(jnp.concatenate([ffn_conv_w[l], ffn_conv_b[l][None, :]], axis=0))
        xout = {}
        for g, cfg in groups.items():
            xout[g] = _ffn_call(xmid[g], mods[g], gains, ffn_up16, conv, ffn_down16, l, tm=cfg["ffn_tm"],
                                tf=cfg["ffn_tf"], seq_len=cfg["seq_len"], rows_per_mod=cfg["rows_per_mod"])
        xp, xs = xout["p"], xout["s"]

    return (xp.reshape(bp, lp, D_MODEL), xs.reshape(bs, ls, D_MODEL), new_k, new_v, new_c, new_n, new_m)
```

```python
import functools
import math

import numpy as np
import jax
import jax.numpy as jnp
from jax import lax
from jax.experimental import pallas as pl
from jax.experimental.pallas import tpu as pltpu

F32 = jnp.float32
BF16 = jnp.bfloat16

D_MODEL = 1024
DEPTH = 2
GRID_W = 64
A_HEADS = 4
A_KV_HEADS = 2
A_HEAD_DIM = 128
A_Q = A_HEADS * A_HEAD_DIM
A_KV = A_KV_HEADS * A_HEAD_DIM
ROPE_THETA = 10000.0
HY_CH = D_MODEL // 2
HY_BANDS = 8
HY_EMB = 1 + 2 * HY_BANDS
HY_W = 64
HY_TARGET = 1e-2
HY_FAST_PCT = 0.3
HY_SLOW_PCT = 1.5
HY_MAX_DECAY = math.log(HY_TARGET) / HY_FAST_PCT
HY_MIN_DECAY = math.log(HY_TARGET) / HY_SLOW_PCT
AH_IN = A_Q + 2 * A_KV + 3 * HY_CH
ML_HEADS = 8
ML_HEAD_DIM = D_MODEL // ML_HEADS
ML_W = ML_HEADS * ML_HEAD_DIM
D_FF = 2816
NORM_EPS = 1e-6
NEG_BIG = -1e30

LANES = 128
VMEM_LIMIT = 56 * 1024 * 1024
COL_TILE = 256
ML_SCAN_CHUNK = 128
INV_SQRT2 = 1.0 / math.sqrt(2.0)


def _cparams(*sem):
    return pltpu.CompilerParams(dimension_semantics=sem, vmem_limit_bytes=VMEM_LIMIT)


def _rms(x, gain):
    return x * lax.rsqrt(jnp.mean(x * x, axis=-1, keepdims=True) + NORM_EPS) * gain


def _modulated(x, gain, shift, scale):
    return _rms(x, gain) * (1.0 + scale) + shift


def _dot(a, b):
    return jnp.dot(a, b, preferred_element_type=F32)


def _dot_nt(a, b):
    return lax.dot_general(a, b, (((1,), (1,)), ((), ())), preferred_element_type=F32)


def _dot_tn(a, b):
    return lax.dot_general(a, b, (((0,), (0,)), ((), ())), preferred_element_type=F32)


def _split3(x):
    hi = x.astype(BF16)
    r1 = x - hi.astype(F32)
    mid = r1.astype(BF16)
    lo = (r1 - mid.astype(F32)).astype(BF16)
    return hi, mid, lo


def _dot_f32(a, b):
    a0, a1, a2 = _split3(a)
    b0, b1, b2 = _split3(b)
    return (_dot(a0, b0) + (_dot(a0, b1) + _dot(a1, b0))
            + (_dot(a0, b2) + _dot(a1, b1) + _dot(a2, b0)))


def _dwconv3(y, conv_ref, seq_len, edge=None):
    rows, cols = y.shape
    pos = lax.broadcasted_iota(jnp.int32, (rows, LANES), 0)
    if edge is None:
        pos = pos % seq_len
        first = pos == 0
        last = pos == seq_len - 1
    else:
        first = pos == 0
        last = pos == rows - 1
    outs = []
    for c0 in range(0, cols, LANES):
        yc = y[:, c0:c0 + LANES]
        before = 0.0 if edge is None else edge[0][:, c0:c0 + LANES]
        after = 0.0 if edge is None else edge[1][:, c0:c0 + LANES]
        prev = jnp.where(first, before, pltpu.roll(yc, 1, 0))
        nxt = jnp.where(last, after, pltpu.roll(yc, rows - 1, 0))
        outs.append(prev * conv_ref[0:1, c0:c0 + LANES] + yc * conv_ref[1:2, c0:c0 + LANES]
                    + nxt * conv_ref[2:3, c0:c0 + LANES] + conv_ref[3:4, c0:c0 + LANES])
    return outs[0] if len(outs) == 1 else jnp.concatenate(outs, axis=1)


def _mod_kernel(c_ref, w_ref, b_ref, o_ref):
    a = c_ref[...]
    a = a * jax.nn.sigmoid(a)
    o_ref[0] = _dot(a.astype(BF16), w_ref[0].astype(BF16)) + b_ref[0]


def _mod_call(cc, w_mod, b_mod):
    tn = 1536
    n = 6 * D_MODEL
    return pl.pallas_call(
        _mod_kernel,
        grid=(DEPTH, n // tn),
        in_specs=[pl.BlockSpec((8, D_MODEL), lambda l, j: (0, 0)),
                  pl.BlockSpec((1, D_MODEL, tn), lambda l, j: (l, 0, j)),
                  pl.BlockSpec((1, 1, tn), lambda l, j: (l, 0, j))],
        out_specs=pl.BlockSpec((1, 8, tn), lambda l, j: (l, 0, j)),
        out_shape=jax.ShapeDtypeStruct((DEPTH, 8, n), F32),
        compiler_params=_cparams("arbitrary", "arbitrary"),
        name="mod",
    )(cc, w_mod, b_mod.reshape(DEPTH, 1, n))


def _ah_inproj_kernel(*refs, seq_len, rope, halo):
    it = iter(refs)
    x_ref = next(it)
    xb_ref, xa_ref = (next(it), next(it)) if halo else (None, None)
    mod_ref, gain_ref, w_ref, qkn_ref, conv_ref = (next(it) for _ in range(5))
    cc_ref, ss_ref = (next(it), next(it)) if rope else (None, None)
    q_ref, k_ref, v_ref, u_ref = (next(it) for _ in range(4))
    kf_ref, vf_ref = (None, None) if rope else (next(it), next(it))
    tm = x_ref.shape[0]
    dh = A_HEAD_DIM

    def modulated(x):
        return _modulated(x, gain_ref[0:1, :], mod_ref[0, 0:1, :], mod_ref[0, 1:2, :]).astype(BF16)

    def head(yh, g):
        yh = _rms(yh, g)
        if rope:
            yh = yh * cc_ref[...] + pltpu.roll(yh, dh // 2, 1) * ss_ref[...]
        return yh

    xn = modulated(x_ref[...])
    yq = _dot(xn, w_ref[:, 0:A_Q])
    for h in range(A_HEADS):
        sl = slice(h * dh, (h + 1) * dh)
        q_ref[:, sl] = (head(yq[:, sl], qkn_ref[0:1, :]) * (dh ** -0.5)).astype(BF16)
    ykv = _dot(xn, w_ref[:, A_Q:A_Q + 2 * A_KV])
    for h in range(A_KV_HEADS):
        sl = slice(h * dh, (h + 1) * dh)
        kh = head(ykv[:, sl], qkn_ref[1:2, :])
        k_ref[:, sl] = kh.astype(BF16)
        if kf_ref is not None:
            kf_ref[:, sl] = kh
    yv = ykv[:, A_KV:2 * A_KV]
    v_ref[...] = yv.astype(BF16)
    if vf_ref is not None:
        vf_ref[...] = yv

    xe = xn
    if halo:
        xe = jnp.concatenate([xn, modulated(jnp.concatenate([xb_ref[...], xa_ref[...]], axis=0))], axis=0)
    cw = 512
    u0 = A_Q + 2 * A_KV
    for t in range(3 * HY_CH // cw):
        cols = slice(u0 + t * cw, u0 + (t + 1) * cw)
        y = _dot(xe, w_ref[:, cols])
        edge = None
        if halo:
            y, edge = _halo_edges(y, tm, seq_len)
        u_ref[:, t * cw:(t + 1) * cw] = _dwconv3(y, conv_ref.at[:, cols], seq_len, edge).astype(BF16)


def _ah_inproj_call(x, mod, gains, w, qkn, conv, rope_tabs, *, tm, seq_len, rows_per_mod):
    rows = x.shape[0]
    rope = rope_tabs is not None
    halo = tm < seq_len
    assert (seq_len % tm == 0) if halo else (tm % seq_len == 0)
    row_blk = lambda cols: pl.BlockSpec((tm, cols), lambda i: (i, 0))
    out_specs = [row_blk(A_Q), row_blk(A_KV), row_blk(A_KV), row_blk(3 * HY_CH)]
    out_shape = [jax.ShapeDtypeStruct((rows, A_Q), BF16), jax.ShapeDtypeStruct((rows, A_KV), BF16),
                 jax.ShapeDtypeStruct((rows, A_KV), BF16), jax.ShapeDtypeStruct((rows, 3 * HY_CH), BF16)]
    if not rope:
        out_specs += [row_blk(A_KV)] * 2
        out_shape += [jax.ShapeDtypeStruct((rows, A_KV), F32)] * 2
    in_specs = [row_blk(D_MODEL)] + (_halo_specs(tm, rows) if halo else []) + [
        pl.BlockSpec((1, 8, D_MODEL), lambda i: (i * tm // rows_per_mod, 0, 0)),
        pl.BlockSpec((8, D_MODEL), lambda i: (0, 0)),
        pl.BlockSpec((D_MODEL, AH_IN), lambda i: (0, 0)),
        pl.BlockSpec((8, A_HEAD_DIM), lambda i: (0, 0)),
        pl.BlockSpec((8, AH_IN), lambda i: (0, 0))]
    args = ([x, x, x] if halo else [x]) + [mod, gains, w, qkn, conv]
    if rope:
        per_seq = max(seq_len // tm, 1)
        in_specs += [pl.BlockSpec((tm, A_HEAD_DIM), lambda i: (i % per_seq, 0))] * 2
        args += list(rope_tabs)
    return pl.pallas_call(
        functools.partial(_ah_inproj_kernel, seq_len=seq_len, rope=rope, halo=halo),
        grid=(rows // tm,),
        in_specs=in_specs,
        out_specs=out_specs,
        out_shape=out_shape,
        compiler_params=_cparams("arbitrary"),
        name="ah_inproj",
    )(*args)


def _attn_kernel(*refs, ctx, bb):
    if ctx:
        q_ref, k_ref, v_ref, kc_ref, vc_ref, o_ref = refs
    else:
        q_ref, k_ref, v_ref, o_ref = refs
    dh = A_HEAD_DIM
    for bi in range(bb):
        k = k_ref[bi]
        if ctx:
            kc = kc_ref[bi]
        else:
            vaug = jnp.concatenate([v_ref[bi], jnp.ones((k.shape[0], dh), BF16)], axis=1)
        for g in range(A_HEADS // A_KV_HEADS):
            sl = slice(g * dh, (g + 1) * dh)
            q = q_ref[bi, :, sl]
            s = _dot_nt(q, k)
            m = jnp.max(s, axis=-1, keepdims=True)
            if ctx:
                sc = _dot_nt(q, kc)
                m = jnp.maximum(m, jnp.max(sc, axis=-1, keepdims=True))
            if ctx:
                p = jnp.exp(s - m)
                pc = jnp.exp(sc - m)
                den = jnp.sum(p, axis=-1, keepdims=True) + jnp.sum(pc, axis=-1, keepdims=True)
                o = _dot(p.astype(BF16), v_ref[bi]) + _dot(pc.astype(BF16), vc_ref[bi])
                o_ref[bi, :, sl] = (o / den).astype(BF16)
            else:
                o = _dot(jnp.exp(s - m).astype(BF16), vaug)
                o_ref[bi, :, sl] = (o[:, :dh] / o[:, dh:]).astype(BF16)


def _attn_call(q, k, v, ctx_k, ctx_v, *, batch, seq_len, tq, bb):
    ctx = ctx_k is not None
    qw = A_Q // A_KV_HEADS
    in_specs = [pl.BlockSpec((bb, tq, qw), lambda b, h, i: (b, i, h)),
                pl.BlockSpec((bb, seq_len, A_HEAD_DIM), lambda b, h, i: (b, 0, h)),
                pl.BlockSpec((bb, seq_len, A_HEAD_DIM), lambda b, h, i: (b, 0, h))]
    args = [q.reshape(batch, seq_len, A_Q), k.reshape(batch, seq_len, A_KV), v.reshape(batch, seq_len, A_KV)]
    if ctx:
        past = ctx_k.shape[1]
        in_specs += [pl.BlockSpec((bb, past, A_HEAD_DIM), lambda b, h, i: (b, 0, h))] * 2
        args += [ctx_k, ctx_v]
    return pl.pallas_call(
        functools.partial(_attn_kernel, ctx=ctx, bb=bb),
        grid=(batch // bb, A_KV_HEADS, seq_len // tq),
        in_specs=in_specs,
        out_specs=pl.BlockSpec((bb, tq, qw), lambda b, h, i: (b, i, h)),
        out_shape=jax.ShapeDtypeStruct((batch, seq_len, A_Q), BF16),
        compiler_params=_cparams("arbitrary", "arbitrary", "arbitrary"),
        name="attn",
    )(*args)


def _hy_filter_kernel(z_ref, w1_ref, b1_ref, w2_ref, b2_ref, w3_ref, b3_ref, sf_ref, dl_ref,
                      cm_ref, sm_ref, gr_ref, gi_ref, hs_ref, hd_ref):
    @pl.when(pl.program_id(0) == 0)
    def _():
        z = z_ref[...]
        h = jnp.sin(sf_ref[0:1, :] * (_dot_f32(z, w1_ref[...]) + b1_ref[...]))
        h = jnp.sin(sf_ref[1:2, :] * (_dot_f32(h, w2_ref[...]) + b2_ref[...]))
        filt = _dot_f32(h, w3_ref[...]) + b3_ref[...]
        win = jnp.exp(-z[:, 0:1] * dl_ref[...])
        hf = filt[:, :HY_CH] * win
        hb = filt[:, HY_CH:] * win
        hs_ref[...] = (hf + hb).astype(BF16)
        hd_ref[...] = (hb - hf).astype(BF16)

    gr_ref[...] = _dot(cm_ref[...], hs_ref[...])
    gi_ref[...] = _dot(sm_ref[...], hd_ref[...])


def _hy_filter_call(z, w1, b1, w2, b2, w3, b3, sf, deltas, cm, sm, *, seq_len, tk):
    full = lambda a: pl.BlockSpec(a.shape, lambda k: (0,) * a.ndim)
    small = [z, w1, b1, w2, b2, w3, b3, sf, deltas]
    return pl.pallas_call(
        _hy_filter_kernel,
        grid=(seq_len // tk,),
        in_specs=[full(a) for a in small] + [pl.BlockSpec((tk, seq_len), lambda k: (k, 0))] * 2,
        out_specs=[pl.BlockSpec((tk, HY_CH), lambda k: (k, 0))] * 2,
        out_shape=[jax.ShapeDtypeStruct((seq_len, HY_CH), F32)] * 2,
        scratch_shapes=[pltpu.VMEM((seq_len, HY_CH), BF16)] * 2,
        compiler_params=_cparams("arbitrary"),
        name="hy_filter",
    )(*small, cm, sm)


def _hy_conv_kernel(x0_ref, x1_ref, v_ref, skip_ref, cm_ref, sm_ref, cmt_ref, smt_ref, gr_ref, gi_ref,
                    o_ref, vv_ref, acc_ref, *, seq_len, bb):
    kb = pl.program_id(2)
    single = seq_len == cm_ref.shape[0]
    gr = gr_ref[...]
    gi = gi_ref[...]
    for bi in range(bb):
        def gated():
            return v_ref[bi].astype(F32) * x1_ref[bi].astype(F32)

        if single:
            vv = gated().astype(BF16)
        else:
            @pl.when(kb == 0)
            def _():
                vv_ref[bi] = gated().astype(BF16)
                acc_ref[bi] = jnp.zeros_like(acc_ref[bi])

            vv = vv_ref[bi]
        vr = _dot(cm_ref[...], vv)
        wi = _dot(sm_ref[...], vv)
        pr = gr * vr + gi * wi
        qi = gr * wi - gi * vr
        y = _dot(cmt_ref[...], pr.astype(BF16)) + _dot(smt_ref[...], qi.astype(BF16))

        def finish(total):
            o_ref[bi] = ((total * (1.0 / seq_len) + skip_ref[...] * gated())
                         * x0_ref[bi].astype(F32)).astype(BF16)

        if single:
            finish(y)
        else:
            acc_ref[bi] += y

            @pl.when(kb == pl.num_programs(2) - 1)
            def _():
                finish(acc_ref[bi])


def _hy_conv_call(u, skip, dft, gr, gi, *, batch, seq_len, tc, tk, bb):
    proj3 = u.reshape(batch, seq_len, 3 * HY_CH)
    cm, sm, cmt, smt = dft
    nch = HY_CH // tc
    return pl.pallas_call(
        functools.partial(_hy_conv_kernel, seq_len=seq_len, bb=bb),
        grid=(batch // bb, nch, seq_len // tk),
        in_specs=[pl.BlockSpec((bb, seq_len, tc), lambda b, c, k: (b, 0, c)),
                  pl.BlockSpec((bb, seq_len, tc), lambda b, c, k: (b, 0, nch + c)),
                  pl.BlockSpec((bb, seq_len, tc), lambda b, c, k: (b, 0, 2 * nch + c)),
                  pl.BlockSpec((1, tc), lambda b, c, k: (0, c)),
                  pl.BlockSpec((tk, seq_len), lambda b, c, k: (k, 0)),
                  pl.BlockSpec((tk, seq_len), lambda b, c, k: (k, 0)),
                  pl.BlockSpec((seq_len, tk), lambda b, c, k: (0, k)),
                  pl.BlockSpec((seq_len, tk), lambda b, c, k: (0, k)),
                  pl.BlockSpec((tk, tc), lambda b, c, k: (k, c)),
                  pl.BlockSpec((tk, tc), lambda b, c, k: (k, c))],
        out_specs=pl.BlockSpec((bb, seq_len, tc), lambda b, c, k: (b, 0, c)),
        out_shape=jax.ShapeDtypeStruct((batch, seq_len, HY_CH), BF16),
        scratch_shapes=[pltpu.VMEM((bb, seq_len, tc), BF16), pltpu.VMEM((bb, seq_len, tc), F32)],
        compiler_params=_cparams("arbitrary", "arbitrary", "arbitrary"),
        name="hy_conv",
    )(proj3, proj3, proj3, skip, cm, sm, cmt, smt, gr, gi)


def _ah_out_kernel(x_ref, a_ref, h_ref, w_ref, mod_ref, gain_ref, o_ref):
    out = _dot(a_ref[...], w_ref[0:A_Q, :]) + _dot(h_ref[...], w_ref[A_Q:, :])
    o_ref[...] = x_ref[...] + mod_ref[0, 2:3, :] * _rms(out, gain_ref[1:2, :])


def _ah_out_call(x, attn, hyo, w, mod, gains, *, tm, rows_per_mod):
    rows = x.shape[0]
    return pl.pallas_call(
        _ah_out_kernel,
        grid=(rows // tm,),
        in_specs=[pl.BlockSpec((tm, D_MODEL), lambda i: (i, 0)),
                  pl.BlockSpec((tm, A_Q), lambda i: (i, 0)),
                  pl.BlockSpec((tm, HY_CH), lambda i: (i, 0)),
                  pl.BlockSpec((A_Q + HY_CH, D_MODEL), lambda i: (0, 0)),
                  pl.BlockSpec((1, 8, D_MODEL), lambda i: (i * tm // rows_per_mod, 0, 0)),
                  pl.BlockSpec((8, D_MODEL), lambda i: (0, 0))],
        out_specs=pl.BlockSpec((tm, D_MODEL), lambda i: (i, 0)),
        out_shape=jax.ShapeDtypeStruct((rows, D_MODEL), F32),
        compiler_params=_cparams("arbitrary"),
        name="ah_out",
    )(x, attn, hyo, w, mod, gains)


def _ffn_kernel(*refs, seq_len, halo, tf):
    if halo:
        x_ref, xb_ref, xa_ref, mod_ref, gain_ref, wu_ref, conv_ref, wd_ref, o_ref = refs
    else:
        x_ref, mod_ref, gain_ref, wu_ref, conv_ref, wd_ref, o_ref = refs
    tm = x_ref.shape[0]

    def modulated(x):
        return _modulated(x, gain_ref[2:3, :], mod_ref[0, 3:4, :], mod_ref[0, 4:5, :]).astype(BF16)

    x = x_ref[...]
    xn = modulated(x)
    if halo:
        xn = jnp.concatenate([xn, modulated(jnp.concatenate([xb_ref[...], xa_ref[...]], axis=0))], axis=0)
    out = None
    for t in range(D_FF // tf):
        cg = slice(t * tf, (t + 1) * tf)
        cl = slice(D_FF + t * tf, D_FF + (t + 1) * tf)
        yg = _dot(xn, wu_ref[0, :, cg])
        yl = _dot(xn, wu_ref[0, :, cl])
        edge_g = edge_l = None
        if halo:
            yg, edge_g = _halo_edges(yg, tm, seq_len)
            yl, edge_l = _halo_edges(yl, tm, seq_len)
        hg = _dwconv3(yg, conv_ref.at[:, cg], seq_len, edge_g)
        hl = _dwconv3(yl, conv_ref.at[:, cl], seq_len, edge_l)
        act = (0.5 * hg * (1.0 + lax.erf(hg * INV_SQRT2))) * hl
        part = _dot(act.astype(BF16), wd_ref[0, cg, :])
        out = part if out is None else out + part
    o_ref[...] = x + mod_ref[0, 5:6, :] * _rms(out, gain_ref[3:4, :])


def _ffn_call(x, mod, gains, w_up, conv, w_down, layer, *, tm, tf, seq_len, rows_per_mod):
    rows = x.shape[0]
    halo = tm < seq_len
    assert (seq_len % tm == 0) if halo else (tm % seq_len == 0)
    x_specs = [pl.BlockSpec((tm, D_MODEL), lambda i: (i, 0))] + (_halo_specs(tm, rows) if halo else [])
    x_args = [x, x, x] if halo else [x]
    resident = pl.Buffered(1)
    return pl.pallas_call(
        functools.partial(_ffn_kernel, seq_len=seq_len, halo=halo, tf=tf),
        grid=(rows // tm,),
        in_specs=x_specs + [
            pl.BlockSpec((1, 8, D_MODEL), lambda i: (i * tm // rows_per_mod, 0, 0)),
            pl.BlockSpec((8, D_MODEL), lambda i: (0, 0)),
            pl.BlockSpec((1, D_MODEL, 2 * D_FF), lambda i: (layer, 0, 0), pipeline_mode=resident),
            pl.BlockSpec((8, 2 * D_FF), lambda i: (0, 0)),
            pl.BlockSpec((1, D_FF, D_MODEL), lambda i: (layer, 0, 0), pipeline_mode=resident)],
        out_specs=pl.BlockSpec((tm, D_MODEL), lambda i: (i, 0)),
        out_shape=jax.ShapeDtypeStruct((rows, D_MODEL), F32),
        compiler_params=_cparams("arbitrary"),
        name="ffn",
    )(*x_args, mod, gains, w_up, conv, w_down)


def _halo_specs(tm, rows):
    sub = 8
    per = tm // sub
    return [pl.BlockSpec((sub, D_MODEL), lambda i, *_: (jnp.maximum(i * per - 1, 0), 0)),
            pl.BlockSpec((sub, D_MODEL), lambda i, *_: (jnp.minimum((i + 1) * per, rows // sub - 1), 0))]


def _halo_edges(y, tm, seq_len):
    i = pl.program_id(0)
    at_start = (i * tm) % seq_len == 0
    at_end = ((i + 1) * tm) % seq_len == 0
    edge = (jnp.where(at_start, 0.0, y[tm + 7:tm + 8, :]), jnp.where(at_end, 0.0, y[tm + 8:tm + 9, :]))
    return y[0:tm, :], edge


def _ml_inproj_kernel(*refs, seq_len, cw, halo):
    if halo:
        x_ref, xb_ref, xa_ref, mod_ref, gain_ref, w_ref, wg_ref, bg_ref, conv_ref = refs[:9]
    else:
        x_ref, mod_ref, gain_ref, w_ref, wg_ref, bg_ref, conv_ref = refs[:7]
    q_ref, kt_ref, v_ref, o_ref, g_ref = refs[-5:]
    tm = x_ref.shape[0]

    def modulated(x):
        return _modulated(x, gain_ref[0:1, :], mod_ref[0, 0:1, :], mod_ref[0, 1:2, :]).astype(BF16)

    xn = modulated(x_ref[...])
    g_ref[...] = (_dot(xn, wg_ref[...]) + bg_ref[...]).T
    xe = xn
    if halo:
        xe = jnp.concatenate([xn, modulated(jnp.concatenate([xb_ref[...], xa_ref[...]], axis=0))], axis=0)

    for t in range(2 * ML_W // cw):
        cols = slice(t * cw, (t + 1) * cw)
        y = _dot(xe, w_ref[:, cols])
        edge = None
        if halo:
            y, edge = _halo_edges(y, tm, seq_len)
        z = _dwconv3(y, conv_ref.at[:, cols], seq_len, edge)
        z = z * jax.nn.sigmoid(z)
        if t * cw < ML_W:
            q_ref[:, cols] = z.astype(BF16)
        else:
            kt_ref[t * cw - ML_W:(t + 1) * cw - ML_W, :] = (z * (ML_HEAD_DIM ** -0.5)).T.astype(BF16)
    for t in range(2 * ML_W // cw):
        y = _dot(xn, w_ref[:, 2 * ML_W + t * cw:2 * ML_W + (t + 1) * cw]).astype(BF16)
        if t * cw < ML_W:
            v_ref[:, t * cw:(t + 1) * cw] = y
        else:
            o_ref[:, t * cw - ML_W:(t + 1) * cw - ML_W] = y


def _ml_inproj_call(x, mod, gains, w, wg, bg, conv, *, tm, cw, seq_len, rows_per_mod):
    rows = x.shape[0]
    n = 4 * ML_W
    halo = tm < seq_len
    assert (seq_len % tm == 0) if halo else (tm % seq_len == 0)
    x_specs = [pl.BlockSpec((tm, D_MODEL), lambda i: (i, 0))] + (_halo_specs(tm, rows) if halo else [])
    return pl.pallas_call(
        functools.partial(_ml_inproj_kernel, seq_len=seq_len, cw=cw, halo=halo),
        grid=(rows // tm,),
        in_specs=x_specs + [
            pl.BlockSpec((1, 8, D_MODEL), lambda i: (i * tm // rows_per_mod, 0, 0)),
            pl.BlockSpec((8, D_MODEL), lambda i: (0, 0)),
            pl.BlockSpec((D_MODEL, n), lambda i: (0, 0)),
            pl.BlockSpec((D_MODEL, LANES), lambda i: (0, 0)),
            pl.BlockSpec((1, LANES), lambda i: (0, 0)),
            pl.BlockSpec((8, n), lambda i: (0, 0))],
        out_specs=[pl.BlockSpec((tm, ML_W), lambda i: (i, 0)),
                   pl.BlockSpec((ML_W, tm), lambda i: (0, i)),
                   pl.BlockSpec((tm, ML_W), lambda i: (i, 0)),
                   pl.BlockSpec((tm, ML_W), lambda i: (i, 0)),
                   pl.BlockSpec((LANES, tm), lambda i: (0, i))],
        out_shape=[jax.ShapeDtypeStruct((rows, ML_W), BF16), jax.ShapeDtypeStruct((ML_W, rows), BF16),
                   jax.ShapeDtypeStruct((rows, ML_W), BF16), jax.ShapeDtypeStruct((rows, ML_W), BF16),
                   jax.ShapeDtypeStruct((LANES, rows), F32)],
        compiler_params=_cparams("arbitrary"),
        name="ml_inproj",
    )(*([x, x, x] if halo else [x]), mod, gains, w, wg, bg, conv)


def _log_sigmoid(x):
    return jnp.minimum(x, 0.0) - jnp.log1p(jnp.exp(-jnp.abs(x)))


def _lane_scan(x, op, reverse, fill):
    n = x.shape[1]
    lane = lax.broadcasted_iota(jnp.int32, x.shape, 1)
    sh = 1
    while sh < n:
        if reverse:
            x = op(x, jnp.where(lane < n - sh, pltpu.roll(x, n - sh, 1), fill))
        else:
            x = op(x, jnp.where(lane >= sh, pltpu.roll(x, sh, 1), fill))
        sh *= 2
    return x


def _bcast_selectors(chunk):
    sel = np.zeros((ML_HEADS, LANES, chunk + LANES), np.float32)
    for h in range(ML_HEADS):
        for g in range(3):
            sel[h, g * ML_HEADS + h, :chunk] = 1.0
            sel[h, (3 + g) * ML_HEADS + h, chunk:] = 1.0
    return jnp.asarray(sel, BF16)


def _ml_scan_kernel(*refs, chunk, has_init):
    if has_init:
        (qf, ktf, vf, qb, ktb, vb, colf, colb, rowf, rowb, sel_ref, c0_ref, n0_ref,
         hf_ref, hb_ref, c_ref, n_ref, s_sc, p16_sc, e_sc) = refs
    else:
        (qf, ktf, vf, qb, ktb, vb, colf, colb, rowf, rowb, sel_ref,
         hf_ref, hb_ref, c_ref, n_ref, s_sc, p16_sc, e_sc) = refs
    nh, dh = ML_HEADS, ML_HEAD_DIM
    step = pl.program_id(1)

    @pl.when(step == 0)
    def _():
        if has_init:
            for d in range(2):
                for h in range(nh):
                    s_sc[d, h, :, 0:dh] = c0_ref[0, d, h].T
                    s_sc[d, h, :, dh:2 * dh] = jnp.broadcast_to(n0_ref[0, d, h:h + 1, :], (dh, dh)).T
        else:
            s_sc[...] = jnp.zeros_like(s_sc)

    row = lax.broadcasted_iota(jnp.int32, (chunk, chunk), 0)
    col = lax.broadcasted_iota(jnp.int32, (chunk, chunk), 1)
    ones_cols = jnp.ones((chunk, dh), BF16)

    for d, (q_ref, kt_ref, v_ref, col_ref, row_ref, h_ref) in enumerate(
            ((qf, ktf, vf, colf, rowf, hf_ref), (qb, ktb, vb, colb, rowb, hb_ref))):
        reverse = d == 1
        cols = col_ref[0, 0, 0]
        r = row_ref[0, 0, 0, 0:nh, :]
        w_tok = row_ref[0, 0, 0, nh:2 * nh, :]
        w_state = row_ref[0, 0, 0, 2 * nh:3 * nh, :]
        m_prev = row_ref[0, 0, 0, 3 * nh:4 * nh, :]
        mask = (col >= row) if reverse else (col <= row)
        for h in range(nh):
            sl = slice(h * dh, (h + 1) * dh)
            bc = _dot(cols, sel_ref[h])
            p_bc = bc[:, :chunk]
            w = jnp.exp(jnp.where(mask, p_bc + r[h:h + 1, :], NEG_BIG))
            p16_sc[d, h] = (_dot(q_ref[0, :, sl], kt_ref[sl, :]) * w).astype(BF16)
            e_sc[d, h, :, 0:dh] = jnp.exp(p_bc[:, :dh] + m_prev[h:h + 1, 0:dh])
            e_sc[d, h, :, dh:2 * dh] = jnp.exp(bc[:, chunk:])
        for h in range(nh):
            sl = slice(h * dh, (h + 1) * dh)
            q16 = q_ref[0, :, sl]
            kt16 = kt_ref[sl, :]
            vaug = jnp.concatenate([v_ref[0, :, sl], ones_cols], axis=1)
            s_old = s_sc[d, h]
            w_inter = e_sc[d, h, :, 0:dh]
            tot = (_dot(p16_sc[d, h], vaug)
                   + jnp.concatenate([w_inter, w_inter], axis=1) * _dot(q16, s_old.astype(BF16)))
            h_ref[0, :, sl] = (tot[:, :dh] / jnp.maximum(jnp.abs(tot[:, dh:]),
                                                         e_sc[d, h, :, dh:2 * dh])).astype(BF16)
            kts = (kt16.astype(F32) * w_tok[h:h + 1, :]).astype(BF16)
            ws = w_state[h:h + 1, 0:dh]
            s_sc[d, h] = jnp.concatenate([ws, ws], axis=1) * s_old + _dot(kts, vaug)

    @pl.when(step == pl.num_programs(1) - 1)
    def _():
        for d in range(2):
            for h in range(nh):
                s_fin = s_sc[d, h]
                c_ref[0, d, h] = s_fin[:, 0:dh].T
                n_ref[0, d, h:h + 1, :] = s_fin[:, dh:2 * dh].T[0:1, :]


def _ml_gate_kernel(*refs, nc, chunk, bb, has_init):
    if has_init:
        gt_ref, m0_ref, cols_ref, rows_ref, m_ref = refs
    else:
        gt_ref, cols_ref, rows_ref, m_ref = refs
    nh = ML_HEADS
    pad = jnp.zeros((LANES - 6 * nh, chunk), F32)
    for d in range(2):
        reverse = d == 1
        last = 0 if reverse else chunk - 1
        for bi in range(bb):
            m = m0_ref[bi, d] if has_init else jnp.zeros((nh, chunk), F32)
            for c in (range(nc - 1, -1, -1) if reverse else range(nc)):
                c0 = (bi * nc + c) * chunk
                gt = gt_ref[0:4 * nh, c0:c0 + chunk]
                b = _lane_scan(_log_sigmoid(gt[nh * (2 + d):nh * (3 + d), :]), jnp.add, reverse, 0.0)
                r = gt[nh * d:nh * (d + 1), :] - b
                mx = jnp.maximum(m, _lane_scan(r, jnp.maximum, reverse, -jnp.inf))
                mt = b + mx
                m_new = jnp.broadcast_to(mt[:, last:last + 1], (nh, chunk))
                b_last = jnp.broadcast_to(b[:, last:last + 1], (nh, chunk))
                rows_ref[bi, c, d, 0:nh, :] = r
                rows_ref[bi, c, d, nh:2 * nh, :] = jnp.exp(r + b_last - m_new)
                rows_ref[bi, c, d, 2 * nh:3 * nh, :] = jnp.exp(b_last + m - m_new)
                rows_ref[bi, c, d, 3 * nh:4 * nh, :] = m
                stack = []
                for x in (-mx, -mt):
                    stack += [t.astype(F32) for t in _split3(x)]
                stack.append(pad)
                cols_ref[bi, c, d] = jnp.concatenate(stack, axis=0).T.astype(BF16)
                m = m_new
            m_ref[bi, d] = m[:, 0:LANES]


def _ml_gate_call(gates, m0, *, batch, seq_len, chunk):
    nc = seq_len // chunk
    bb = max(1, 16 // nc)
    nh = ML_HEADS
    has_init = m0 is not None
    in_specs = [pl.BlockSpec((LANES, bb * seq_len), lambda i: (0, i))]
    args = [gates]
    if has_init:
        in_specs.append(pl.BlockSpec((bb, 2, nh, chunk), lambda i: (i, 0, 0, 0)))
        args.append(jnp.broadcast_to(m0[..., None], m0.shape + (chunk,)))
    return pl.pallas_call(
        functools.partial(_ml_gate_kernel, nc=nc, chunk=chunk, bb=bb, has_init=has_init),
        grid=(batch // bb,),
        in_specs=in_specs,
        out_specs=[pl.BlockSpec((bb, nc, 2, chunk, LANES), lambda i: (i, 0, 0, 0, 0)),
                   pl.BlockSpec((bb, nc, 2, 4 * nh, chunk), lambda i: (i, 0, 0, 0, 0)),
                   pl.BlockSpec((bb, 2, nh, LANES), lambda i: (i, 0, 0, 0))],
        out_shape=[jax.ShapeDtypeStruct((batch, nc, 2, chunk, LANES), BF16),
                   jax.ShapeDtypeStruct((batch, nc, 2, 4 * nh, chunk), F32),
                   jax.ShapeDtypeStruct((batch, 2, nh, LANES), F32)],
        compiler_params=_cparams("arbitrary"),
        name="ml_gate",
    )(*args)


def _ml_scan_call(q, kt, v, cols, rows, init, *, batch, seq_len, chunk):
    nc = seq_len // chunk
    nh = ML_HEADS
    q3 = q.reshape(batch, seq_len, ML_W)
    v3 = v.reshape(batch, seq_len, ML_W)
    sel = _bcast_selectors(chunk)
    has_init = init is not None

    fwd3 = lambda b, i: (b, i, 0)
    bwd3 = lambda b, i: (b, nc - 1 - i, 0)
    in_specs = [pl.BlockSpec((1, chunk, ML_W), fwd3),
                pl.BlockSpec((ML_W, chunk), lambda b, i: (0, b * nc + i)),
                pl.BlockSpec((1, chunk, ML_W), fwd3),
                pl.BlockSpec((1, chunk, ML_W), bwd3),
                pl.BlockSpec((ML_W, chunk), lambda b, i: (0, b * nc + nc - 1 - i)),
                pl.BlockSpec((1, chunk, ML_W), bwd3),
                pl.BlockSpec((1, 1, 1, chunk, LANES), lambda b, i: (b, i, 0, 0, 0)),
                pl.BlockSpec((1, 1, 1, chunk, LANES), lambda b, i: (b, nc - 1 - i, 1, 0, 0)),
                pl.BlockSpec((1, 1, 1, 4 * nh, chunk), lambda b, i: (b, i, 0, 0, 0)),
                pl.BlockSpec((1, 1, 1, 4 * nh, chunk), lambda b, i: (b, nc - 1 - i, 1, 0, 0)),
                pl.BlockSpec(sel.shape, lambda b, i: (0, 0, 0))]
    args = [q3, kt, v3, q3, kt, v3, cols, cols, rows, rows, sel]
    c_spec = pl.BlockSpec((1, 2, nh, ML_HEAD_DIM, ML_HEAD_DIM), lambda b, i: (b, 0, 0, 0, 0))
    n_spec = pl.BlockSpec((1, 2, nh, ML_HEAD_DIM), lambda b, i: (b, 0, 0, 0))
    if has_init:
        in_specs += [c_spec, n_spec]
        args += list(init)
    return pl.pallas_call(
        functools.partial(_ml_scan_kernel, chunk=chunk, has_init=has_init),
        grid=(batch, nc),
        in_specs=in_specs,
        out_specs=[pl.BlockSpec((1, chunk, ML_W), fwd3),
                   pl.BlockSpec((1, chunk, ML_W), bwd3),
                   c_spec, n_spec],
        out_shape=[jax.ShapeDtypeStruct((batch, seq_len, ML_W), BF16),
                   jax.ShapeDtypeStruct((batch, seq_len, ML_W), BF16),
                   jax.ShapeDtypeStruct((batch, 2, nh, ML_HEAD_DIM, ML_HEAD_DIM), F32),
                   jax.ShapeDtypeStruct((batch, 2, nh, ML_HEAD_DIM), F32)],
        scratch_shapes=[pltpu.VMEM((2, nh, ML_HEAD_DIM, 2 * ML_HEAD_DIM), F32),
                        pltpu.VMEM((2, nh, chunk, chunk), BF16),
                        pltpu.VMEM((2, nh, chunk, 2 * ML_HEAD_DIM), F32)],
        compiler_params=_cparams("arbitrary", "arbitrary"),
        name="ml_scan",
    )(*args)


def _ml_out_kernel(x_ref, hf_ref, hb_ref, og_ref, hn_ref, w_ref, mod_ref, gain_ref, o_ref):
    h = hf_ref[...].astype(F32) + hb_ref[...].astype(F32)
    parts = []
    for hd in range(ML_HEADS):
        sl = slice(hd * ML_HEAD_DIM, (hd + 1) * ML_HEAD_DIM)
        parts.append(_rms(h[:, sl], hn_ref[0:1, sl]))
    a = jnp.concatenate(parts, axis=1) * jax.nn.sigmoid(og_ref[...].astype(F32))
    out = _dot(a.astype(BF16), w_ref[...])
    o_ref[...] = x_ref[...] + mod_ref[0, 2:3, :] * _rms(out, gain_ref[1:2, :])


def _ml_out_call(x, hf, hb, og, head_norm, w, mod, gains, *, tm, rows_per_mod):
    rows = x.shape[0]
    return pl.pallas_call(
        _ml_out_kernel,
        grid=(rows // tm,),
        in_specs=[pl.BlockSpec((tm, D_MODEL), lambda i: (i, 0)),
                  pl.BlockSpec((tm, ML_W), lambda i: (i, 0)),
                  pl.BlockSpec((tm, ML_W), lambda i: (i, 0)),
                  pl.BlockSpec((tm, ML_W), lambda i: (i, 0)),
                  pl.BlockSpec((1, ML_W), lambda i: (0, 0)),
                  pl.BlockSpec((ML_W, D_MODEL), lambda i: (0, 0)),
                  pl.BlockSpec((1, 8, D_MODEL), lambda i: (i * tm // rows_per_mod, 0, 0)),
                  pl.BlockSpec((8, D_MODEL), lambda i: (0, 0))],
        out_specs=pl.BlockSpec((tm, D_MODEL), lambda i: (i, 0)),
        out_shape=jax.ShapeDtypeStruct((rows, D_MODEL), F32),
        compiler_params=_cparams("arbitrary"),
        name="ml_out",
    )(x, hf, hb, og, head_norm, w, mod, gains)


@functools.lru_cache(maxsize=None)
def _dft_tables_np(seq_len):
    n4 = 4 * seq_len
    ar = np.arange(seq_len, dtype=np.int64)
    idx = ((2 * ar + 1)[:, None] * ar[None, :]) % n4
    ang = (2.0 * np.pi / n4) * idx.astype(np.float64)
    cm = np.cos(ang).astype(np.float32)
    sm = np.sin(ang).astype(np.float32)
    return tuple(np.ascontiguousarray(a) for a in (cm, sm, cm.T, sm.T))


def _dft_tables(seq_len):
    return tuple(jnp.asarray(a).astype(BF16) for a in _dft_tables_np(seq_len))


def _hy_features(seq_len):
    t = np.linspace(0.0, 1.0, seq_len, dtype=np.float32)
    bands = np.arange(1, HY_BANDS + 1, dtype=np.float32)
    ang = (np.float32(2.0 * np.pi) * t[:, None]) * bands
    z = np.concatenate([t[:, None], np.cos(ang), np.sin(ang)], axis=-1).astype(np.float32)
    z = np.pad(z, ((0, 0), (0, 32 - HY_EMB)))
    deltas = np.abs(np.linspace(HY_MIN_DECAY, HY_MAX_DECAY, HY_CH, dtype=np.float32))[None, :]
    return jnp.asarray(z), jnp.asarray(deltas)


def _rope_tables(seq_len):
    rows = seq_len // GRID_W
    row = np.repeat(np.arange(rows, dtype=np.float32), GRID_W)
    col = np.tile(np.arange(GRID_W, dtype=np.float32), rows)
    n_freq = A_HEAD_DIM // 4
    inv = (np.float32(ROPE_THETA) ** (-np.arange(n_freq, dtype=np.float32) / n_freq)).astype(np.float32)
    ang = np.concatenate([row[:, None] * inv, col[:, None] * inv], axis=-1).astype(np.float32)
    cos, sin = np.cos(ang), np.sin(ang)
    return (jnp.asarray(np.concatenate([cos, cos], axis=-1), F32),
            jnp.asarray(np.concatenate([-sin, sin], axis=-1), F32))


def _pad_rows(a, rows=8):
    return jnp.pad(a, ((0, rows - a.shape[0]), (0, 0)))


def kernel(x_prompt, x_sample, cache_attn_k, cache_attn_v, state_mlstm_C, state_mlstm_n, state_mlstm_m, c, c_ctx, w_mod, b_mod, norm_mix_pre, norm_mix_post, norm_ffn_pre, norm_ffn_post, ffn_w_up, ffn_conv_w, ffn_conv_b, ffn_w_down, ah_w_in, ah_w_out, attn_q_norm, attn_k_norm, hy_conv_w, hy_conv_b, hy_w1, hy_b1, hy_w2, hy_b2, hy_w3, hy_b3, hy_sin_freq, hy_skip, ml_w_in, ml_b_gates, ml_conv_w, ml_conv_b, ml_head_norm, ml_w_out):
    bp, lp, _ = x_prompt.shape
    bs, ls, _ = x_sample.shape
    past = cache_attn_k.shape[2]
    xp = x_prompt.reshape(bp * lp, D_MODEL)
    xs = x_sample.reshape(bs * ls, D_MODEL)
    groups = {
        "p": dict(batch=bp, seq_len=lp, tm=1024, rows_per_mod=bp * lp, ffn_tm=512, ffn_tf=D_FF // 2,
                  seq_bb=4, hy_tc=HY_CH, hy_tk=256),
        "s": dict(batch=bs, seq_len=ls, tm=ls, rows_per_mod=ls, ffn_tm=512, ffn_tf=D_FF // 2,
                  seq_bb=1, hy_tc=COL_TILE, hy_tk=512),
    }
    ffn_up16 = ffn_w_up.astype(BF16)
    ffn_down16 = ffn_w_down.astype(BF16)

    cc = jnp.concatenate([c, c_ctx[None, :], jnp.zeros((8 - bs - 1, D_MODEL), F32)], axis=0)
    mod_all = _mod_call(cc, w_mod, b_mod).reshape(DEPTH, 8, 6, D_MODEL)
    mod_all = jnp.pad(mod_all, ((0, 0), (0, 0), (0, 2), (0, 0)))
    rope = _rope_tables(ls)

    new_k = new_v = new_c = new_n = new_m = None
    for l in range(DEPTH):
        j = l // 2
        mods = {"s": mod_all[l, :bs], "p": mod_all[l, bs:bs + 1]}
        gains = _pad_rows(jnp.stack([norm_mix_pre[l], norm_mix_post[l], norm_ffn_pre[l], norm_ffn_post[l]]))
        xin = {"p": xp, "s": xs}
        xmid = {}
        if l % 2 == 0:
            w_in = ah_w_in[j].astype(BF16)
            w_out = ah_w_out[j].astype(BF16)
            qkn = _pad_rows(jnp.stack([attn_q_norm[j], attn_k_norm[j]]))
            conv = jnp.pad(_pad_rows(jnp.concatenate([hy_conv_w[j], hy_conv_b[j][None, :]], axis=0)),
                           ((0, 0), (A_Q + 2 * A_KV, 0)))
            w1 = jnp.pad(hy_w1[j], ((0, 32 - HY_EMB), (0, 0)))
            sf = _pad_rows(hy_sin_freq[j])
            for g, cfg in groups.items():
                batch, seq_len, tm, rpm = cfg["batch"], cfg["seq_len"], cfg["tm"], cfg["rows_per_mod"]
                outs = _ah_inproj_call(xin[g], mods[g], gains, w_in, qkn, conv, rope if g == "s" else None,
                                       tm=512, seq_len=seq_len, rows_per_mod=rpm)
                q, k, v, u = outs[:4]
                if g == "s":
                    ctx_k = cache_attn_k[:, j].reshape(bs, past, A_KV).astype(BF16)
                    ctx_v = cache_attn_v[:, j].reshape(bs, past, A_KV).astype(BF16)
                else:
                    ctx_k = ctx_v = None
                    new_k = outs[4].reshape(bp, 1, lp, A_KV_HEADS, A_HEAD_DIM)
                    new_v = outs[5].reshape(bp, 1, lp, A_KV_HEADS, A_HEAD_DIM)
                attn = _attn_call(q, k, v, ctx_k, ctx_v, batch=batch, seq_len=seq_len, tq=256, bb=cfg["seq_bb"])
                z, deltas = _hy_features(seq_len)
                dft = _dft_tables(seq_len)
                tk = cfg["hy_tk"]
                gr, gi = _hy_filter_call(z, w1, hy_b1[j][None, :], hy_w2[j], hy_b2[j][None, :], hy_w3[j],
                                         hy_b3[j][None, :], sf, deltas, dft[0], dft[1], seq_len=seq_len, tk=tk)
                hyo = _hy_conv_call(u, hy_skip[j][None, :], dft, gr, gi, batch=batch, seq_len=seq_len,
                                    tc=cfg["hy_tc"], tk=tk, bb=cfg["seq_bb"])
                xmid[g] = _ah_out_call(xin[g], attn.reshape(batch * seq_len, A_Q),
                                       hyo.reshape(batch * seq_len, HY_CH), w_out, mods[g], gains,
                                       tm=512, rows_per_mod=rpm)
        else:
            w_in = ml_w_in[j].astype(BF16)
            w_g = jnp.pad(ml_w_in[j][:, 4 * ML_W:], ((0, 0), (0, LANES - 4 * ML_HEADS))).astype(BF16)
            b_g = jnp.pad(ml_b_gates[j], (0, LANES - 4 * ML_HEADS))[None, :]
            w_out = ml_w_out[j].astype(BF16)
            conv = jnp.pad(_pad_rows(jnp.concatenate([ml_conv_w[j], ml_conv_b[j][None, :]], axis=0)),
                           ((0, 0), (0, 2 * ML_W)))
            for g, cfg in groups.items():
                batch, seq_len, tm, rpm = cfg["batch"], cfg["seq_len"], cfg["tm"], cfg["rows_per_mod"]
                q, kt, v, og, gates = _ml_inproj_call(xin[g], mods[g], gains, w_in, w_g, b_g, conv, tm=512,
                                                      cw=512, seq_len=seq_len, rows_per_mod=rpm)
                if g == "s":
                    init, m0 = (state_mlstm_C[:, j], state_mlstm_n[:, j]), state_mlstm_m[:, j]
                else:
                    init = m0 = None
                cols, rows, m_new = _ml_gate_call(gates, m0, batch=batch, seq_len=seq_len, chunk=ML_SCAN_CHUNK)
                hf, hb, c_new, n_new = _ml_scan_call(q, kt, v, cols, rows, init, batch=batch, seq_len=seq_len,
                                                     chunk=ML_SCAN_CHUNK)
                if g == "p":
                    new_c, new_n, new_m = c_new[:, None], n_new[:, None], m_new[:, None, :, :, 0]
                xmid[g] = _ml_out_call(xin[g], hf.reshape(batch * seq_len, ML_W), hb.reshape(batch * seq_len, ML_W),
                                       og, ml_head_norm[j][None, :], w_out, mods[g], gains,
                                       tm=512, rows_per_mod=rpm)
        conv = _pad_rows(jnp.concatenate([ffn_conv_w[l], ffn_conv_b[l][None, :]], axis=0))
        xout = {}
        for g, cfg in groups.items():
            xout[g] = _ffn_call(xmid[g], mods[g], gains, ffn_up16, conv, ffn_down16, l, tm=cfg["ffn_tm"],
                                tf=cfg["ffn_tf"], seq_len=cfg["seq_len"], rows_per_mod=cfg["rows_per_mod"])
        xp, xs = xout["p"], xout["s"]

    return (xp.reshape(bp, lp, D_MODEL), xs.reshape(bs, ls, D_MODEL), new_k, new_v, new_c, new_n, new_m)
```

```python
import functools
import math

import numpy as np
import jax
import jax.numpy as jnp
from jax import lax
from jax.experimental import pallas as pl
from jax.experimental.pallas import tpu as pltpu

F32 = jnp.float32
BF16 = jnp.bfloat16

D_MODEL = 1024
DEPTH = 2
GRID_W = 64
A_HEADS = 4
A_KV_HEADS = 2
A_HEAD_DIM = 128
A_Q = A_HEADS * A_HEAD_DIM
A_KV = A_KV_HEADS * A_HEAD_DIM
ROPE_THETA = 10000.0
HY_CH = D_MODEL // 2
HY_BANDS = 8
HY_EMB = 1 + 2 * HY_BANDS
HY_W = 64
HY_TARGET = 1e-2
HY_FAST_PCT = 0.3
HY_SLOW_PCT = 1.5
HY_MAX_DECAY = math.log(HY_TARGET) / HY_FAST_PCT
HY_MIN_DECAY = math.log(HY_TARGET) / HY_SLOW_PCT
AH_IN = A_Q + 2 * A_KV + 3 * HY_CH
ML_HEADS = 8
ML_HEAD_DIM = D_MODEL // ML_HEADS
ML_W = ML_HEADS * ML_HEAD_DIM
D_FF = 2816
NORM_EPS = 1e-6
NEG_BIG = -1e30

LANES = 128
VMEM_LIMIT = 56 * 1024 * 1024
COL_TILE = 256
ML_SCAN_CHUNK = 128
INV_SQRT2 = 1.0 / math.sqrt(2.0)


def _cparams(*sem):
    return pltpu.CompilerParams(dimension_semantics=sem, vmem_limit_bytes=VMEM_LIMIT)


def _rms(x, gain):
    return x * lax.rsqrt(jnp.mean(x * x, axis=-1, keepdims=True) + NORM_EPS) * gain


def _modulated(x, gain, shift, scale):
    return _rms(x, gain) * (1.0 + scale) + shift


def _dot(a, b):
    return jnp.dot(a, b, preferred_element_type=F32)


def _dot_nt(a, b):
    return lax.dot_general(a, b, (((1,), (1,)), ((), ())), preferred_element_type=F32)


def _dot_tn(a, b):
    return lax.dot_general(a, b, (((0,), (0,)), ((), ())), preferred_element_type=F32)


def _split3(x):
    hi = x.astype(BF16)
    r1 = x - hi.astype(F32)
    mid = r1.astype(BF16)
    lo = (r1 - mid.astype(F32)).astype(BF16)
    return hi, mid, lo


def _dot_f32(a, b):
    a0, a1, a2 = _split3(a)
    b0, b1, b2 = _split3(b)
    return (_dot(a0, b0) + (_dot(a0, b1) + _dot(a1, b0))
            + (_dot(a0, b2) + _dot(a1, b1) + _dot(a2, b0)))


def _dwconv3(y, conv_ref, seq_len, edge=None):
    rows, cols = y.shape
    pos = lax.broadcasted_iota(jnp.int32, (rows, LANES), 0)
    if edge is None:
        pos = pos % seq_len
        first = pos == 0
        last = pos == seq_len - 1
    else:
        first = pos == 0
        last = pos == rows - 1
    outs = []
    for c0 in range(0, cols, LANES):
        yc = y[:, c0:c0 + LANES]
        before = 0.0 if edge is None else edge[0][:, c0:c0 + LANES]
        after = 0.0 if edge is None else edge[1][:, c0:c0 + LANES]
        prev = jnp.where(first, before, pltpu.roll(yc, 1, 0))
        nxt = jnp.where(last, after, pltpu.roll(yc, rows - 1, 0))
        outs.append(prev * conv_ref[0:1, c0:c0 + LANES] + yc * conv_ref[1:2, c0:c0 + LANES]
                    + nxt * conv_ref[2:3, c0:c0 + LANES] + conv_ref[3:4, c0:c0 + LANES])
    return outs[0] if len(outs) == 1 else jnp.concatenate(outs, axis=1)


def _mod_kernel(c_ref, w_ref, b_ref, o_ref):
    a = c_ref[...]
    a = a * jax.nn.sigmoid(a)
    o_ref[0] = _dot(a.astype(BF16), w_ref[0].astype(BF16)) + b_ref[0]


def _mod_call(cc, w_mod, b_mod):
    tn = 1536
    n = 6 * D_MODEL
    return pl.pallas_call(
        _mod_kernel,
        grid=(DEPTH, n // tn),
        in_specs=[pl.BlockSpec((8, D_MODEL), lambda l, j: (0, 0)),
                  pl.BlockSpec((1, D_MODEL, tn), lambda l, j: (l, 0, j)),
                  pl.BlockSpec((1, 1, tn), lambda l, j: (l, 0, j))],
        out_specs=pl.BlockSpec((1, 8, tn), lambda l, j: (l, 0, j)),
        out_shape=jax.ShapeDtypeStruct((DEPTH, 8, n), F32),
        compiler_params=_cparams("arbitrary", "arbitrary"),
        name="mod",
    )(cc, w_mod, b_mod.reshape(DEPTH, 1, n))


def _ah_inproj_kernel(*refs, seq_len, rope, halo):
    it = iter(refs)
    x_ref = next(it)
    xb_ref, xa_ref = (next(it), next(it)) if halo else (None, None)
    mod_ref, gain_ref, w_ref, qkn_ref, conv_ref = (next(it) for _ in range(5))
    cc_ref, ss_ref = (next(it), next(it)) if rope else (None, None)
    q_ref, k_ref, v_ref, u_ref = (next(it) for _ in range(4))
    kf_ref, vf_ref = (None, None) if rope else (next(it), next(it))
    tm = x_ref.shape[0]
    dh = A_HEAD_DIM

    def modulated(x):
        return _modulated(x, gain_ref[0:1, :], mod_ref[0, 0:1, :], mod_ref[0, 1:2, :]).astype(BF16)

    def head(yh, g):
        yh = _rms(yh, g)
        if rope:
            yh = yh * cc_ref[...] + pltpu.roll(yh, dh // 2, 1) * ss_ref[...]
        return yh

    xn = modulated(x_ref[...])
    yq = _dot(xn, w_ref[:, 0:A_Q])
    for h in range(A_HEADS):
        sl = slice(h * dh, (h + 1) * dh)
        q_ref[:, sl] = (head(yq[:, sl], qkn_ref[0:1, :]) * (dh ** -0.5)).astype(BF16)
    ykv = _dot(xn, w_ref[:, A_Q:A_Q + 2 * A_KV])
    for h in range(A_KV_HEADS):
        sl = slice(h * dh, (h + 1) * dh)
        kh = head(ykv[:, sl], qkn_ref[1:2, :])
        k_ref[:, sl] = kh.astype(BF16)
        if kf_ref is not None:
            kf_ref[:, sl] = kh
    yv = ykv[:, A_KV:2 * A_KV]
    v_ref[...] = yv.astype(BF16)
    if vf_ref is not None:
        vf_ref[...] = yv

    xe = xn
    if halo:
        xe = jnp.concatenate([xn, modulated(jnp.concatenate([xb_ref[...], xa_ref[...]], axis=0))], axis=0)
    cw = 512
    u0 = A_Q + 2 * A_KV
    for t in range(3 * HY_CH // cw):
        cols = slice(u0 + t * cw, u0 + (t + 1) * cw)
        y = _dot(xe, w_ref[:, cols])
        edge = None
        if halo:
            y, edge = _halo_edges(y, tm, seq_len)
        u_ref[:, t * cw:(t + 1) * cw] = _dwconv3(y, conv_ref.at[:, cols], seq_len, edge).astype(BF16)


def _ah_inproj_call(x, mod, gains, w, qkn, conv, rope_tabs, *, tm, seq_len, rows_per_mod):
    rows = x.shape[0]
    rope = rope_tabs is not None
    halo = tm < seq_len
    assert (seq_len % tm == 0) if halo else (tm % seq_len == 0)
    row_blk = lambda cols: pl.BlockSpec((tm, cols), lambda i: (i, 0))
    out_specs = [row_blk(A_Q), row_blk(A_KV), row_blk(A_KV), row_blk(3 * HY_CH)]
    out_shape = [jax.ShapeDtypeStruct((rows, A_Q), BF16), jax.ShapeDtypeStruct((rows, A_KV), BF16),
                 jax.ShapeDtypeStruct((rows, A_KV), BF16), jax.ShapeDtypeStruct((rows, 3 * HY_CH), BF16)]
    if not rope:
        out_specs += [row_blk(A_KV)] * 2
        out_shape += [jax.ShapeDtypeStruct((rows, A_KV), F32)] * 2
    in_specs = [row_blk(D_MODEL)] + (_halo_specs(tm, rows) if halo else []) + [
        pl.BlockSpec((1, 8, D_MODEL), lambda i: (i * tm // rows_per_mod, 0, 0)),
        pl.BlockSpec((8, D_MODEL), lambda i: (0, 0)),
        pl.BlockSpec((D_MODEL, AH_IN), lambda i: (0, 0)),
        pl.BlockSpec((8, A_HEAD_DIM), lambda i: (0, 0)),
        pl.BlockSpec((8, AH_IN), lambda i: (0, 0))]
    args = ([x, x, x] if halo else [x]) + [mod, gains, w, qkn, conv]
    if rope:
        per_seq = max(seq_len // tm, 1)
        in_specs += [pl.BlockSpec((tm, A_HEAD_DIM), lambda i: (i % per_seq, 0))] * 2
        args += list(rope_tabs)
    return pl.pallas_call(
        functools.partial(_ah_inproj_kernel, seq_len=seq_len, rope=rope, halo=halo),
        grid=(rows // tm,),
        in_specs=in_specs,
        out_specs=out_specs,
        out_shape=out_shape,
        compiler_params=_cparams("arbitrary"),
        name="ah_inproj",
    )(*args)


def _attn_kernel(*refs, ctx, bb):
    if ctx:
        q_ref, k_ref, v_ref, kc_ref, vc_ref, o_ref = refs
    else:
        q_ref, k_ref, v_ref, o_ref = refs
    dh = A_HEAD_DIM
    for bi in range(bb):
        k = k_ref[bi]
        if ctx:
            kc = kc_ref[bi]
        else:
            vaug = jnp.concatenate([v_ref[bi], jnp.ones((k.shape[0], dh), BF16)], axis=1)
        for g in range(A_HEADS // A_KV_HEADS):
            sl = slice(g * dh, (g + 1) * dh)
            q = q_ref[bi, :, sl]
            s = _dot_nt(q, k)
            m = jnp.max(s, axis=-1, keepdims=True)
            if ctx:
                sc = _dot_nt(q, kc)
                m = jnp.maximum(m, jnp.max(sc, axis=-1, keepdims=True))
            if ctx:
                p = jnp.exp(s - m)
                pc = jnp.exp(sc - m)
                den = jnp.sum(p, axis=-1, keepdims=True) + jnp.sum(pc, axis=-1, keepdims=True)
                o = _dot(p.astype(BF16), v_ref[bi]) + _dot(pc.astype(BF16), vc_ref[bi])
                o_ref[bi, :, sl] = (o / den).astype(BF16)
            else:
                o = _dot(jnp.exp(s - m).astype(BF16), vaug)
                o_ref[bi, :, sl] = (o[:, :dh] / o[:, dh:]).astype(BF16)


def _attn_call(q, k, v, ctx_k, ctx_v, *, batch, seq_len, tq, bb):
    ctx = ctx_k is not None
    qw = A_Q // A_KV_HEADS
    in_specs = [pl.BlockSpec((bb, tq, qw), lambda b, h, i: (b, i, h)),
                pl.BlockSpec((bb, seq_len, A_HEAD_DIM), lambda b, h, i: (b, 0, h)),
                pl.BlockSpec((bb, seq_len, A_HEAD_DIM), lambda b, h, i: (b, 0, h))]
    args = [q.reshape(batch, seq_len, A_Q), k.reshape(batch, seq_len, A_KV), v.reshape(batch, seq_len, A_KV)]
    if ctx:
        past = ctx_k.shape[1]
        in_specs += [pl.BlockSpec((bb, past, A_HEAD_DIM), lambda b, h, i: (b, 0, h))] * 2
        args += [ctx_k, ctx_v]
    return pl.pallas_call(
        functools.partial(_attn_kernel, ctx=ctx, bb=bb),
        grid=(batch // bb, A_KV_HEADS, seq_len // tq),
        in_specs=in_specs,
        out_specs=pl.BlockSpec((bb, tq, qw), lambda b, h, i: (b, i, h)),
        out_shape=jax.ShapeDtypeStruct((batch, seq_len, A_Q), BF16),
        compiler_params=_cparams("arbitrary", "arbitrary", "arbitrary"),
        name="attn",
    )(*args)


def _hy_filter_kernel(z_ref, w1_ref, b1_ref, w2_ref, b2_ref, w3_ref, b3_ref, sf_ref, dl_ref,
                      cm_ref, sm_ref, gr_ref, gi_ref, hs_ref, hd_ref):
    @pl.when(pl.program_id(0) == 0)
    def _():
        z = z_ref[...]
        h = jnp.sin(sf_ref[0:1, :] * (_dot_f32(z, w1_ref[...]) + b1_ref[...]))
        h = jnp.sin(sf_ref[1:2, :] * (_dot_f32(h, w2_ref[...]) + b2_ref[...]))
        filt = _dot_f32(h, w3_ref[...]) + b3_ref[...]
        win = jnp.exp(-z[:, 0:1] * dl_ref[...])
        hf = filt[:, :HY_CH] * win
        hb = filt[:, HY_CH:] * win
        hs_ref[...] = (hf + hb).astype(BF16)
        hd_ref[...] = (hb - hf).astype(BF16)

    gr_ref[...] = _dot(cm_ref[...], hs_ref[...])
    gi_ref[...] = _dot(sm_ref[...], hd_ref[...])


def _hy_filter_call(z, w1, b1, w2, b2, w3, b3, sf, deltas, cm, sm, *, seq_len, tk):
    full = lambda a: pl.BlockSpec(a.shape, lambda k: (0,) * a.ndim)
    small = [z, w1, b1, w2, b2, w3, b3, sf, deltas]
    return pl.pallas_call(
        _hy_filter_kernel,
        grid=(seq_len // tk,),
        in_specs=[full(a) for a in small] + [pl.BlockSpec((tk, seq_len), lambda k: (k, 0))] * 2,
        out_specs=[pl.BlockSpec((tk, HY_CH), lambda k: (k, 0))] * 2,
        out_shape=[jax.ShapeDtypeStruct((seq_len, HY_CH), F32)] * 2,
        scratch_shapes=[pltpu.VMEM((seq_len, HY_CH), BF16)] * 2,
        compiler_params=_cparams("arbitrary"),
        name="hy_filter",
    )(*small, cm, sm)


def _hy_conv_kernel(x0_ref, x1_ref, v_ref, skip_ref, cm_ref, sm_ref, cmt_ref, smt_ref, gr_ref, gi_ref,
                    o_ref, vv_ref, acc_ref, *, seq_len, bb):
    kb = pl.program_id(2)
    single = seq_len == cm_ref.shape[0]
    gr = gr_ref[...]
    gi = gi_ref[...]
    for bi in range(bb):
        def gated():
            return v_ref[bi].astype(F32) * x1_ref[bi].astype(F32)

        if single:
            vv = gated().astype(BF16)
        else:
            @pl.when(kb == 0)
            def _():
                vv_ref[bi] = gated().astype(BF16)
                acc_ref[bi] = jnp.zeros_like(acc_ref[bi])

            vv = vv_ref[bi]
        vr = _dot(cm_ref[...], vv)
        wi = _dot(sm_ref[...], vv)
        pr = gr * vr + gi * wi
        qi = gr * wi - gi * vr
        y = _dot(cmt_ref[...], pr.astype(BF16)) + _dot(smt_ref[...], qi.astype(BF16))

        def finish(total):
            o_ref[bi] = ((total * (1.0 / seq_len) + skip_ref[...] * gated())
                         * x0_ref[bi].astype(F32)).astype(BF16)

        if single:
            finish(y)
        else:
            acc_ref[bi] += y

            @pl.when(kb == pl.num_programs(2) - 1)
            def _():
                finish(acc_ref[bi])


def _hy_conv_call(u, skip, dft, gr, gi, *, batch, seq_len, tc, tk, bb):
    proj3 = u.reshape(batch, seq_len, 3 * HY_CH)
    cm, sm, cmt, smt = dft
    nch = HY_CH // tc
    return pl.pallas_call(
        functools.partial(_hy_conv_kernel, seq_len=seq_len, bb=bb),
        grid=(batch // bb, nch, seq_len // tk),
        in_specs=[pl.BlockSpec((bb, seq_len, tc), lambda b, c, k: (b, 0, c)),
                  pl.BlockSpec((bb, seq_len, tc), lambda b, c, k: (b, 0, nch + c)),
                  pl.BlockSpec((bb, seq_len, tc), lambda b, c, k: (b, 0, 2 * nch + c)),
                  pl.BlockSpec((1, tc), lambda b, c, k: (0, c)),
                  pl.BlockSpec((tk, seq_len), lambda b, c, k: (k, 0)),
                  pl.BlockSpec((tk, seq_len), lambda b, c, k: (k, 0)),
                  pl.BlockSpec((seq_len, tk), lambda b, c, k: (0, k)),
                  pl.BlockSpec((seq_len, tk), lambda b, c, k: (0, k)),
                  pl.BlockSpec((tk, tc), lambda b, c, k: (k, c)),
                  pl.BlockSpec((tk, tc), lambda b, c, k: (k, c))],
        out_specs=pl.BlockSpec((bb, seq_len, tc), lambda b, c, k: (b, 0, c)),
        out_shape=jax.ShapeDtypeStruct((batch, seq_len, HY_CH), BF16),
        scratch_shapes=[pltpu.VMEM((bb, seq_len, tc), BF16), pltpu.VMEM((bb, seq_len, tc), F32)],
        compiler_params=_cparams("arbitrary", "arbitrary", "arbitrary"),
        name="hy_conv",
    )(proj3, proj3, proj3, skip, cm, sm, cmt, smt, gr, gi)


HALO_ROWS = 16


def _mix_ffn_kernel(*refs, seq_len, halo, tf, mixer):
    it = iter(refs)

    def rows_of():
        parts = [next(it)[...] for _ in range(3 if halo else 1)]
        return parts[0] if len(parts) == 1 else jnp.concatenate(parts, axis=0)

    x = rows_of()
    mix_in = [rows_of() for _ in range(2 if mixer == "ah" else 3)]
    mod_ref, gain_ref, wo_ref = next(it), next(it), next(it)
    hn_ref = next(it) if mixer == "ml" else None
    wu_ref, conv_ref, wd_ref, o_ref = next(it), next(it), next(it), next(it)
    tm = o_ref.shape[0]

    if mixer == "ah":
        mixed = _dot(mix_in[0], wo_ref[0:A_Q, :]) + _dot(mix_in[1], wo_ref[A_Q:, :])
    else:
        h = mix_in[0].astype(F32) + mix_in[1].astype(F32)
        parts = []
        for hd in range(ML_HEADS):
            sl = slice(hd * ML_HEAD_DIM, (hd + 1) * ML_HEAD_DIM)
            parts.append(_rms(h[:, sl], hn_ref[0:1, sl]))
        gated = jnp.concatenate(parts, axis=1) * jax.nn.sigmoid(mix_in[2].astype(F32))
        mixed = _dot(gated.astype(BF16), wo_ref[...])
    x_mid = x + mod_ref[0, 2:3, :] * _rms(mixed, gain_ref[1:2, :])

    xn = _modulated(x_mid, gain_ref[2:3, :], mod_ref[0, 3:4, :], mod_ref[0, 4:5, :]).astype(BF16)
    out = None
    for t in range(D_FF // tf):
        cg = slice(t * tf, (t + 1) * tf)
        cl = slice(D_FF + t * tf, D_FF + (t + 1) * tf)
        yg = _dot(xn, wu_ref[0, :, cg])
        yl = _dot(xn, wu_ref[0, :, cl])
        edge_g = edge_l = None
        if halo:
            yg, edge_g = _halo_edges(yg, tm, seq_len, HALO_ROWS)
            yl, edge_l = _halo_edges(yl, tm, seq_len, HALO_ROWS)
        hg = _dwconv3(yg, conv_ref.at[:, cg], seq_len, edge_g)
        hl = _dwconv3(yl, conv_ref.at[:, cl], seq_len, edge_l)
        act = (0.5 * hg * (1.0 + lax.erf(hg * INV_SQRT2))) * hl
        part = _dot(act.astype(BF16), wd_ref[0, cg, :])
        out = part if out is None else out + part
    o_ref[...] = x_mid[0:tm, :] + mod_ref[0, 5:6, :] * _rms(out, gain_ref[3:4, :])


def _mix_ffn_call(x, mix_in, w_out, head_norm, mod, gains, w_up, conv, w_down, layer, *,
                  tm, tf, seq_len, rows_per_mod):
    rows = x.shape[0]
    mixer = "ah" if head_norm is None else "ml"
    halo = tm < seq_len
    assert (seq_len % tm == 0) if halo else (tm % seq_len == 0)
    row_specs, row_args = [], []
    for a in [x] + list(mix_in):
        cols = a.shape[1]
        row_specs.append(pl.BlockSpec((tm, cols), lambda i: (i, 0)))
        row_args.append(a)
        if halo:
            row_specs += _halo_specs(tm, rows, cols, HALO_ROWS)
            row_args += [a, a]
    whole = lambda a: pl.BlockSpec(a.shape, lambda i: (0,) * a.ndim)
    resident = pl.Buffered(1)
    extra = [] if head_norm is None else [head_norm]
    return pl.pallas_call(
        functools.partial(_mix_ffn_kernel, seq_len=seq_len, halo=halo, tf=tf, mixer=mixer),
        grid=(rows // tm,),
        in_specs=row_specs + [
            pl.BlockSpec((1, 8, D_MODEL), lambda i: (i * tm // rows_per_mod, 0, 0)),
            whole(gains), whole(w_out)] + [whole(a) for a in extra] + [
            pl.BlockSpec((1, D_MODEL, 2 * D_FF), lambda i: (layer, 0, 0), pipeline_mode=resident),
            whole(conv),
            pl.BlockSpec((1, D_FF, D_MODEL), lambda i: (layer, 0, 0), pipeline_mode=resident)],
        out_specs=pl.BlockSpec((tm, D_MODEL), lambda i: (i, 0)),
        out_shape=jax.ShapeDtypeStruct((rows, D_MODEL), F32),
        compiler_params=_cparams("arbitrary"),
        name="mix_ffn",
    )(*row_args, mod, gains, w_out, *extra, w_up, conv, w_down)


def _halo_specs(tm, rows, cols=D_MODEL, sub=8):
    per = tm // sub
    return [pl.BlockSpec((sub, cols), lambda i, *_: (jnp.maximum(i * per - 1, 0), 0)),
            pl.BlockSpec((sub, cols), lambda i, *_: (jnp.minimum((i + 1) * per, rows // sub - 1), 0))]


def _halo_edges(y, tm, seq_len, sub=8):
    i = pl.program_id(0)
    at_start = (i * tm) % seq_len == 0
    at_end = ((i + 1) * tm) % seq_len == 0
    edge = (jnp.where(at_start, 0.0, y[tm + sub - 1:tm + sub, :]),
            jnp.where(at_end, 0.0, y[tm + sub:tm + sub + 1, :]))
    return y[0:tm, :], edge


def _ml_inproj_kernel(*refs, seq_len, cw, halo):
    if halo:
        x_ref, xb_ref, xa_ref, mod_ref, gain_ref, w_ref, wg_ref, bg_ref, conv_ref = refs[:9]
    else:
        x_ref, mod_ref, gain_ref, w_ref, wg_ref, bg_ref, conv_ref = refs[:7]
    q_ref, kt_ref, v_ref, o_ref, g_ref = refs[-5:]
    tm = x_ref.shape[0]

    def modulated(x):
        return _modulated(x, gain_ref[0:1, :], mod_ref[0, 0:1, :], mod_ref[0, 1:2, :]).astype(BF16)

    xn = modulated(x_ref[...])
    g_ref[...] = (_dot(xn, wg_ref[...]) + bg_ref[...]).T
    xe = xn
    if halo:
        xe = jnp.concatenate([xn, modulated(jnp.concatenate([xb_ref[...], xa_ref[...]], axis=0))], axis=0)

    for t in range(2 * ML_W // cw):
        cols = slice(t * cw, (t + 1) * cw)
        y = _dot(xe, w_ref[:, cols])
        edge = None
        if halo:
            y, edge = _halo_edges(y, tm, seq_len)
        z = _dwconv3(y, conv_ref.at[:, cols], seq_len, edge)
        z = z * jax.nn.sigmoid(z)
        if t * cw < ML_W:
            q_ref[:, cols] = z.astype(BF16)
        else:
            kt_ref[t * cw - ML_W:(t + 1) * cw - ML_W, :] = (z * (ML_HEAD_DIM ** -0.5)).T.astype(BF16)
    for t in range(2 * ML_W // cw):
        y = _dot(xn, w_ref[:, 2 * ML_W + t * cw:2 * ML_W + (t + 1) * cw]).astype(BF16)
        if t * cw < ML_W:
            v_ref[:, t * cw:(t + 1) * cw] = y
        else:
            o_ref[:, t * cw - ML_W:(t + 1) * cw - ML_W] = y


def _ml_inproj_call(x, mod, gains, w, wg, bg, conv, *, tm, cw, seq_len, rows_per_mod):
    rows = x.shape[0]
    n = 4 * ML_W
    halo = tm < seq_len
    assert (seq_len % tm == 0) if halo else (tm % seq_len == 0)
    x_specs = [pl.BlockSpec((tm, D_MODEL), lambda i: (i, 0))] + (_halo_specs(tm, rows) if halo else [])
    return pl.pallas_call(
        functools.partial(_ml_inproj_kernel, seq_len=seq_len, cw=cw, halo=halo),
        grid=(rows // tm,),
        in_specs=x_specs + [
            pl.BlockSpec((1, 8, D_MODEL), lambda i: (i * tm // rows_per_mod, 0, 0)),
            pl.BlockSpec((8, D_MODEL), lambda i: (0, 0)),
            pl.BlockSpec((D_MODEL, n), lambda i: (0, 0)),
            pl.BlockSpec((D_MODEL, LANES), lambda i: (0, 0)),
            pl.BlockSpec((1, LANES), lambda i: (0, 0)),
            pl.BlockSpec((8, n), lambda i: (0, 0))],
        out_specs=[pl.BlockSpec((tm, ML_W), lambda i: (i, 0)),
                   pl.BlockSpec((ML_W, tm), lambda i: (0, i)),
                   pl.BlockSpec((tm, ML_W), lambda i: (i, 0)),
                   pl.BlockSpec((tm, ML_W), lambda i: (i, 0)),
                   pl.BlockSpec((LANES, tm), lambda i: (0, i))],
        out_shape=[jax.ShapeDtypeStruct((rows, ML_W), BF16), jax.ShapeDtypeStruct((ML_W, rows), BF16),
                   jax.ShapeDtypeStruct((rows, ML_W), BF16), jax.ShapeDtypeStruct((rows, ML_W), BF16),
                   jax.ShapeDtypeStruct((LANES, rows), F32)],
        compiler_params=_cparams("arbitrary"),
        name="ml_inproj",
    )(*([x, x, x] if halo else [x]), mod, gains, w, wg, bg, conv)


def _log_sigmoid(x):
    return jnp.minimum(x, 0.0) - jnp.log1p(jnp.exp(-jnp.abs(x)))


def _lane_scan(x, op, reverse, fill, seg):
    n = x.shape[1]
    pos = lax.broadcasted_iota(jnp.int32, x.shape, 1) % seg
    sh = 1
    while sh < seg:
        if reverse:
            x = op(x, jnp.where(pos < seg - sh, pltpu.roll(x, n - sh, 1), fill))
        else:
            x = op(x, jnp.where(pos >= sh, pltpu.roll(x, sh, 1), fill))
        sh *= 2
    return x


def _bcast_selectors(chunk):
    sel = np.zeros((ML_HEADS, LANES, chunk + LANES), np.float32)
    for h in range(ML_HEADS):
        for g in range(3):
            sel[h, g * ML_HEADS + h, :chunk] = 1.0
            sel[h, (3 + g) * ML_HEADS + h, chunk:] = 1.0
    return jnp.asarray(sel, BF16)


def _ml_scan_kernel(*refs, chunk, cps, has_init):
    if has_init:
        (qf, ktf, vf, qb, ktb, vb, colf, colb, rowf, rowb, sel_ref, c0_ref, n0_ref,
         hf_ref, hb_ref, c_ref, n_ref, s_sc, p16_sc, e_sc) = refs
    else:
        (qf, ktf, vf, qb, ktb, vb, colf, colb, rowf, rowb, sel_ref,
         hf_ref, hb_ref, c_ref, n_ref, s_sc, p16_sc, e_sc) = refs
    nh, dh = ML_HEADS, ML_HEAD_DIM
    step = pl.program_id(1)

    @pl.when(step == 0)
    def _():
        if has_init:
            for d in range(2):
                for h in range(nh):
                    s_sc[d, h, :, 0:dh] = c0_ref[0, d, h].T
                    s_sc[d, h, :, dh:2 * dh] = jnp.broadcast_to(n0_ref[0, d, h:h + 1, :], (dh, dh)).T
        else:
            s_sc[...] = jnp.zeros_like(s_sc)

    row = lax.broadcasted_iota(jnp.int32, (chunk, chunk), 0)
    col = lax.broadcasted_iota(jnp.int32, (chunk, chunk), 1)
    ones_cols = jnp.ones((chunk, dh), BF16)

    dirs = ((qf, ktf, vf, colf, rowf, hf_ref), (qb, ktb, vb, colb, rowb, hb_ref))
    for sub in range(cps):
        for d, (q_ref, kt_ref, _, col_ref, row_ref, _) in enumerate(dirs):
            reverse = d == 1
            lc = cps - 1 - sub if reverse else sub
            tok = slice(lc * chunk, (lc + 1) * chunk)
            cols = col_ref[0, lc, 0]
            r = row_ref[0, lc, 0, 0:nh, :]
            m_prev = row_ref[0, lc, 0, 3 * nh:4 * nh, :]
            mask = (col >= row) if reverse else (col <= row)
            for h in range(nh):
                sl = slice(h * dh, (h + 1) * dh)
                bc = _dot(cols, sel_ref[h])
                p_bc = bc[:, :chunk]
                w = jnp.exp(jnp.where(mask, p_bc + r[h:h + 1, :], NEG_BIG))
                p16_sc[sub, d, h] = (_dot(q_ref[0, tok, sl], kt_ref[sl, tok]) * w).astype(BF16)
                e_sc[sub, d, h, :, 0:dh] = jnp.exp(p_bc[:, :dh] + m_prev[h:h + 1, 0:dh])
                e_sc[sub, d, h, :, dh:2 * dh] = jnp.exp(bc[:, chunk:])
    for sub in range(cps):
        for d, (q_ref, kt_ref, v_ref, _, row_ref, h_ref) in enumerate(dirs):
            lc = cps - 1 - sub if d == 1 else sub
            tok = slice(lc * chunk, (lc + 1) * chunk)
            w_tok = row_ref[0, lc, 0, nh:2 * nh, :]
            w_state = row_ref[0, lc, 0, 2 * nh:3 * nh, :]
            for h in range(nh):
                sl = slice(h * dh, (h + 1) * dh)
                q16 = q_ref[0, tok, sl]
                kt16 = kt_ref[sl, tok]
                vaug = jnp.concatenate([v_ref[0, tok, sl], ones_cols], axis=1)
                s_old = s_sc[d, h]
                w_inter = e_sc[sub, d, h, :, 0:dh]
                tot = (_dot(p16_sc[sub, d, h], vaug)
                       + jnp.concatenate([w_inter, w_inter], axis=1) * _dot(q16, s_old.astype(BF16)))
                h_ref[0, tok, sl] = (tot[:, :dh] / jnp.maximum(jnp.abs(tot[:, dh:]),
                                                               e_sc[sub, d, h, :, dh:2 * dh])).astype(BF16)
                kts = (kt16.astype(F32) * w_tok[h:h + 1, :]).astype(BF16)
                ws = w_state[h:h + 1, 0:dh]
                s_sc[d, h] = jnp.concatenate([ws, ws], axis=1) * s_old + _dot(kts, vaug)

    @pl.when(step == pl.num_programs(1) - 1)
    def _():
        for d in range(2):
            for h in range(nh):
                s_fin = s_sc[d, h]
                c_ref[0, d, h] = s_fin[:, 0:dh].T
                n_ref[0, d, h:h + 1, :] = s_fin[:, dh:2 * dh].T[0:1, :]


def _ml_gate_kernel(*refs, nc, chunk, bb, has_init):
    if has_init:
        gt_ref, m0_ref, cols_ref, rows_ref, m_ref = refs
    else:
        gt_ref, cols_ref, rows_ref, m_ref = refs
    nh = ML_HEADS
    pad = jnp.zeros((LANES - 6 * nh, chunk), F32)
    for d in range(2):
        reverse = d == 1
        last = 0 if reverse else chunk - 1
        b_all = _lane_scan(_log_sigmoid(gt_ref[nh * (2 + d):nh * (3 + d), :]), jnp.add, reverse, 0.0, chunk)
        r_all = gt_ref[nh * d:nh * (d + 1), :] - b_all
        g_all = _lane_scan(r_all, jnp.maximum, reverse, -jnp.inf, chunk)
        for bi in range(bb):
            m = m0_ref[bi, d] if has_init else jnp.zeros((nh, chunk), F32)
            for c in (range(nc - 1, -1, -1) if reverse else range(nc)):
                c0 = (bi * nc + c) * chunk
                b = b_all[:, c0:c0 + chunk]
                r = r_all[:, c0:c0 + chunk]
                mx = jnp.maximum(m, g_all[:, c0:c0 + chunk])
                mt = b + mx
                m_new = jnp.broadcast_to(mt[:, last:last + 1], (nh, chunk))
                b_last = jnp.broadcast_to(b[:, last:last + 1], (nh, chunk))
                rows_ref[bi, c, d, 0:nh, :] = r
                rows_ref[bi, c, d, nh:2 * nh, :] = jnp.exp(r + b_last - m_new)
                rows_ref[bi, c, d, 2 * nh:3 * nh, :] = jnp.exp(b_last + m - m_new)
                rows_ref[bi, c, d, 3 * nh:4 * nh, :] = m
                stack = []
                for x in (-mx, -mt):
                    stack += [t.astype(F32) for t in _split3(x)]
                stack.append(pad)
                cols_ref[bi, c, d] = jnp.concatenate(stack, axis=0).T.astype(BF16)
                m = m_new
            m_ref[bi, d] = m[:, 0:LANES]


def _ml_gate_call(gates, m0, *, batch, seq_len, chunk):
    nc = seq_len // chunk
    bb = max(1, 16 // nc)
    nh = ML_HEADS
    has_init = m0 is not None
    in_specs = [pl.BlockSpec((LANES, bb * seq_len), lambda i: (0, i))]
    args = [gates]
    if has_init:
        in_specs.append(pl.BlockSpec((bb, 2, nh, chunk), lambda i: (i, 0, 0, 0)))
        args.append(jnp.broadcast_to(m0[..., None], m0.shape + (chunk,)))
    return pl.pallas_call(
        functools.partial(_ml_gate_kernel, nc=nc, chunk=chunk, bb=bb, has_init=has_init),
        grid=(batch // bb,),
        in_specs=in_specs,
        out_specs=[pl.BlockSpec((bb, nc, 2, chunk, LANES), lambda i: (i, 0, 0, 0, 0)),
                   pl.BlockSpec((bb, nc, 2, 4 * nh, chunk), lambda i: (i, 0, 0, 0, 0)),
                   pl.BlockSpec((bb, 2, nh, LANES), lambda i: (i, 0, 0, 0))],
        out_shape=[jax.ShapeDtypeStruct((batch, nc, 2, chunk, LANES), BF16),
                   jax.ShapeDtypeStruct((batch, nc, 2, 4 * nh, chunk), F32),
                   jax.ShapeDtypeStruct((batch, 2, nh, LANES), F32)],
        compiler_params=_cparams("arbitrary"),
        name="ml_gate",
    )(*args)


def _ml_scan_call(q, kt, v, cols, rows, init, *, batch, seq_len, chunk, cps):
    nc = seq_len // (chunk * cps)
    nh = ML_HEADS
    blk = chunk * cps
    q3 = q.reshape(batch, seq_len, ML_W)
    v3 = v.reshape(batch, seq_len, ML_W)
    sel = _bcast_selectors(chunk)
    has_init = init is not None

    fwd3 = lambda b, i: (b, i, 0)
    bwd3 = lambda b, i: (b, nc - 1 - i, 0)
    in_specs = [pl.BlockSpec((1, blk, ML_W), fwd3),
                pl.BlockSpec((ML_W, blk), lambda b, i: (0, b * nc + i)),
                pl.BlockSpec((1, blk, ML_W), fwd3),
                pl.BlockSpec((1, blk, ML_W), bwd3),
                pl.BlockSpec((ML_W, blk), lambda b, i: (0, b * nc + nc - 1 - i)),
                pl.BlockSpec((1, blk, ML_W), bwd3),
                pl.BlockSpec((1, cps, 1, chunk, LANES), lambda b, i: (b, i, 0, 0, 0)),
                pl.BlockSpec((1, cps, 1, chunk, LANES), lambda b, i: (b, nc - 1 - i, 1, 0, 0)),
                pl.BlockSpec((1, cps, 1, 4 * nh, chunk), lambda b, i: (b, i, 0, 0, 0)),
                pl.BlockSpec((1, cps, 1, 4 * nh, chunk), lambda b, i: (b, nc - 1 - i, 1, 0, 0)),
                pl.BlockSpec(sel.shape, lambda b, i: (0, 0, 0))]
    args = [q3, kt, v3, q3, kt, v3, cols, cols, rows, rows, sel]
    c_spec = pl.BlockSpec((1, 2, nh, ML_HEAD_DIM, ML_HEAD_DIM), lambda b, i: (b, 0, 0, 0, 0))
    n_spec = pl.BlockSpec((1, 2, nh, ML_HEAD_DIM), lambda b, i: (b, 0, 0, 0))
    if has_init:
        in_specs += [c_spec, n_spec]
        args += list(init)
    return pl.pallas_call(
        functools.partial(_ml_scan_kernel, chunk=chunk, cps=cps, has_init=has_init),
        grid=(batch, nc),
        in_specs=in_specs,
        out_specs=[pl.BlockSpec((1, blk, ML_W), fwd3),
                   pl.BlockSpec((1, blk, ML_W), bwd3),
                   c_spec, n_spec],
        out_shape=[jax.ShapeDtypeStruct((batch, seq_len, ML_W), BF16),
                   jax.ShapeDtypeStruct((batch, seq_len, ML_W), BF16),
                   jax.ShapeDtypeStruct((batch, 2, nh, ML_HEAD_DIM, ML_HEAD_DIM), F32),
                   jax.ShapeDtypeStruct((batch, 2, nh, ML_HEAD_DIM), F32)],
        scratch_shapes=[pltpu.VMEM((2, nh, ML_HEAD_DIM, 2 * ML_HEAD_DIM), F32),
                        pltpu.VMEM((cps, 2, nh, chunk, chunk), BF16),
                        pltpu.VMEM((cps, 2, nh, chunk, 2 * ML_HEAD_DIM), F32)],
        compiler_params=_cparams("arbitrary", "arbitrary"),
        name="ml_scan",
    )(*args)


@functools.lru_cache(maxsize=None)
def _dft_tables_np(seq_len):
    n4 = 4 * seq_len
    ar = np.arange(seq_len, dtype=np.int64)
    idx = ((2 * ar + 1)[:, None] * ar[None, :]) % n4
    ang = (2.0 * np.pi / n4) * idx.astype(np.float64)
    cm = np.cos(ang).astype(np.float32)
    sm = np.sin(ang).astype(np.float32)
    return tuple(np.ascontiguousarray(a) for a in (cm, sm, cm.T, sm.T))


def _dft_tables(seq_len):
    return tuple(jnp.asarray(a).astype(BF16) for a in _dft_tables_np(seq_len))


def _hy_features(seq_len):
    t = np.linspace(0.0, 1.0, seq_len, dtype=np.float32)
    bands = np.arange(1, HY_BANDS + 1, dtype=np.float32)
    ang = (np.float32(2.0 * np.pi) * t[:, None]) * bands
    z = np.concatenate([t[:, None], np.cos(ang), np.sin(ang)], axis=-1).astype(np.float32)
    z = np.pad(z, ((0, 0), (0, 32 - HY_EMB)))
    deltas = np.abs(np.linspace(HY_MIN_DECAY, HY_MAX_DECAY, HY_CH, dtype=np.float32))[None, :]
    return jnp.asarray(z), jnp.asarray(deltas)


def _rope_tables(seq_len):
    rows = seq_len // GRID_W
    row = np.repeat(np.arange(rows, dtype=np.float32), GRID_W)
    col = np.tile(np.arange(GRID_W, dtype=np.float32), rows)
    n_freq = A_HEAD_DIM // 4
    inv = (np.float32(ROPE_THETA) ** (-np.arange(n_freq, dtype=np.float32) / n_freq)).astype(np.float32)
    ang = np.concatenate([row[:, None] * inv, col[:, None] * inv], axis=-1).astype(np.float32)
    cos, sin = np.cos(ang), np.sin(ang)
    return (jnp.asarray(np.concatenate([cos, cos], axis=-1), F32),
            jnp.asarray(np.concatenate([-sin, sin], axis=-1), F32))


def _pad_rows(a, rows=8):
    return jnp.pad(a, ((0, rows - a.shape[0]), (0, 0)))


def kernel(x_prompt, x_sample, cache_attn_k, cache_attn_v, state_mlstm_C, state_mlstm_n, state_mlstm_m, c, c_ctx, w_mod, b_mod, norm_mix_pre, norm_mix_post, norm_ffn_pre, norm_ffn_post, ffn_w_up, ffn_conv_w, ffn_conv_b, ffn_w_down, ah_w_in, ah_w_out, attn_q_norm, attn_k_norm, hy_conv_w, hy_conv_b, hy_w1, hy_b1, hy_w2, hy_b2, hy_w3, hy_b3, hy_sin_freq, hy_skip, ml_w_in, ml_b_gates, ml_conv_w, ml_conv_b, ml_head_norm, ml_w_out):
    bp, lp, _ = x_prompt.shape
    bs, ls, _ = x_sample.shape
    past = cache_attn_k.shape[2]
    xp = x_prompt.reshape(bp * lp, D_MODEL)
    xs = x_sample.reshape(bs * ls, D_MODEL)
    groups = {
        "p": dict(batch=bp, seq_len=lp, tm=1024, rows_per_mod=bp * lp, ffn_tm=512, ffn_tf=D_FF // 2,
                  seq_bb=4, hy_tc=HY_CH, hy_tk=256),
        "s": dict(batch=bs, seq_len=ls, tm=ls, rows_per_mod=ls, ffn_tm=512, ffn_tf=D_FF // 2,
                  seq_bb=1, hy_tc=COL_TILE, hy_tk=512),
    }
    ffn_up16 = ffn_w_up.astype(BF16)
    ffn_down16 = ffn_w_down.astype(BF16)

    cc = jnp.concatenate([c, c_ctx[None, :], jnp.zeros((8 - bs - 1, D_MODEL), F32)], axis=0)
    mod_all = _mod_call(cc, w_mod, b_mod).reshape(DEPTH, 8, 6, D_MODEL)
    mod_all = jnp.pad(mod_all, ((0, 0), (0, 0), (0, 2), (0, 0)))
    rope = _rope_tables(ls)

    new_k = new_v = new_c = new_n = new_m = None
    for l in range(DEPTH):
        j = l // 2
        mods = {"s": mod_all[l, :bs], "p": mod_all[l, bs:bs + 1]}
        gains = _pad_rows(jnp.stack([norm_mix_pre[l], norm_mix_post[l], norm_ffn_pre[l], norm_ffn_post[l]]))
        xin = {"p": xp, "s": xs}
        xmid = {}
        if l % 2 == 0:
            w_in = ah_w_in[j].astype(BF16)
            w_out = ah_w_out[j].astype(BF16)
            qkn = _pad_rows(jnp.stack([attn_q_norm[j], attn_k_norm[j]]))
            conv = jnp.pad(_pad_rows(jnp.concatenate([hy_conv_w[j], hy_conv_b[j][None, :]], axis=0)),
                           ((0, 0), (A_Q + 2 * A_KV, 0)))
            w1 = jnp.pad(hy_w1[j], ((0, 32 - HY_EMB), (0, 0)))
            sf = _pad_rows(hy_sin_freq[j])
            for g, cfg in groups.items():
                batch, seq_len, tm, rpm = cfg["batch"], cfg["seq_len"], cfg["tm"], cfg["rows_per_mod"]
                outs = _ah_inproj_call(xin[g], mods[g], gains, w_in, qkn, conv, rope if g == "s" else None,
                                       tm=512, seq_len=seq_len, rows_per_mod=rpm)
                q, k, v, u = outs[:4]
                if g == "s":
                    ctx_k = cache_attn_k[:, j].reshape(bs, past, A_KV).astype(BF16)
                    ctx_v = cache_attn_v[:, j].reshape(bs, past, A_KV).astype(BF16)
                else:
                    ctx_k = ctx_v = None
                    new_k = outs[4].reshape(bp, 1, lp, A_KV_HEADS, A_HEAD_DIM)
                    new_v = outs[5].reshape(bp, 1, lp, A_KV_HEADS, A_HEAD_DIM)
                attn = _attn_call(q, k, v, ctx_k, ctx_v, batch=batch, seq_len=seq_len, tq=256, bb=cfg["seq_bb"])
                z, deltas = _hy_features(seq_len)
                dft = _dft_tables(seq_len)
                tk = cfg["hy_tk"]
                gr, gi = _hy_filter_call(z, w1, hy_b1[j][None, :], hy_w2[j], hy_b2[j][None, :], hy_w3[j],
                                         hy_b3[j][None, :], sf, deltas, dft[0], dft[1], seq_len=seq_len, tk=tk)
                hyo = _hy_conv_call(u, hy_skip[j][None, :], dft, gr, gi, batch=batch, seq_len=seq_len,
                                    tc=cfg["hy_tc"], tk=tk, bb=cfg["seq_bb"])
                xmid[g] = [attn.reshape(batch * seq_len, A_Q), hyo.reshape(batch * seq_len, HY_CH)]
            head_norm = None
        else:
            w_in = ml_w_in[j].astype(BF16)
            w_g = jnp.pad(ml_w_in[j][:, 4 * ML_W:], ((0, 0), (0, LANES - 4 * ML_HEADS))).astype(BF16)
            b_g = jnp.pad(ml_b_gates[j], (0, LANES - 4 * ML_HEADS))[None, :]
            w_out = ml_w_out[j].astype(BF16)
            conv = jnp.pad(_pad_rows(jnp.concatenate([ml_conv_w[j], ml_conv_b[j][None, :]], axis=0)),
                           ((0, 0), (0, 2 * ML_W)))
            for g, cfg in groups.items():
                batch, seq_len, tm, rpm = cfg["batch"], cfg["seq_len"], cfg["tm"], cfg["rows_per_mod"]
                q, kt, v, og, gates = _ml_inproj_call(xin[g], mods[g], gains, w_in, w_g, b_g, conv, tm=512,
                                                      cw=512, seq_len=seq_len, rows_per_mod=rpm)
                if g == "s":
                    init, m0 = (state_mlstm_C[:, j], state_mlstm_n[:, j]), state_mlstm_m[:, j]
                else:
                    init = m0 = None
                cols, rows, m_new = _ml_gate_call(gates, m0, batch=batch, seq_len=seq_len, chunk=ML_SCAN_CHUNK)
                hf, hb, c_new, n_new = _ml_scan_call(q, kt, v, cols, rows, init, batch=batch, seq_len=seq_len,
                                                     chunk=ML_SCAN_CHUNK, cps=2)
                if g == "p":
                    new_c, new_n, new_m = c_new[:, None], n_new[:, None], m_new[:, None, :, :, 0]
                xmid[g] = [hf.reshape(batch * seq_len, ML_W), hb.reshape(batch * seq_len, ML_W), og]
            head_norm = ml_head_norm[j][None, :]
        conv = _pad_rows(jnp.concatenate([ffn_conv_w[l], ffn_conv_b[l][None, :]], axis=0))
        xout = {}
        for g, cfg in groups.items():
            xout[g] = _mix_ffn_call(xin[g], xmid[g], w_out, head_norm, mods[g], gains, ffn_up16, conv, ffn_down16,
                                    l, tm=cfg["ffn_tm"], tf=cfg["ffn_tf"], seq_len=cfg["seq_len"],
                                    rows_per_mod=cfg["rows_per_mod"])
        xp, xs = xout["p"], xout["s"]

    return (xp.reshape(bp, lp, D_MODEL), xs.reshape(bs, ls, D_MODEL), new_k, new_v, new_c, new_n, new_m)
```

```python
import functools
import math

import numpy as np
import jax
import jax.numpy as jnp
from jax import lax
from jax.experimental import pallas as pl
from jax.experimental.pallas import tpu as pltpu

F32 = jnp.float32
BF16 = jnp.bfloat16

D_MODEL = 1024
DEPTH = 2
GRID_W = 64
A_HEADS = 4
A_KV_HEADS = 2
A_HEAD_DIM = 128
A_Q = A_HEADS * A_HEAD_DIM
A_KV = A_KV_HEADS * A_HEAD_DIM
ROPE_THETA = 10000.0
HY_CH = D_MODEL // 2
HY_BANDS = 8
HY_EMB = 1 + 2 * HY_BANDS
HY_W = 64
HY_TARGET = 1e-2
HY_FAST_PCT = 0.3
HY_SLOW_PCT = 1.5
HY_MAX_DECAY = math.log(HY_TARGET) / HY_FAST_PCT
HY_MIN_DECAY = math.log(HY_TARGET) / HY_SLOW_PCT
AH_IN = A_Q + 2 * A_KV + 3 * HY_CH
ML_HEADS = 8
ML_HEAD_DIM = D_MODEL // ML_HEADS
ML_W = ML_HEADS * ML_HEAD_DIM
D_FF = 2816
NORM_EPS = 1e-6
NEG_BIG = -1e30

LANES = 128
VMEM_LIMIT = 56 * 1024 * 1024
COL_TILE = 256
ML_SCAN_CHUNK = 128
INV_SQRT2 = 1.0 / math.sqrt(2.0)


def _cparams(*sem):
    return pltpu.CompilerParams(dimension_semantics=sem, vmem_limit_bytes=VMEM_LIMIT)


def _rms(x, gain):
    return x * lax.rsqrt(jnp.mean(x * x, axis=-1, keepdims=True) + NORM_EPS) * gain


def _modulated(x, gain, shift, scale):
    return _rms(x, gain * (1.0 + scale)) + shift


def _dot(a, b):
    return jnp.dot(a, b, preferred_element_type=F32)


def _dot_nt(a, b):
    return lax.dot_general(a, b, (((1,), (1,)), ((), ())), preferred_element_type=F32)


def _dot_tn(a, b):
    return lax.dot_general(a, b, (((0,), (0,)), ((), ())), preferred_element_type=F32)


def _split3(x):
    hi = x.astype(BF16)
    r1 = x - hi.astype(F32)
    mid = r1.astype(BF16)
    lo = (r1 - mid.astype(F32)).astype(BF16)
    return hi, mid, lo


def _dot_x3(a, b):
    a0 = a.astype(BF16)
    a1 = (a - a0.astype(F32)).astype(BF16)
    b0 = b.astype(BF16)
    b1 = (b - b0.astype(F32)).astype(BF16)
    return _dot(a0, b0) + (_dot(a0, b1) + _dot(a1, b0))


def _dwconv3(y, conv_ref, seq_len, edge=None):
    rows, cols = y.shape
    pos = lax.broadcasted_iota(jnp.int32, (rows, LANES), 0)
    if edge is None:
        pos = pos % seq_len
        first = pos == 0
        last = pos == seq_len - 1
    else:
        first = pos == 0
        last = pos == rows - 1
    outs = []
    for c0 in range(0, cols, LANES):
        yc = y[:, c0:c0 + LANES]
        before = 0.0 if edge is None else edge[0][:, c0:c0 + LANES]
        after = 0.0 if edge is None else edge[1][:, c0:c0 + LANES]
        prev = jnp.where(first, before, pltpu.roll(yc, 1, 0))
        nxt = jnp.where(last, after, pltpu.roll(yc, rows - 1, 0))
        outs.append(prev * conv_ref[0:1, c0:c0 + LANES] + yc * conv_ref[1:2, c0:c0 + LANES]
                    + nxt * conv_ref[2:3, c0:c0 + LANES] + conv_ref[3:4, c0:c0 + LANES])
    return outs[0] if len(outs) == 1 else jnp.concatenate(outs, axis=1)


def _mod_kernel(c_ref, w_ref, b_ref, o_ref):
    a = c_ref[...]
    a = a * jax.nn.sigmoid(a)
    o_ref[0] = _dot(a.astype(BF16), w_ref[0].astype(BF16)) + b_ref[0]


def _mod_call(cc, w_mod, b_mod):
    tn = 1536
    n = 6 * D_MODEL
    return pl.pallas_call(
        _mod_kernel,
        grid=(DEPTH, n // tn),
        in_specs=[pl.BlockSpec((8, D_MODEL), lambda l, j: (0, 0)),
                  pl.BlockSpec((1, D_MODEL, tn), lambda l, j: (l, 0, j)),
                  pl.BlockSpec((1, 1, tn), lambda l, j: (l, 0, j))],
        out_specs=pl.BlockSpec((1, 8, tn), lambda l, j: (l, 0, j)),
        out_shape=jax.ShapeDtypeStruct((DEPTH, 8, n), F32),
        compiler_params=_cparams("arbitrary", "arbitrary"),
        name="mod",
    )(cc, w_mod, b_mod.reshape(DEPTH, 1, n))


def _ah_inproj_kernel(*refs, seq_len, rope, halo, cw):
    it = iter(refs)
    x_ref = next(it)
    xb_ref, xa_ref = (next(it), next(it)) if halo else (None, None)
    mod_ref, gain_ref, w_ref, qkn_ref, conv_ref = (next(it) for _ in range(5))
    cc_ref, ss_ref = (next(it), next(it)) if rope else (None, None)
    q_ref, k_ref, v_ref, u_ref = (next(it) for _ in range(4))
    kf_ref, vf_ref = (None, None) if rope else (next(it), next(it))
    tm = x_ref.shape[0]
    dh = A_HEAD_DIM

    def modulated(x):
        return _modulated(x, gain_ref[0:1, :], mod_ref[0, 0:1, :], mod_ref[0, 1:2, :]).astype(BF16)

    def head(yh, g):
        yh = _rms(yh, g)
        if rope:
            yh = yh * cc_ref[...] + pltpu.roll(yh, dh // 2, 1) * ss_ref[...]
        return yh

    xn = modulated(x_ref[...])
    yq = _dot(xn, w_ref[:, 0:A_Q])
    for h in range(A_HEADS):
        sl = slice(h * dh, (h + 1) * dh)
        q_ref[:, sl] = (head(yq[:, sl], qkn_ref[0:1, :]) * (dh ** -0.5)).astype(BF16)
    ykv = _dot(xn, w_ref[:, A_Q:A_Q + 2 * A_KV])
    for h in range(A_KV_HEADS):
        sl = slice(h * dh, (h + 1) * dh)
        kh = head(ykv[:, sl], qkn_ref[1:2, :])
        k_ref[:, sl] = kh.astype(BF16)
        if kf_ref is not None:
            kf_ref[:, sl] = kh
    yv = ykv[:, A_KV:2 * A_KV]
    v_ref[...] = yv.astype(BF16)
    if vf_ref is not None:
        vf_ref[...] = yv

    xe = xn
    if halo:
        xe = jnp.concatenate([xn, modulated(jnp.concatenate([xb_ref[...], xa_ref[...]], axis=0))], axis=0)
    u0 = A_Q + 2 * A_KV
    for t in range(3 * HY_CH // cw):
        cols = slice(u0 + t * cw, u0 + (t + 1) * cw)
        y = _dot(xe, w_ref[:, cols])
        edge = None
        if halo:
            y, edge = _halo_edges(y, tm, seq_len)
        u_ref[:, t * cw:(t + 1) * cw] = _dwconv3(y, conv_ref.at[:, cols], seq_len, edge).astype(BF16)


def _ah_inproj_call(x, mod, gains, w, qkn, conv, rope_tabs, *, tm, cw, seq_len, rows_per_mod):
    rows = x.shape[0]
    rope = rope_tabs is not None
    halo = tm < seq_len
    assert (seq_len % tm == 0) if halo else (tm % seq_len == 0)
    row_blk = lambda cols: pl.BlockSpec((tm, cols), lambda i: (i, 0))
    out_specs = [row_blk(A_Q), row_blk(A_KV), row_blk(A_KV), row_blk(3 * HY_CH)]
    out_shape = [jax.ShapeDtypeStruct((rows, A_Q), BF16), jax.ShapeDtypeStruct((rows, A_KV), BF16),
                 jax.ShapeDtypeStruct((rows, A_KV), BF16), jax.ShapeDtypeStruct((rows, 3 * HY_CH), BF16)]
    if not rope:
        out_specs += [row_blk(A_KV)] * 2
        out_shape += [jax.ShapeDtypeStruct((rows, A_KV), F32)] * 2
    in_specs = [row_blk(D_MODEL)] + (_halo_specs(tm, rows) if halo else []) + [
        pl.BlockSpec((1, 8, D_MODEL), lambda i: (i * tm // rows_per_mod, 0, 0)),
        pl.BlockSpec((8, D_MODEL), lambda i: (0, 0)),
        pl.BlockSpec((D_MODEL, AH_IN), lambda i: (0, 0)),
        pl.BlockSpec((8, A_HEAD_DIM), lambda i: (0, 0)),
        pl.BlockSpec((8, AH_IN), lambda i: (0, 0))]
    args = ([x, x, x] if halo else [x]) + [mod, gains, w, qkn, conv]
    if rope:
        per_seq = max(seq_len // tm, 1)
        in_specs += [pl.BlockSpec((tm, A_HEAD_DIM), lambda i: (i % per_seq, 0))] * 2
        args += list(rope_tabs)
    return pl.pallas_call(
        functools.partial(_ah_inproj_kernel, seq_len=seq_len, rope=rope, halo=halo, cw=cw),
        grid=(rows // tm,),
        in_specs=in_specs,
        out_specs=out_specs,
        out_shape=out_shape,
        compiler_params=_cparams("arbitrary"),
        name="ah_inproj",
    )(*args)


def _attn_kernel(*refs, ctx, bb):
    if ctx:
        q_ref, k_ref, v_ref, kc_ref, vc_ref, o_ref = refs
    else:
        q_ref, k_ref, v_ref, o_ref = refs
    dh = A_HEAD_DIM
    for bi in range(bb):
        k = k_ref[bi]
        if ctx:
            kc = kc_ref[bi]
        else:
            vaug = jnp.concatenate([v_ref[bi], jnp.ones((k.shape[0], dh), BF16)], axis=1)
        for g in range(A_HEADS // A_KV_HEADS):
            sl = slice(g * dh, (g + 1) * dh)
            q = q_ref[bi, :, sl]
            s = _dot_nt(q, k)
            m = jnp.max(s, axis=-1, keepdims=True)
            if ctx:
                sc = _dot_nt(q, kc)
                m = jnp.maximum(m, jnp.max(sc, axis=-1, keepdims=True))
            if ctx:
                p = jnp.exp(s - m)
                pc = jnp.exp(sc - m)
                den = jnp.sum(p, axis=-1, keepdims=True) + jnp.sum(pc, axis=-1, keepdims=True)
                o = _dot(p.astype(BF16), v_ref[bi]) + _dot(pc.astype(BF16), vc_ref[bi])
                o_ref[bi, :, sl] = (o / den).astype(BF16)
            else:
                o = _dot(jnp.exp(s - m).astype(BF16), vaug)
                o_ref[bi, :, sl] = (o[:, :dh] / o[:, dh:]).astype(BF16)


def _attn_call(q, k, v, ctx_k, ctx_v, *, batch, seq_len, tq, bb):
    ctx = ctx_k is not None
    qw = A_Q // A_KV_HEADS
    in_specs = [pl.BlockSpec((bb, tq, qw), lambda b, h, i: (b, i, h)),
                pl.BlockSpec((bb, seq_len, A_HEAD_DIM), lambda b, h, i: (b, 0, h)),
                pl.BlockSpec((bb, seq_len, A_HEAD_DIM), lambda b, h, i: (b, 0, h))]
    args = [q.reshape(batch, seq_len, A_Q), k.reshape(batch, seq_len, A_KV), v.reshape(batch, seq_len, A_KV)]
    if ctx:
        past = ctx_k.shape[1]
        in_specs += [pl.BlockSpec((bb, past, A_HEAD_DIM), lambda b, h, i: (b, 0, h))] * 2
        args += [ctx_k, ctx_v]
    return pl.pallas_call(
        functools.partial(_attn_kernel, ctx=ctx, bb=bb),
        grid=(batch // bb, A_KV_HEADS, seq_len // tq),
        in_specs=in_specs,
        out_specs=pl.BlockSpec((bb, tq, qw), lambda b, h, i: (b, i, h)),
        out_shape=jax.ShapeDtypeStruct((batch, seq_len, A_Q), BF16),
        compiler_params=_cparams("arbitrary", "arbitrary", "arbitrary"),
        name="attn",
    )(*args)


def _hy_filter_kernel(z_ref, w1_ref, b1_ref, w2_ref, b2_ref, w3_ref, b3_ref, sf_ref, dl_ref,
                      cm_ref, sm_ref, gr_ref, gi_ref, hs_ref, hd_ref):
    @pl.when(pl.program_id(0) == 0)
    def _():
        z = z_ref[...]
        h = jnp.sin(sf_ref[0:1, :] * (_dot_x3(z, w1_ref[...]) + b1_ref[...]))
        h = jnp.sin(sf_ref[1:2, :] * (_dot_x3(h, w2_ref[...]) + b2_ref[...]))
        filt = _dot_x3(h, w3_ref[...]) + b3_ref[...]
        win = jnp.exp(-z[:, 0:1] * dl_ref[...])
        hf = filt[:, :HY_CH] * win
        hb = filt[:, HY_CH:] * win
        hs_ref[...] = (hf + hb).astype(BF16)
        hd_ref[...] = (hb - hf).astype(BF16)

    gr_ref[...] = _dot(cm_ref[...], hs_ref[...])
    gi_ref[...] = _dot(sm_ref[...], hd_ref[...])


def _hy_filter_call(z, w1, b1, w2, b2, w3, b3, sf, deltas, cm, sm, *, seq_len, tk):
    full = lambda a: pl.BlockSpec(a.shape, lambda k: (0,) * a.ndim)
    small = [z, w1, b1, w2, b2, w3, b3, sf, deltas]
    return pl.pallas_call(
        _hy_filter_kernel,
        grid=(seq_len // tk,),
        in_specs=[full(a) for a in small] + [pl.BlockSpec((tk, seq_len), lambda k: (k, 0))] * 2,
        out_specs=[pl.BlockSpec((tk, HY_CH), lambda k: (k, 0))] * 2,
        out_shape=[jax.ShapeDtypeStruct((seq_len, HY_CH), F32)] * 2,
        scratch_shapes=[pltpu.VMEM((seq_len, HY_CH), BF16)] * 2,
        compiler_params=_cparams("arbitrary"),
        name="hy_filter",
    )(*small, cm, sm)


def _hy_conv_kernel(x0_ref, x1_ref, v_ref, skip_ref, cm_ref, sm_ref, cmt_ref, smt_ref, gr_ref, gi_ref,
                    o_ref, vv_ref, acc_ref, *, seq_len, bb):
    kb = pl.program_id(2)
    single = seq_len == cm_ref.shape[0]
    gr = gr_ref[...]
    gi = gi_ref[...]
    for bi in range(bb):
        def gated():
            return v_ref[bi].astype(F32) * x1_ref[bi].astype(F32)

        if single:
            vv = gated().astype(BF16)
        else:
            @pl.when(kb == 0)
            def _():
                vv_ref[bi] = gated().astype(BF16)
                acc_ref[bi] = jnp.zeros_like(acc_ref[bi])

            vv = vv_ref[bi]
        vr = _dot(cm_ref[...], vv)
        wi = _dot(sm_ref[...], vv)
        pr = gr * vr + gi * wi
        qi = gr * wi - gi * vr
        y = _dot(cmt_ref[...], pr.astype(BF16)) + _dot(smt_ref[...], qi.astype(BF16))

        def finish(total):
            o_ref[bi] = ((total * (1.0 / seq_len) + skip_ref[...] * gated())
                         * x0_ref[bi].astype(F32)).astype(BF16)

        if single:
            finish(y)
        else:
            acc_ref[bi] += y

            @pl.when(kb == pl.num_programs(2) - 1)
            def _():
                finish(acc_ref[bi])


def _hy_conv_call(u, skip, dft, gr, gi, *, batch, seq_len, tc, tk, bb):
    proj3 = u.reshape(batch, seq_len, 3 * HY_CH)
    cm, sm, cmt, smt = dft
    nch = HY_CH // tc
    return pl.pallas_call(
        functools.partial(_hy_conv_kernel, seq_len=seq_len, bb=bb),
        grid=(batch // bb, nch, seq_len // tk),
        in_specs=[pl.BlockSpec((bb, seq_len, tc), lambda b, c, k: (b, 0, c)),
                  pl.BlockSpec((bb, seq_len, tc), lambda b, c, k: (b, 0, nch + c)),
                  pl.BlockSpec((bb, seq_len, tc), lambda b, c, k: (b, 0, 2 * nch + c)),
                  pl.BlockSpec((1, tc), lambda b, c, k: (0, c)),
                  pl.BlockSpec((tk, seq_len), lambda b, c, k: (k, 0)),
                  pl.BlockSpec((tk, seq_len), lambda b, c, k: (k, 0)),
                  pl.BlockSpec((seq_len, tk), lambda b, c, k: (0, k)),
                  pl.BlockSpec((seq_len, tk), lambda b, c, k: (0, k)),
                  pl.BlockSpec((tk, tc), lambda b, c, k: (k, c)),
                  pl.BlockSpec((tk, tc), lambda b, c, k: (k, c))],
        out_specs=pl.BlockSpec((bb, seq_len, tc), lambda b, c, k: (b, 0, c)),
        out_shape=jax.ShapeDtypeStruct((batch, seq_len, HY_CH), BF16),
        scratch_shapes=[pltpu.VMEM((bb, seq_len, tc), BF16), pltpu.VMEM((bb, seq_len, tc), F32)],
        compiler_params=_cparams("arbitrary", "arbitrary", "arbitrary"),
        name="hy_conv",
    )(proj3, proj3, proj3, skip, cm, sm, cmt, smt, gr, gi)


HALO_ROWS = 16


def _mix_ffn_kernel(*refs, seq_len, halo, tf, mixer):
    it = iter(refs)

    def rows_of():
        parts = [next(it)[...] for _ in range(3 if halo else 1)]
        return parts[0] if len(parts) == 1 else jnp.concatenate(parts, axis=0)

    x = rows_of()
    mix_in = [rows_of() for _ in range(2 if mixer == "ah" else 3)]
    mod_ref, gain_ref, wo_ref = next(it), next(it), next(it)
    hn_ref = next(it) if mixer == "ml" else None
    wu_ref, conv_ref, wd_ref, o_ref = next(it), next(it), next(it), next(it)
    tm = o_ref.shape[0]

    if mixer == "ah":
        mixed = _dot(mix_in[0], wo_ref[0:A_Q, :]) + _dot(mix_in[1], wo_ref[A_Q:, :])
    else:
        h = mix_in[0].astype(F32) + mix_in[1].astype(F32)
        parts = []
        for hd in range(ML_HEADS):
            sl = slice(hd * ML_HEAD_DIM, (hd + 1) * ML_HEAD_DIM)
            parts.append(_rms(h[:, sl], hn_ref[0:1, sl]))
        gated = jnp.concatenate(parts, axis=1) * jax.nn.sigmoid(mix_in[2].astype(F32))
        mixed = _dot(gated.astype(BF16), wo_ref[...])
    x_mid = x + _rms(mixed, gain_ref[1:2, :] * mod_ref[0, 2:3, :])

    xn = _modulated(x_mid, gain_ref[2:3, :], mod_ref[0, 3:4, :], mod_ref[0, 4:5, :]).astype(BF16)
    out = None
    for t in range(D_FF // tf):
        cg = slice(t * tf, (t + 1) * tf)
        cl = slice(D_FF + t * tf, D_FF + (t + 1) * tf)
        yg = _dot(xn, wu_ref[0, :, cg])
        yl = _dot(xn, wu_ref[0, :, cl])
        edge_g = edge_l = None
        if halo:
            yg, edge_g = _halo_edges(yg, tm, seq_len, HALO_ROWS)
            yl, edge_l = _halo_edges(yl, tm, seq_len, HALO_ROWS)
        hg = _dwconv3(yg, conv_ref.at[:, cg], seq_len, edge_g)
        hl = _dwconv3(yl, conv_ref.at[:, cl], seq_len, edge_l)
        act = (0.5 * hg * (1.0 + lax.erf(hg * INV_SQRT2))) * hl
        part = _dot(act.astype(BF16), wd_ref[0, cg, :])
        out = part if out is None else out + part
    o_ref[...] = x_mid[0:tm, :] + _rms(out, gain_ref[3:4, :] * mod_ref[0, 5:6, :])


def _mix_ffn_call(x, mix_in, w_out, head_norm, mod, gains, w_up, conv, w_down, layer, *,
                  tm, tf, seq_len, rows_per_mod):
    rows = x.shape[0]
    mixer = "ah" if head_norm is None else "ml"
    halo = tm < seq_len
    assert (seq_len % tm == 0) if halo else (tm % seq_len == 0)
    row_specs, row_args = [], []
    for a in [x] + list(mix_in):
        cols = a.shape[1]
        row_specs.append(pl.BlockSpec((tm, cols), lambda i: (i, 0)))
        row_args.append(a)
        if halo:
            row_specs += _halo_specs(tm, rows, cols, HALO_ROWS)
            row_args += [a, a]
    whole = lambda a: pl.BlockSpec(a.shape, lambda i: (0,) * a.ndim)
    resident = pl.Buffered(1)
    extra = [] if head_norm is None else [head_norm]
    return pl.pallas_call(
        functools.partial(_mix_ffn_kernel, seq_len=seq_len, halo=halo, tf=tf, mixer=mixer),
        grid=(rows // tm,),
        in_specs=row_specs + [
            pl.BlockSpec((1, 8, D_MODEL), lambda i: (i * tm // rows_per_mod, 0, 0)),
            whole(gains), whole(w_out)] + [whole(a) for a in extra] + [
            pl.BlockSpec((1, D_MODEL, 2 * D_FF), lambda i: (layer, 0, 0), pipeline_mode=resident),
            whole(conv),
            pl.BlockSpec((1, D_FF, D_MODEL), lambda i: (layer, 0, 0), pipeline_mode=resident)],
        out_specs=pl.BlockSpec((tm, D_MODEL), lambda i: (i, 0)),
        out_shape=jax.ShapeDtypeStruct((rows, D_MODEL), F32),
        compiler_params=_cparams("arbitrary"),
        name="mix_ffn",
    )(*row_args, mod, gains, w_out, *extra, w_up, conv, w_down)


def _halo_specs(tm, rows, cols=D_MODEL, sub=8):
    per = tm // sub
    return [pl.BlockSpec((sub, cols), lambda i, *_: (jnp.maximum(i * per - 1, 0), 0)),
            pl.BlockSpec((sub, cols), lambda i, *_: (jnp.minimum((i + 1) * per, rows // sub - 1), 0))]


def _halo_edges(y, tm, seq_len, sub=8):
    i = pl.program_id(0)
    at_start = (i * tm) % seq_len == 0
    at_end = ((i + 1) * tm) % seq_len == 0
    edge = (jnp.where(at_start, 0.0, y[tm + sub - 1:tm + sub, :]),
            jnp.where(at_end, 0.0, y[tm + sub:tm + sub + 1, :]))
    return y[0:tm, :], edge


def _ml_inproj_kernel(*refs, seq_len, cw, halo):
    if halo:
        x_ref, xb_ref, xa_ref, mod_ref, gain_ref, w_ref, wg_ref, bg_ref, conv_ref = refs[:9]
    else:
        x_ref, mod_ref, gain_ref, w_ref, wg_ref, bg_ref, conv_ref = refs[:7]
    q_ref, kt_ref, v_ref, o_ref, g_ref = refs[-5:]
    tm = x_ref.shape[0]

    def modulated(x):
        return _modulated(x, gain_ref[0:1, :], mod_ref[0, 0:1, :], mod_ref[0, 1:2, :]).astype(BF16)

    xn = modulated(x_ref[...])
    g_ref[...] = (_dot(xn, wg_ref[...]) + bg_ref[...]).T
    xe = xn
    if halo:
        xe = jnp.concatenate([xn, modulated(jnp.concatenate([xb_ref[...], xa_ref[...]], axis=0))], axis=0)

    for t in range(2 * ML_W // cw):
        cols = slice(t * cw, (t + 1) * cw)
        y = _dot(xe, w_ref[:, cols])
        edge = None
        if halo:
            y, edge = _halo_edges(y, tm, seq_len)
        z = _dwconv3(y, conv_ref.at[:, cols], seq_len, edge)
        z = z * jax.nn.sigmoid(z)
        if t * cw < ML_W:
            q_ref[:, cols] = z.astype(BF16)
        else:
            kt_ref[t * cw - ML_W:(t + 1) * cw - ML_W, :] = (z * (ML_HEAD_DIM ** -0.5)).T.astype(BF16)
        y = _dot(xn, w_ref[:, 2 * ML_W + t * cw:2 * ML_W + (t + 1) * cw]).astype(BF16)
        if t * cw < ML_W:
            v_ref[:, t * cw:(t + 1) * cw] = y
        else:
            o_ref[:, t * cw - ML_W:(t + 1) * cw - ML_W] = y


def _ml_inproj_call(x, mod, gains, w, wg, bg, conv, *, tm, cw, seq_len, rows_per_mod):
    rows = x.shape[0]
    n = 4 * ML_W
    halo = tm < seq_len
    assert (seq_len % tm == 0) if halo else (tm % seq_len == 0)
    x_specs = [pl.BlockSpec((tm, D_MODEL), lambda i: (i, 0))] + (_halo_specs(tm, rows) if halo else [])
    return pl.pallas_call(
        functools.partial(_ml_inproj_kernel, seq_len=seq_len, cw=cw, halo=halo),
        grid=(rows // tm,),
        in_specs=x_specs + [
            pl.BlockSpec((1, 8, D_MODEL), lambda i: (i * tm // rows_per_mod, 0, 0)),
            pl.BlockSpec((8, D_MODEL), lambda i: (0, 0)),
            pl.BlockSpec((D_MODEL, n), lambda i: (0, 0)),
            pl.BlockSpec((D_MODEL, LANES), lambda i: (0, 0)),
            pl.BlockSpec((1, LANES), lambda i: (0, 0)),
            pl.BlockSpec((8, n), lambda i: (0, 0))],
        out_specs=[pl.BlockSpec((tm, ML_W), lambda i: (i, 0)),
                   pl.BlockSpec((ML_W, tm), lambda i: (0, i)),
                   pl.BlockSpec((tm, ML_W), lambda i: (i, 0)),
                   pl.BlockSpec((tm, ML_W), lambda i: (i, 0)),
                   pl.BlockSpec((LANES, tm), lambda i: (0, i))],
        out_shape=[jax.ShapeDtypeStruct((rows, ML_W), BF16), jax.ShapeDtypeStruct((ML_W, rows), BF16),
                   jax.ShapeDtypeStruct((rows, ML_W), BF16), jax.ShapeDtypeStruct((rows, ML_W), BF16),
                   jax.ShapeDtypeStruct((LANES, rows), F32)],
        compiler_params=_cparams("arbitrary"),
        name="ml_inproj",
    )(*([x, x, x] if halo else [x]), mod, gains, w, wg, bg, conv)


def _log_sigmoid(x):
    return jnp.minimum(x, 0.0) - jnp.log1p(jnp.exp(-jnp.abs(x)))


def _lane_scan(x, op, reverse, fill, seg):
    n = x.shape[1]
    pos = lax.broadcasted_iota(jnp.int32, x.shape, 1) % seg
    sh = 1
    while sh < seg:
        if reverse:
            x = op(x, jnp.where(pos < seg - sh, pltpu.roll(x, n - sh, 1), fill))
        else:
            x = op(x, jnp.where(pos >= sh, pltpu.roll(x, sh, 1), fill))
        sh *= 2
    return x


def _bcast_selectors(chunk):
    sel = np.zeros((ML_HEADS, LANES, chunk + LANES), np.float32)
    for h in range(ML_HEADS):
        for g in range(3):
            sel[h, g * ML_HEADS + h, :chunk] = 1.0
            sel[h, (3 + g) * ML_HEADS + h, chunk:] = 1.0
    return jnp.asarray(sel, BF16)


def _ml_scan_kernel(*refs, chunk, cps, has_init):
    if has_init:
        (qf, ktf, vf, qb, ktb, vb, colf, colb, rowf, rowb, sel_ref, c0_ref, n0_ref,
         hf_ref, hb_ref, c_ref, n_ref, s_sc, p16_sc, e_sc) = refs
    else:
        (qf, ktf, vf, qb, ktb, vb, colf, colb, rowf, rowb, sel_ref,
         hf_ref, hb_ref, c_ref, n_ref, s_sc, p16_sc, e_sc) = refs
    nh, dh = ML_HEADS, ML_HEAD_DIM
    step = pl.program_id(1)

    @pl.when(step == 0)
    def _():
        if has_init:
            for d in range(2):
                for h in range(nh):
                    s_sc[d, h, :, 0:dh] = c0_ref[0, d, h].T
                    s_sc[d, h, :, dh:2 * dh] = jnp.broadcast_to(n0_ref[0, d, h:h + 1, :], (dh, dh)).T
        else:
            s_sc[...] = jnp.zeros_like(s_sc)

    row = lax.broadcasted_iota(jnp.int32, (chunk, chunk), 0)
    col = lax.broadcasted_iota(jnp.int32, (chunk, chunk), 1)
    ones_cols = jnp.ones((chunk, dh), BF16)

    dirs = ((qf, ktf, vf, colf, rowf, hf_ref), (qb, ktb, vb, colb, rowb, hb_ref))
    for sub in range(cps):
        for d, (q_ref, kt_ref, _, col_ref, row_ref, _) in enumerate(dirs):
            reverse = d == 1
            lc = cps - 1 - sub if reverse else sub
            tok = slice(lc * chunk, (lc + 1) * chunk)
            cols = col_ref[0, lc, 0]
            r = row_ref[0, lc, 0, 0:nh, :]
            m_prev = row_ref[0, lc, 0, 3 * nh:4 * nh, :]
            mask = (col >= row) if reverse else (col <= row)
            for h in range(nh):
                sl = slice(h * dh, (h + 1) * dh)
                bc = _dot(cols, sel_ref[h])
                p_bc = bc[:, :chunk]
                w = jnp.exp(jnp.where(mask, p_bc + r[h:h + 1, :], NEG_BIG))
                p16_sc[sub, d, h] = (_dot(q_ref[0, tok, sl], kt_ref[sl, tok]) * w).astype(BF16)
                e_sc[sub, d, h, :, 0:dh] = jnp.exp(p_bc[:, :dh] + m_prev[h:h + 1, 0:dh])
                e_sc[sub, d, h, :, dh:2 * dh] = jnp.exp(bc[:, chunk:])
    for sub in range(cps):
        for d, (q_ref, kt_ref, v_ref, _, row_ref, h_ref) in enumerate(dirs):
            lc = cps - 1 - sub if d == 1 else sub
            tok = slice(lc * chunk, (lc + 1) * chunk)
            w_tok = row_ref[0, lc, 0, nh:2 * nh, :]
            w_state = row_ref[0, lc, 0, 2 * nh:3 * nh, :]
            for h in range(nh):
                sl = slice(h * dh, (h + 1) * dh)
                q16 = q_ref[0, tok, sl]
                kt16 = kt_ref[sl, tok]
                vaug = jnp.concatenate([v_ref[0, tok, sl], ones_cols], axis=1)
                s_old = s_sc[d, h]
                w_inter = e_sc[sub, d, h, :, 0:dh]
                tot = (_dot(p16_sc[sub, d, h], vaug)
                       + jnp.concatenate([w_inter, w_inter], axis=1) * _dot(q16, s_old.astype(BF16)))
                h_ref[0, tok, sl] = (tot[:, :dh] / jnp.maximum(jnp.abs(tot[:, dh:]),
                                                               e_sc[sub, d, h, :, dh:2 * dh])).astype(BF16)
                kts = (kt16.astype(F32) * w_tok[h:h + 1, :]).astype(BF16)
                ws = w_state[h:h + 1, 0:dh]
                s_sc[d, h] = jnp.concatenate([ws, ws], axis=1) * s_old + _dot(kts, vaug)

    @pl.when(step == pl.num_programs(1) - 1)
    def _():
        for d in range(2):
            for h in range(nh):
                s_fin = s_sc[d, h]
                c_ref[0, d, h] = s_fin[:, 0:dh].T
                n_ref[0, d, h:h + 1, :] = s_fin[:, dh:2 * dh].T[0:1, :]


def _ml_gate_kernel(*refs, nc, chunk, bb, has_init):
    if has_init:
        gt_ref, m0_ref, cols_ref, rows_ref, m_ref = refs
    else:
        gt_ref, cols_ref, rows_ref, m_ref = refs
    nh = ML_HEADS
    pad = jnp.zeros((LANES - 6 * nh, chunk), F32)
    for d in range(2):
        reverse = d == 1
        last = 0 if reverse else chunk - 1
        b_all = _lane_scan(_log_sigmoid(gt_ref[nh * (2 + d):nh * (3 + d), :]), jnp.add, reverse, 0.0, chunk)
        r_all = gt_ref[nh * d:nh * (d + 1), :] - b_all
        g_all = _lane_scan(r_all, jnp.maximum, reverse, -jnp.inf, chunk)
        for bi in range(bb):
            m = m0_ref[bi, d] if has_init else jnp.zeros((nh, chunk), F32)
            for c in (range(nc - 1, -1, -1) if reverse else range(nc)):
                c0 = (bi * nc + c) * chunk
                b = b_all[:, c0:c0 + chunk]
                r = r_all[:, c0:c0 + chunk]
                mx = jnp.maximum(m, g_all[:, c0:c0 + chunk])
                mt = b + mx
                m_new = jnp.broadcast_to(mt[:, last:last + 1], (nh, chunk))
                b_last = jnp.broadcast_to(b[:, last:last + 1], (nh, chunk))
                rows_ref[bi, c, d, 0:nh, :] = r
                rows_ref[bi, c, d, nh:2 * nh, :] = jnp.exp(r + b_last - m_new)
                rows_ref[bi, c, d, 2 * nh:3 * nh, :] = jnp.exp(b_last + m - m_new)
                rows_ref[bi, c, d, 3 * nh:4 * nh, :] = m
                stack = []
                for x in (-mx, -mt):
                    stack += [t.astype(F32) for t in _split3(x)]
                stack.append(pad)
                cols_ref[bi, c, d] = jnp.concatenate(stack, axis=0).T.astype(BF16)
                m = m_new
            m_ref[bi, d] = m[:, 0:LANES]


def _ml_gate_call(gates, m0, *, batch, seq_len, chunk):
    nc = seq_len // chunk
    bb = max(1, 16 // nc)
    nh = ML_HEADS
    has_init = m0 is not None
    in_specs = [pl.BlockSpec((LANES, bb * seq_len), lambda i: (0, i))]
    args = [gates]
    if has_init:
        in_specs.append(pl.BlockSpec((bb, 2, nh, chunk), lambda i: (i, 0, 0, 0)))
        args.append(jnp.broadcast_to(m0[..., None], m0.shape + (chunk,)))
    return pl.pallas_call(
        functools.partial(_ml_gate_kernel, nc=nc, chunk=chunk, bb=bb, has_init=has_init),
        grid=(batch // bb,),
        in_specs=in_specs,
        out_specs=[pl.BlockSpec((bb, nc, 2, chunk, LANES), lambda i: (i, 0, 0, 0, 0)),
                   pl.BlockSpec((bb, nc, 2, 4 * nh, chunk), lambda i: (i, 0, 0, 0, 0)),
                   pl.BlockSpec((bb, 2, nh, LANES), lambda i: (i, 0, 0, 0))],
        out_shape=[jax.ShapeDtypeStruct((batch, nc, 2, chunk, LANES), BF16),
                   jax.ShapeDtypeStruct((batch, nc, 2, 4 * nh, chunk), F32),
                   jax.ShapeDtypeStruct((batch, 2, nh, LANES), F32)],
        compiler_params=_cparams("arbitrary"),
        name="ml_gate",
    )(*args)


def _ml_scan_call(q, kt, v, cols, rows, init, *, batch, seq_len, chunk, cps):
    nc = seq_len // (chunk * cps)
    nh = ML_HEADS
    blk = chunk * cps
    q3 = q.reshape(batch, seq_len, ML_W)
    v3 = v.reshape(batch, seq_len, ML_W)
    sel = _bcast_selectors(chunk)
    has_init = init is not None

    fwd3 = lambda b, i: (b, i, 0)
    bwd3 = lambda b, i: (b, nc - 1 - i, 0)
    in_specs = [pl.BlockSpec((1, blk, ML_W), fwd3),
                pl.BlockSpec((ML_W, blk), lambda b, i: (0, b * nc + i)),
                pl.BlockSpec((1, blk, ML_W), fwd3),
                pl.BlockSpec((1, blk, ML_W), bwd3),
                pl.BlockSpec((ML_W, blk), lambda b, i: (0, b * nc + nc - 1 - i)),
                pl.BlockSpec((1, blk, ML_W), bwd3),
                pl.BlockSpec((1, cps, 1, chunk, LANES), lambda b, i: (b, i, 0, 0, 0)),
                pl.BlockSpec((1, cps, 1, chunk, LANES), lambda b, i: (b, nc - 1 - i, 1, 0, 0)),
                pl.BlockSpec((1, cps, 1, 4 * nh, chunk), lambda b, i: (b, i, 0, 0, 0)),
                pl.BlockSpec((1, cps, 1, 4 * nh, chunk), lambda b, i: (b, nc - 1 - i, 1, 0, 0)),
                pl.BlockSpec(sel.shape, lambda b, i: (0, 0, 0))]
    args = [q3, kt, v3, q3, kt, v3, cols, cols, rows, rows, sel]
    c_spec = pl.BlockSpec((1, 2, nh, ML_HEAD_DIM, ML_HEAD_DIM), lambda b, i: (b, 0, 0, 0, 0))
    n_spec = pl.BlockSpec((1, 2, nh, ML_HEAD_DIM), lambda b, i: (b, 0, 0, 0))
    if has_init:
        in_specs += [c_spec, n_spec]
        args += list(init)
    return pl.pallas_call(
        functools.partial(_ml_scan_kernel, chunk=chunk, cps=cps, has_init=has_init),
        grid=(batch, nc),
        in_specs=in_specs,
        out_specs=[pl.BlockSpec((1, blk, ML_W), fwd3),
                   pl.BlockSpec((1, blk, ML_W), bwd3),
                   c_spec, n_spec],
        out_shape=[jax.ShapeDtypeStruct((batch, seq_len, ML_W), BF16),
                   jax.ShapeDtypeStruct((batch, seq_len, ML_W), BF16),
                   jax.ShapeDtypeStruct((batch, 2, nh, ML_HEAD_DIM, ML_HEAD_DIM), F32),
                   jax.ShapeDtypeStruct((batch, 2, nh, ML_HEAD_DIM), F32)],
        scratch_shapes=[pltpu.VMEM((2, nh, ML_HEAD_DIM, 2 * ML_HEAD_DIM), F32),
                        pltpu.VMEM((cps, 2, nh, chunk, chunk), BF16),
                        pltpu.VMEM((cps, 2, nh, chunk, 2 * ML_HEAD_DIM), F32)],
        compiler_params=_cparams("arbitrary", "arbitrary"),
        name="ml_scan",
    )(*args)


@functools.lru_cache(maxsize=None)
def _dft_tables_np(seq_len):
    n4 = 4 * seq_len
    ar = np.arange(seq_len, dtype=np.int64)
    idx = ((2 * ar + 1)[:, None] * ar[None, :]) % n4
    ang = (2.0 * np.pi / n4) * idx.astype(np.float64)
    cm = np.cos(ang).astype(np.float32)
    sm = np.sin(ang).astype(np.float32)
    return tuple(np.ascontiguousarray(a) for a in (cm, sm, cm.T, sm.T))


def _dft_tables(seq_len):
    return tuple(jnp.asarray(a).astype(BF16) for a in _dft_tables_np(seq_len))


def _hy_features(seq_len):
    t = np.linspace(0.0, 1.0, seq_len, dtype=np.float32)
    bands = np.arange(1, HY_BANDS + 1, dtype=np.float32)
    ang = (np.float32(2.0 * np.pi) * t[:, None]) * bands
    z = np.concatenate([t[:, None], np.cos(ang), np.sin(ang)], axis=-1).astype(np.float32)
    z = np.pad(z, ((0, 0), (0, 32 - HY_EMB)))
    deltas = np.abs(np.linspace(HY_MIN_DECAY, HY_MAX_DECAY, HY_CH, dtype=np.float32))[None, :]
    return jnp.asarray(z), jnp.asarray(deltas)


def _rope_tables(seq_len):
    rows = seq_len // GRID_W
    row = np.repeat(np.arange(rows, dtype=np.float32), GRID_W)
    col = np.tile(np.arange(GRID_W, dtype=np.float32), rows)
    n_freq = A_HEAD_DIM // 4
    inv = (np.float32(ROPE_THETA) ** (-np.arange(n_freq, dtype=np.float32) / n_freq)).astype(np.float32)
    ang = np.concatenate([row[:, None] * inv, col[:, None] * inv], axis=-1).astype(np.float32)
    cos, sin = np.cos(ang), np.sin(ang)
    return (jnp.asarray(np.concatenate([cos, cos], axis=-1), F32),
            jnp.asarray(np.concatenate([-sin, sin], axis=-1), F32))


def _pad_rows(a, rows=8):
    return jnp.pad(a, ((0, rows - a.shape[0]), (0, 0)))


def kernel(x_prompt, x_sample, cache_attn_k, cache_attn_v, state_mlstm_C, state_mlstm_n, state_mlstm_m, c, c_ctx, w_mod, b_mod, norm_mix_pre, norm_mix_post, norm_ffn_pre, norm_ffn_post, ffn_w_up, ffn_conv_w, ffn_conv_b, ffn_w_down, ah_w_in, ah_w_out, attn_q_norm, attn_k_norm, hy_conv_w, hy_conv_b, hy_w1, hy_b1, hy_w2, hy_b2, hy_w3, hy_b3, hy_sin_freq, hy_skip, ml_w_in, ml_b_gates, ml_conv_w, ml_conv_b, ml_head_norm, ml_w_out):
    bp, lp, _ = x_prompt.shape
    bs, ls, _ = x_sample.shape
    past = cache_attn_k.shape[2]
    xp = x_prompt.reshape(bp * lp, D_MODEL)
    xs = x_sample.reshape(bs * ls, D_MODEL)
    groups = {
        "p": dict(batch=bp, seq_len=lp, tm=1024, rows_per_mod=bp * lp, ffn_tm=512, ffn_tf=D_FF // 2,
                  seq_bb=4, hy_tc=HY_CH, hy_tk=256, ml_cw=COL_TILE),
        "s": dict(batch=bs, seq_len=ls, tm=ls, rows_per_mod=ls, ffn_tm=512, ffn_tf=D_FF // 2,
                  seq_bb=1, hy_tc=HY_CH, hy_tk=512, ml_cw=512),
    }
    ffn_up16 = ffn_w_up.astype(BF16)
    ffn_down16 = ffn_w_down.astype(BF16)

    cc = jnp.concatenate([c, c_ctx[None, :], jnp.zeros((8 - bs - 1, D_MODEL), F32)], axis=0)
    mod_all = _mod_call(cc, w_mod, b_mod).reshape(DEPTH, 8, 6, D_MODEL)
    mod_all = jnp.pad(mod_all, ((0, 0), (0, 0), (0, 2), (0, 0)))
    rope = _rope_tables(ls)

    new_k = new_v = new_c = new_n = new_m = None
    for l in range(DEPTH):
        j = l // 2
        mods = {"s": mod_all[l, :bs], "p": mod_all[l, bs:bs + 1]}
        gains = _pad_rows(jnp.stack([norm_mix_pre[l], norm_mix_post[l], norm_ffn_pre[l], norm_ffn_post[l]]))
        xin = {"p": xp, "s": xs}
        xmid = {}
        if l % 2 == 0:
            w_in = ah_w_in[j].astype(BF16)
            w_out = ah_w_out[j].astype(BF16)
            qkn = _pad_rows(jnp.stack([attn_q_norm[j], attn_k_norm[j]]))
            conv = jnp.pad(_pad_rows(jnp.concatenate([hy_conv_w[j], hy_conv_b[j][None, :]], axis=0)),
                           ((0, 0), (A_Q + 2 * A_KV, 0)))
            w1 = jnp.pad(hy_w1[j], ((0, 32 - HY_EMB), (0, 0)))
            sf = _pad_rows(hy_sin_freq[j])
            for g, cfg in groups.items():
                batch, seq_len, tm, rpm = cfg["batch"], cfg["seq_len"], cfg["tm"], cfg["rows_per_mod"]
                outs = _ah_inproj_call(xin[g], mods[g], gains, w_in, qkn, conv, rope if g == "s" else None,
                                       tm=512, cw=cfg["ml_cw"], seq_len=seq_len, rows_per_mod=rpm)
                q, k, v, u = outs[:4]
                if g == "s":
                    ctx_k = cache_attn_k[:, j].reshape(bs, past, A_KV).astype(BF16)
                    ctx_v = cache_attn_v[:, j].reshape(bs, past, A_KV).astype(BF16)
                else:
                    ctx_k = ctx_v = None
                    new_k = outs[4].reshape(bp, 1, lp, A_KV_HEADS, A_HEAD_DIM)
                    new_v = outs[5].reshape(bp, 1, lp, A_KV_HEADS, A_HEAD_DIM)
                attn = _attn_call(q, k, v, ctx_k, ctx_v, batch=batch, seq_len=seq_len, tq=256, bb=cfg["seq_bb"])
                z, deltas = _hy_features(seq_len)
                dft = _dft_tables(seq_len)
                tk = cfg["hy_tk"]
                gr, gi = _hy_filter_call(z, w1, hy_b1[j][None, :], hy_w2[j], hy_b2[j][None, :], hy_w3[j],
                                         hy_b3[j][None, :], sf, deltas, dft[0], dft[1], seq_len=seq_len, tk=tk)
                hyo = _hy_conv_call(u, hy_skip[j][None, :], dft, gr, gi, batch=batch, seq_len=seq_len,
                                    tc=cfg["hy_tc"], tk=tk, bb=cfg["seq_bb"])
                xmid[g] = [attn.reshape(batch * seq_len, A_Q), hyo.reshape(batch * seq_len, HY_CH)]
            head_norm = None
        else:
            w_in = ml_w_in[j].astype(BF16)
            w_g = jnp.pad(ml_w_in[j][:, 4 * ML_W:], ((0, 0), (0, LANES - 4 * ML_HEADS))).astype(BF16)
            b_g = jnp.pad(ml_b_gates[j], (0, LANES - 4 * ML_HEADS))[None, :]
            w_out = ml_w_out[j].astype(BF16)
            conv = jnp.pad(_pad_rows(jnp.concatenate([ml_conv_w[j], ml_conv_b[j][None, :]], axis=0)),
                           ((0, 0), (0, 2 * ML_W)))
            for g, cfg in groups.items():
                batch, seq_len, tm, rpm = cfg["batch"], cfg["seq_len"], cfg["tm"], cfg["rows_per_mod"]
                q, kt, v, og, gates = _ml_inproj_call(xin[g], mods[g], gains, w_in, w_g, b_g, conv, tm=512,
                                                      cw=cfg["ml_cw"], seq_len=seq_len, rows_per_mod=rpm)
                if g == "s":
                    init, m0 = (state_mlstm_C[:, j], state_mlstm_n[:, j]), state_mlstm_m[:, j]
                else:
                    init = m0 = None
                cols, rows, m_new = _ml_gate_call(gates, m0, batch=batch, seq_len=seq_len, chunk=ML_SCAN_CHUNK)
                hf, hb, c_new, n_new = _ml_scan_call(q, kt, v, cols, rows, init, batch=batch, seq_len=seq_len,
                                                     chunk=ML_SCAN_CHUNK, cps=2)
                if g == "p":
                    new_c, new_n, new_m = c_new[:, None], n_new[:, None], m_new[:, None, :, :, 0]
                xmid[g] = [hf.reshape(batch * seq_len, ML_W), hb.reshape(batch * seq_len, ML_W), og]
            head_norm = ml_head_norm[j][None, :]
        conv = _pad_rows(jnp.concatenate([ffn_conv_w[l], ffn_conv_b[l][None, :]], axis=0))
        xout = {}
        for g, cfg in groups.items():
            xout[g] = _mix_ffn_call(xin[g], xmid[g], w_out, head_norm, mods[g], gains, ffn_up16, conv, ffn_down16,
                                    l, tm=cfg["ffn_tm"], tf=cfg["ffn_tf"], seq_len=cfg["seq_len"],
                                    rows_per_mod=cfg["rows_per_mod"])
        xp, xs = xout["p"], xout["s"]

    return (xp.reshape(bp, lp, D_MODEL), xs.reshape(bs, ls, D_MODEL), new_k, new_v, new_c, new_n, new_m)
```

```python
import functools
import math

import numpy as np
import jax
import jax.numpy as jnp
from jax import lax
from jax.experimental import pallas as pl
from jax.experimental.pallas import tpu as pltpu

F32 = jnp.float32
BF16 = jnp.bfloat16

D_MODEL = 1024
DEPTH = 2
GRID_W = 64
A_HEADS = 4
A_KV_HEADS = 2
A_HEAD_DIM = 128
A_Q = A_HEADS * A_HEAD_DIM
A_KV = A_KV_HEADS * A_HEAD_DIM
ROPE_THETA = 10000.0
HY_CH = D_MODEL // 2
HY_BANDS = 8
HY_EMB = 1 + 2 * HY_BANDS
HY_W = 64
HY_TARGET = 1e-2
HY_FAST_PCT = 0.3
HY_SLOW_PCT = 1.5
HY_MAX_DECAY = math.log(HY_TARGET) / HY_FAST_PCT
HY_MIN_DECAY = math.log(HY_TARGET) / HY_SLOW_PCT
AH_IN = A_Q + 2 * A_KV + 3 * HY_CH
ML_HEADS = 8
ML_HEAD_DIM = D_MODEL // ML_HEADS
ML_W = ML_HEADS * ML_HEAD_DIM
D_FF = 2816
NORM_EPS = 1e-6
NEG_BIG = -1e30

LANES = 128
VMEM_LIMIT = 56 * 1024 * 1024
COL_TILE = 256
ML_SCAN_CHUNK = 128
INV_SQRT2 = 1.0 / math.sqrt(2.0)


def _cparams(*sem):
    return pltpu.CompilerParams(dimension_semantics=sem, vmem_limit_bytes=VMEM_LIMIT)


def _rms(x, gain):
    return x * lax.rsqrt(jnp.mean(x * x, axis=-1, keepdims=True) + NORM_EPS) * gain


def _modulated(x, gain, shift, scale):
    return _rms(x, gain * (1.0 + scale)) + shift


def _dot(a, b):
    return jnp.dot(a, b, preferred_element_type=F32)


def _dot_nt(a, b):
    return lax.dot_general(a, b, (((1,), (1,)), ((), ())), preferred_element_type=F32)


def _dot_tn(a, b):
    return lax.dot_general(a, b, (((0,), (0,)), ((), ())), preferred_element_type=F32)


def _split3(x):
    hi = x.astype(BF16)
    r1 = x - hi.astype(F32)
    mid = r1.astype(BF16)
    lo = (r1 - mid.astype(F32)).astype(BF16)
    return hi, mid, lo


def _dot_x3(a, b):
    a0 = a.astype(BF16)
    a1 = (a - a0.astype(F32)).astype(BF16)
    b0 = b.astype(BF16)
    b1 = (b - b0.astype(F32)).astype(BF16)
    return _dot(a0, b0) + (_dot(a0, b1) + _dot(a1, b0))


def _dwconv3(y, conv_ref, seq_len, edge=None):
    rows, cols = y.shape
    pos = lax.broadcasted_iota(jnp.int32, (rows, LANES), 0)
    if edge is None:
        pos = pos % seq_len
        first = pos == 0
        last = pos == seq_len - 1
    else:
        first = pos == 0
        last = pos == rows - 1
    outs = []
    for c0 in range(0, cols, LANES):
        yc = y[:, c0:c0 + LANES]
        before = 0.0 if edge is None else edge[0][:, c0:c0 + LANES]
        after = 0.0 if edge is None else edge[1][:, c0:c0 + LANES]
        prev = jnp.where(first, before, pltpu.roll(yc, 1, 0))
        nxt = jnp.where(last, after, pltpu.roll(yc, rows - 1, 0))
        outs.append(prev * conv_ref[0:1, c0:c0 + LANES] + yc * conv_ref[1:2, c0:c0 + LANES]
                    + nxt * conv_ref[2:3, c0:c0 + LANES] + conv_ref[3:4, c0:c0 + LANES])
    return outs[0] if len(outs) == 1 else jnp.concatenate(outs, axis=1)


def _mod_kernel(c_ref, w_ref, b_ref, o_ref):
    a = c_ref[...]
    a = a * jax.nn.sigmoid(a)
    o_ref[0] = _dot(a.astype(BF16), w_ref[0].astype(BF16)) + b_ref[0]


def _mod_call(cc, w_mod, b_mod):
    tn = 1536
    n = 6 * D_MODEL
    return pl.pallas_call(
        _mod_kernel,
        grid=(DEPTH, n // tn),
        in_specs=[pl.BlockSpec((8, D_MODEL), lambda l, j: (0, 0)),
                  pl.BlockSpec((1, D_MODEL, tn), lambda l, j: (l, 0, j)),
                  pl.BlockSpec((1, 1, tn), lambda l, j: (l, 0, j))],
        out_specs=pl.BlockSpec((1, 8, tn), lambda l, j: (l, 0, j)),
        out_shape=jax.ShapeDtypeStruct((DEPTH, 8, n), F32),
        compiler_params=_cparams("arbitrary", "arbitrary"),
        name="mod",
    )(cc, w_mod, b_mod.reshape(DEPTH, 1, n))


def _ah_inproj_kernel(*refs, seq_len, rope, halo, cw):
    it = iter(refs)
    x_ref = next(it)
    xb_ref, xa_ref = (next(it), next(it)) if halo else (None, None)
    mod_ref, gain_ref, w_ref, qkn_ref, conv_ref = (next(it) for _ in range(5))
    cc_ref, ss_ref = (next(it), next(it)) if rope else (None, None)
    q_ref, k_ref, v_ref, u_ref = (next(it) for _ in range(4))
    kf_ref, vf_ref = (None, None) if rope else (next(it), next(it))
    tm = x_ref.shape[0]
    dh = A_HEAD_DIM

    def modulated(x):
        return _modulated(x, gain_ref[0:1, :], mod_ref[0, 0:1, :], mod_ref[0, 1:2, :]).astype(BF16)

    def head(yh, g):
        yh = _rms(yh, g)
        if rope:
            yh = yh * cc_ref[...] + pltpu.roll(yh, dh // 2, 1) * ss_ref[...]
        return yh

    xn = modulated(x_ref[...])
    yq = _dot(xn, w_ref[:, 0:A_Q])
    for h in range(A_HEADS):
        sl = slice(h * dh, (h + 1) * dh)
        q_ref[:, sl] = (head(yq[:, sl], qkn_ref[0:1, :]) * (dh ** -0.5)).astype(BF16)
    ykv = _dot(xn, w_ref[:, A_Q:A_Q + 2 * A_KV])
    for h in range(A_KV_HEADS):
        sl = slice(h * dh, (h + 1) * dh)
        kh = head(ykv[:, sl], qkn_ref[1:2, :])
        k_ref[:, sl] = kh.astype(BF16)
        if kf_ref is not None:
            kf_ref[pl.ds(h, tm, stride=A_KV_HEADS), :] = kh
    yv = ykv[:, A_KV:2 * A_KV]
    v_ref[...] = yv.astype(BF16)
    if vf_ref is not None:
        for h in range(A_KV_HEADS):
            vf_ref[pl.ds(h, tm, stride=A_KV_HEADS), :] = yv[:, h * dh:(h + 1) * dh]

    xe = xn
    if halo:
        xe = jnp.concatenate([xn, modulated(jnp.concatenate([xb_ref[...], xa_ref[...]], axis=0))], axis=0)
    u0 = A_Q + 2 * A_KV
    for t in range(3 * HY_CH // cw):
        cols = slice(u0 + t * cw, u0 + (t + 1) * cw)
        y = _dot(xe, w_ref[:, cols])
        edge = None
        if halo:
            y, edge = _halo_edges(y, tm, seq_len)
        u_ref[:, t * cw:(t + 1) * cw] = _dwconv3(y, conv_ref.at[:, cols], seq_len, edge).astype(BF16)


def _ah_inproj_call(x, mod, gains, w, qkn, conv, rope_tabs, *, tm, cw, seq_len, rows_per_mod):
    rows = x.shape[0]
    rope = rope_tabs is not None
    halo = tm < seq_len
    assert (seq_len % tm == 0) if halo else (tm % seq_len == 0)
    row_blk = lambda cols: pl.BlockSpec((tm, cols), lambda i: (i, 0))
    out_specs = [row_blk(A_Q), row_blk(A_KV), row_blk(A_KV), row_blk(3 * HY_CH)]
    out_shape = [jax.ShapeDtypeStruct((rows, A_Q), BF16), jax.ShapeDtypeStruct((rows, A_KV), BF16),
                 jax.ShapeDtypeStruct((rows, A_KV), BF16), jax.ShapeDtypeStruct((rows, 3 * HY_CH), BF16)]
    if not rope:
        out_specs += [pl.BlockSpec((tm * A_KV_HEADS, A_HEAD_DIM), lambda i: (i, 0))] * 2
        out_shape += [jax.ShapeDtypeStruct((rows * A_KV_HEADS, A_HEAD_DIM), F32)] * 2
    in_specs = [row_blk(D_MODEL)] + (_halo_specs(tm, rows) if halo else []) + [
        pl.BlockSpec((1, 8, D_MODEL), lambda i: (i * tm // rows_per_mod, 0, 0)),
        pl.BlockSpec((8, D_MODEL), lambda i: (0, 0)),
        pl.BlockSpec((D_MODEL, AH_IN), lambda i: (0, 0)),
        pl.BlockSpec((8, A_HEAD_DIM), lambda i: (0, 0)),
        pl.BlockSpec((8, AH_IN), lambda i: (0, 0))]
    args = ([x, x, x] if halo else [x]) + [mod, gains, w, qkn, conv]
    if rope:
        per_seq = max(seq_len // tm, 1)
        in_specs += [pl.BlockSpec((tm, A_HEAD_DIM), lambda i: (i % per_seq, 0))] * 2
        args += list(rope_tabs)
    return pl.pallas_call(
        functools.partial(_ah_inproj_kernel, seq_len=seq_len, rope=rope, halo=halo, cw=cw),
        grid=(rows // tm,),
        in_specs=in_specs,
        out_specs=out_specs,
        out_shape=out_shape,
        compiler_params=_cparams("arbitrary"),
        name="ah_inproj",
    )(*args)


def _attn_kernel(*refs, ctx, bb):
    if ctx:
        q_ref, k_ref, v_ref, kc_ref, vc_ref, o_ref = refs
    else:
        q_ref, k_ref, v_ref, o_ref = refs
    dh = A_HEAD_DIM
    for bi in range(bb):
        k = k_ref[bi]
        if ctx:
            kc = kc_ref[bi]
        else:
            vaug = jnp.concatenate([v_ref[bi], jnp.ones((k.shape[0], dh), BF16)], axis=1)
        for g in range(A_HEADS // A_KV_HEADS):
            sl = slice(g * dh, (g + 1) * dh)
            q = q_ref[bi, :, sl]
            s = _dot_nt(q, k)
            m = jnp.max(s, axis=-1, keepdims=True)
            if ctx:
                sc = _dot_nt(q, kc)
                m = jnp.maximum(m, jnp.max(sc, axis=-1, keepdims=True))
            if ctx:
                p = jnp.exp(s - m)
                pc = jnp.exp(sc - m)
                den = jnp.sum(p, axis=-1, keepdims=True) + jnp.sum(pc, axis=-1, keepdims=True)
                o = _dot(p.astype(BF16), v_ref[bi]) + _dot(pc.astype(BF16), vc_ref[bi])
                o_ref[bi, :, sl] = (o / den).astype(BF16)
            else:
                o = _dot(jnp.exp(s - m).astype(BF16), vaug)
                o_ref[bi, :, sl] = (o[:, :dh] / o[:, dh:]).astype(BF16)


def _attn_call(q, k, v, ctx_k, ctx_v, *, batch, seq_len, tq, bb):
    ctx = ctx_k is not None
    qw = A_Q // A_KV_HEADS
    in_specs = [pl.BlockSpec((bb, tq, qw), lambda b, h, i: (b, i, h)),
                pl.BlockSpec((bb, seq_len, A_HEAD_DIM), lambda b, h, i: (b, 0, h)),
                pl.BlockSpec((bb, seq_len, A_HEAD_DIM), lambda b, h, i: (b, 0, h))]
    args = [q.reshape(batch, seq_len, A_Q), k.reshape(batch, seq_len, A_KV), v.reshape(batch, seq_len, A_KV)]
    if ctx:
        past = ctx_k.shape[1]
        in_specs += [pl.BlockSpec((bb, past, A_HEAD_DIM), lambda b, h, i: (b, 0, h))] * 2
        args += [ctx_k, ctx_v]
    return pl.pallas_call(
        functools.partial(_attn_kernel, ctx=ctx, bb=bb),
        grid=(batch // bb, A_KV_HEADS, seq_len // tq),
        in_specs=in_specs,
        out_specs=pl.BlockSpec((bb, tq, qw), lambda b, h, i: (b, i, h)),
        out_shape=jax.ShapeDtypeStruct((batch, seq_len, A_Q), BF16),
        compiler_params=_cparams("arbitrary", "arbitrary", "arbitrary"),
        name="attn",
    )(*args)


def _hy_filter_kernel(z_ref, w1_ref, b1_ref, w2_ref, b2_ref, w3_ref, b3_ref, sf_ref, dl_ref,
                      cm_ref, sm_ref, gr_ref, gi_ref, hs_ref, hd_ref):
    @pl.when(pl.program_id(0) == 0)
    def _():
        z = z_ref[...]
        h = jnp.sin(sf_ref[0:1, :] * (_dot_x3(z, w1_ref[...]) + b1_ref[...]))
        h = jnp.sin(sf_ref[1:2, :] * (_dot_x3(h, w2_ref[...]) + b2_ref[...]))
        filt = _dot_x3(h, w3_ref[...]) + b3_ref[...]
        win = jnp.exp(-z[:, 0:1] * dl_ref[...])
        hf = filt[:, :HY_CH] * win
        hb = filt[:, HY_CH:] * win
        hs_ref[...] = (hf + hb).astype(BF16)
        hd_ref[...] = (hb - hf).astype(BF16)

    gr_ref[...] = _dot(cm_ref[...], hs_ref[...])
    gi_ref[...] = _dot(sm_ref[...], hd_ref[...])


def _hy_filter_call(z, w1, b1, w2, b2, w3, b3, sf, deltas, cm, sm, *, seq_len, tk):
    full = lambda a: pl.BlockSpec(a.shape, lambda k: (0,) * a.ndim)
    small = [z, w1, b1, w2, b2, w3, b3, sf, deltas]
    return pl.pallas_call(
        _hy_filter_kernel,
        grid=(seq_len // tk,),
        in_specs=[full(a) for a in small] + [pl.BlockSpec((tk, seq_len), lambda k: (k, 0))] * 2,
        out_specs=[pl.BlockSpec((tk, HY_CH), lambda k: (k, 0))] * 2,
        out_shape=[jax.ShapeDtypeStruct((seq_len, HY_CH), F32)] * 2,
        scratch_shapes=[pltpu.VMEM((seq_len, HY_CH), BF16)] * 2,
        compiler_params=_cparams("arbitrary"),
        name="hy_filter",
    )(*small, cm, sm)


def _hy_conv_kernel(x0_ref, x1_ref, v_ref, skip_ref, cm_ref, sm_ref, cmt_ref, smt_ref, gr_ref, gi_ref,
                    o_ref, vv_ref, acc_ref, *, seq_len, bb):
    kb = pl.program_id(2)
    single = seq_len == cm_ref.shape[0]
    gr = gr_ref[...]
    gi = gi_ref[...]
    for bi in range(bb):
        def gated():
            return v_ref[bi].astype(F32) * x1_ref[bi].astype(F32)

        if single:
            vv = gated().astype(BF16)
        else:
            @pl.when(kb == 0)
            def _():
                vv_ref[bi] = gated().astype(BF16)
                acc_ref[bi] = jnp.zeros_like(acc_ref[bi])

            vv = vv_ref[bi]
        vr = _dot(cm_ref[...], vv)
        wi = _dot(sm_ref[...], vv)
        pr = gr * vr + gi * wi
        qi = gr * wi - gi * vr
        y = _dot(cmt_ref[...], pr.astype(BF16)) + _dot(smt_ref[...], qi.astype(BF16))

        def finish(total):
            o_ref[bi] = ((total * (1.0 / seq_len) + skip_ref[...] * gated())
                         * x0_ref[bi].astype(F32)).astype(BF16)

        if single:
            finish(y)
        else:
            acc_ref[bi] += y

            @pl.when(kb == pl.num_programs(2) - 1)
            def _():
                finish(acc_ref[bi])


def _hy_conv_call(u, skip, dft, gr, gi, *, batch, seq_len, tc, tk, bb):
    proj3 = u.reshape(batch, seq_len, 3 * HY_CH)
    cm, sm, cmt, smt = dft
    nch = HY_CH // tc
    return pl.pallas_call(
        functools.partial(_hy_conv_kernel, seq_len=seq_len, bb=bb),
        grid=(batch // bb, nch, seq_len // tk),
        in_specs=[pl.BlockSpec((bb, seq_len, tc), lambda b, c, k: (b, 0, c)),
                  pl.BlockSpec((bb, seq_len, tc), lambda b, c, k: (b, 0, nch + c)),
                  pl.BlockSpec((bb, seq_len, tc), lambda b, c, k: (b, 0, 2 * nch + c)),
                  pl.BlockSpec((1, tc), lambda b, c, k: (0, c)),
                  pl.BlockSpec((tk, seq_len), lambda b, c, k: (k, 0)),
                  pl.BlockSpec((tk, seq_len), lambda b, c, k: (k, 0)),
                  pl.BlockSpec((seq_len, tk), lambda b, c, k: (0, k)),
                  pl.BlockSpec((seq_len, tk), lambda b, c, k: (0, k)),
                  pl.BlockSpec((tk, tc), lambda b, c, k: (k, c)),
                  pl.BlockSpec((tk, tc), lambda b, c, k: (k, c))],
        out_specs=pl.BlockSpec((bb, seq_len, tc), lambda b, c, k: (b, 0, c)),
        out_shape=jax.ShapeDtypeStruct((batch, seq_len, HY_CH), BF16),
        scratch_shapes=[pltpu.VMEM((bb, seq_len, tc), BF16), pltpu.VMEM((bb, seq_len, tc), F32)],
        compiler_params=_cparams("arbitrary", "arbitrary", "arbitrary"),
        name="hy_conv",
    )(proj3, proj3, proj3, skip, cm, sm, cmt, smt, gr, gi)


HALO_ROWS = 16


def _mix_ffn_kernel(*refs, seq_len, halo, tf, mixer):
    it = iter(refs)

    def rows_of():
        parts = [next(it)[...] for _ in range(3 if halo else 1)]
        return parts[0] if len(parts) == 1 else jnp.concatenate(parts, axis=0)

    x = rows_of()
    mix_in = [rows_of() for _ in range(2 if mixer == "ah" else 3)]
    mod_ref, gain_ref, wo_ref = next(it), next(it), next(it)
    hn_ref = next(it) if mixer == "ml" else None
    wu_ref, conv_ref, wd_ref, o_ref = next(it), next(it), next(it), next(it)
    tm = o_ref.shape[0]

    if mixer == "ah":
        mixed = _dot(mix_in[0], wo_ref[0:A_Q, :]) + _dot(mix_in[1], wo_ref[A_Q:, :])
    else:
        h = mix_in[0].astype(F32) + mix_in[1].astype(F32)
        parts = []
        for hd in range(ML_HEADS):
            sl = slice(hd * ML_HEAD_DIM, (hd + 1) * ML_HEAD_DIM)
            parts.append(_rms(h[:, sl], hn_ref[0:1, sl]))
        gated = jnp.concatenate(parts, axis=1) * jax.nn.sigmoid(mix_in[2].astype(F32))
        mixed = _dot(gated.astype(BF16), wo_ref[...])
    x_mid = x + _rms(mixed, gain_ref[1:2, :] * mod_ref[0, 2:3, :])

    xn = _modulated(x_mid, gain_ref[2:3, :], mod_ref[0, 3:4, :], mod_ref[0, 4:5, :]).astype(BF16)
    out = None
    for t in range(D_FF // tf):
        cg = slice(t * tf, (t + 1) * tf)
        cl = slice(D_FF + t * tf, D_FF + (t + 1) * tf)
        yg = _dot(xn, wu_ref[0, :, cg])
        yl = _dot(xn, wu_ref[0, :, cl])
        edge_g = edge_l = None
        if halo:
            yg, edge_g = _halo_edges(yg, tm, seq_len, HALO_ROWS)
            yl, edge_l = _halo_edges(yl, tm, seq_len, HALO_ROWS)
        hg = _dwconv3(yg, conv_ref.at[:, cg], seq_len, edge_g)
        hl = _dwconv3(yl, conv_ref.at[:, cl], seq_len, edge_l)
        act = (0.5 * hg * (1.0 + lax.erf(hg * INV_SQRT2))) * hl
        part = _dot(act.astype(BF16), wd_ref[0, cg, :])
        out = part if out is None else out + part
    o_ref[...] = x_mid[0:tm, :] + _rms(out, gain_ref[3:4, :] * mod_ref[0, 5:6, :])


def _mix_ffn_call(x, mix_in, w_out, head_norm, mod, gains, w_up, conv, w_down, layer, *,
                  tm, tf, seq_len, rows_per_mod):
    rows = x.shape[0]
    mixer = "ah" if head_norm is None else "ml"
    halo = tm < seq_len
    assert (seq_len % tm == 0) if halo else (tm % seq_len == 0)
    row_specs, row_args = [], []
    for a in [x] + list(mix_in):
        cols = a.shape[1]
        row_specs.append(pl.BlockSpec((tm, cols), lambda i: (i, 0)))
        row_args.append(a)
        if halo:
            row_specs += _halo_specs(tm, rows, cols, HALO_ROWS)
            row_args += [a, a]
    whole = lambda a: pl.BlockSpec(a.shape, lambda i: (0,) * a.ndim)
    resident = pl.Buffered(1)
    extra = [] if head_norm is None else [head_norm]
    return pl.pallas_call(
        functools.partial(_mix_ffn_kernel, seq_len=seq_len, halo=halo, tf=tf, mixer=mixer),
        grid=(rows // tm,),
        in_specs=row_specs + [
            pl.BlockSpec((1, 8, D_MODEL), lambda i: (i * tm // rows_per_mod, 0, 0)),
            whole(gains), whole(w_out)] + [whole(a) for a in extra] + [
            pl.BlockSpec((1, D_MODEL, 2 * D_FF), lambda i: (layer, 0, 0), pipeline_mode=resident),
            whole(conv),
            pl.BlockSpec((1, D_FF, D_MODEL), lambda i: (layer, 0, 0), pipeline_mode=resident)],
        out_specs=pl.BlockSpec((tm, D_MODEL), lambda i: (i, 0)),
        out_shape=jax.ShapeDtypeStruct((rows, D_MODEL), F32),
        compiler_params=_cparams("arbitrary"),
        name="mix_ffn",
    )(*row_args, mod, gains, w_out, *extra, w_up, conv, w_down)


def _halo_specs(tm, rows, cols=D_MODEL, sub=8):
    per = tm // sub
    return [pl.BlockSpec((sub, cols), lambda i, *_: (jnp.maximum(i * per - 1, 0), 0)),
            pl.BlockSpec((sub, cols), lambda i, *_: (jnp.minimum((i + 1) * per, rows // sub - 1), 0))]


def _halo_edges(y, tm, seq_len, sub=8):
    i = pl.program_id(0)
    at_start = (i * tm) % seq_len == 0
    at_end = ((i + 1) * tm) % seq_len == 0
    edge = (jnp.where(at_start, 0.0, y[tm + sub - 1:tm + sub, :]),
            jnp.where(at_end, 0.0, y[tm + sub:tm + sub + 1, :]))
    return y[0:tm, :], edge


def _ml_inproj_kernel(*refs, seq_len, cw, halo):
    if halo:
        x_ref, xb_ref, xa_ref, mod_ref, gain_ref, w_ref, wg_ref, bg_ref, conv_ref = refs[:9]
    else:
        x_ref, mod_ref, gain_ref, w_ref, wg_ref, bg_ref, conv_ref = refs[:7]
    q_ref, kt_ref, v_ref, o_ref, g_ref = refs[-5:]
    tm = x_ref.shape[0]

    def modulated(x):
        return _modulated(x, gain_ref[0:1, :], mod_ref[0, 0:1, :], mod_ref[0, 1:2, :]).astype(BF16)

    xn = modulated(x_ref[...])
    g_ref[...] = (_dot(xn, wg_ref[...]) + bg_ref[...]).T
    xe = xn
    if halo:
        xe = jnp.concatenate([xn, modulated(jnp.concatenate([xb_ref[...], xa_ref[...]], axis=0))], axis=0)

    for t in range(2 * ML_W // cw):
        cols = slice(t * cw, (t + 1) * cw)
        y = _dot(xe, w_ref[:, cols])
        edge = None
        if halo:
            y, edge = _halo_edges(y, tm, seq_len)
        z = _dwconv3(y, conv_ref.at[:, cols], seq_len, edge)
        z = z * jax.nn.sigmoid(z)
        if t * cw < ML_W:
            q_ref[:, cols] = z.astype(BF16)
        else:
            kt_ref[t * cw - ML_W:(t + 1) * cw - ML_W, :] = (z * (ML_HEAD_DIM ** -0.5)).T.astype(BF16)
        y = _dot(xn, w_ref[:, 2 * ML_W + t * cw:2 * ML_W + (t + 1) * cw]).astype(BF16)
        if t * cw < ML_W:
            v_ref[:, t * cw:(t + 1) * cw] = y
        else:
            o_ref[:, t * cw - ML_W:(t + 1) * cw - ML_W] = y


def _ml_inproj_call(x, mod, gains, w, wg, bg, conv, *, tm, cw, seq_len, rows_per_mod):
    rows = x.shape[0]
    n = 4 * ML_W
    halo = tm < seq_len
    assert (seq_len % tm == 0) if halo else (tm % seq_len == 0)
    x_specs = [pl.BlockSpec((tm, D_MODEL), lambda i: (i, 0))] + (_halo_specs(tm, rows) if halo else [])
    return pl.pallas_call(
        functools.partial(_ml_inproj_kernel, seq_len=seq_len, cw=cw, halo=halo),
        grid=(rows // tm,),
        in_specs=x_specs + [
            pl.BlockSpec((1, 8, D_MODEL), lambda i: (i * tm // rows_per_mod, 0, 0)),
            pl.BlockSpec((8, D_MODEL), lambda i: (0, 0)),
            pl.BlockSpec((D_MODEL, n), lambda i: (0, 0)),
            pl.BlockSpec((D_MODEL, LANES), lambda i: (0, 0)),
            pl.BlockSpec((1, LANES), lambda i: (0, 0)),
            pl.BlockSpec((8, n), lambda i: (0, 0))],
        out_specs=[pl.BlockSpec((tm, ML_W), lambda i: (i, 0)),
                   pl.BlockSpec((ML_W, tm), lambda i: (0, i)),
                   pl.BlockSpec((tm, ML_W), lambda i: (i, 0)),
                   pl.BlockSpec((tm, ML_W), lambda i: (i, 0)),
                   pl.BlockSpec((LANES, tm), lambda i: (0, i))],
        out_shape=[jax.ShapeDtypeStruct((rows, ML_W), BF16), jax.ShapeDtypeStruct((ML_W, rows), BF16),
                   jax.ShapeDtypeStruct((rows, ML_W), BF16), jax.ShapeDtypeStruct((rows, ML_W), BF16),
                   jax.ShapeDtypeStruct((LANES, rows), F32)],
        compiler_params=_cparams("arbitrary"),
        name="ml_inproj",
    )(*([x, x, x] if halo else [x]), mod, gains, w, wg, bg, conv)


def _log_sigmoid(x):
    return jnp.minimum(x, 0.0) - jnp.log1p(jnp.exp(-jnp.abs(x)))


def _lane_scan(x, op, reverse, fill, seg):
    n = x.shape[1]
    pos = lax.broadcasted_iota(jnp.int32, x.shape, 1) % seg
    sh = 1
    while sh < seg:
        if reverse:
            x = op(x, jnp.where(pos < seg - sh, pltpu.roll(x, n - sh, 1), fill))
        else:
            x = op(x, jnp.where(pos >= sh, pltpu.roll(x, sh, 1), fill))
        sh *= 2
    return x


def _bcast_selectors(chunk):
    sel = np.zeros((ML_HEADS, LANES, chunk + LANES), np.float32)
    for h in range(ML_HEADS):
        for g in range(3):
            sel[h, g * ML_HEADS + h, :chunk] = 1.0
            sel[h, (3 + g) * ML_HEADS + h, chunk:] = 1.0
    return jnp.asarray(sel, BF16)


def _ml_scan_kernel(*refs, chunk, cps, has_init, whole_seq):
    if has_init:
        (qf, ktf, vf, qb, ktb, vb, colf, colb, rowf, rowb, sel_ref, c0_ref, n0_ref,
         hf_ref, hb_ref, c_ref, n_ref, s_sc, p16_sc, e_sc) = refs
    else:
        (qf, ktf, vf, qb, ktb, vb, colf, colb, rowf, rowb, sel_ref,
         hf_ref, hb_ref, c_ref, n_ref, s_sc, p16_sc, e_sc) = refs
    nh, dh = ML_HEADS, ML_HEAD_DIM
    step = pl.program_id(1)
    zero_start = (not has_init) and whole_seq

    if not zero_start:
        @pl.when(step == 0)
        def _():
            if has_init:
                for d in range(2):
                    for h in range(nh):
                        s_sc[d, h, :, 0:dh] = c0_ref[0, d, h].T
                        s_sc[d, h, :, dh:2 * dh] = jnp.broadcast_to(n0_ref[0, d, h:h + 1, :], (dh, dh)).T
            else:
                s_sc[...] = jnp.zeros_like(s_sc)

    row = lax.broadcasted_iota(jnp.int32, (chunk, chunk), 0)
    col = lax.broadcasted_iota(jnp.int32, (chunk, chunk), 1)
    ones_cols = jnp.ones((chunk, dh), BF16)

    dirs = ((qf, ktf, vf, colf, rowf, hf_ref), (qb, ktb, vb, colb, rowb, hb_ref))
    for sub in range(cps):
        for d, (q_ref, kt_ref, _, col_ref, row_ref, _) in enumerate(dirs):
            reverse = d == 1
            lc = cps - 1 - sub if reverse else sub
            tok = slice(lc * chunk, (lc + 1) * chunk)
            cols = col_ref[0, lc, 0]
            r = row_ref[0, lc, 0, 0:nh, :]
            m_prev = row_ref[0, lc, 0, 3 * nh:4 * nh, :]
            mask = (col >= row) if reverse else (col <= row)
            for h in range(nh):
                sl = slice(h * dh, (h + 1) * dh)
                bc = _dot(cols, sel_ref[h])
                p_bc = bc[:, :chunk]
                w = jnp.exp(jnp.where(mask, p_bc + r[h:h + 1, :], NEG_BIG))
                p16_sc[sub, d, h] = (_dot(q_ref[0, tok, sl], kt_ref[sl, tok]) * w).astype(BF16)
                if not (zero_start and sub == 0):
                    e_sc[sub, d, h, :, 0:dh] = jnp.exp(p_bc[:, :dh] + m_prev[h:h + 1, 0:dh])
                e_sc[sub, d, h, :, dh:2 * dh] = jnp.exp(bc[:, chunk:])
    for sub in range(cps):
        for d, (q_ref, kt_ref, v_ref, _, row_ref, h_ref) in enumerate(dirs):
            lc = cps - 1 - sub if d == 1 else sub
            tok = slice(lc * chunk, (lc + 1) * chunk)
            w_tok = row_ref[0, lc, 0, nh:2 * nh, :]
            w_state = row_ref[0, lc, 0, 2 * nh:3 * nh, :]
            for h in range(nh):
                sl = slice(h * dh, (h + 1) * dh)
                q16 = q_ref[0, tok, sl]
                kt16 = kt_ref[sl, tok]
                vaug = jnp.concatenate([v_ref[0, tok, sl], ones_cols], axis=1)
                tot = _dot(p16_sc[sub, d, h], vaug)
                kts = (kt16.astype(F32) * w_tok[h:h + 1, :]).astype(BF16)
                s_new = _dot(kts, vaug)
                if not (zero_start and sub == 0):
                    s_old = s_sc[d, h]
                    w_inter = e_sc[sub, d, h, :, 0:dh]
                    tot = tot + jnp.concatenate([w_inter, w_inter], axis=1) * _dot(q16, s_old.astype(BF16))
                    ws = w_state[h:h + 1, 0:dh]
                    s_new = jnp.concatenate([ws, ws], axis=1) * s_old + s_new
                h_ref[0, tok, sl] = (tot[:, :dh] / jnp.maximum(jnp.abs(tot[:, dh:]),
                                                               e_sc[sub, d, h, :, dh:2 * dh])).astype(BF16)
                s_sc[d, h] = s_new

    @pl.when(step == pl.num_programs(1) - 1)
    def _():
        for d in range(2):
            for h in range(nh):
                s_fin = s_sc[d, h]
                c_ref[0, d, h] = s_fin[:, 0:dh].T
                n_ref[0, d, h:h + 1, :] = s_fin[:, dh:2 * dh].T[0:1, :]


def _ml_gate_kernel(*refs, nc, chunk, bb, has_init):
    if has_init:
        gt_ref, m0_ref, cols_ref, rows_ref, m_ref = refs
    else:
        gt_ref, cols_ref, rows_ref, m_ref = refs
    nh = ML_HEADS
    pad = jnp.zeros((LANES - 6 * nh, chunk), F32)
    for d in range(2):
        reverse = d == 1
        last = 0 if reverse else chunk - 1
        b_all = _lane_scan(_log_sigmoid(gt_ref[nh * (2 + d):nh * (3 + d), :]), jnp.add, reverse, 0.0, chunk)
        r_all = gt_ref[nh * d:nh * (d + 1), :] - b_all
        g_all = _lane_scan(r_all, jnp.maximum, reverse, -jnp.inf, chunk)
        for bi in range(bb):
            m = m0_ref[bi, d] if has_init else jnp.zeros((nh, chunk), F32)
            for c in (range(nc - 1, -1, -1) if reverse else range(nc)):
                c0 = (bi * nc + c) * chunk
                b = b_all[:, c0:c0 + chunk]
                r = r_all[:, c0:c0 + chunk]
                mx = jnp.maximum(m, g_all[:, c0:c0 + chunk])
                mt = b + mx
                m_new = jnp.broadcast_to(mt[:, last:last + 1], (nh, chunk))
                b_last = jnp.broadcast_to(b[:, last:last + 1], (nh, chunk))
                rows_ref[bi, c, d, 0:nh, :] = r
                rows_ref[bi, c, d, nh:2 * nh, :] = jnp.exp(r + b_last - m_new)
                rows_ref[bi, c, d, 2 * nh:3 * nh, :] = jnp.exp(b_last + m - m_new)
                rows_ref[bi, c, d, 3 * nh:4 * nh, :] = m
                stack = []
                for x in (-mx, -mt):
                    stack += [t.astype(F32) for t in _split3(x)]
                stack.append(pad)
                cols_ref[bi, c, d] = jnp.concatenate(stack, axis=0).T.astype(BF16)
                m = m_new
            m_ref[bi, d] = m[:, 0:LANES]


def _ml_gate_call(gates, m0, *, batch, seq_len, chunk):
    nc = seq_len // chunk
    bb = max(1, 16 // nc)
    nh = ML_HEADS
    has_init = m0 is not None
    in_specs = [pl.BlockSpec((LANES, bb * seq_len), lambda i: (0, i))]
    args = [gates]
    if has_init:
        in_specs.append(pl.BlockSpec((bb, 2, nh, chunk), lambda i: (i, 0, 0, 0)))
        args.append(jnp.broadcast_to(m0[..., None], m0.shape + (chunk,)))
    return pl.pallas_call(
        functools.partial(_ml_gate_kernel, nc=nc, chunk=chunk, bb=bb, has_init=has_init),
        grid=(batch // bb,),
        in_specs=in_specs,
        out_specs=[pl.BlockSpec((bb, nc, 2, chunk, LANES), lambda i: (i, 0, 0, 0, 0)),
                   pl.BlockSpec((bb, nc, 2, 4 * nh, chunk), lambda i: (i, 0, 0, 0, 0)),
                   pl.BlockSpec((bb, 2, nh, LANES), lambda i: (i, 0, 0, 0))],
        out_shape=[jax.ShapeDtypeStruct((batch, nc, 2, chunk, LANES), BF16),
                   jax.ShapeDtypeStruct((batch, nc, 2, 4 * nh, chunk), F32),
                   jax.ShapeDtypeStruct((batch, 2, nh, LANES), F32)],
        compiler_params=_cparams("arbitrary"),
        name="ml_gate",
    )(*args)


def _ml_scan_call(q, kt, v, cols, rows, init, *, batch, seq_len, chunk, cps):
    nc = seq_len // (chunk * cps)
    nh = ML_HEADS
    blk = chunk * cps
    q3 = q.reshape(batch, seq_len, ML_W)
    v3 = v.reshape(batch, seq_len, ML_W)
    sel = _bcast_selectors(chunk)
    has_init = init is not None

    fwd3 = lambda b, i: (b, i, 0)
    bwd3 = lambda b, i: (b, nc - 1 - i, 0)
    in_specs = [pl.BlockSpec((1, blk, ML_W), fwd3),
                pl.BlockSpec((ML_W, blk), lambda b, i: (0, b * nc + i)),
                pl.BlockSpec((1, blk, ML_W), fwd3),
                pl.BlockSpec((1, blk, ML_W), bwd3),
                pl.BlockSpec((ML_W, blk), lambda b, i: (0, b * nc + nc - 1 - i)),
                pl.BlockSpec((1, blk, ML_W), bwd3),
                pl.BlockSpec((1, cps, 1, chunk, LANES), lambda b, i: (b, i, 0, 0, 0)),
                pl.BlockSpec((1, cps, 1, chunk, LANES), lambda b, i: (b, nc - 1 - i, 1, 0, 0)),
                pl.BlockSpec((1, cps, 1, 4 * nh, chunk), lambda b, i: (b, i, 0, 0, 0)),
                pl.BlockSpec((1, cps, 1, 4 * nh, chunk), lambda b, i: (b, nc - 1 - i, 1, 0, 0)),
                pl.BlockSpec(sel.shape, lambda b, i: (0, 0, 0))]
    args = [q3, kt, v3, q3, kt, v3, cols, cols, rows, rows, sel]
    c_spec = pl.BlockSpec((1, 2, nh, ML_HEAD_DIM, ML_HEAD_DIM), lambda b, i: (b, 0, 0, 0, 0))
    n_spec = pl.BlockSpec((1, 2, nh, ML_HEAD_DIM), lambda b, i: (b, 0, 0, 0))
    if has_init:
        in_specs += [c_spec, n_spec]
        args += list(init)
    return pl.pallas_call(
        functools.partial(_ml_scan_kernel, chunk=chunk, cps=cps, has_init=has_init, whole_seq=nc == 1),
        grid=(batch, nc),
        in_specs=in_specs,
        out_specs=[pl.BlockSpec((1, blk, ML_W), fwd3),
                   pl.BlockSpec((1, blk, ML_W), bwd3),
                   c_spec, n_spec],
        out_shape=[jax.ShapeDtypeStruct((batch, seq_len, ML_W), BF16),
                   jax.ShapeDtypeStruct((batch, seq_len, ML_W), BF16),
                   jax.ShapeDtypeStruct((batch, 2, nh, ML_HEAD_DIM, ML_HEAD_DIM), F32),
                   jax.ShapeDtypeStruct((batch, 2, nh, ML_HEAD_DIM), F32)],
        scratch_shapes=[pltpu.VMEM((2, nh, ML_HEAD_DIM, 2 * ML_HEAD_DIM), F32),
                        pltpu.VMEM((cps, 2, nh, chunk, chunk), BF16),
                        pltpu.VMEM((cps, 2, nh, chunk, 2 * ML_HEAD_DIM), F32)],
        compiler_params=_cparams("arbitrary", "arbitrary"),
        name="ml_scan",
    )(*args)


@functools.lru_cache(maxsize=None)
def _dft_tables_np(seq_len):
    n4 = 4 * seq_len
    ar = np.arange(seq_len, dtype=np.int64)
    idx = ((2 * ar + 1)[:, None] * ar[None, :]) % n4
    ang = (2.0 * np.pi / n4) * idx.astype(np.float64)
    cm = np.cos(ang).astype(np.float32)
    sm = np.sin(ang).astype(np.float32)
    return tuple(np.ascontiguousarray(a) for a in (cm, sm, cm.T, sm.T))


def _dft_tables(seq_len):
    return tuple(jnp.asarray(a).astype(BF16) for a in _dft_tables_np(seq_len))


def _hy_features(seq_len):
    t = np.linspace(0.0, 1.0, seq_len, dtype=np.float32)
    bands = np.arange(1, HY_BANDS + 1, dtype=np.float32)
    ang = (np.float32(2.0 * np.pi) * t[:, None]) * bands
    z = np.concatenate([t[:, None], np.cos(ang), np.sin(ang)], axis=-1).astype(np.float32)
    z = np.pad(z, ((0, 0), (0, 32 - HY_EMB)))
    deltas = np.abs(np.linspace(HY_MIN_DECAY, HY_MAX_DECAY, HY_CH, dtype=np.float32))[None, :]
    return jnp.asarray(z), jnp.asarray(deltas)


def _rope_tables(seq_len):
    rows = seq_len // GRID_W
    row = np.repeat(np.arange(rows, dtype=np.float32), GRID_W)
    col = np.tile(np.arange(GRID_W, dtype=np.float32), rows)
    n_freq = A_HEAD_DIM // 4
    inv = (np.float32(ROPE_THETA) ** (-np.arange(n_freq, dtype=np.float32) / n_freq)).astype(np.float32)
    ang = np.concatenate([row[:, None] * inv, col[:, None] * inv], axis=-1).astype(np.float32)
    cos, sin = np.cos(ang), np.sin(ang)
    return (jnp.asarray(np.concatenate([cos, cos], axis=-1), F32),
            jnp.asarray(np.concatenate([-sin, sin], axis=-1), F32))


def _pad_rows(a, rows=8):
    return jnp.pad(a, ((0, rows - a.shape[0]), (0, 0)))


def kernel(x_prompt, x_sample, cache_attn_k, cache_attn_v, state_mlstm_C, state_mlstm_n, state_mlstm_m, c, c_ctx, w_mod, b_mod, norm_mix_pre, norm_mix_post, norm_ffn_pre, norm_ffn_post, ffn_w_up, ffn_conv_w, ffn_conv_b, ffn_w_down, ah_w_in, ah_w_out, attn_q_norm, attn_k_norm, hy_conv_w, hy_conv_b, hy_w1, hy_b1, hy_w2, hy_b2, hy_w3, hy_b3, hy_sin_freq, hy_skip, ml_w_in, ml_b_gates, ml_conv_w, ml_conv_b, ml_head_norm, ml_w_out):
    bp, lp, _ = x_prompt.shape
    bs, ls, _ = x_sample.shape
    past = cache_attn_k.shape[2]
    xp = x_prompt.reshape(bp * lp, D_MODEL)
    xs = x_sample.reshape(bs * ls, D_MODEL)
    groups = {
        "p": dict(batch=bp, seq_len=lp, tm=1024, rows_per_mod=bp * lp, ffn_tm=512, ffn_tf=D_FF // 2,
                  seq_bb=4, hy_tc=HY_CH, hy_tk=256, ml_cw=COL_TILE),
        "s": dict(batch=bs, seq_len=ls, tm=ls, rows_per_mod=ls, ffn_tm=512, ffn_tf=D_FF // 2,
                  seq_bb=1, hy_tc=HY_CH, hy_tk=512, ml_cw=512),
    }
    ffn_up16 = ffn_w_up.astype(BF16)
    ffn_down16 = ffn_w_down.astype(BF16)

    cc = jnp.concatenate([c, c_ctx[None, :], jnp.zeros((8 - bs - 1, D_MODEL), F32)], axis=0)
    mod_all = _mod_call(cc, w_mod, b_mod).reshape(DEPTH, 8, 6, D_MODEL)
    mod_all = jnp.pad(mod_all, ((0, 0), (0, 0), (0, 2), (0, 0)))
    rope = _rope_tables(ls)

    new_k = new_v = new_c = new_n = new_m = None
    for l in range(DEPTH):
        j = l // 2
        mods = {"s": mod_all[l, :bs], "p": mod_all[l, bs:bs + 1]}
        gains = _pad_rows(jnp.stack([norm_mix_pre[l], norm_mix_post[l], norm_ffn_pre[l], norm_ffn_post[l]]))
        xin = {"p": xp, "s": xs}
        xmid = {}
        if l % 2 == 0:
            w_in = ah_w_in[j].astype(BF16)
            w_out = ah_w_out[j].astype(BF16)
            qkn = _pad_rows(jnp.stack([attn_q_norm[j], attn_k_norm[j]]))
            conv = jnp.pad(_pad_rows(jnp.concatenate([hy_conv_w[j], hy_conv_b[j][None, :]], axis=0)),
                           ((0, 0), (A_Q + 2 * A_KV, 0)))
            w1 = jnp.pad(hy_w1[j], ((0, 32 - HY_EMB), (0, 0)))
            sf = _pad_rows(hy_sin_freq[j])
            for g, cfg in groups.items():
                batch, seq_len, tm, rpm = cfg["batch"], cfg["seq_len"], cfg["tm"], cfg["rows_per_mod"]
                outs = _ah_inproj_call(xin[g], mods[g], gains, w_in, qkn, conv, rope if g == "s" else None,
                                       tm=512, cw=cfg["ml_cw"], seq_len=seq_len, rows_per_mod=rpm)
                q, k, v, u = outs[:4]
                if g == "s":
                    ctx_k = cache_attn_k[:, j].reshape(bs, past, A_KV).astype(BF16)
                    ctx_v = cache_attn_v[:, j].reshape(bs, past, A_KV).astype(BF16)
                else:
                    ctx_k = ctx_v = None
                    new_k = outs[4].reshape(bp, 1, lp, A_KV_HEADS, A_HEAD_DIM)
                    new_v = outs[5].reshape(bp, 1, lp, A_KV_HEADS, A_HEAD_DIM)
                attn = _attn_call(q, k, v, ctx_k, ctx_v, batch=batch, seq_len=seq_len, tq=256,
                                  bb=cfg["seq_bb"])
                z, deltas = _hy_features(seq_len)
                dft = _dft_tables(seq_len)
                tk = cfg["hy_tk"]
                gr, gi = _hy_filter_call(z, w1, hy_b1[j][None, :], hy_w2[j], hy_b2[j][None, :], hy_w3[j],
                                         hy_b3[j][None, :], sf, deltas, dft[0], dft[1], seq_len=seq_len, tk=tk)
                hyo = _hy_conv_call(u, hy_skip[j][None, :], dft, gr, gi, batch=batch, seq_len=seq_len,
                                    tc=cfg["hy_tc"], tk=tk, bb=cfg["seq_bb"])
                xmid[g] = [attn.reshape(batch * seq_len, A_Q), hyo.reshape(batch * seq_len, HY_CH)]
            head_norm = None
        else:
            w_in = ml_w_in[j].astype(BF16)
            w_g = jnp.pad(ml_w_in[j][:, 4 * ML_W:], ((0, 0), (0, LANES - 4 * ML_HEADS))).astype(BF16)
            b_g = jnp.pad(ml_b_gates[j], (0, LANES - 4 * ML_HEADS))[None, :]
            w_out = ml_w_out[j].astype(BF16)
            conv = jnp.pad(_pad_rows(jnp.concatenate([ml_conv_w[j], ml_conv_b[j][None, :]], axis=0)),
                           ((0, 0), (0, 2 * ML_W)))
            for g, cfg in groups.items():
                batch, seq_len, tm, rpm = cfg["batch"], cfg["seq_len"], cfg["tm"], cfg["rows_per_mod"]
                q, kt, v, og, gates = _ml_inproj_call(xin[g], mods[g], gains, w_in, w_g, b_g, conv, tm=512,
                                                      cw=cfg["ml_cw"], seq_len=seq_len, rows_per_mod=rpm)
                if g == "s":
                    init, m0 = (state_mlstm_C[:, j], state_mlstm_n[:, j]), state_mlstm_m[:, j]
                else:
                    init = m0 = None
                cols, rows, m_new = _ml_gate_call(gates, m0, batch=batch, seq_len=seq_len, chunk=ML_SCAN_CHUNK)
                hf, hb, c_new, n_new = _ml_scan_call(q, kt, v, cols, rows, init, batch=batch, seq_len=seq_len,
                                                     chunk=ML_SCAN_CHUNK, cps=2)
                if g == "p":
                    new_c, new_n, new_m = c_new[:, None], n_new[:, None], m_new[:, None, :, :, 0]
                xmid[g] = [hf.reshape(batch * seq_len, ML_W), hb.reshape(batch * seq_len, ML_W), og]
            head_norm = ml_head_norm[j][None, :]
        conv = _pad_rows(jnp.concatenate([ffn_conv_w[l], ffn_conv_b[l][None, :]], axis=0))
        xout = {}
        for g, cfg in groups.items():
            xout[g] = _mix_ffn_call(xin[g], xmid[g], w_out, head_norm, mods[g], gains, ffn_up16, conv, ffn_down16,
                                    l, tm=cfg["ffn_tm"], tf=cfg["ffn_tf"], seq_len=cfg["seq_len"],
                                    rows_per_mod=cfg["rows_per_mod"])
        xp, xs = xout["p"], xout["s"]

    return (xp.reshape(bp, lp, D_MODEL), xs.reshape(bs, ls, D_MODEL), new_k, new_v, new_c, new_n, new_m)
```

```python
import functools
import math

import numpy as np
import jax
import jax.numpy as jnp
from jax import lax
from jax.experimental import pallas as pl
from jax.experimental.pallas import tpu as pltpu

F32 = jnp.float32
BF16 = jnp.bfloat16

D_MODEL = 1024
DEPTH = 2
GRID_W = 64
A_HEADS = 4
A_KV_HEADS = 2
A_HEAD_DIM = 128
A_Q = A_HEADS * A_HEAD_DIM
A_KV = A_KV_HEADS * A_HEAD_DIM
ROPE_THETA = 10000.0
HY_CH = D_MODEL // 2
HY_BANDS = 8
HY_EMB = 1 + 2 * HY_BANDS
HY_W = 64
HY_TARGET = 1e-2
HY_FAST_PCT = 0.3
HY_SLOW_PCT = 1.5
HY_MAX_DECAY = math.log(HY_TARGET) / HY_FAST_PCT
HY_MIN_DECAY = math.log(HY_TARGET) / HY_SLOW_PCT
AH_IN = A_Q + 2 * A_KV + 3 * HY_CH
ML_HEADS = 8
ML_HEAD_DIM = D_MODEL // ML_HEADS
ML_W = ML_HEADS * ML_HEAD_DIM
D_FF = 2816
NORM_EPS = 1e-6
NEG_BIG = -1e30

LANES = 128
VMEM_LIMIT = 56 * 1024 * 1024
COL_TILE = 256
ML_SCAN_CHUNK = 128
INV_SQRT2 = 1.0 / math.sqrt(2.0)


def _cparams(*sem):
    return pltpu.CompilerParams(dimension_semantics=sem, vmem_limit_bytes=VMEM_LIMIT)


def _rms(x, gain):
    return x * lax.rsqrt(jnp.mean(x * x, axis=-1, keepdims=True) + NORM_EPS) * gain


def _modulated(x, gain, shift, scale):
    return _rms(x, gain * (1.0 + scale)) + shift


def _dot(a, b):
    return jnp.dot(a, b, preferred_element_type=F32)


def _dot_nt(a, b):
    return lax.dot_general(a, b, (((1,), (1,)), ((), ())), preferred_element_type=F32)


def _dot_tn(a, b):
    return lax.dot_general(a, b, (((0,), (0,)), ((), ())), preferred_element_type=F32)


def _split3(x):
    hi = x.astype(BF16)
    r1 = x - hi.astype(F32)
    mid = r1.astype(BF16)
    lo = (r1 - mid.astype(F32)).astype(BF16)
    return hi, mid, lo


def _dot_x3(a, b):
    a0 = a.astype(BF16)
    a1 = (a - a0.astype(F32)).astype(BF16)
    b0 = b.astype(BF16)
    b1 = (b - b0.astype(F32)).astype(BF16)
    return _dot(a0, b0) + (_dot(a0, b1) + _dot(a1, b0))


def _dwconv3(y, conv_ref, seq_len, edge=None):
    rows, cols = y.shape
    pos = lax.broadcasted_iota(jnp.int32, (rows, LANES), 0)
    if edge is None:
        pos = pos % seq_len
        first = pos == 0
        last = pos == seq_len - 1
    else:
        first = pos == 0
        last = pos == rows - 1
    outs = []
    for c0 in range(0, cols, LANES):
        yc = y[:, c0:c0 + LANES]
        before = 0.0 if edge is None else edge[0][:, c0:c0 + LANES]
        after = 0.0 if edge is None else edge[1][:, c0:c0 + LANES]
        prev = jnp.where(first, before, pltpu.roll(yc, 1, 0))
        nxt = jnp.where(last, after, pltpu.roll(yc, rows - 1, 0))
        outs.append(prev * conv_ref[0:1, c0:c0 + LANES] + yc * conv_ref[1:2, c0:c0 + LANES]
                    + nxt * conv_ref[2:3, c0:c0 + LANES] + conv_ref[3:4, c0:c0 + LANES])
    return outs[0] if len(outs) == 1 else jnp.concatenate(outs, axis=1)


def _mod_kernel(c_ref, w_ref, b_ref, o_ref):
    a = c_ref[...]
    a = a * jax.nn.sigmoid(a)
    o_ref[0] = _dot(a.astype(BF16), w_ref[0].astype(BF16)) + b_ref[0]


def _mod_call(cc, w_mod, b_mod):
    tn = 1536
    n = 6 * D_MODEL
    return pl.pallas_call(
        _mod_kernel,
        grid=(DEPTH, n // tn),
        in_specs=[pl.BlockSpec((8, D_MODEL), lambda l, j: (0, 0)),
                  pl.BlockSpec((1, D_MODEL, tn), lambda l, j: (l, 0, j)),
                  pl.BlockSpec((1, 1, tn), lambda l, j: (l, 0, j))],
        out_specs=pl.BlockSpec((1, 8, tn), lambda l, j: (l, 0, j)),
        out_shape=jax.ShapeDtypeStruct((DEPTH, 8, n), F32),
        compiler_params=_cparams("arbitrary", "arbitrary"),
        name="mod",
    )(cc, w_mod, b_mod.reshape(DEPTH, 1, n))


def _ah_inproj_kernel(*refs, seq_len, rope, halo, cw):
    it = iter(refs)
    x_ref = next(it)
    xb_ref, xa_ref = (next(it), next(it)) if halo else (None, None)
    mod_ref, gain_ref, w_ref, qkn_ref, conv_ref = (next(it) for _ in range(5))
    cc_ref, ss_ref = (next(it), next(it)) if rope else (None, None)
    q_ref, k_ref, v_ref, u_ref = (next(it) for _ in range(4))
    kf_ref, vf_ref = (None, None) if rope else (next(it), next(it))
    tm = x_ref.shape[0]
    dh = A_HEAD_DIM

    def modulated(x):
        return _modulated(x, gain_ref[0:1, :], mod_ref[0, 0:1, :], mod_ref[0, 1:2, :]).astype(BF16)

    def head(yh, g):
        yh = _rms(yh, g)
        if rope:
            yh = yh * cc_ref[...] + pltpu.roll(yh, dh // 2, 1) * ss_ref[...]
        return yh

    xn = modulated(x_ref[...])
    yq = _dot(xn, w_ref[:, 0:A_Q])
    for h in range(A_HEADS):
        sl = slice(h * dh, (h + 1) * dh)
        q_ref[:, sl] = (head(yq[:, sl], qkn_ref[0:1, :]) * (dh ** -0.5)).astype(BF16)
    ykv = _dot(xn, w_ref[:, A_Q:A_Q + 2 * A_KV])
    for h in range(A_KV_HEADS):
        sl = slice(h * dh, (h + 1) * dh)
        kh = head(ykv[:, sl], qkn_ref[1:2, :])
        k_ref[:, sl] = kh.astype(BF16)
        if kf_ref is not None:
            kf_ref[pl.ds(h, tm, stride=A_KV_HEADS), :] = kh
    yv = ykv[:, A_KV:2 * A_KV]
    v_ref[...] = yv.astype(BF16)
    if vf_ref is not None:
        for h in range(A_KV_HEADS):
            vf_ref[pl.ds(h, tm, stride=A_KV_HEADS), :] = yv[:, h * dh:(h + 1) * dh]

    xe = xn
    if halo:
        xe = jnp.concatenate([xn, modulated(jnp.concatenate([xb_ref[...], xa_ref[...]], axis=0))], axis=0)
    u0 = A_Q + 2 * A_KV
    for t in range(3 * HY_CH // cw):
        cols = slice(u0 + t * cw, u0 + (t + 1) * cw)
        y = _dot(xe, w_ref[:, cols])
        edge = None
        if halo:
            y, edge = _halo_edges(y, tm, seq_len)
        u_ref[:, t * cw:(t + 1) * cw] = _dwconv3(y, conv_ref.at[:, cols], seq_len, edge).astype(BF16)


def _ah_inproj_call(x, mod, gains, w, qkn, conv, rope_tabs, *, tm, cw, seq_len, rows_per_mod):
    rows = x.shape[0]
    rope = rope_tabs is not None
    halo = tm < seq_len
    assert (seq_len % tm == 0) if halo else (tm % seq_len == 0)
    row_blk = lambda cols: pl.BlockSpec((tm, cols), lambda i: (i, 0))
    out_specs = [row_blk(A_Q), row_blk(A_KV), row_blk(A_KV), row_blk(3 * HY_CH)]
    out_shape = [jax.ShapeDtypeStruct((rows, A_Q), BF16), jax.ShapeDtypeStruct((rows, A_KV), BF16),
                 jax.ShapeDtypeStruct((rows, A_KV), BF16), jax.ShapeDtypeStruct((rows, 3 * HY_CH), BF16)]
    if not rope:
        out_specs += [pl.BlockSpec((tm * A_KV_HEADS, A_HEAD_DIM), lambda i: (i, 0))] * 2
        out_shape += [jax.ShapeDtypeStruct((rows * A_KV_HEADS, A_HEAD_DIM), F32)] * 2
    in_specs = [row_blk(D_MODEL)] + (_halo_specs(tm, rows) if halo else []) + [
        pl.BlockSpec((1, 8, D_MODEL), lambda i: (i * tm // rows_per_mod, 0, 0)),
        pl.BlockSpec((8, D_MODEL), lambda i: (0, 0)),
        pl.BlockSpec((D_MODEL, AH_IN), lambda i: (0, 0)),
        pl.BlockSpec((8, A_HEAD_DIM), lambda i: (0, 0)),
        pl.BlockSpec((8, AH_IN), lambda i: (0, 0))]
    args = ([x, x, x] if halo else [x]) + [mod, gains, w, qkn, conv]
    if rope:
        per_seq = max(seq_len // tm, 1)
        in_specs += [pl.BlockSpec((tm, A_HEAD_DIM), lambda i: (i % per_seq, 0))] * 2
        args += list(rope_tabs)
    return pl.pallas_call(
        functools.partial(_ah_inproj_kernel, seq_len=seq_len, rope=rope, halo=halo, cw=cw),
        grid=(rows // tm,),
        in_specs=in_specs,
        out_specs=out_specs,
        out_shape=out_shape,
        compiler_params=_cparams("arbitrary"),
        name="ah_inproj",
    )(*args)


def _attn_kernel(*refs, ctx, bb):
    if ctx:
        q_ref, k_ref, v_ref, kc_ref, vc_ref, o_ref = refs
    else:
        q_ref, k_ref, v_ref, o_ref = refs
    dh = A_HEAD_DIM
    for bi in range(bb):
        k = k_ref[bi]
        if ctx:
            kc = kc_ref[bi]
        else:
            vaug = jnp.concatenate([v_ref[bi], jnp.ones((k.shape[0], dh), BF16)], axis=1)
        for g in range(A_HEADS // A_KV_HEADS):
            sl = slice(g * dh, (g + 1) * dh)
            q = q_ref[bi, :, sl]
            s = _dot_nt(q, k)
            m = jnp.max(s, axis=-1, keepdims=True)
            if ctx:
                sc = _dot_nt(q, kc)
                m = jnp.maximum(m, jnp.max(sc, axis=-1, keepdims=True))
            if ctx:
                p = jnp.exp(s - m)
                pc = jnp.exp(sc - m)
                den = jnp.sum(p, axis=-1, keepdims=True) + jnp.sum(pc, axis=-1, keepdims=True)
                o = _dot(p.astype(BF16), v_ref[bi]) + _dot(pc.astype(BF16), vc_ref[bi])
                o_ref[bi, :, sl] = (o / den).astype(BF16)
            else:
                o = _dot(jnp.exp(s - m).astype(BF16), vaug)
                o_ref[bi, :, sl] = (o[:, :dh] / o[:, dh:]).astype(BF16)


def _attn_call(q, k, v, ctx_k, ctx_v, *, batch, seq_len, tq, bb):
    ctx = ctx_k is not None
    qw = A_Q // A_KV_HEADS
    in_specs = [pl.BlockSpec((bb, tq, qw), lambda b, h, i: (b, i, h)),
                pl.BlockSpec((bb, seq_len, A_HEAD_DIM), lambda b, h, i: (b, 0, h)),
                pl.BlockSpec((bb, seq_len, A_HEAD_DIM), lambda b, h, i: (b, 0, h))]
    args = [q.reshape(batch, seq_len, A_Q), k.reshape(batch, seq_len, A_KV), v.reshape(batch, seq_len, A_KV)]
    if ctx:
        past = ctx_k.shape[1]
        in_specs += [pl.BlockSpec((bb, past, A_HEAD_DIM), lambda b, h, i: (b, 0, h))] * 2
        args += [ctx_k, ctx_v]
    return pl.pallas_call(
        functools.partial(_attn_kernel, ctx=ctx, bb=bb),
        grid=(batch // bb, A_KV_HEADS, seq_len // tq),
        in_specs=in_specs,
        out_specs=pl.BlockSpec((bb, tq, qw), lambda b, h, i: (b, i, h)),
        out_shape=jax.ShapeDtypeStruct((batch, seq_len, A_Q), BF16),
        compiler_params=_cparams("arbitrary", "arbitrary", "arbitrary"),
        name="attn",
    )(*args)


def _hy_filter_kernel(z_ref, w1_ref, b1_ref, w2_ref, b2_ref, w3_ref, b3_ref, sf_ref, dl_ref,
                      cm_ref, sm_ref, gr_ref, gi_ref, hs_ref, hd_ref):
    @pl.when(pl.program_id(0) == 0)
    def _():
        z = z_ref[...]
        h = jnp.sin(sf_ref[0:1, :] * (_dot_x3(z, w1_ref[...]) + b1_ref[...]))
        h = jnp.sin(sf_ref[1:2, :] * (_dot_x3(h, w2_ref[...]) + b2_ref[...]))
        filt = _dot_x3(h, w3_ref[...]) + b3_ref[...]
        win = jnp.exp(-z[:, 0:1] * dl_ref[...])
        hf = filt[:, :HY_CH] * win
        hb = filt[:, HY_CH:] * win
        hs_ref[...] = (hf + hb).astype(BF16)
        hd_ref[...] = (hb - hf).astype(BF16)

    gr_ref[...] = _dot(cm_ref[...], hs_ref[...])
    gi_ref[...] = _dot(sm_ref[...], hd_ref[...])


def _hy_filter_call(z, w1, b1, w2, b2, w3, b3, sf, deltas, cm, sm, *, seq_len, tk):
    full = lambda a: pl.BlockSpec(a.shape, lambda k: (0,) * a.ndim)
    small = [z, w1, b1, w2, b2, w3, b3, sf, deltas]
    return pl.pallas_call(
        _hy_filter_kernel,
        grid=(seq_len // tk,),
        in_specs=[full(a) for a in small] + [pl.BlockSpec((tk, seq_len), lambda k: (k, 0))] * 2,
        out_specs=[pl.BlockSpec((tk, HY_CH), lambda k: (k, 0))] * 2,
        out_shape=[jax.ShapeDtypeStruct((seq_len, HY_CH), F32)] * 2,
        scratch_shapes=[pltpu.VMEM((seq_len, HY_CH), BF16)] * 2,
        compiler_params=_cparams("arbitrary"),
        name="hy_filter",
    )(*small, cm, sm)


def _hy_conv_kernel(x0_ref, x1_ref, v_ref, skip_ref, cm_ref, sm_ref, cmt_ref, smt_ref, gr_ref, gi_ref,
                    o_ref, vv_ref, acc_ref, *, seq_len, bb):
    kb = pl.program_id(2)
    single = seq_len == cm_ref.shape[0]
    gr = gr_ref[...]
    gi = gi_ref[...]
    for bi in range(bb):
        def gated():
            return v_ref[bi].astype(F32) * x1_ref[bi].astype(F32)

        if single:
            vv = gated().astype(BF16)
        else:
            @pl.when(kb == 0)
            def _():
                vv_ref[bi] = gated().astype(BF16)
                acc_ref[bi] = jnp.zeros_like(acc_ref[bi])

            vv = vv_ref[bi]
        vr = _dot(cm_ref[...], vv)
        wi = _dot(sm_ref[...], vv)
        pr = gr * vr + gi * wi
        qi = gr * wi - gi * vr
        y = _dot(cmt_ref[...], pr.astype(BF16)) + _dot(smt_ref[...], qi.astype(BF16))

        def finish(total):
            o_ref[bi] = ((total * (1.0 / seq_len) + skip_ref[...] * gated())
                         * x0_ref[bi].astype(F32)).astype(BF16)

        if single:
            finish(y)
        else:
            acc_ref[bi] += y

            @pl.when(kb == pl.num_programs(2) - 1)
            def _():
                finish(acc_ref[bi])


def _hy_conv_call(u, skip, dft, gr, gi, *, batch, seq_len, tc, tk, bb):
    proj3 = u.reshape(batch, seq_len, 3 * HY_CH)
    cm, sm, cmt, smt = dft
    nch = HY_CH // tc
    return pl.pallas_call(
        functools.partial(_hy_conv_kernel, seq_len=seq_len, bb=bb),
        grid=(batch // bb, nch, seq_len // tk),
        in_specs=[pl.BlockSpec((bb, seq_len, tc), lambda b, c, k: (b, 0, c)),
                  pl.BlockSpec((bb, seq_len, tc), lambda b, c, k: (b, 0, nch + c)),
                  pl.BlockSpec((bb, seq_len, tc), lambda b, c, k: (b, 0, 2 * nch + c)),
                  pl.BlockSpec((1, tc), lambda b, c, k: (0, c)),
                  pl.BlockSpec((tk, seq_len), lambda b, c, k: (k, 0)),
                  pl.BlockSpec((tk, seq_len), lambda b, c, k: (k, 0)),
                  pl.BlockSpec((seq_len, tk), lambda b, c, k: (0, k)),
                  pl.BlockSpec((seq_len, tk), lambda b, c, k: (0, k)),
                  pl.BlockSpec((tk, tc), lambda b, c, k: (k, c)),
                  pl.BlockSpec((tk, tc), lambda b, c, k: (k, c))],
        out_specs=pl.BlockSpec((bb, seq_len, tc), lambda b, c, k: (b, 0, c)),
        out_shape=jax.ShapeDtypeStruct((batch, seq_len, HY_CH), BF16),
        scratch_shapes=[pltpu.VMEM((bb, seq_len, tc), BF16), pltpu.VMEM((bb, seq_len, tc), F32)],
        compiler_params=_cparams("arbitrary", "arbitrary", "arbitrary"),
        name="hy_conv",
    )(proj3, proj3, proj3, skip, cm, sm, cmt, smt, gr, gi)


HALO_ROWS = 16


def _mix_ffn_kernel(*refs, seq_len, halo, tf, mixer):
    it = iter(refs)

    def rows_of():
        parts = [next(it)[...] for _ in range(3 if halo else 1)]
        return parts[0] if len(parts) == 1 else jnp.concatenate(parts, axis=0)

    x = rows_of()
    mix_in = [rows_of() for _ in range(2 if mixer == "ah" else 3)]
    mod_ref, gain_ref, wo_ref = next(it), next(it), next(it)
    hn_ref = next(it) if mixer == "ml" else None
    wu_ref, conv_ref, wd_ref, o_ref = next(it), next(it), next(it), next(it)
    tm = o_ref.shape[0]

    if mixer == "ah":
        mixed = _dot(mix_in[0], wo_ref[0:A_Q, :]) + _dot(mix_in[1], wo_ref[A_Q:, :])
    else:
        h = mix_in[0].astype(F32) + mix_in[1].astype(F32)
        parts = []
        for hd in range(ML_HEADS):
            sl = slice(hd * ML_HEAD_DIM, (hd + 1) * ML_HEAD_DIM)
            parts.append(_rms(h[:, sl], hn_ref[0:1, sl]))
        gated = jnp.concatenate(parts, axis=1) * jax.nn.sigmoid(mix_in[2].astype(F32))
        mixed = _dot(gated.astype(BF16), wo_ref[...])
    x_mid = x + _rms(mixed, gain_ref[1:2, :] * mod_ref[0, 2:3, :])

    xn = _modulated(x_mid, gain_ref[2:3, :], mod_ref[0, 3:4, :], mod_ref[0, 4:5, :]).astype(BF16)
    out = None
    for t in range(D_FF // tf):
        cg = slice(t * tf, (t + 1) * tf)
        cl = slice(D_FF + t * tf, D_FF + (t + 1) * tf)
        yg = _dot(xn, wu_ref[0, :, cg])
        yl = _dot(xn, wu_ref[0, :, cl])
        edge_g = edge_l = None
        if halo:
            yg, edge_g = _halo_edges(yg, tm, seq_len, HALO_ROWS)
            yl, edge_l = _halo_edges(yl, tm, seq_len, HALO_ROWS)
        hg = _dwconv3(yg, conv_ref.at[:, cg], seq_len, edge_g)
        hl = _dwconv3(yl, conv_ref.at[:, cl], seq_len, edge_l)
        act = (0.5 * hg * (1.0 + lax.erf(hg * INV_SQRT2))) * hl
        part = _dot(act.astype(BF16), wd_ref[0, cg, :])
        out = part if out is None else out + part
    o_ref[...] = x_mid[0:tm, :] + _rms(out, gain_ref[3:4, :] * mod_ref[0, 5:6, :])


def _mix_ffn_call(x, mix_in, w_out, head_norm, mod, gains, w_up, conv, w_down, layer, *,
                  tm, tf, seq_len, rows_per_mod):
    rows = x.shape[0]
    mixer = "ah" if head_norm is None else "ml"
    halo = tm < seq_len
    assert (seq_len % tm == 0) if halo else (tm % seq_len == 0)
    row_specs, row_args = [], []
    for a in [x] + list(mix_in):
        cols = a.shape[1]
        row_specs.append(pl.BlockSpec((tm, cols), lambda i: (i, 0)))
        row_args.append(a)
        if halo:
            row_specs += _halo_specs(tm, rows, cols, HALO_ROWS)
            row_args += [a, a]
    whole = lambda a: pl.BlockSpec(a.shape, lambda i: (0,) * a.ndim)
    resident = pl.Buffered(1)
    extra = [] if head_norm is None else [head_norm]
    return pl.pallas_call(
        functools.partial(_mix_ffn_kernel, seq_len=seq_len, halo=halo, tf=tf, mixer=mixer),
        grid=(rows // tm,),
        in_specs=row_specs + [
            pl.BlockSpec((1, 8, D_MODEL), lambda i: (i * tm // rows_per_mod, 0, 0)),
            whole(gains), whole(w_out)] + [whole(a) for a in extra] + [
            pl.BlockSpec((1, D_MODEL, 2 * D_FF), lambda i: (layer, 0, 0), pipeline_mode=resident),
            whole(conv),
            pl.BlockSpec((1, D_FF, D_MODEL), lambda i: (layer, 0, 0), pipeline_mode=resident)],
        out_specs=pl.BlockSpec((tm, D_MODEL), lambda i: (i, 0)),
        out_shape=jax.ShapeDtypeStruct((rows, D_MODEL), F32),
        compiler_params=_cparams("arbitrary"),
        name="mix_ffn",
    )(*row_args, mod, gains, w_out, *extra, w_up, conv, w_down)


def _halo_specs(tm, rows, cols=D_MODEL, sub=8):
    per = tm // sub
    return [pl.BlockSpec((sub, cols), lambda i, *_: (jnp.maximum(i * per - 1, 0), 0)),
            pl.BlockSpec((sub, cols), lambda i, *_: (jnp.minimum((i + 1) * per, rows // sub - 1), 0))]


def _halo_edges(y, tm, seq_len, sub=8):
    i = pl.program_id(0)
    at_start = (i * tm) % seq_len == 0
    at_end = ((i + 1) * tm) % seq_len == 0
    edge = (jnp.where(at_start, 0.0, y[tm + sub - 1:tm + sub, :]),
            jnp.where(at_end, 0.0, y[tm + sub:tm + sub + 1, :]))
    return y[0:tm, :], edge


def _ml_inproj_kernel(*refs, seq_len, cw, halo):
    if halo:
        x_ref, xb_ref, xa_ref, mod_ref, gain_ref, w_ref, wg_ref, bg_ref, conv_ref = refs[:9]
    else:
        x_ref, mod_ref, gain_ref, w_ref, wg_ref, bg_ref, conv_ref = refs[:7]
    q_ref, kt_ref, v_ref, o_ref, g_ref = refs[-5:]
    tm = x_ref.shape[0]

    def modulated(x):
        return _modulated(x, gain_ref[0:1, :], mod_ref[0, 0:1, :], mod_ref[0, 1:2, :]).astype(BF16)

    xn = modulated(x_ref[...])
    g_ref[...] = (_dot(xn, wg_ref[...]) + bg_ref[...]).T
    xe = xn
    if halo:
        xe = jnp.concatenate([xn, modulated(jnp.concatenate([xb_ref[...], xa_ref[...]], axis=0))], axis=0)

    for t in range(2 * ML_W // cw):
        cols = slice(t * cw, (t + 1) * cw)
        y = _dot(xe, w_ref[:, cols])
        edge = None
        if halo:
            y, edge = _halo_edges(y, tm, seq_len)
        z = _dwconv3(y, conv_ref.at[:, cols], seq_len, edge)
        z = z * jax.nn.sigmoid(z)
        if t * cw < ML_W:
            q_ref[:, cols] = z.astype(BF16)
        else:
            kt_ref[t * cw - ML_W:(t + 1) * cw - ML_W, :] = (z * (ML_HEAD_DIM ** -0.5)).T.astype(BF16)
        y = _dot(xn, w_ref[:, 2 * ML_W + t * cw:2 * ML_W + (t + 1) * cw]).astype(BF16)
        if t * cw < ML_W:
            v_ref[:, t * cw:(t + 1) * cw] = y
        else:
            o_ref[:, t * cw - ML_W:(t + 1) * cw - ML_W] = y


def _ml_inproj_call(x, mod, gains, w, wg, bg, conv, *, tm, cw, seq_len, rows_per_mod):
    rows = x.shape[0]
    n = 4 * ML_W
    halo = tm < seq_len
    assert (seq_len % tm == 0) if halo else (tm % seq_len == 0)
    x_specs = [pl.BlockSpec((tm, D_MODEL), lambda i: (i, 0))] + (_halo_specs(tm, rows) if halo else [])
    return pl.pallas_call(
        functools.partial(_ml_inproj_kernel, seq_len=seq_len, cw=cw, halo=halo),
        grid=(rows // tm,),
        in_specs=x_specs + [
            pl.BlockSpec((1, 8, D_MODEL), lambda i: (i * tm // rows_per_mod, 0, 0)),
            pl.BlockSpec((8, D_MODEL), lambda i: (0, 0)),
            pl.BlockSpec((D_MODEL, n), lambda i: (0, 0)),
            pl.BlockSpec((D_MODEL, LANES), lambda i: (0, 0)),
            pl.BlockSpec((1, LANES), lambda i: (0, 0)),
            pl.BlockSpec((8, n), lambda i: (0, 0))],
        out_specs=[pl.BlockSpec((tm, ML_W), lambda i: (i, 0)),
                   pl.BlockSpec((ML_W, tm), lambda i: (0, i)),
                   pl.BlockSpec((tm, ML_W), lambda i: (i, 0)),
                   pl.BlockSpec((tm, ML_W), lambda i: (i, 0)),
                   pl.BlockSpec((LANES, tm), lambda i: (0, i))],
        out_shape=[jax.ShapeDtypeStruct((rows, ML_W), BF16), jax.ShapeDtypeStruct((ML_W, rows), BF16),
                   jax.ShapeDtypeStruct((rows, ML_W), BF16), jax.ShapeDtypeStruct((rows, ML_W), BF16),
                   jax.ShapeDtypeStruct((LANES, rows), F32)],
        compiler_params=_cparams("arbitrary"),
        name="ml_inproj",
    )(*([x, x, x] if halo else [x]), mod, gains, w, wg, bg, conv)


def _log_sigmoid(x):
    return jnp.minimum(x, 0.0) - jnp.log1p(jnp.exp(-jnp.abs(x)))


def _lane_scan(x, op, reverse, fill, seg):
    n = x.shape[1]
    pos = lax.broadcasted_iota(jnp.int32, x.shape, 1) % seg
    sh = 1
    while sh < seg:
        if reverse:
            x = op(x, jnp.where(pos < seg - sh, pltpu.roll(x, n - sh, 1), fill))
        else:
            x = op(x, jnp.where(pos >= sh, pltpu.roll(x, sh, 1), fill))
        sh *= 2
    return x


def _bcast_selectors(chunk):
    sel = np.zeros((ML_HEADS, LANES, chunk + LANES), np.float32)
    for h in range(ML_HEADS):
        for g in range(3):
            sel[h, g * ML_HEADS + h, :chunk] = 1.0
            sel[h, (3 + g) * ML_HEADS + h, chunk:] = 1.0
    return jnp.asarray(sel, BF16)


def _ml_scan_kernel(*refs, chunk, cps, has_init, whole_seq):
    if has_init:
        (qf, ktf, vf, qb, ktb, vb, colf, colb, rowf, rowb, sel_ref, c0_ref, n0_ref,
         hf_ref, hb_ref, c_ref, n_ref, s_sc, p16_sc, e_sc) = refs
    else:
        (qf, ktf, vf, qb, ktb, vb, colf, colb, rowf, rowb, sel_ref,
         hf_ref, hb_ref, c_ref, n_ref, s_sc, p16_sc, e_sc) = refs
    nh, dh = ML_HEADS, ML_HEAD_DIM
    step = pl.program_id(1)
    zero_start = (not has_init) and whole_seq

    if not zero_start:
        @pl.when(step == 0)
        def _():
            if has_init:
                for d in range(2):
                    for h in range(nh):
                        s_sc[d, h, :, 0:dh] = c0_ref[0, d, h].T
                        s_sc[d, h, :, dh:2 * dh] = jnp.broadcast_to(n0_ref[0, d, h:h + 1, :], (dh, dh)).T
            else:
                s_sc[...] = jnp.zeros_like(s_sc)

    row = lax.broadcasted_iota(jnp.int32, (chunk, chunk), 0)
    col = lax.broadcasted_iota(jnp.int32, (chunk, chunk), 1)
    ones_cols = jnp.ones((chunk, dh), BF16)

    dirs = ((qf, ktf, vf, colf, rowf, hf_ref), (qb, ktb, vb, colb, rowb, hb_ref))
    for sub in range(cps):
        for d, (q_ref, kt_ref, _, col_ref, row_ref, _) in enumerate(dirs):
            reverse = d == 1
            lc = cps - 1 - sub if reverse else sub
            tok = slice(lc * chunk, (lc + 1) * chunk)
            cols = col_ref[0, lc, 0]
            r = row_ref[0, lc, 0, 0:nh, :]
            m_prev = row_ref[0, lc, 0, 3 * nh:4 * nh, :]
            mask = (col >= row) if reverse else (col <= row)
            for h in range(nh):
                sl = slice(h * dh, (h + 1) * dh)
                bc = _dot(cols, sel_ref[h])
                p_bc = bc[:, :chunk]
                w = jnp.exp(jnp.where(mask, p_bc + r[h:h + 1, :], NEG_BIG))
                p16_sc[sub, d, h] = (_dot(q_ref[0, tok, sl], kt_ref[sl, tok]) * w).astype(BF16)
                if not (zero_start and sub == 0):
                    e_sc[sub, d, h, :, 0:dh] = jnp.exp(p_bc[:, :dh] + m_prev[h:h + 1, 0:dh])
                e_sc[sub, d, h, :, dh:2 * dh] = jnp.exp(bc[:, chunk:])
    for sub in range(cps):
        for d, (q_ref, kt_ref, v_ref, _, row_ref, h_ref) in enumerate(dirs):
            lc = cps - 1 - sub if d == 1 else sub
            tok = slice(lc * chunk, (lc + 1) * chunk)
            w_tok = row_ref[0, lc, 0, nh:2 * nh, :]
            w_state = row_ref[0, lc, 0, 2 * nh:3 * nh, :]
            for h in range(nh):
                sl = slice(h * dh, (h + 1) * dh)
                q16 = q_ref[0, tok, sl]
                kt16 = kt_ref[sl, tok]
                vaug = jnp.concatenate([v_ref[0, tok, sl], ones_cols], axis=1)
                tot = _dot(p16_sc[sub, d, h], vaug)
                kts = (kt16.astype(F32) * w_tok[h:h + 1, :]).astype(BF16)
                s_new = _dot(kts, vaug)
                if not (zero_start and sub == 0):
                    s_old = s_sc[d, h]
                    w_inter = e_sc[sub, d, h, :, 0:dh]
                    tot = tot + jnp.concatenate([w_inter, w_inter], axis=1) * _dot(q16, s_old.astype(BF16))
                    ws = w_state[h:h + 1, 0:dh]
                    s_new = jnp.concatenate([ws, ws], axis=1) * s_old + s_new
                h_ref[0, tok, sl] = (tot[:, :dh] / jnp.maximum(jnp.abs(tot[:, dh:]),
                                                               e_sc[sub, d, h, :, dh:2 * dh])).astype(BF16)
                s_sc[d, h] = s_new

    @pl.when(step == pl.num_programs(1) - 1)
    def _():
        for d in range(2):
            for h in range(nh):
                s_fin = s_sc[d, h]
                c_ref[0, d, h] = s_fin[:, 0:dh].T
                n_ref[0, d, h:h + 1, :] = s_fin[:, dh:2 * dh].T[0:1, :]


def _ml_gate_kernel(*refs, nc, chunk, bb, has_init):
    if has_init:
        gt_ref, m0_ref, cols_ref, rows_ref, m_ref = refs
    else:
        gt_ref, cols_ref, rows_ref, m_ref = refs
    nh = ML_HEADS
    pad = jnp.zeros((LANES - 6 * nh, chunk), F32)
    for d in range(2):
        reverse = d == 1
        last = 0 if reverse else chunk - 1
        b_all = _lane_scan(_log_sigmoid(gt_ref[nh * (2 + d):nh * (3 + d), :]), jnp.add, reverse, 0.0, chunk)
        r_all = gt_ref[nh * d:nh * (d + 1), :] - b_all
        g_all = _lane_scan(r_all, jnp.maximum, reverse, -jnp.inf, chunk)
        for bi in range(bb):
            m = m0_ref[bi, d] if has_init else jnp.zeros((nh, chunk), F32)
            for c in (range(nc - 1, -1, -1) if reverse else range(nc)):
                c0 = (bi * nc + c) * chunk
                b = b_all[:, c0:c0 + chunk]
                r = r_all[:, c0:c0 + chunk]
                mx = jnp.maximum(m, g_all[:, c0:c0 + chunk])
                mt = b + mx
                m_new = jnp.broadcast_to(mt[:, last:last + 1], (nh, chunk))
                b_last = jnp.broadcast_to(b[:, last:last + 1], (nh, chunk))
                rows_ref[bi, c, d, 0:nh, :] = r
                rows_ref[bi, c, d, nh:2 * nh, :] = jnp.exp(r + b_last - m_new)
                rows_ref[bi, c, d, 2 * nh:3 * nh, :] = jnp.exp(b_last + m - m_new)
                rows_ref[bi, c, d, 3 * nh:4 * nh, :] = m
                stack = []
                for x in (-mx, -mt):
                    stack += [t.astype(F32) for t in _split3(x)]
                stack.append(pad)
                cols_ref[bi, c, d] = jnp.concatenate(stack, axis=0).T.astype(BF16)
                m = m_new
            m_ref[bi, d] = m[:, 0:LANES]


def _ml_gate_call(gates, m0, *, batch, seq_len, chunk):
    nc = seq_len // chunk
    bb = max(1, 16 // nc)
    nh = ML_HEADS
    has_init = m0 is not None
    in_specs = [pl.BlockSpec((LANES, bb * seq_len), lambda i: (0, i))]
    args = [gates]
    if has_init:
        in_specs.append(pl.BlockSpec((bb, 2, nh, chunk), lambda i: (i, 0, 0, 0)))
        args.append(jnp.broadcast_to(m0[..., None], m0.shape + (chunk,)))
    return pl.pallas_call(
        functools.partial(_ml_gate_kernel, nc=nc, chunk=chunk, bb=bb, has_init=has_init),
        grid=(batch // bb,),
        in_specs=in_specs,
        out_specs=[pl.BlockSpec((bb, nc, 2, chunk, LANES), lambda i: (i, 0, 0, 0, 0)),
                   pl.BlockSpec((bb, nc, 2, 4 * nh, chunk), lambda i: (i, 0, 0, 0, 0)),
                   pl.BlockSpec((bb, 2, nh, LANES), lambda i: (i, 0, 0, 0))],
        out_shape=[jax.ShapeDtypeStruct((batch, nc, 2, chunk, LANES), BF16),
                   jax.ShapeDtypeStruct((batch, nc, 2, 4 * nh, chunk), F32),
                   jax.ShapeDtypeStruct((batch, 2, nh, LANES), F32)],
        compiler_params=_cparams("arbitrary"),
        name="ml_gate",
    )(*args)


def _ml_scan_call(q, kt, v, cols, rows, init, *, batch, seq_len, chunk, cps):
    nc = seq_len // (chunk * cps)
    nh = ML_HEADS
    blk = chunk * cps
    q3 = q.reshape(batch, seq_len, ML_W)
    v3 = v.reshape(batch, seq_len, ML_W)
    sel = _bcast_selectors(chunk)
    has_init = init is not None

    fwd3 = lambda b, i: (b, i, 0)
    bwd3 = lambda b, i: (b, nc - 1 - i, 0)
    in_specs = [pl.BlockSpec((1, blk, ML_W), fwd3),
                pl.BlockSpec((ML_W, blk), lambda b, i: (0, b * nc + i)),
                pl.BlockSpec((1, blk, ML_W), fwd3),
                pl.BlockSpec((1, blk, ML_W), bwd3),
                pl.BlockSpec((ML_W, blk), lambda b, i: (0, b * nc + nc - 1 - i)),
                pl.BlockSpec((1, blk, ML_W), bwd3),
                pl.BlockSpec((1, cps, 1, chunk, LANES), lambda b, i: (b, i, 0, 0, 0)),
                pl.BlockSpec((1, cps, 1, chunk, LANES), lambda b, i: (b, nc - 1 - i, 1, 0, 0)),
                pl.BlockSpec((1, cps, 1, 4 * nh, chunk), lambda b, i: (b, i, 0, 0, 0)),
                pl.BlockSpec((1, cps, 1, 4 * nh, chunk), lambda b, i: (b, nc - 1 - i, 1, 0, 0)),
                pl.BlockSpec(sel.shape, lambda b, i: (0, 0, 0))]
    args = [q3, kt, v3, q3, kt, v3, cols, cols, rows, rows, sel]
    c_spec = pl.BlockSpec((1, 2, nh, ML_HEAD_DIM, ML_HEAD_DIM), lambda b, i: (b, 0, 0, 0, 0))
    n_spec = pl.BlockSpec((1, 2, nh, ML_HEAD_DIM), lambda b, i: (b, 0, 0, 0))
    if has_init:
        in_specs += [c_spec, n_spec]
        args += list(init)
    return pl.pallas_call(
        functools.partial(_ml_scan_kernel, chunk=chunk, cps=cps, has_init=has_init, whole_seq=nc == 1),
        grid=(batch, nc),
        in_specs=in_specs,
        out_specs=[pl.BlockSpec((1, blk, ML_W), fwd3),
                   pl.BlockSpec((1, blk, ML_W), bwd3),
                   c_spec, n_spec],
        out_shape=[jax.ShapeDtypeStruct((batch, seq_len, ML_W), BF16),
                   jax.ShapeDtypeStruct((batch, seq_len, ML_W), BF16),
                   jax.ShapeDtypeStruct((batch, 2, nh, ML_HEAD_DIM, ML_HEAD_DIM), F32),
                   jax.ShapeDtypeStruct((batch, 2, nh, ML_HEAD_DIM), F32)],
        scratch_shapes=[pltpu.VMEM((2, nh, ML_HEAD_DIM, 2 * ML_HEAD_DIM), F32),
                        pltpu.VMEM((cps, 2, nh, chunk, chunk), BF16),
                        pltpu.VMEM((cps, 2, nh, chunk, 2 * ML_HEAD_DIM), F32)],
        compiler_params=_cparams("arbitrary", "arbitrary"),
        name="ml_scan",
    )(*args)


@functools.lru_cache(maxsize=None)
def _dft_tables_np(seq_len):
    n4 = 4 * seq_len
    ar = np.arange(seq_len, dtype=np.int64)
    idx = ((2 * ar + 1)[:, None] * ar[None, :]) % n4
    ang = (2.0 * np.pi / n4) * idx.astype(np.float64)
    cm = np.cos(ang).astype(np.float32)
    sm = np.sin(ang).astype(np.float32)
    return tuple(np.ascontiguousarray(a) for a in (cm, sm, cm.T, sm.T))


def _dft_tables(seq_len):
    return tuple(jnp.asarray(a).astype(BF16) for a in _dft_tables_np(seq_len))


def _hy_features(seq_len):
    t = np.linspace(0.0, 1.0, seq_len, dtype=np.float32)
    bands = np.arange(1, HY_BANDS + 1, dtype=np.float32)
    ang = (np.float32(2.0 * np.pi) * t[:, None]) * bands
    z = np.concatenate([t[:, None], np.cos(ang), np.sin(ang)], axis=-1).astype(np.float32)
    z = np.pad(z, ((0, 0), (0, 32 - HY_EMB)))
    deltas = np.abs(np.linspace(HY_MIN_DECAY, HY_MAX_DECAY, HY_CH, dtype=np.float32))[None, :]
    return jnp.asarray(z), jnp.asarray(deltas)


def _rope_tables(seq_len):
    rows = seq_len // GRID_W
    row = np.repeat(np.arange(rows, dtype=np.float32), GRID_W)
    col = np.tile(np.arange(GRID_W, dtype=np.float32), rows)
    n_freq = A_HEAD_DIM // 4
    inv = (np.float32(ROPE_THETA) ** (-np.arange(n_freq, dtype=np.float32) / n_freq)).astype(np.float32)
    ang = np.concatenate([row[:, None] * inv, col[:, None] * inv], axis=-1).astype(np.float32)
    cos, sin = np.cos(ang), np.sin(ang)
    return (jnp.asarray(np.concatenate([cos, cos], axis=-1), F32),
            jnp.asarray(np.concatenate([-sin, sin], axis=-1), F32))


def _pad_rows(a, rows=8):
    return jnp.pad(a, ((0, rows - a.shape[0]), (0, 0)))


def kernel(x_prompt, x_sample, cache_attn_k, cache_attn_v, state_mlstm_C, state_mlstm_n, state_mlstm_m, c, c_ctx, w_mod, b_mod, norm_mix_pre, norm_mix_post, norm_ffn_pre, norm_ffn_post, ffn_w_up, ffn_conv_w, ffn_conv_b, ffn_w_down, ah_w_in, ah_w_out, attn_q_norm, attn_k_norm, hy_conv_w, hy_conv_b, hy_w1, hy_b1, hy_w2, hy_b2, hy_w3, hy_b3, hy_sin_freq, hy_skip, ml_w_in, ml_b_gates, ml_conv_w, ml_conv_b, ml_head_norm, ml_w_out):
    bp, lp, _ = x_prompt.shape
    bs, ls, _ = x_sample.shape
    past = cache_attn_k.shape[2]
    xp = x_prompt.reshape(bp * lp, D_MODEL)
    xs = x_sample.reshape(bs * ls, D_MODEL)
    groups = {
        "p": dict(batch=bp, seq_len=lp, tm=1024, rows_per_mod=bp * lp, ffn_tm=512, ffn_tf=D_FF,
                  seq_bb=4, hy_tc=HY_CH, hy_tk=256, ml_cw=COL_TILE),
        "s": dict(batch=bs, seq_len=ls, tm=ls, rows_per_mod=ls, ffn_tm=512, ffn_tf=D_FF,
                  seq_bb=1, hy_tc=HY_CH, hy_tk=512, ml_cw=512),
    }
    ffn_up16 = ffn_w_up.astype(BF16)
    ffn_down16 = ffn_w_down.astype(BF16)

    cc = jnp.concatenate([c, c_ctx[None, :], jnp.zeros((8 - bs - 1, D_MODEL), F32)], axis=0)
    mod_all = _mod_call(cc, w_mod, b_mod).reshape(DEPTH, 8, 6, D_MODEL)
    mod_all = jnp.pad(mod_all, ((0, 0), (0, 0), (0, 2), (0, 0)))
    rope = _rope_tables(ls)

    new_k = new_v = new_c = new_n = new_m = None
    for l in range(DEPTH):
        j = l // 2
        mods = {"s": mod_all[l, :bs], "p": mod_all[l, bs:bs + 1]}
        gains = _pad_rows(jnp.stack([norm_mix_pre[l], norm_mix_post[l], norm_ffn_pre[l], norm_ffn_post[l]]))
        xin = {"p": xp, "s": xs}
        xmid = {}
        if l % 2 == 0:
            w_in = ah_w_in[j].astype(BF16)
            w_out = ah_w_out[j].astype(BF16)
            qkn = _pad_rows(jnp.stack([attn_q_norm[j], attn_k_norm[j]]))
            conv = jnp.pad(_pad_rows(jnp.concatenate([hy_conv_w[j], hy_conv_b[j][None, :]], axis=0)),
                           ((0, 0), (A_Q + 2 * A_KV, 0)))
            w1 = jnp.pad(hy_w1[j], ((0, 32 - HY_EMB), (0, 0)))
            sf = _pad_rows(hy_sin_freq[j])
            for g, cfg in groups.items():
                batch, seq_len, tm, rpm = cfg["batch"], cfg["seq_len"], cfg["tm"], cfg["rows_per_mod"]
                outs = _ah_inproj_call(xin[g], mods[g], gains, w_in, qkn, conv, rope if g == "s" else None,
                                       tm=512, cw=cfg["ml_cw"], seq_len=seq_len, rows_per_mod=rpm)
                q, k, v, u = outs[:4]
                if g == "s":
                    ctx_k = cache_attn_k[:, j].reshape(bs, past, A_KV).astype(BF16)
                    ctx_v = cache_attn_v[:, j].reshape(bs, past, A_KV).astype(BF16)
                else:
                    ctx_k = ctx_v = None
                    new_k = outs[4].reshape(bp, 1, lp, A_KV_HEADS, A_HEAD_DIM)
                    new_v = outs[5].reshape(bp, 1, lp, A_KV_HEADS, A_HEAD_DIM)
                attn = _attn_call(q, k, v, ctx_k, ctx_v, batch=batch, seq_len=seq_len, tq=256,
                                  bb=cfg["seq_bb"])
                z, deltas = _hy_features(seq_len)
                dft = _dft_tables(seq_len)
                tk = cfg["hy_tk"]
                gr, gi = _hy_filter_call(z, w1, hy_b1[j][None, :], hy_w2[j], hy_b2[j][None, :], hy_w3[j],
                                         hy_b3[j][None, :], sf, deltas, dft[0], dft[1], seq_len=seq_len, tk=tk)
                hyo = _hy_conv_call(u, hy_skip[j][None, :], dft, gr, gi, batch=batch, seq_len=seq_len,
                                    tc=cfg["hy_tc"], tk=tk, bb=cfg["seq_bb"])
                xmid[g] = [attn.reshape(batch * seq_len, A_Q), hyo.reshape(batch * seq_len, HY_CH)]
            head_norm = None
        else:
            w_in = ml_w_in[j].astype(BF16)
            w_g = jnp.pad(ml_w_in[j][:, 4 * ML_W:], ((0, 0), (0, LANES - 4 * ML_HEADS))).astype(BF16)
            b_g = jnp.pad(ml_b_gates[j], (0, LANES - 4 * ML_HEADS))[None, :]
            w_out = ml_w_out[j].astype(BF16)
            conv = jnp.pad(_pad_rows(jnp.concatenate([ml_conv_w[j], ml_conv_b[j][None, :]], axis=0)),
                           ((0, 0), (0, 2 * ML_W)))
            for g, cfg in groups.items():
                batch, seq_len, tm, rpm = cfg["batch"], cfg["seq_len"], cfg["tm"], cfg["rows_per_mod"]
                q, kt, v, og, gates = _ml_inproj_call(xin[g], mods[g], gains, w_in, w_g, b_g, conv, tm=512,
                                                      cw=cfg["ml_cw"], seq_len=seq_len, rows_per_mod=rpm)
                if g == "s":
                    init, m0 = (state_mlstm_C[:, j], state_mlstm_n[:, j]), state_mlstm_m[:, j]
                else:
                    init = m0 = None
                cols, rows, m_new = _ml_gate_call(gates, m0, batch=batch, seq_len=seq_len, chunk=ML_SCAN_CHUNK)
                hf, hb, c_new, n_new = _ml_scan_call(q, kt, v, cols, rows, init, batch=batch, seq_len=seq_len,
                                                     chunk=ML_SCAN_CHUNK, cps=2)
                if g == "p":
                    new_c, new_n, new_m = c_new[:, None], n_new[:, None], m_new[:, None, :, :, 0]
                xmid[g] = [hf.reshape(batch * seq_len, ML_W), hb.reshape(batch * seq_len, ML_W), og]
            head_norm = ml_head_norm[j][None, :]
        conv = _pad_rows(jnp.concatenate([ffn_conv_w[l], ffn_conv_b[l][None, :]], axis=0))
        xout = {}
        for g, cfg in groups.items():
            xout[g] = _mix_ffn_call(xin[g], xmid[g], w_out, head_norm, mods[g], gains, ffn_up16, conv, ffn_down16,
                                    l, tm=cfg["ffn_tm"], tf=cfg["ffn_tf"], seq_len=cfg["seq_len"],
                                    rows_per_mod=cfg["rows_per_mod"])
        xp, xs = xout["p"], xout["s"]

    return (xp.reshape(bp, lp, D_MODEL), xs.reshape(bs, ls, D_MODEL), new_k, new_v, new_c, new_n, new_m)
```

```python
import functools
import math

import numpy as np
import jax
import jax.numpy as jnp
from jax import lax
from jax.experimental import pallas as pl
from jax.experimental.pallas import tpu as pltpu

F32 = jnp.float32
BF16 = jnp.bfloat16

D_MODEL = 1024
DEPTH = 2
GRID_W = 64
A_HEADS = 4
A_KV_HEADS = 2
A_HEAD_DIM = 128
A_Q = A_HEADS * A_HEAD_DIM
A_KV = A_KV_HEADS * A_HEAD_DIM
ROPE_THETA = 10000.0
HY_CH = D_MODEL // 2
HY_BANDS = 8
HY_EMB = 1 + 2 * HY_BANDS
HY_W = 64
HY_TARGET = 1e-2
HY_FAST_PCT = 0.3
HY_SLOW_PCT = 1.5
HY_MAX_DECAY = math.log(HY_TARGET) / HY_FAST_PCT
HY_MIN_DECAY = math.log(HY_TARGET) / HY_SLOW_PCT
AH_IN = A_Q + 2 * A_KV + 3 * HY_CH
ML_HEADS = 8
ML_HEAD_DIM = D_MODEL // ML_HEADS
ML_W = ML_HEADS * ML_HEAD_DIM
D_FF = 2816
NORM_EPS = 1e-6
NEG_BIG = -1e30

LANES = 128
VMEM_LIMIT = 56 * 1024 * 1024
COL_TILE = 256
ML_SCAN_CHUNK = 128
INV_SQRT2 = 1.0 / math.sqrt(2.0)


def _cparams(*sem):
    return pltpu.CompilerParams(dimension_semantics=sem, vmem_limit_bytes=VMEM_LIMIT)


def _rms(x, gain):
    return x * lax.rsqrt(jnp.mean(x * x, axis=-1, keepdims=True) + NORM_EPS) * gain


def _modulated(x, gain, shift, scale):
    return _rms(x, gain * (1.0 + scale)) + shift


def _dot(a, b):
    return jnp.dot(a, b, preferred_element_type=F32)


def _dot_nt(a, b):
    return lax.dot_general(a, b, (((1,), (1,)), ((), ())), preferred_element_type=F32)


def _dot_tn(a, b):
    return lax.dot_general(a, b, (((0,), (0,)), ((), ())), preferred_element_type=F32)


def _split3(x):
    hi = x.astype(BF16)
    r1 = x - hi.astype(F32)
    mid = r1.astype(BF16)
    lo = (r1 - mid.astype(F32)).astype(BF16)
    return hi, mid, lo


def _dot_x3(a, b):
    a0 = a.astype(BF16)
    a1 = (a - a0.astype(F32)).astype(BF16)
    b0 = b.astype(BF16)
    b1 = (b - b0.astype(F32)).astype(BF16)
    return _dot(a0, b0) + (_dot(a0, b1) + _dot(a1, b0))


def _dwconv3(y, conv_ref, seq_len, edge=None):
    rows, cols = y.shape
    pos = lax.broadcasted_iota(jnp.int32, (rows, LANES), 0)
    if edge is None:
        pos = pos % seq_len
        first = pos == 0
        last = pos == seq_len - 1
    else:
        first = pos == 0
        last = pos == rows - 1
    outs = []
    for c0 in range(0, cols, LANES):
        yc = y[:, c0:c0 + LANES]
        before = 0.0 if edge is None else edge[0][:, c0:c0 + LANES]
        after = 0.0 if edge is None else edge[1][:, c0:c0 + LANES]
        prev = jnp.where(first, before, pltpu.roll(yc, 1, 0))
        nxt = jnp.where(last, after, pltpu.roll(yc, rows - 1, 0))
        outs.append(prev * conv_ref[0:1, c0:c0 + LANES] + yc * conv_ref[1:2, c0:c0 + LANES]
                    + nxt * conv_ref[2:3, c0:c0 + LANES] + conv_ref[3:4, c0:c0 + LANES])
    return outs[0] if len(outs) == 1 else jnp.concatenate(outs, axis=1)


def _mod_kernel(c_ref, w_ref, b_ref, o_ref):
    a = c_ref[...]
    a = a * jax.nn.sigmoid(a)
    o_ref[0] = _dot(a.astype(BF16), w_ref[0].astype(BF16)) + b_ref[0]


def _mod_call(cc, w_mod, b_mod):
    tn = 1536
    n = 6 * D_MODEL
    return pl.pallas_call(
        _mod_kernel,
        grid=(DEPTH, n // tn),
        in_specs=[pl.BlockSpec((8, D_MODEL), lambda l, j: (0, 0)),
                  pl.BlockSpec((1, D_MODEL, tn), lambda l, j: (l, 0, j)),
                  pl.BlockSpec((1, 1, tn), lambda l, j: (l, 0, j))],
        out_specs=pl.BlockSpec((1, 8, tn), lambda l, j: (l, 0, j)),
        out_shape=jax.ShapeDtypeStruct((DEPTH, 8, n), F32),
        compiler_params=_cparams("arbitrary", "arbitrary"),
        name="mod",
    )(cc, w_mod, b_mod.reshape(DEPTH, 1, n))


def _ah_inproj_kernel(*refs, seq_len, rope, halo, cw):
    it = iter(refs)
    x_ref = next(it)
    xb_ref, xa_ref = (next(it), next(it)) if halo else (None, None)
    mod_ref, gain_ref, w_ref, qkn_ref, conv_ref = (next(it) for _ in range(5))
    cc_ref, ss_ref = (next(it), next(it)) if rope else (None, None)
    q_ref, k_ref, v_ref, u_ref = (next(it) for _ in range(4))
    kf_ref, vf_ref = (None, None) if rope else (next(it), next(it))
    tm = x_ref.shape[0]
    dh = A_HEAD_DIM

    def modulated(x):
        return _modulated(x, gain_ref[0:1, :], mod_ref[0, 0:1, :], mod_ref[0, 1:2, :]).astype(BF16)

    def head(yh, g):
        yh = _rms(yh, g)
        if rope:
            yh = yh * cc_ref[...] + pltpu.roll(yh, dh // 2, 1) * ss_ref[...]
        return yh

    xn = modulated(x_ref[...])
    yq = _dot(xn, w_ref[:, 0:A_Q].astype(BF16))
    for h in range(A_HEADS):
        sl = slice(h * dh, (h + 1) * dh)
        q_ref[:, sl] = (head(yq[:, sl], qkn_ref[0:1, :]) * (dh ** -0.5)).astype(BF16)
    ykv = _dot(xn, w_ref[:, A_Q:A_Q + 2 * A_KV].astype(BF16))
    for h in range(A_KV_HEADS):
        sl = slice(h * dh, (h + 1) * dh)
        kh = head(ykv[:, sl], qkn_ref[1:2, :])
        k_ref[:, sl] = kh.astype(BF16)
        if kf_ref is not None:
            kf_ref[pl.ds(h, tm, stride=A_KV_HEADS), :] = kh
    yv = ykv[:, A_KV:2 * A_KV]
    v_ref[...] = yv.astype(BF16)
    if vf_ref is not None:
        for h in range(A_KV_HEADS):
            vf_ref[pl.ds(h, tm, stride=A_KV_HEADS), :] = yv[:, h * dh:(h + 1) * dh]

    xe = xn
    if halo:
        xe = jnp.concatenate([xn, modulated(jnp.concatenate([xb_ref[...], xa_ref[...]], axis=0))], axis=0)
    u0 = A_Q + 2 * A_KV
    for t in range(3 * HY_CH // cw):
        cols = slice(u0 + t * cw, u0 + (t + 1) * cw)
        y = _dot(xe, w_ref[:, cols].astype(BF16))
        edge = None
        if halo:
            y, edge = _halo_edges(y, tm, seq_len)
        u_ref[:, t * cw:(t + 1) * cw] = _dwconv3(y, conv_ref.at[:, cols], seq_len, edge).astype(BF16)


def _ah_inproj_call(x, mod, gains, w, qkn, conv, rope_tabs, *, tm, cw, seq_len, rows_per_mod):
    rows = x.shape[0]
    rope = rope_tabs is not None
    halo = tm < seq_len
    assert (seq_len % tm == 0) if halo else (tm % seq_len == 0)
    row_blk = lambda cols: pl.BlockSpec((tm, cols), lambda i: (i, 0))
    out_specs = [row_blk(A_Q), row_blk(A_KV), row_blk(A_KV), row_blk(3 * HY_CH)]
    out_shape = [jax.ShapeDtypeStruct((rows, A_Q), BF16), jax.ShapeDtypeStruct((rows, A_KV), BF16),
                 jax.ShapeDtypeStruct((rows, A_KV), BF16), jax.ShapeDtypeStruct((rows, 3 * HY_CH), BF16)]
    if not rope:
        out_specs += [pl.BlockSpec((tm * A_KV_HEADS, A_HEAD_DIM), lambda i: (i, 0))] * 2
        out_shape += [jax.ShapeDtypeStruct((rows * A_KV_HEADS, A_HEAD_DIM), F32)] * 2
    in_specs = [row_blk(D_MODEL)] + (_halo_specs(tm, rows) if halo else []) + [
        pl.BlockSpec((1, 8, D_MODEL), lambda i: (i * tm // rows_per_mod, 0, 0)),
        pl.BlockSpec((8, D_MODEL), lambda i: (0, 0)),
        pl.BlockSpec((D_MODEL, AH_IN), lambda i: (0, 0), pipeline_mode=pl.Buffered(1)),
        pl.BlockSpec((8, A_HEAD_DIM), lambda i: (0, 0)),
        pl.BlockSpec((8, AH_IN), lambda i: (0, 0))]
    args = ([x, x, x] if halo else [x]) + [mod, gains, w, qkn, conv]
    if rope:
        per_seq = max(seq_len // tm, 1)
        in_specs += [pl.BlockSpec((tm, A_HEAD_DIM), lambda i: (i % per_seq, 0))] * 2
        args += list(rope_tabs)
    return pl.pallas_call(
        functools.partial(_ah_inproj_kernel, seq_len=seq_len, rope=rope, halo=halo, cw=cw),
        grid=(rows // tm,),
        in_specs=in_specs,
        out_specs=out_specs,
        out_shape=out_shape,
        compiler_params=_cparams("arbitrary"),
        name="ah_inproj",
    )(*args)


def _attn_kernel(*refs, ctx, bb):
    if ctx:
        q_ref, k_ref, v_ref, kc_ref, vc_ref, o_ref = refs
    else:
        q_ref, k_ref, v_ref, o_ref = refs
    dh = A_HEAD_DIM
    for bi in range(bb):
        k = k_ref[bi]
        if ctx:
            kc = kc_ref[bi]
        else:
            vaug = jnp.concatenate([v_ref[bi], jnp.ones((k.shape[0], dh), BF16)], axis=1)
        for g in range(A_HEADS // A_KV_HEADS):
            sl = slice(g * dh, (g + 1) * dh)
            q = q_ref[bi, :, sl]
            s = _dot_nt(q, k)
            m = jnp.max(s, axis=-1, keepdims=True)
            if ctx:
                sc = _dot_nt(q, kc)
                m = jnp.maximum(m, jnp.max(sc, axis=-1, keepdims=True))
            if ctx:
                p = jnp.exp(s - m)
                pc = jnp.exp(sc - m)
                den = jnp.sum(p, axis=-1, keepdims=True) + jnp.sum(pc, axis=-1, keepdims=True)
                o = _dot(p.astype(BF16), v_ref[bi]) + _dot(pc.astype(BF16), vc_ref[bi])
                o_ref[bi, :, sl] = (o / den).astype(BF16)
            else:
                o = _dot(jnp.exp(s - m).astype(BF16), vaug)
                o_ref[bi, :, sl] = (o[:, :dh] / o[:, dh:]).astype(BF16)


def _attn_call(q, k, v, ctx_k, ctx_v, *, batch, seq_len, tq, bb):
    ctx = ctx_k is not None
    qw = A_Q // A_KV_HEADS
    in_specs = [pl.BlockSpec((bb, tq, qw), lambda b, h, i: (b, i, h)),
                pl.BlockSpec((bb, seq_len, A_HEAD_DIM), lambda b, h, i: (b, 0, h)),
                pl.BlockSpec((bb, seq_len, A_HEAD_DIM), lambda b, h, i: (b, 0, h))]
    args = [q.reshape(batch, seq_len, A_Q), k.reshape(batch, seq_len, A_KV), v.reshape(batch, seq_len, A_KV)]
    if ctx:
        past = ctx_k.shape[1]
        in_specs += [pl.BlockSpec((bb, past, A_HEAD_DIM), lambda b, h, i: (b, 0, h))] * 2
        args += [ctx_k, ctx_v]
    return pl.pallas_call(
        functools.partial(_attn_kernel, ctx=ctx, bb=bb),
        grid=(batch // bb, A_KV_HEADS, seq_len // tq),
        in_specs=in_specs,
        out_specs=pl.BlockSpec((bb, tq, qw), lambda b, h, i: (b, i, h)),
        out_shape=jax.ShapeDtypeStruct((batch, seq_len, A_Q), BF16),
        compiler_params=_cparams("arbitrary", "arbitrary", "arbitrary"),
        name="attn",
    )(*args)


def _hy_filter_kernel(z_ref, w1_ref, b1_ref, w2_ref, b2_ref, w3_ref, b3_ref, sf_ref, dl_ref,
                      cm_ref, sm_ref, gr_ref, gi_ref, hs_ref, hd_ref):
    @pl.when(pl.program_id(0) == 0)
    def _():
        z = z_ref[...]
        h = jnp.sin(sf_ref[0:1, :] * (_dot_x3(z, w1_ref[...]) + b1_ref[...]))
        h = jnp.sin(sf_ref[1:2, :] * (_dot_x3(h, w2_ref[...]) + b2_ref[...]))
        filt = _dot_x3(h, w3_ref[...]) + b3_ref[...]
        win = jnp.exp(-z[:, 0:1] * dl_ref[...])
        hf = filt[:, :HY_CH] * win
        hb = filt[:, HY_CH:] * win
        hs_ref[...] = (hf + hb).astype(BF16)
        hd_ref[...] = (hb - hf).astype(BF16)

    gr_ref[...] = _dot(cm_ref[...], hs_ref[...])
    gi_ref[...] = _dot(sm_ref[...], hd_ref[...])


def _hy_filter_call(z, w1, b1, w2, b2, w3, b3, sf, deltas, cm, sm, *, seq_len, tk):
    full = lambda a: pl.BlockSpec(a.shape, lambda k: (0,) * a.ndim)
    small = [z, w1, b1, w2, b2, w3, b3, sf, deltas]
    return pl.pallas_call(
        _hy_filter_kernel,
        grid=(seq_len // tk,),
        in_specs=[full(a) for a in small] + [pl.BlockSpec((tk, seq_len), lambda k: (k, 0))] * 2,
        out_specs=[pl.BlockSpec((tk, HY_CH), lambda k: (k, 0))] * 2,
        out_shape=[jax.ShapeDtypeStruct((seq_len, HY_CH), F32)] * 2,
        scratch_shapes=[pltpu.VMEM((seq_len, HY_CH), BF16)] * 2,
        compiler_params=_cparams("arbitrary"),
        name="hy_filter",
    )(*small, cm, sm)


def _hy_conv_kernel(x0_ref, x1_ref, v_ref, skip_ref, cm_ref, sm_ref, cmt_ref, smt_ref, gr_ref, gi_ref,
                    o_ref, vv_ref, acc_ref, *, seq_len, bb):
    kb = pl.program_id(2)
    single = seq_len == cm_ref.shape[0]
    gr = gr_ref[...]
    gi = gi_ref[...]
    for bi in range(bb):
        def gated():
            return v_ref[bi].astype(F32) * x1_ref[bi].astype(F32)

        if single:
            vv = gated().astype(BF16)
        else:
            @pl.when(kb == 0)
            def _():
                vv_ref[bi] = gated().astype(BF16)
                acc_ref[bi] = jnp.zeros_like(acc_ref[bi])

            vv = vv_ref[bi]
        vr = _dot(cm_ref[...], vv)
        wi = _dot(sm_ref[...], vv)
        pr = gr * vr + gi * wi
        qi = gr * wi - gi * vr
        y = _dot(cmt_ref[...], pr.astype(BF16)) + _dot(smt_ref[...], qi.astype(BF16))

        def finish(total):
            o_ref[bi] = ((total * (1.0 / seq_len) + skip_ref[...] * gated())
                         * x0_ref[bi].astype(F32)).astype(BF16)

        if single:
            finish(y)
        else:
            acc_ref[bi] += y

            @pl.when(kb == pl.num_programs(2) - 1)
            def _():
                finish(acc_ref[bi])


def _hy_conv_call(u, skip, dft, gr, gi, *, batch, seq_len, tc, tk, bb):
    proj3 = u.reshape(batch, seq_len, 3 * HY_CH)
    cm, sm, cmt, smt = dft
    nch = HY_CH // tc
    return pl.pallas_call(
        functools.partial(_hy_conv_kernel, seq_len=seq_len, bb=bb),
        grid=(batch // bb, nch, seq_len // tk),
        in_specs=[pl.BlockSpec((bb, seq_len, tc), lambda b, c, k: (b, 0, c)),
                  pl.BlockSpec((bb, seq_len, tc), lambda b, c, k: (b, 0, nch + c)),
                  pl.BlockSpec((bb, seq_len, tc), lambda b, c, k: (b, 0, 2 * nch + c)),
                  pl.BlockSpec((1, tc), lambda b, c, k: (0, c)),
                  pl.BlockSpec((tk, seq_len), lambda b, c, k: (k, 0)),
                  pl.BlockSpec((tk, seq_len), lambda b, c, k: (k, 0)),
                  pl.BlockSpec((seq_len, tk), lambda b, c, k: (0, k)),
                  pl.BlockSpec((seq_len, tk), lambda b, c, k: (0, k)),
                  pl.BlockSpec((tk, tc), lambda b, c, k: (k, c)),
                  pl.BlockSpec((tk, tc), lambda b, c, k: (k, c))],
        out_specs=pl.BlockSpec((bb, seq_len, tc), lambda b, c, k: (b, 0, c)),
        out_shape=jax.ShapeDtypeStruct((batch, seq_len, HY_CH), BF16),
        scratch_shapes=[pltpu.VMEM((bb, seq_len, tc), BF16), pltpu.VMEM((bb, seq_len, tc), F32)],
        compiler_params=_cparams("arbitrary", "arbitrary", "arbitrary"),
        name="hy_conv",
    )(proj3, proj3, proj3, skip, cm, sm, cmt, smt, gr, gi)


HALO_ROWS = 16


def _mix_ffn_kernel(*refs, seq_len, halo, tf, mixer):
    it = iter(refs)

    def rows_of():
        parts = [next(it)[...] for _ in range(3 if halo else 1)]
        return parts[0] if len(parts) == 1 else jnp.concatenate(parts, axis=0)

    x = rows_of()
    mix_in = [rows_of() for _ in range(2 if mixer == "ah" else 3)]
    mod_ref, gain_ref, wo_ref = next(it), next(it), next(it)
    hn_ref = next(it) if mixer == "ml" else None
    wu_ref, conv_ref, wd_ref, o_ref = next(it), next(it), next(it), next(it)
    tm = o_ref.shape[0]

    if mixer == "ah":
        mixed = _dot(mix_in[0], wo_ref[0:A_Q, :]) + _dot(mix_in[1], wo_ref[A_Q:, :])
    else:
        h = mix_in[0].astype(F32) + mix_in[1].astype(F32)
        parts = []
        for hd in range(ML_HEADS):
            sl = slice(hd * ML_HEAD_DIM, (hd + 1) * ML_HEAD_DIM)
            parts.append(_rms(h[:, sl], hn_ref[0:1, sl]))
        gated = jnp.concatenate(parts, axis=1) * jax.nn.sigmoid(mix_in[2].astype(F32))
        mixed = _dot(gated.astype(BF16), wo_ref[...])
    x_mid = x + _rms(mixed, gain_ref[1:2, :] * mod_ref[0, 2:3, :])

    xn = _modulated(x_mid, gain_ref[2:3, :], mod_ref[0, 3:4, :], mod_ref[0, 4:5, :]).astype(BF16)
    out = None
    for t in range(D_FF // tf):
        cg = slice(t * tf, (t + 1) * tf)
        cl = slice(D_FF + t * tf, D_FF + (t + 1) * tf)
        yg = _dot(xn, wu_ref[0, :, cg])
        yl = _dot(xn, wu_ref[0, :, cl])
        edge_g = edge_l = None
        if halo:
            yg, edge_g = _halo_edges(yg, tm, seq_len, HALO_ROWS)
            yl, edge_l = _halo_edges(yl, tm, seq_len, HALO_ROWS)
        hg = _dwconv3(yg, conv_ref.at[:, cg], seq_len, edge_g)
        hl = _dwconv3(yl, conv_ref.at[:, cl], seq_len, edge_l)
        act = (0.5 * hg * (1.0 + lax.erf(hg * INV_SQRT2))) * hl
        part = _dot(act.astype(BF16), wd_ref[0, cg, :])
        out = part if out is None else out + part
    o_ref[...] = x_mid[0:tm, :] + _rms(out, gain_ref[3:4, :] * mod_ref[0, 5:6, :])


def _mix_ffn_call(x, mix_in, w_out, head_norm, mod, gains, w_up, conv, w_down, layer, *,
                  tm, tf, seq_len, rows_per_mod):
    rows = x.shape[0]
    mixer = "ah" if head_norm is None else "ml"
    halo = tm < seq_len
    assert (seq_len % tm == 0) if halo else (tm % seq_len == 0)
    row_specs, row_args = [], []
    for a in [x] + list(mix_in):
        cols = a.shape[1]
        row_specs.append(pl.BlockSpec((tm, cols), lambda i: (i, 0)))
        row_args.append(a)
        if halo:
            row_specs += _halo_specs(tm, rows, cols, HALO_ROWS)
            row_args += [a, a]
    whole = lambda a: pl.BlockSpec(a.shape, lambda i: (0,) * a.ndim)
    resident = pl.Buffered(1)
    extra = [] if head_norm is None else [head_norm]
    return pl.pallas_call(
        functools.partial(_mix_ffn_kernel, seq_len=seq_len, halo=halo, tf=tf, mixer=mixer),
        grid=(rows // tm,),
        in_specs=row_specs + [
            pl.BlockSpec((1, 8, D_MODEL), lambda i: (i * tm // rows_per_mod, 0, 0)),
            whole(gains), whole(w_out)] + [whole(a) for a in extra] + [
            pl.BlockSpec((1, D_MODEL, 2 * D_FF), lambda i: (layer, 0, 0), pipeline_mode=resident),
            whole(conv),
            pl.BlockSpec((1, D_FF, D_MODEL), lambda i: (layer, 0, 0), pipeline_mode=resident)],
        out_specs=pl.BlockSpec((tm, D_MODEL), lambda i: (i, 0)),
        out_shape=jax.ShapeDtypeStruct((rows, D_MODEL), F32),
        compiler_params=_cparams("arbitrary"),
        name="mix_ffn",
    )(*row_args, mod, gains, w_out, *extra, w_up, conv, w_down)


def _halo_specs(tm, rows, cols=D_MODEL, sub=8):
    per = tm // sub
    return [pl.BlockSpec((sub, cols), lambda i, *_: (jnp.maximum(i * per - 1, 0), 0)),
            pl.BlockSpec((sub, cols), lambda i, *_: (jnp.minimum((i + 1) * per, rows // sub - 1), 0))]


def _halo_edges(y, tm, seq_len, sub=8):
    i = pl.program_id(0)
    at_start = (i * tm) % seq_len == 0
    at_end = ((i + 1) * tm) % seq_len == 0
    edge = (jnp.where(at_start, 0.0, y[tm + sub - 1:tm + sub, :]),
            jnp.where(at_end, 0.0, y[tm + sub:tm + sub + 1, :]))
    return y[0:tm, :], edge


def _ml_inproj_kernel(*refs, seq_len, cw, halo):
    if halo:
        x_ref, xb_ref, xa_ref, mod_ref, gain_ref, w_ref, wg_ref, bg_ref, conv_ref = refs[:9]
    else:
        x_ref, mod_ref, gain_ref, w_ref, wg_ref, bg_ref, conv_ref = refs[:7]
    q_ref, kt_ref, v_ref, o_ref, g_ref = refs[-5:]
    tm = x_ref.shape[0]

    def modulated(x):
        return _modulated(x, gain_ref[0:1, :], mod_ref[0, 0:1, :], mod_ref[0, 1:2, :]).astype(BF16)

    xn = modulated(x_ref[...])
    g_ref[...] = (_dot(xn, wg_ref[...]) + bg_ref[...]).T
    xe = xn
    if halo:
        xe = jnp.concatenate([xn, modulated(jnp.concatenate([xb_ref[...], xa_ref[...]], axis=0))], axis=0)

    for t in range(2 * ML_W // cw):
        cols = slice(t * cw, (t + 1) * cw)
        y = _dot(xe, w_ref[:, cols].astype(BF16))
        edge = None
        if halo:
            y, edge = _halo_edges(y, tm, seq_len)
        z = _dwconv3(y, conv_ref.at[:, cols], seq_len, edge)
        z = z * jax.nn.sigmoid(z)
        if t * cw < ML_W:
            q_ref[:, cols] = z.astype(BF16)
        else:
            kt_ref[t * cw - ML_W:(t + 1) * cw - ML_W, :] = (z * (ML_HEAD_DIM ** -0.5)).T.astype(BF16)
        y = _dot(xn, w_ref[:, 2 * ML_W + t * cw:2 * ML_W + (t + 1) * cw].astype(BF16)).astype(BF16)
        if t * cw < ML_W:
            v_ref[:, t * cw:(t + 1) * cw] = y
        else:
            o_ref[:, t * cw - ML_W:(t + 1) * cw - ML_W] = y


def _ml_inproj_call(x, mod, gains, w, wg, bg, conv, *, tm, cw, seq_len, rows_per_mod):
    rows = x.shape[0]
    n = 4 * ML_W
    halo = tm < seq_len
    assert (seq_len % tm == 0) if halo else (tm % seq_len == 0)
    x_specs = [pl.BlockSpec((tm, D_MODEL), lambda i: (i, 0))] + (_halo_specs(tm, rows) if halo else [])
    return pl.pallas_call(
        functools.partial(_ml_inproj_kernel, seq_len=seq_len, cw=cw, halo=halo),
        grid=(rows // tm,),
        in_specs=x_specs + [
            pl.BlockSpec((1, 8, D_MODEL), lambda i: (i * tm // rows_per_mod, 0, 0)),
            pl.BlockSpec((8, D_MODEL), lambda i: (0, 0)),
            pl.BlockSpec((D_MODEL, n), lambda i: (0, 0), pipeline_mode=pl.Buffered(1)),
            pl.BlockSpec((D_MODEL, LANES), lambda i: (0, 0)),
            pl.BlockSpec((1, LANES), lambda i: (0, 0)),
            pl.BlockSpec((8, n), lambda i: (0, 0))],
        out_specs=[pl.BlockSpec((tm, ML_W), lambda i: (i, 0)),
                   pl.BlockSpec((ML_W, tm), lambda i: (0, i)),
                   pl.BlockSpec((tm, ML_W), lambda i: (i, 0)),
                   pl.BlockSpec((tm, ML_W), lambda i: (i, 0)),
                   pl.BlockSpec((LANES, tm), lambda i: (0, i))],
        out_shape=[jax.ShapeDtypeStruct((rows, ML_W), BF16), jax.ShapeDtypeStruct((ML_W, rows), BF16),
                   jax.ShapeDtypeStruct((rows, ML_W), BF16), jax.ShapeDtypeStruct((rows, ML_W), BF16),
                   jax.ShapeDtypeStruct((LANES, rows), F32)],
        compiler_params=_cparams("arbitrary"),
        name="ml_inproj",
    )(*([x, x, x] if halo else [x]), mod, gains, w, wg, bg, conv)


def _log_sigmoid(x):
    return jnp.minimum(x, 0.0) - jnp.log1p(jnp.exp(-jnp.abs(x)))


def _lane_scan(x, op, reverse, fill, seg):
    n = x.shape[1]
    pos = lax.broadcasted_iota(jnp.int32, x.shape, 1) % seg
    sh = 1
    while sh < seg:
        if reverse:
            x = op(x, jnp.where(pos < seg - sh, pltpu.roll(x, n - sh, 1), fill))
        else:
            x = op(x, jnp.where(pos >= sh, pltpu.roll(x, sh, 1), fill))
        sh *= 2
    return x


def _bcast_selectors(chunk):
    sel = np.zeros((ML_HEADS, LANES, chunk + LANES), np.float32)
    for h in range(ML_HEADS):
        for g in range(3):
            sel[h, g * ML_HEADS + h, :chunk] = 1.0
            sel[h, (3 + g) * ML_HEADS + h, chunk:] = 1.0
    return jnp.asarray(sel, BF16)


def _ml_scan_kernel(*refs, chunk, cps, has_init, whole_seq):
    if has_init:
        (qf, ktf, vf, qb, ktb, vb, colf, colb, rowf, rowb, sel_ref, c0_ref, n0_ref,
         hf_ref, hb_ref, c_ref, n_ref, s_sc, p16_sc, e_sc) = refs
    else:
        (qf, ktf, vf, qb, ktb, vb, colf, colb, rowf, rowb, sel_ref,
         hf_ref, hb_ref, c_ref, n_ref, s_sc, p16_sc, e_sc) = refs
    nh, dh = ML_HEADS, ML_HEAD_DIM
    step = pl.program_id(1)
    zero_start = (not has_init) and whole_seq

    if not zero_start:
        @pl.when(step == 0)
        def _():
            if has_init:
                for d in range(2):
                    for h in range(nh):
                        s_sc[d, h, :, 0:dh] = c0_ref[0, d, h].T
                        s_sc[d, h, :, dh:2 * dh] = jnp.broadcast_to(n0_ref[0, d, h:h + 1, :], (dh, dh)).T
            else:
                s_sc[...] = jnp.zeros_like(s_sc)

    row = lax.broadcasted_iota(jnp.int32, (chunk, chunk), 0)
    col = lax.broadcasted_iota(jnp.int32, (chunk, chunk), 1)
    ones_cols = jnp.ones((chunk, dh), BF16)

    dirs = ((qf, ktf, vf, colf, rowf, hf_ref), (qb, ktb, vb, colb, rowb, hb_ref))
    for sub in range(cps):
        for d, (q_ref, kt_ref, _, col_ref, row_ref, _) in enumerate(dirs):
            reverse = d == 1
            lc = cps - 1 - sub if reverse else sub
            tok = slice(lc * chunk, (lc + 1) * chunk)
            cols = col_ref[0, lc, 0]
            r = row_ref[0, lc, 0, 0:nh, :]
            m_prev = row_ref[0, lc, 0, 3 * nh:4 * nh, :]
            mask = (col >= row) if reverse else (col <= row)
            for h in range(nh):
                sl = slice(h * dh, (h + 1) * dh)
                bc = _dot(cols, sel_ref[h])
                p_bc = bc[:, :chunk]
                w = jnp.exp(jnp.where(mask, p_bc + r[h:h + 1, :], NEG_BIG))
                p16_sc[sub, d, h] = (_dot(q_ref[0, tok, sl], kt_ref[sl, tok]) * w).astype(BF16)
                if not (zero_start and sub == 0):
                    e_sc[sub, d, h, :, 0:dh] = jnp.exp(p_bc[:, :dh] + m_prev[h:h + 1, 0:dh])
                e_sc[sub, d, h, :, dh:2 * dh] = jnp.exp(bc[:, chunk:])
    for sub in range(cps):
        for d, (q_ref, kt_ref, v_ref, _, row_ref, h_ref) in enumerate(dirs):
            lc = cps - 1 - sub if d == 1 else sub
            tok = slice(lc * chunk, (lc + 1) * chunk)
            w_tok = row_ref[0, lc, 0, nh:2 * nh, :]
            w_state = row_ref[0, lc, 0, 2 * nh:3 * nh, :]
            for h in range(nh):
                sl = slice(h * dh, (h + 1) * dh)
                q16 = q_ref[0, tok, sl]
                kt16 = kt_ref[sl, tok]
                vaug = jnp.concatenate([v_ref[0, tok, sl], ones_cols], axis=1)
                tot = _dot(p16_sc[sub, d, h], vaug)
                kts = (kt16.astype(F32) * w_tok[h:h + 1, :]).astype(BF16)
                s_new = _dot(kts, vaug)
                if not (zero_start and sub == 0):
                    s_old = s_sc[d, h]
                    w_inter = e_sc[sub, d, h, :, 0:dh]
                    tot = tot + jnp.concatenate([w_inter, w_inter], axis=1) * _dot(q16, s_old.astype(BF16))
                    ws = w_state[h:h + 1, 0:dh]
                    s_new = jnp.concatenate([ws, ws], axis=1) * s_old + s_new
                h_ref[0, tok, sl] = (tot[:, :dh] / jnp.maximum(jnp.abs(tot[:, dh:]),
                                                               e_sc[sub, d, h, :, dh:2 * dh])).astype(BF16)
                s_sc[d, h] = s_new

    @pl.when(step == pl.num_programs(1) - 1)
    def _():
        for d in range(2):
            for h in range(nh):
                s_fin = s_sc[d, h]
                c_ref[0, d, h] = s_fin[:, 0:dh].T
                n_ref[0, d, h:h + 1, :] = s_fin[:, dh:2 * dh].T[0:1, :]


def _ml_gate_kernel(*refs, nc, chunk, bb, has_init):
    if has_init:
        gt_ref, m0_ref, cols_ref, rows_ref, m_ref = refs
    else:
        gt_ref, cols_ref, rows_ref, m_ref = refs
    nh = ML_HEADS
    pad = jnp.zeros((LANES - 6 * nh, chunk), F32)
    for d in range(2):
        reverse = d == 1
        last = 0 if reverse else chunk - 1
        b_all = _lane_scan(_log_sigmoid(gt_ref[nh * (2 + d):nh * (3 + d), :]), jnp.add, reverse, 0.0, chunk)
        r_all = gt_ref[nh * d:nh * (d + 1), :] - b_all
        g_all = _lane_scan(r_all, jnp.maximum, reverse, -jnp.inf, chunk)
        for bi in range(bb):
            m = m0_ref[bi, d] if has_init else jnp.zeros((nh, chunk), F32)
            for c in (range(nc - 1, -1, -1) if reverse else range(nc)):
                c0 = (bi * nc + c) * chunk
                b = b_all[:, c0:c0 + chunk]
                r = r_all[:, c0:c0 + chunk]
                mx = jnp.maximum(m, g_all[:, c0:c0 + chunk])
                mt = b + mx
                m_new = jnp.broadcast_to(mt[:, last:last + 1], (nh, chunk))
                b_last = jnp.broadcast_to(b[:, last:last + 1], (nh, chunk))
                rows_ref[bi, c, d, 0:nh, :] = r
                rows_ref[bi, c, d, nh:2 * nh, :] = jnp.exp(r + b_last - m_new)
                rows_ref[bi, c, d, 2 * nh:3 * nh, :] = jnp.exp(b_last + m - m_new)
                rows_ref[bi, c, d, 3 * nh:4 * nh, :] = m
                stack = []
                for x in (-mx, -mt):
                    stack += [t.astype(F32) for t in _split3(x)]
                stack.append(pad)
                cols_ref[bi, c, d] = jnp.concatenate(stack, axis=0).T.astype(BF16)
                m = m_new
            m_ref[bi, d] = m[:, 0:LANES]


def _ml_gate_call(gates, m0, *, batch, seq_len, chunk):
    nc = seq_len // chunk
    bb = max(1, 16 // nc)
    nh = ML_HEADS
    has_init = m0 is not None
    in_specs = [pl.BlockSpec((LANES, bb * seq_len), lambda i: (0, i))]
    args = [gates]
    if has_init:
        in_specs.append(pl.BlockSpec((bb, 2, nh, chunk), lambda i: (i, 0, 0, 0)))
        args.append(jnp.broadcast_to(m0[..., None], m0.shape + (chunk,)))
    return pl.pallas_call(
        functools.partial(_ml_gate_kernel, nc=nc, chunk=chunk, bb=bb, has_init=has_init),
        grid=(batch // bb,),
        in_specs=in_specs,
        out_specs=[pl.BlockSpec((bb, nc, 2, chunk, LANES), lambda i: (i, 0, 0, 0, 0)),
                   pl.BlockSpec((bb, nc, 2, 4 * nh, chunk), lambda i: (i, 0, 0, 0, 0)),
                   pl.BlockSpec((bb, 2, nh, LANES), lambda i: (i, 0, 0, 0))],
        out_shape=[jax.ShapeDtypeStruct((batch, nc, 2, chunk, LANES), BF16),
                   jax.ShapeDtypeStruct((batch, nc, 2, 4 * nh, chunk), F32),
                   jax.ShapeDtypeStruct((batch, 2, nh, LANES), F32)],
        compiler_params=_cparams("arbitrary"),
        name="ml_gate",
    )(*args)


def _ml_scan_call(q, kt, v, cols, rows, init, *, batch, seq_len, chunk, cps):
    nc = seq_len // (chunk * cps)
    nh = ML_HEADS
    blk = chunk * cps
    q3 = q.reshape(batch, seq_len, ML_W)
    v3 = v.reshape(batch, seq_len, ML_W)
    sel = _bcast_selectors(chunk)
    has_init = init is not None

    fwd3 = lambda b, i: (b, i, 0)
    bwd3 = lambda b, i: (b, nc - 1 - i, 0)
    in_specs = [pl.BlockSpec((1, blk, ML_W), fwd3),
                pl.BlockSpec((ML_W, blk), lambda b, i: (0, b * nc + i)),
                pl.BlockSpec((1, blk, ML_W), fwd3),
                pl.BlockSpec((1, blk, ML_W), bwd3),
                pl.BlockSpec((ML_W, blk), lambda b, i: (0, b * nc + nc - 1 - i)),
                pl.BlockSpec((1, blk, ML_W), bwd3),
                pl.BlockSpec((1, cps, 1, chunk, LANES), lambda b, i: (b, i, 0, 0, 0)),
                pl.BlockSpec((1, cps, 1, chunk, LANES), lambda b, i: (b, nc - 1 - i, 1, 0, 0)),
                pl.BlockSpec((1, cps, 1, 4 * nh, chunk), lambda b, i: (b, i, 0, 0, 0)),
                pl.BlockSpec((1, cps, 1, 4 * nh, chunk), lambda b, i: (b, nc - 1 - i, 1, 0, 0)),
                pl.BlockSpec(sel.shape, lambda b, i: (0, 0, 0))]
    args = [q3, kt, v3, q3, kt, v3, cols, cols, rows, rows, sel]
    c_spec = pl.BlockSpec((1, 2, nh, ML_HEAD_DIM, ML_HEAD_DIM), lambda b, i: (b, 0, 0, 0, 0))
    n_spec = pl.BlockSpec((1, 2, nh, ML_HEAD_DIM), lambda b, i: (b, 0, 0, 0))
    if has_init:
        in_specs += [c_spec, n_spec]
        args += list(init)
    return pl.pallas_call(
        functools.partial(_ml_scan_kernel, chunk=chunk, cps=cps, has_init=has_init, whole_seq=nc == 1),
        grid=(batch, nc),
        in_specs=in_specs,
        out_specs=[pl.BlockSpec((1, blk, ML_W), fwd3),
                   pl.BlockSpec((1, blk, ML_W), bwd3),
                   c_spec, n_spec],
        out_shape=[jax.ShapeDtypeStruct((batch, seq_len, ML_W), BF16),
                   jax.ShapeDtypeStruct((batch, seq_len, ML_W), BF16),
                   jax.ShapeDtypeStruct((batch, 2, nh, ML_HEAD_DIM, ML_HEAD_DIM), F32),
                   jax.ShapeDtypeStruct((batch, 2, nh, ML_HEAD_DIM), F32)],
        scratch_shapes=[pltpu.VMEM((2, nh, ML_HEAD_DIM, 2 * ML_HEAD_DIM), F32),
                        pltpu.VMEM((cps, 2, nh, chunk, chunk), BF16),
                        pltpu.VMEM((cps, 2, nh, chunk, 2 * ML_HEAD_DIM), F32)],
        compiler_params=_cparams("arbitrary", "arbitrary"),
        name="ml_scan",
    )(*args)


@functools.lru_cache(maxsize=None)
def _dft_tables_np(seq_len):
    n4 = 4 * seq_len
    ar = np.arange(seq_len, dtype=np.int64)
    idx = ((2 * ar + 1)[:, None] * ar[None, :]) % n4
    ang = (2.0 * np.pi / n4) * idx.astype(np.float64)
    cm = np.cos(ang).astype(np.float32)
    sm = np.sin(ang).astype(np.float32)
    return tuple(np.ascontiguousarray(a) for a in (cm, sm, cm.T, sm.T))


def _dft_tables(seq_len):
    return tuple(jnp.asarray(a).astype(BF16) for a in _dft_tables_np(seq_len))


def _hy_features(seq_len):
    t = np.linspace(0.0, 1.0, seq_len, dtype=np.float32)
    bands = np.arange(1, HY_BANDS + 1, dtype=np.float32)
    ang = (np.float32(2.0 * np.pi) * t[:, None]) * bands
    z = np.concatenate([t[:, None], np.cos(ang), np.sin(ang)], axis=-1).astype(np.float32)
    z = np.pad(z, ((0, 0), (0, 32 - HY_EMB)))
    deltas = np.abs(np.linspace(HY_MIN_DECAY, HY_MAX_DECAY, HY_CH, dtype=np.float32))[None, :]
    return jnp.asarray(z), jnp.asarray(deltas)


def _rope_tables(seq_len):
    rows = seq_len // GRID_W
    row = np.repeat(np.arange(rows, dtype=np.float32), GRID_W)
    col = np.tile(np.arange(GRID_W, dtype=np.float32), rows)
    n_freq = A_HEAD_DIM // 4
    inv = (np.float32(ROPE_THETA) ** (-np.arange(n_freq, dtype=np.float32) / n_freq)).astype(np.float32)
    ang = np.concatenate([row[:, None] * inv, col[:, None] * inv], axis=-1).astype(np.float32)
    cos, sin = np.cos(ang), np.sin(ang)
    return (jnp.asarray(np.concatenate([cos, cos], axis=-1), F32),
            jnp.asarray(np.concatenate([-sin, sin], axis=-1), F32))


def _pad_rows(a, rows=8):
    return jnp.pad(a, ((0, rows - a.shape[0]), (0, 0)))


def kernel(x_prompt, x_sample, cache_attn_k, cache_attn_v, state_mlstm_C, state_mlstm_n, state_mlstm_m, c, c_ctx, w_mod, b_mod, norm_mix_pre, norm_mix_post, norm_ffn_pre, norm_ffn_post, ffn_w_up, ffn_conv_w, ffn_conv_b, ffn_w_down, ah_w_in, ah_w_out, attn_q_norm, attn_k_norm, hy_conv_w, hy_conv_b, hy_w1, hy_b1, hy_w2, hy_b2, hy_w3, hy_b3, hy_sin_freq, hy_skip, ml_w_in, ml_b_gates, ml_conv_w, ml_conv_b, ml_head_norm, ml_w_out):
    bp, lp, _ = x_prompt.shape
    bs, ls, _ = x_sample.shape
    past = cache_attn_k.shape[2]
    xp = x_prompt.reshape(bp * lp, D_MODEL)
    xs = x_sample.reshape(bs * ls, D_MODEL)
    groups = {
        "p": dict(batch=bp, seq_len=lp, tm=1024, rows_per_mod=bp * lp, ffn_tm=512, ffn_tf=D_FF,
                  seq_bb=4, hy_tc=HY_CH, hy_tk=256, ml_cw=COL_TILE),
        "s": dict(batch=bs, seq_len=ls, tm=ls, rows_per_mod=ls, ffn_tm=512, ffn_tf=D_FF,
                  seq_bb=1, hy_tc=HY_CH, hy_tk=512, ml_cw=512),
    }
    ffn_up16 = ffn_w_up.astype(BF16)
    ffn_down16 = ffn_w_down.astype(BF16)

    cc = jnp.concatenate([c, c_ctx[None, :], jnp.zeros((8 - bs - 1, D_MODEL), F32)], axis=0)
    mod_all = _mod_call(cc, w_mod, b_mod).reshape(DEPTH, 8, 6, D_MODEL)
    mod_all = jnp.pad(mod_all, ((0, 0), (0, 0), (0, 2), (0, 0)))
    rope = _rope_tables(ls)

    new_k = new_v = new_c = new_n = new_m = None
    for l in range(DEPTH):
        j = l // 2
        mods = {"s": mod_all[l, :bs], "p": mod_all[l, bs:bs + 1]}
        gains = _pad_rows(jnp.stack([norm_mix_pre[l], norm_mix_post[l], norm_ffn_pre[l], norm_ffn_post[l]]))
        xin = {"p": xp, "s": xs}
        xmid = {}
        if l % 2 == 0:
            w_in = ah_w_in[j]
            w_out = ah_w_out[j].astype(BF16)
            qkn = _pad_rows(jnp.stack([attn_q_norm[j], attn_k_norm[j]]))
            conv = jnp.pad(_pad_rows(jnp.concatenate([hy_conv_w[j], hy_conv_b[j][None, :]], axis=0)),
                           ((0, 0), (A_Q + 2 * A_KV, 0)))
            w1 = jnp.pad(hy_w1[j], ((0, 32 - HY_EMB), (0, 0)))
            sf = _pad_rows(hy_sin_freq[j])
            for g, cfg in groups.items():
                batch, seq_len, tm, rpm = cfg["batch"], cfg["seq_len"], cfg["tm"], cfg["rows_per_mod"]
                outs = _ah_inproj_call(xin[g], mods[g], gains, w_in, qkn, conv, rope if g == "s" else None,
                                       tm=512, cw=cfg["ml_cw"], seq_len=seq_len, rows_per_mod=rpm)
                q, k, v, u = outs[:4]
                if g == "s":
                    ctx_k = cache_attn_k[:, j].reshape(bs, past, A_KV).astype(BF16)
                    ctx_v = cache_attn_v[:, j].reshape(bs, past, A_KV).astype(BF16)
                else:
                    ctx_k = ctx_v = None
                    new_k = outs[4].reshape(bp, 1, lp, A_KV_HEADS, A_HEAD_DIM)
                    new_v = outs[5].reshape(bp, 1, lp, A_KV_HEADS, A_HEAD_DIM)
                attn = _attn_call(q, k, v, ctx_k, ctx_v, batch=batch, seq_len=seq_len, tq=256,
                                  bb=cfg["seq_bb"])
                z, deltas = _hy_features(seq_len)
                dft = _dft_tables(seq_len)
                tk = cfg["hy_tk"]
                gr, gi = _hy_filter_call(z, w1, hy_b1[j][None, :], hy_w2[j], hy_b2[j][None, :], hy_w3[j],
                                         hy_b3[j][None, :], sf, deltas, dft[0], dft[1], seq_len=seq_len, tk=tk)
                hyo = _hy_conv_call(u, hy_skip[j][None, :], dft, gr, gi, batch=batch, seq_len=seq_len,
                                    tc=cfg["hy_tc"], tk=tk, bb=cfg["seq_bb"])
                xmid[g] = [attn.reshape(batch * seq_len, A_Q), hyo.reshape(batch * seq_len, HY_CH)]
            head_norm = None
        else:
            w_in = ml_w_in[j]
            w_g = jnp.pad(ml_w_in[j][:, 4 * ML_W:], ((0, 0), (0, LANES - 4 * ML_HEADS))).astype(BF16)
            b_g = jnp.pad(ml_b_gates[j], (0, LANES - 4 * ML_HEADS))[None, :]
            w_out = ml_w_out[j].astype(BF16)
            conv = jnp.pad(_pad_rows(jnp.concatenate([ml_conv_w[j], ml_conv_b[j][None, :]], axis=0)),
                           ((0, 0), (0, 2 * ML_W)))
            for g, cfg in groups.items():
                batch, seq_len, tm, rpm = cfg["batch"], cfg["seq_len"], cfg["tm"], cfg["rows_per_mod"]
                q, kt, v, og, gates = _ml_inproj_call(xin[g], mods[g], gains, w_in, w_g, b_g, conv, tm=512,
                                                      cw=cfg["ml_cw"], seq_len=seq_len, rows_per_mod=rpm)
                if g == "s":
                    init, m0 = (state_mlstm_C[:, j], state_mlstm_n[:, j]), state_mlstm_m[:, j]
                else:
                    init = m0 = None
                cols, rows, m_new = _ml_gate_call(gates, m0, batch=batch, seq_len=seq_len, chunk=ML_SCAN_CHUNK)
                hf, hb, c_new, n_new = _ml_scan_call(q, kt, v, cols, rows, init, batch=batch, seq_len=seq_len,
                                                     chunk=ML_SCAN_CHUNK, cps=4 if g == "s" else 2)
                if g == "p":
                    new_c, new_n, new_m = c_new[:, None], n_new[:, None], m_new[:, None, :, :, 0]
                xmid[g] = [hf.reshape(batch * seq_len, ML_W), hb.reshape(batch * seq_len, ML_W), og]
            head_norm = ml_head_norm[j][None, :]
        conv = _pad_rows(jnp.concatenate([ffn_conv_w[l], ffn_conv_b[l][None, :]], axis=0))
        xout = {}
        for g, cfg in groups.items():
            xout[g] = _mix_ffn_call(xin[g], xmid[g], w_out, head_norm, mods[g], gains, ffn_up16, conv, ffn_down16,
                                    l, tm=cfg["ffn_tm"], tf=cfg["ffn_tf"], seq_len=cfg["seq_len"],
                                    rows_per_mod=cfg["rows_per_mod"])
        xp, xs = xout["p"], xout["s"]

    return (xp.reshape(bp, lp, D_MODEL), xs.reshape(bs, ls, D_MODEL), new_k, new_v, new_c, new_n, new_m)
```

```python
import functools
import math

import numpy as np
import jax
import jax.numpy as jnp
from jax import lax
from jax.experimental import pallas as pl
from jax.experimental.pallas import tpu as pltpu

F32 = jnp.float32
BF16 = jnp.bfloat16

D_MODEL = 1024
DEPTH = 2
GRID_W = 64
A_HEADS = 4
A_KV_HEADS = 2
A_HEAD_DIM = 128
A_Q = A_HEADS * A_HEAD_DIM
A_KV = A_KV_HEADS * A_HEAD_DIM
ROPE_THETA = 10000.0
HY_CH = D_MODEL // 2
HY_BANDS = 8
HY_EMB = 1 + 2 * HY_BANDS
HY_W = 64
HY_TARGET = 1e-2
HY_FAST_PCT = 0.3
HY_SLOW_PCT = 1.5
HY_MAX_DECAY = math.log(HY_TARGET) / HY_FAST_PCT
HY_MIN_DECAY = math.log(HY_TARGET) / HY_SLOW_PCT
AH_IN = A_Q + 2 * A_KV + 3 * HY_CH
ML_HEADS = 8
ML_HEAD_DIM = D_MODEL // ML_HEADS
ML_W = ML_HEADS * ML_HEAD_DIM
D_FF = 2816
NORM_EPS = 1e-6
NEG_BIG = -1e30

LANES = 128
VMEM_LIMIT = 56 * 1024 * 1024
COL_TILE = 256
ML_SCAN_CHUNK = 128
INV_SQRT2 = 1.0 / math.sqrt(2.0)


def _cparams(*sem):
    return pltpu.CompilerParams(dimension_semantics=sem, vmem_limit_bytes=VMEM_LIMIT)


def _rms(x, gain):
    return x * lax.rsqrt(jnp.mean(x * x, axis=-1, keepdims=True) + NORM_EPS) * gain


def _modulated(x, gain, shift, scale):
    return _rms(x, gain * (1.0 + scale)) + shift


def _dot(a, b):
    return jnp.dot(a, b, preferred_element_type=F32)


def _dot_nt(a, b):
    return lax.dot_general(a, b, (((1,), (1,)), ((), ())), preferred_element_type=F32)


def _dot_tn(a, b):
    return lax.dot_general(a, b, (((0,), (0,)), ((), ())), preferred_element_type=F32)


def _split3(x):
    hi = x.astype(BF16)
    r1 = x - hi.astype(F32)
    mid = r1.astype(BF16)
    lo = (r1 - mid.astype(F32)).astype(BF16)
    return hi, mid, lo


def _dot_x3(a, b):
    a0 = a.astype(BF16)
    a1 = (a - a0.astype(F32)).astype(BF16)
    b0 = b.astype(BF16)
    b1 = (b - b0.astype(F32)).astype(BF16)
    return _dot(a0, b0) + (_dot(a0, b1) + _dot(a1, b0))


def _dwconv3(y, conv_ref, seq_len, edge=None):
    rows, cols = y.shape
    pos = lax.broadcasted_iota(jnp.int32, (rows, LANES), 0)
    if edge is None:
        pos = pos % seq_len
        first = pos == 0
        last = pos == seq_len - 1
    else:
        first = pos == 0
        last = pos == rows - 1
    outs = []
    for c0 in range(0, cols, LANES):
        yc = y[:, c0:c0 + LANES]
        before = 0.0 if edge is None else edge[0][:, c0:c0 + LANES]
        after = 0.0 if edge is None else edge[1][:, c0:c0 + LANES]
        prev = jnp.where(first, before, pltpu.roll(yc, 1, 0))
        nxt = jnp.where(last, after, pltpu.roll(yc, rows - 1, 0))
        outs.append(prev * conv_ref[0:1, c0:c0 + LANES] + yc * conv_ref[1:2, c0:c0 + LANES]
                    + nxt * conv_ref[2:3, c0:c0 + LANES] + conv_ref[3:4, c0:c0 + LANES])
    return outs[0] if len(outs) == 1 else jnp.concatenate(outs, axis=1)


def _mod_kernel(c_ref, w_ref, b_ref, o_ref):
    a = c_ref[...]
    a = a * jax.nn.sigmoid(a)
    o_ref[0] = _dot(a.astype(BF16), w_ref[0].astype(BF16)) + b_ref[0]


def _mod_call(cc, w_mod, b_mod):
    tn = 3072
    n = 6 * D_MODEL
    return pl.pallas_call(
        _mod_kernel,
        grid=(DEPTH, n // tn),
        in_specs=[pl.BlockSpec((8, D_MODEL), lambda l, j: (0, 0)),
                  pl.BlockSpec((1, D_MODEL, tn), lambda l, j: (l, 0, j)),
                  pl.BlockSpec((1, 1, tn), lambda l, j: (l, 0, j))],
        out_specs=pl.BlockSpec((1, 8, tn), lambda l, j: (l, 0, j)),
        out_shape=jax.ShapeDtypeStruct((DEPTH, 8, n), F32),
        compiler_params=_cparams("arbitrary", "arbitrary"),
        name="mod",
    )(cc, w_mod, b_mod.reshape(DEPTH, 1, n))


def _ah_inproj_kernel(*refs, seq_len, rope, halo, cw):
    it = iter(refs)
    x_ref = next(it)
    xb_ref, xa_ref = (next(it), next(it)) if halo else (None, None)
    mod_ref, gain_ref, w_ref, qkn_ref, conv_ref = (next(it) for _ in range(5))
    cc_ref, ss_ref = (next(it), next(it)) if rope else (None, None)
    q_ref, k_ref, v_ref, u_ref = (next(it) for _ in range(4))
    kf_ref, vf_ref = (None, None) if rope else (next(it), next(it))
    tm = x_ref.shape[0]
    dh = A_HEAD_DIM

    def modulated(x):
        return _modulated(x, gain_ref[0:1, :], mod_ref[0, 0:1, :], mod_ref[0, 1:2, :]).astype(BF16)

    def head(yh, g):
        yh = _rms(yh, g)
        if rope:
            yh = yh * cc_ref[...] + pltpu.roll(yh, dh // 2, 1) * ss_ref[...]
        return yh

    xn = modulated(x_ref[...])
    yq = _dot(xn, w_ref[:, 0:A_Q])
    for h in range(A_HEADS):
        sl = slice(h * dh, (h + 1) * dh)
        q_ref[:, sl] = (head(yq[:, sl], qkn_ref[0:1, :]) * (dh ** -0.5)).astype(BF16)
    ykv = _dot(xn, w_ref[:, A_Q:A_Q + 2 * A_KV])
    for h in range(A_KV_HEADS):
        sl = slice(h * dh, (h + 1) * dh)
        kh = head(ykv[:, sl], qkn_ref[1:2, :])
        k_ref[:, sl] = kh.astype(BF16)
        if kf_ref is not None:
            kf_ref[pl.ds(h, tm, stride=A_KV_HEADS), :] = kh
    yv = ykv[:, A_KV:2 * A_KV]
    v_ref[...] = yv.astype(BF16)
    if vf_ref is not None:
        for h in range(A_KV_HEADS):
            vf_ref[pl.ds(h, tm, stride=A_KV_HEADS), :] = yv[:, h * dh:(h + 1) * dh]

    xe = xn
    if halo:
        xe = jnp.concatenate([xn, modulated(jnp.concatenate([xb_ref[...], xa_ref[...]], axis=0))], axis=0)
    u0 = A_Q + 2 * A_KV
    for t in range(3 * HY_CH // cw):
        cols = slice(u0 + t * cw, u0 + (t + 1) * cw)
        y = _dot(xe, w_ref[:, cols])
        edge = None
        if halo:
            y, edge = _halo_edges(y, tm, seq_len)
        u_ref[:, t * cw:(t + 1) * cw] = _dwconv3(y, conv_ref.at[:, cols], seq_len, edge).astype(BF16)


def _ah_inproj_call(x, mod, gains, w, qkn, conv, rope_tabs, *, tm, cw, seq_len, rows_per_mod):
    rows = x.shape[0]
    rope = rope_tabs is not None
    halo = tm < seq_len
    assert (seq_len % tm == 0) if halo else (tm % seq_len == 0)
    row_blk = lambda cols: pl.BlockSpec((tm, cols), lambda i: (i, 0))
    out_specs = [row_blk(A_Q), row_blk(A_KV), row_blk(A_KV), row_blk(3 * HY_CH)]
    out_shape = [jax.ShapeDtypeStruct((rows, A_Q), BF16), jax.ShapeDtypeStruct((rows, A_KV), BF16),
                 jax.ShapeDtypeStruct((rows, A_KV), BF16), jax.ShapeDtypeStruct((rows, 3 * HY_CH), BF16)]
    if not rope:
        out_specs += [pl.BlockSpec((tm * A_KV_HEADS, A_HEAD_DIM), lambda i: (i, 0))] * 2
        out_shape += [jax.ShapeDtypeStruct((rows * A_KV_HEADS, A_HEAD_DIM), F32)] * 2
    in_specs = [row_blk(D_MODEL)] + (_halo_specs(tm, rows) if halo else []) + [
        pl.BlockSpec((1, 8, D_MODEL), lambda i: (i * tm // rows_per_mod, 0, 0)),
        pl.BlockSpec((8, D_MODEL), lambda i: (0, 0)),
        pl.BlockSpec((D_MODEL, AH_IN), lambda i: (0, 0)),
        pl.BlockSpec((8, A_HEAD_DIM), lambda i: (0, 0)),
        pl.BlockSpec((8, AH_IN), lambda i: (0, 0))]
    args = ([x, x, x] if halo else [x]) + [mod, gains, w, qkn, conv]
    if rope:
        per_seq = max(seq_len // tm, 1)
        in_specs += [pl.BlockSpec((tm, A_HEAD_DIM), lambda i: (i % per_seq, 0))] * 2
        args += list(rope_tabs)
    return pl.pallas_call(
        functools.partial(_ah_inproj_kernel, seq_len=seq_len, rope=rope, halo=halo, cw=cw),
        grid=(rows // tm,),
        in_specs=in_specs,
        out_specs=out_specs,
        out_shape=out_shape,
        compiler_params=_cparams("arbitrary"),
        name="ah_inproj",
    )(*args)


def _attn_kernel(*refs, ctx, bb):
    if ctx:
        q_ref, k_ref, v_ref, kc_ref, vc_ref, o_ref = refs
    else:
        q_ref, k_ref, v_ref, o_ref = refs
    dh = A_HEAD_DIM
    for bi in range(bb):
        k = k_ref[bi]
        if ctx:
            kc = kc_ref[bi]
        else:
            vaug = jnp.concatenate([v_ref[bi], jnp.ones((k.shape[0], dh), BF16)], axis=1)
        for g in range(A_HEADS // A_KV_HEADS):
            sl = slice(g * dh, (g + 1) * dh)
            q = q_ref[bi, :, sl]
            s = _dot_nt(q, k)
            m = jnp.max(s, axis=-1, keepdims=True)
            if ctx:
                sc = _dot_nt(q, kc)
                m = jnp.maximum(m, jnp.max(sc, axis=-1, keepdims=True))
            if ctx:
                p = jnp.exp(s - m)
                pc = jnp.exp(sc - m)
                den = jnp.sum(p, axis=-1, keepdims=True) + jnp.sum(pc, axis=-1, keepdims=True)
                o = _dot(p.astype(BF16), v_ref[bi]) + _dot(pc.astype(BF16), vc_ref[bi])
                o_ref[bi, :, sl] = (o / den).astype(BF16)
            else:
                o = _dot(jnp.exp(s - m).astype(BF16), vaug)
                o_ref[bi, :, sl] = (o[:, :dh] / o[:, dh:]).astype(BF16)


def _attn_call(q, k, v, ctx_k, ctx_v, *, batch, seq_len, tq, bb):
    ctx = ctx_k is not None
    qw = A_Q // A_KV_HEADS
    in_specs = [pl.BlockSpec((bb, tq, qw), lambda b, h, i: (b, i, h)),
                pl.BlockSpec((bb, seq_len, A_HEAD_DIM), lambda b, h, i: (b, 0, h)),
                pl.BlockSpec((bb, seq_len, A_HEAD_DIM), lambda b, h, i: (b, 0, h))]
    args = [q.reshape(batch, seq_len, A_Q), k.reshape(batch, seq_len, A_KV), v.reshape(batch, seq_len, A_KV)]
    if ctx:
        past = ctx_k.shape[1]
        in_specs += [pl.BlockSpec((bb, past, A_HEAD_DIM), lambda b, h, i: (b, 0, h))] * 2
        args += [ctx_k, ctx_v]
    return pl.pallas_call(
        functools.partial(_attn_kernel, ctx=ctx, bb=bb),
        grid=(batch // bb, A_KV_HEADS, seq_len // tq),
        in_specs=in_specs,
        out_specs=pl.BlockSpec((bb, tq, qw), lambda b, h, i: (b, i, h)),
        out_shape=jax.ShapeDtypeStruct((batch, seq_len, A_Q), BF16),
        compiler_params=_cparams("arbitrary", "arbitrary", "arbitrary"),
        name="attn",
    )(*args)


def _hy_filter_kernel(z_ref, w1_ref, b1_ref, w2_ref, b2_ref, w3_ref, b3_ref, sf_ref, dl_ref,
                      cm_ref, sm_ref, gr_ref, gi_ref, hs_ref, hd_ref):
    @pl.when(pl.program_id(0) == 0)
    def _():
        z = z_ref[...]
        h = jnp.sin(sf_ref[0:1, :] * (_dot_x3(z, w1_ref[...]) + b1_ref[...]))
        h = jnp.sin(sf_ref[1:2, :] * (_dot_x3(h, w2_ref[...]) + b2_ref[...]))
        filt = _dot_x3(h, w3_ref[...]) + b3_ref[...]
        win = jnp.exp(-z[:, 0:1] * dl_ref[...])
        hf = filt[:, :HY_CH] * win
        hb = filt[:, HY_CH:] * win
        hs_ref[...] = (hf + hb).astype(BF16)
        hd_ref[...] = (hb - hf).astype(BF16)

    gr_ref[...] = _dot(cm_ref[...], hs_ref[...])
    gi_ref[...] = _dot(sm_ref[...], hd_ref[...])


def _hy_filter_call(z, w1, b1, w2, b2, w3, b3, sf, deltas, cm, sm, *, seq_len, tk):
    full = lambda a: pl.BlockSpec(a.shape, lambda k: (0,) * a.ndim)
    small = [z, w1, b1, w2, b2, w3, b3, sf, deltas]
    return pl.pallas_call(
        _hy_filter_kernel,
        grid=(seq_len // tk,),
        in_specs=[full(a) for a in small] + [pl.BlockSpec((tk, seq_len), lambda k: (k, 0))] * 2,
        out_specs=[pl.BlockSpec((tk, HY_CH), lambda k: (k, 0))] * 2,
        out_shape=[jax.ShapeDtypeStruct((seq_len, HY_CH), F32)] * 2,
        scratch_shapes=[pltpu.VMEM((seq_len, HY_CH), BF16)] * 2,
        compiler_params=_cparams("arbitrary"),
        name="hy_filter",
    )(*small, cm, sm)


def _hy_conv_kernel(x0_ref, x1_ref, v_ref, skip_ref, cm_ref, sm_ref, cmt_ref, smt_ref, gr_ref, gi_ref,
                    o_ref, vv_ref, acc_ref, *, seq_len, bb):
    kb = pl.program_id(2)
    single = seq_len == cm_ref.shape[0]
    gr = gr_ref[...]
    gi = gi_ref[...]
    for bi in range(bb):
        def gated():
            return v_ref[bi].astype(F32) * x1_ref[bi].astype(F32)

        if single:
            vv = gated().astype(BF16)
        else:
            @pl.when(kb == 0)
            def _():
                vv_ref[bi] = gated().astype(BF16)
                acc_ref[bi] = jnp.zeros_like(acc_ref[bi])

            vv = vv_ref[bi]
        vr = _dot(cm_ref[...], vv)
        wi = _dot(sm_ref[...], vv)
        pr = gr * vr + gi * wi
        qi = gr * wi - gi * vr
        y = _dot(cmt_ref[...], pr.astype(BF16)) + _dot(smt_ref[...], qi.astype(BF16))

        def finish(total):
            o_ref[bi] = ((total * (1.0 / seq_len) + skip_ref[...] * gated())
                         * x0_ref[bi].astype(F32)).astype(BF16)

        if single:
            finish(y)
        else:
            acc_ref[bi] += y

            @pl.when(kb == pl.num_programs(2) - 1)
            def _():
                finish(acc_ref[bi])


def _hy_conv_call(u, skip, dft, gr, gi, *, batch, seq_len, tc, tk, bb):
    proj3 = u.reshape(batch, seq_len, 3 * HY_CH)
    cm, sm, cmt, smt = dft
    nch = HY_CH // tc
    return pl.pallas_call(
        functools.partial(_hy_conv_kernel, seq_len=seq_len, bb=bb),
        grid=(batch // bb, nch, seq_len // tk),
        in_specs=[pl.BlockSpec((bb, seq_len, tc), lambda b, c, k: (b, 0, c)),
                  pl.BlockSpec((bb, seq_len, tc), lambda b, c, k: (b, 0, nch + c)),
                  pl.BlockSpec((bb, seq_len, tc), lambda b, c, k: (b, 0, 2 * nch + c)),
                  pl.BlockSpec((1, tc), lambda b, c, k: (0, c)),
                  pl.BlockSpec((tk, seq_len), lambda b, c, k: (k, 0)),
                  pl.BlockSpec((tk, seq_len), lambda b, c, k: (k, 0)),
                  pl.BlockSpec((seq_len, tk), lambda b, c, k: (0, k)),
                  pl.BlockSpec((seq_len, tk), lambda b, c, k: (0, k)),
                  pl.BlockSpec((tk, tc), lambda b, c, k: (k, c)),
                  pl.BlockSpec((tk, tc), lambda b, c, k: (k, c))],
        out_specs=pl.BlockSpec((bb, seq_len, tc), lambda b, c, k: (b, 0, c)),
        out_shape=jax.ShapeDtypeStruct((batch, seq_len, HY_CH), BF16),
        scratch_shapes=[pltpu.VMEM((bb, seq_len, tc), BF16), pltpu.VMEM((bb, seq_len, tc), F32)],
        compiler_params=_cparams("arbitrary", "arbitrary", "arbitrary"),
        name="hy_conv",
    )(proj3, proj3, proj3, skip, cm, sm, cmt, smt, gr, gi)


HALO_ROWS = 16


def _mix_ffn_kernel(*refs, seq_len, halo, tf, mixer):
    it = iter(refs)

    def rows_of():
        parts = [next(it)[...] for _ in range(3 if halo else 1)]
        return parts[0] if len(parts) == 1 else jnp.concatenate(parts, axis=0)

    x = rows_of()
    mix_in = [rows_of() for _ in range(2 if mixer == "ah" else 3)]
    mod_ref, gain_ref, wo_ref = next(it), next(it), next(it)
    hn_ref = next(it) if mixer == "ml" else None
    wu_ref, conv_ref, wd_ref, o_ref = next(it), next(it), next(it), next(it)
    tm = o_ref.shape[0]

    if mixer == "ah":
        mixed = _dot(mix_in[0], wo_ref[0:A_Q, :]) + _dot(mix_in[1], wo_ref[A_Q:, :])
    else:
        h = mix_in[0].astype(F32) + mix_in[1].astype(F32)
        parts = []
        for hd in range(ML_HEADS):
            sl = slice(hd * ML_HEAD_DIM, (hd + 1) * ML_HEAD_DIM)
            parts.append(_rms(h[:, sl], hn_ref[0:1, sl]))
        gated = jnp.concatenate(parts, axis=1) * jax.nn.sigmoid(mix_in[2].astype(F32))
        mixed = _dot(gated.astype(BF16), wo_ref[...])
    x_mid = x + _rms(mixed, gain_ref[1:2, :] * mod_ref[0, 2:3, :])

    xn = _modulated(x_mid, gain_ref[2:3, :], mod_ref[0, 3:4, :], mod_ref[0, 4:5, :]).astype(BF16)
    out = None
    for t in range(D_FF // tf):
        cg = slice(t * tf, (t + 1) * tf)
        cl = slice(D_FF + t * tf, D_FF + (t + 1) * tf)
        yg = _dot(xn, wu_ref[0, :, cg])
        yl = _dot(xn, wu_ref[0, :, cl])
        edge_g = edge_l = None
        if halo:
            yg, edge_g = _halo_edges(yg, tm, seq_len, HALO_ROWS)
            yl, edge_l = _halo_edges(yl, tm, seq_len, HALO_ROWS)
        hg = _dwconv3(yg, conv_ref.at[:, cg], seq_len, edge_g)
        hl = _dwconv3(yl, conv_ref.at[:, cl], seq_len, edge_l)
        act = (0.5 * hg * (1.0 + lax.erf(hg * INV_SQRT2))) * hl
        part = _dot(act.astype(BF16), wd_ref[0, cg, :])
        out = part if out is None else out + part
    o_ref[...] = x_mid[0:tm, :] + _rms(out, gain_ref[3:4, :] * mod_ref[0, 5:6, :])


def _mix_ffn_call(x, mix_in, w_out, head_norm, mod, gains, w_up, conv, w_down, layer, *,
                  tm, tf, seq_len, rows_per_mod):
    rows = x.shape[0]
    mixer = "ah" if head_norm is None else "ml"
    halo = tm < seq_len
    assert (seq_len % tm == 0) if halo else (tm % seq_len == 0)
    row_specs, row_args = [], []
    for a in [x] + list(mix_in):
        cols = a.shape[1]
        row_specs.append(pl.BlockSpec((tm, cols), lambda i: (i, 0)))
        row_args.append(a)
        if halo:
            row_specs += _halo_specs(tm, rows, cols, HALO_ROWS)
            row_args += [a, a]
    whole = lambda a: pl.BlockSpec(a.shape, lambda i: (0,) * a.ndim)
    resident = pl.Buffered(1)
    extra = [] if head_norm is None else [head_norm]
    return pl.pallas_call(
        functools.partial(_mix_ffn_kernel, seq_len=seq_len, halo=halo, tf=tf, mixer=mixer),
        grid=(rows // tm,),
        in_specs=row_specs + [
            pl.BlockSpec((1, 8, D_MODEL), lambda i: (i * tm // rows_per_mod, 0, 0)),
            whole(gains), whole(w_out)] + [whole(a) for a in extra] + [
            pl.BlockSpec((1, D_MODEL, 2 * D_FF), lambda i: (layer, 0, 0), pipeline_mode=resident),
            whole(conv),
            pl.BlockSpec((1, D_FF, D_MODEL), lambda i: (layer, 0, 0), pipeline_mode=resident)],
        out_specs=pl.BlockSpec((tm, D_MODEL), lambda i: (i, 0)),
        out_shape=jax.ShapeDtypeStruct((rows, D_MODEL), F32),
        compiler_params=_cparams("arbitrary"),
        name="mix_ffn",
    )(*row_args, mod, gains, w_out, *extra, w_up, conv, w_down)


def _halo_specs(tm, rows, cols=D_MODEL, sub=8):
    per = tm // sub
    return [pl.BlockSpec((sub, cols), lambda i, *_: (jnp.maximum(i * per - 1, 0), 0)),
            pl.BlockSpec((sub, cols), lambda i, *_: (jnp.minimum((i + 1) * per, rows // sub - 1), 0))]


def _halo_edges(y, tm, seq_len, sub=8):
    i = pl.program_id(0)
    at_start = (i * tm) % seq_len == 0
    at_end = ((i + 1) * tm) % seq_len == 0
    edge = (jnp.where(at_start, 0.0, y[tm + sub - 1:tm + sub, :]),
            jnp.where(at_end, 0.0, y[tm + sub:tm + sub + 1, :]))
    return y[0:tm, :], edge


def _ml_inproj_kernel(*refs, seq_len, cw, halo):
    if halo:
        x_ref, xb_ref, xa_ref, mod_ref, gain_ref, w_ref, wg_ref, bg_ref, conv_ref = refs[:9]
    else:
        x_ref, mod_ref, gain_ref, w_ref, wg_ref, bg_ref, conv_ref = refs[:7]
    q_ref, kt_ref, v_ref, o_ref, g_ref = refs[-5:]
    tm = x_ref.shape[0]

    def modulated(x):
        return _modulated(x, gain_ref[0:1, :], mod_ref[0, 0:1, :], mod_ref[0, 1:2, :]).astype(BF16)

    xn = modulated(x_ref[...])
    g_ref[...] = (_dot(xn, wg_ref[...]) + bg_ref[...]).T
    xe = xn
    if halo:
        xe = jnp.concatenate([xn, modulated(jnp.concatenate([xb_ref[...], xa_ref[...]], axis=0))], axis=0)

    for t in range(2 * ML_W // cw):
        cols = slice(t * cw, (t + 1) * cw)
        y = _dot(xe, w_ref[:, cols])
        edge = None
        if halo:
            y, edge = _halo_edges(y, tm, seq_len)
        z = _dwconv3(y, conv_ref.at[:, cols], seq_len, edge)
        z = z * jax.nn.sigmoid(z)
        if t * cw < ML_W:
            q_ref[:, cols] = z.astype(BF16)
        else:
            kt_ref[t * cw - ML_W:(t + 1) * cw - ML_W, :] = (z * (ML_HEAD_DIM ** -0.5)).T.astype(BF16)
        y = _dot(xn, w_ref[:, 2 * ML_W + t * cw:2 * ML_W + (t + 1) * cw]).astype(BF16)
        if t * cw < ML_W:
            v_ref[:, t * cw:(t + 1) * cw] = y
        else:
            o_ref[:, t * cw - ML_W:(t + 1) * cw - ML_W] = y


def _ml_inproj_call(x, mod, gains, w, wg, bg, conv, *, tm, cw, seq_len, rows_per_mod):
    rows = x.shape[0]
    n = 4 * ML_W
    halo = tm < seq_len
    assert (seq_len % tm == 0) if halo else (tm % seq_len == 0)
    x_specs = [pl.BlockSpec((tm, D_MODEL), lambda i: (i, 0))] + (_halo_specs(tm, rows) if halo else [])
    return pl.pallas_call(
        functools.partial(_ml_inproj_kernel, seq_len=seq_len, cw=cw, halo=halo),
        grid=(rows // tm,),
        in_specs=x_specs + [
            pl.BlockSpec((1, 8, D_MODEL), lambda i: (i * tm // rows_per_mod, 0, 0)),
            pl.BlockSpec((8, D_MODEL), lambda i: (0, 0)),
            pl.BlockSpec((D_MODEL, n), lambda i: (0, 0)),
            pl.BlockSpec((D_MODEL, LANES), lambda i: (0, 0)),
            pl.BlockSpec((1, LANES), lambda i: (0, 0)),
            pl.BlockSpec((8, n), lambda i: (0, 0))],
        out_specs=[pl.BlockSpec((tm, ML_W), lambda i: (i, 0)),
                   pl.BlockSpec((ML_W, tm), lambda i: (0, i)),
                   pl.BlockSpec((tm, ML_W), lambda i: (i, 0)),
                   pl.BlockSpec((tm, ML_W), lambda i: (i, 0)),
                   pl.BlockSpec((LANES, tm), lambda i: (0, i))],
        out_shape=[jax.ShapeDtypeStruct((rows, ML_W), BF16), jax.ShapeDtypeStruct((ML_W, rows), BF16),
                   jax.ShapeDtypeStruct((rows, ML_W), BF16), jax.ShapeDtypeStruct((rows, ML_W), BF16),
                   jax.ShapeDtypeStruct((LANES, rows), F32)],
        compiler_params=_cparams("arbitrary"),
        name="ml_inproj",
    )(*([x, x, x] if halo else [x]), mod, gains, w, wg, bg, conv)


def _log_sigmoid(x):
    return jnp.minimum(x, 0.0) - jnp.log1p(jnp.exp(-jnp.abs(x)))


def _lane_scan(x, op, reverse, fill, seg):
    n = x.shape[1]
    pos = lax.broadcasted_iota(jnp.int32, x.shape, 1) % seg
    sh = 1
    while sh < seg:
        if reverse:
            x = op(x, jnp.where(pos < seg - sh, pltpu.roll(x, n - sh, 1), fill))
        else:
            x = op(x, jnp.where(pos >= sh, pltpu.roll(x, sh, 1), fill))
        sh *= 2
    return x


def _bcast_selectors(chunk):
    sel = np.zeros((ML_HEADS, LANES, chunk + LANES), np.float32)
    for h in range(ML_HEADS):
        for g in range(3):
            sel[h, g * ML_HEADS + h, :chunk] = 1.0
            sel[h, (3 + g) * ML_HEADS + h, chunk:] = 1.0
    return jnp.asarray(sel, BF16)


def _ml_scan_kernel(*refs, chunk, cps, has_init, whole_seq):
    if has_init:
        (qf, ktf, vf, qb, ktb, vb, colf, colb, rowf, rowb, sel_ref, c0_ref, n0_ref,
         hf_ref, hb_ref, c_ref, n_ref, s_sc, p16_sc, e_sc) = refs
    else:
        (qf, ktf, vf, qb, ktb, vb, colf, colb, rowf, rowb, sel_ref,
         hf_ref, hb_ref, c_ref, n_ref, s_sc, p16_sc, e_sc) = refs
    nh, dh = ML_HEADS, ML_HEAD_DIM
    step = pl.program_id(1)
    zero_start = (not has_init) and whole_seq

    if not zero_start:
        @pl.when(step == 0)
        def _():
            if has_init:
                for d in range(2):
                    for h in range(nh):
                        s_sc[d, h, :, 0:dh] = c0_ref[0, d, h].T
                        s_sc[d, h, :, dh:2 * dh] = jnp.broadcast_to(n0_ref[0, d, h:h + 1, :], (dh, dh)).T
            else:
                s_sc[...] = jnp.zeros_like(s_sc)

    row = lax.broadcasted_iota(jnp.int32, (chunk, chunk), 0)
    col = lax.broadcasted_iota(jnp.int32, (chunk, chunk), 1)
    ones_cols = jnp.ones((chunk, dh), BF16)

    dirs = ((qf, ktf, vf, colf, rowf, hf_ref), (qb, ktb, vb, colb, rowb, hb_ref))
    for sub in range(cps):
        for d, (q_ref, kt_ref, _, col_ref, row_ref, _) in enumerate(dirs):
            reverse = d == 1
            lc = cps - 1 - sub if reverse else sub
            tok = slice(lc * chunk, (lc + 1) * chunk)
            cols = col_ref[0, lc, 0]
            r = row_ref[0, lc, 0, 0:nh, :]
            m_prev = row_ref[0, lc, 0, 3 * nh:4 * nh, :]
            mask = (col >= row) if reverse else (col <= row)
            for h in range(nh):
                sl = slice(h * dh, (h + 1) * dh)
                bc = _dot(cols, sel_ref[h])
                p_bc = bc[:, :chunk]
                w = jnp.exp(jnp.where(mask, p_bc + r[h:h + 1, :], NEG_BIG))
                p16_sc[sub, d, h] = (_dot(q_ref[0, tok, sl], kt_ref[sl, tok]) * w).astype(BF16)
                if not (zero_start and sub == 0):
                    e_sc[sub, d, h, :, 0:dh] = jnp.exp(p_bc[:, :dh] + m_prev[h:h + 1, 0:dh])
                e_sc[sub, d, h, :, dh:2 * dh] = jnp.exp(bc[:, chunk:])
    for sub in range(cps):
        for d, (q_ref, kt_ref, v_ref, _, row_ref, h_ref) in enumerate(dirs):
            lc = cps - 1 - sub if d == 1 else sub
            tok = slice(lc * chunk, (lc + 1) * chunk)
            w_tok = row_ref[0, lc, 0, nh:2 * nh, :]
            w_state = row_ref[0, lc, 0, 2 * nh:3 * nh, :]
            for h in range(nh):
                sl = slice(h * dh, (h + 1) * dh)
                q16 = q_ref[0, tok, sl]
                kt16 = kt_ref[sl, tok]
                vaug = jnp.concatenate([v_ref[0, tok, sl], ones_cols], axis=1)
                tot = _dot(p16_sc[sub, d, h], vaug)
                kts = (kt16.astype(F32) * w_tok[h:h + 1, :]).astype(BF16)
                s_new = _dot(kts, vaug)
                if not (zero_start and sub == 0):
                    s_old = s_sc[d, h]
                    w_inter = e_sc[sub, d, h, :, 0:dh]
                    tot = tot + jnp.concatenate([w_inter, w_inter], axis=1) * _dot(q16, s_old.astype(BF16))
                    ws = w_state[h:h + 1, 0:dh]
                    s_new = jnp.concatenate([ws, ws], axis=1) * s_old + s_new
                h_ref[0, tok, sl] = (tot[:, :dh] / jnp.maximum(jnp.abs(tot[:, dh:]),
                                                               e_sc[sub, d, h, :, dh:2 * dh])).astype(BF16)
                s_sc[d, h] = s_new

    @pl.when(step == pl.num_programs(1) - 1)
    def _():
        for d in range(2):
            for h in range(nh):
                s_fin = s_sc[d, h]
                c_ref[0, d, h] = s_fin[:, 0:dh].T
                n_ref[0, d, h:h + 1, :] = s_fin[:, dh:2 * dh].T[0:1, :]


def _ml_gate_kernel(*refs, nc, chunk, bb, has_init):
    if has_init:
        gt_ref, m0_ref, cols_ref, rows_ref, m_ref = refs
    else:
        gt_ref, cols_ref, rows_ref, m_ref = refs
    nh = ML_HEADS
    pad = jnp.zeros((LANES - 6 * nh, chunk), F32)
    for d in range(2):
        reverse = d == 1
        last = 0 if reverse else chunk - 1
        b_all = _lane_scan(_log_sigmoid(gt_ref[nh * (2 + d):nh * (3 + d), :]), jnp.add, reverse, 0.0, chunk)
        r_all = gt_ref[nh * d:nh * (d + 1), :] - b_all
        g_all = _lane_scan(r_all, jnp.maximum, reverse, -jnp.inf, chunk)
        for bi in range(bb):
            m = m0_ref[bi, d] if has_init else jnp.zeros((nh, chunk), F32)
            for c in (range(nc - 1, -1, -1) if reverse else range(nc)):
                c0 = (bi * nc + c) * chunk
                b = b_all[:, c0:c0 + chunk]
                r = r_all[:, c0:c0 + chunk]
                mx = jnp.maximum(m, g_all[:, c0:c0 + chunk])
                mt = b + mx
                m_new = jnp.broadcast_to(mt[:, last:last + 1], (nh, chunk))
                b_last = jnp.broadcast_to(b[:, last:last + 1], (nh, chunk))
                rows_ref[bi, c, d, 0:nh, :] = r
                rows_ref[bi, c, d, nh:2 * nh, :] = jnp.exp(r + b_last - m_new)
                rows_ref[bi, c, d, 2 * nh:3 * nh, :] = jnp.exp(b_last + m - m_new)
                rows_ref[bi, c, d, 3 * nh:4 * nh, :] = m
                stack = []
                for x in (-mx, -mt):
                    stack += [t.astype(F32) for t in _split3(x)]
                stack.append(pad)
                cols_ref[bi, c, d] = jnp.concatenate(stack, axis=0).T.astype(BF16)
                m = m_new
            m_ref[bi, d] = m[:, 0:LANES]


def _ml_gate_call(gates, m0, *, batch, seq_len, chunk):
    nc = seq_len // chunk
    bb = max(1, 16 // nc)
    nh = ML_HEADS
    has_init = m0 is not None
    in_specs = [pl.BlockSpec((LANES, bb * seq_len), lambda i: (0, i))]
    args = [gates]
    if has_init:
        in_specs.append(pl.BlockSpec((bb, 2, nh, chunk), lambda i: (i, 0, 0, 0)))
        args.append(jnp.broadcast_to(m0[..., None], m0.shape + (chunk,)))
    return pl.pallas_call(
        functools.partial(_ml_gate_kernel, nc=nc, chunk=chunk, bb=bb, has_init=has_init),
        grid=(batch // bb,),
        in_specs=in_specs,
        out_specs=[pl.BlockSpec((bb, nc, 2, chunk, LANES), lambda i: (i, 0, 0, 0, 0)),
                   pl.BlockSpec((bb, nc, 2, 4 * nh, chunk), lambda i: (i, 0, 0, 0, 0)),
                   pl.BlockSpec((bb, 2, nh, LANES), lambda i: (i, 0, 0, 0))],
        out_shape=[jax.ShapeDtypeStruct((batch, nc, 2, chunk, LANES), BF16),
                   jax.ShapeDtypeStruct((batch, nc, 2, 4 * nh, chunk), F32),
                   jax.ShapeDtypeStruct((batch, 2, nh, LANES), F32)],
        compiler_params=_cparams("arbitrary"),
        name="ml_gate",
    )(*args)


def _ml_scan_call(q, kt, v, cols, rows, init, *, batch, seq_len, chunk, cps):
    nc = seq_len // (chunk * cps)
    nh = ML_HEADS
    blk = chunk * cps
    q3 = q.reshape(batch, seq_len, ML_W)
    v3 = v.reshape(batch, seq_len, ML_W)
    sel = _bcast_selectors(chunk)
    has_init = init is not None

    fwd3 = lambda b, i: (b, i, 0)
    bwd3 = lambda b, i: (b, nc - 1 - i, 0)
    in_specs = [pl.BlockSpec((1, blk, ML_W), fwd3),
                pl.BlockSpec((ML_W, blk), lambda b, i: (0, b * nc + i)),
                pl.BlockSpec((1, blk, ML_W), fwd3),
                pl.BlockSpec((1, blk, ML_W), bwd3),
                pl.BlockSpec((ML_W, blk), lambda b, i: (0, b * nc + nc - 1 - i)),
                pl.BlockSpec((1, blk, ML_W), bwd3),
                pl.BlockSpec((1, cps, 1, chunk, LANES), lambda b, i: (b, i, 0, 0, 0)),
                pl.BlockSpec((1, cps, 1, chunk, LANES), lambda b, i: (b, nc - 1 - i, 1, 0, 0)),
                pl.BlockSpec((1, cps, 1, 4 * nh, chunk), lambda b, i: (b, i, 0, 0, 0)),
                pl.BlockSpec((1, cps, 1, 4 * nh, chunk), lambda b, i: (b, nc - 1 - i, 1, 0, 0)),
                pl.BlockSpec(sel.shape, lambda b, i: (0, 0, 0))]
    args = [q3, kt, v3, q3, kt, v3, cols, cols, rows, rows, sel]
    c_spec = pl.BlockSpec((1, 2, nh, ML_HEAD_DIM, ML_HEAD_DIM), lambda b, i: (b, 0, 0, 0, 0))
    n_spec = pl.BlockSpec((1, 2, nh, ML_HEAD_DIM), lambda b, i: (b, 0, 0, 0))
    if has_init:
        in_specs += [c_spec, n_spec]
        args += list(init)
    return pl.pallas_call(
        functools.partial(_ml_scan_kernel, chunk=chunk, cps=cps, has_init=has_init, whole_seq=nc == 1),
        grid=(batch, nc),
        in_specs=in_specs,
        out_specs=[pl.BlockSpec((1, blk, ML_W), fwd3),
                   pl.BlockSpec((1, blk, ML_W), bwd3),
                   c_spec, n_spec],
        out_shape=[jax.ShapeDtypeStruct((batch, seq_len, ML_W), BF16),
                   jax.ShapeDtypeStruct((batch, seq_len, ML_W), BF16),
                   jax.ShapeDtypeStruct((batch, 2, nh, ML_HEAD_DIM, ML_HEAD_DIM), F32),
                   jax.ShapeDtypeStruct((batch, 2, nh, ML_HEAD_DIM), F32)],
        scratch_shapes=[pltpu.VMEM((2, nh, ML_HEAD_DIM, 2 * ML_HEAD_DIM), F32),
                        pltpu.VMEM((cps, 2, nh, chunk, chunk), BF16),
                        pltpu.VMEM((cps, 2, nh, chunk, 2 * ML_HEAD_DIM), F32)],
        compiler_params=_cparams("arbitrary", "arbitrary"),
        name="ml_scan",
    )(*args)


@functools.lru_cache(maxsize=None)
def _dft_tables_np(seq_len):
    n4 = 4 * seq_len
    ar = np.arange(seq_len, dtype=np.int64)
    idx = ((2 * ar + 1)[:, None] * ar[None, :]) % n4
    ang = (2.0 * np.pi / n4) * idx.astype(np.float64)
    cm = np.cos(ang).astype(np.float32)
    sm = np.sin(ang).astype(np.float32)
    return tuple(np.ascontiguousarray(a) for a in (cm, sm, cm.T, sm.T))


def _dft_tables(seq_len):
    return tuple(jnp.asarray(a).astype(BF16) for a in _dft_tables_np(seq_len))


def _hy_features(seq_len):
    t = np.linspace(0.0, 1.0, seq_len, dtype=np.float32)
    bands = np.arange(1, HY_BANDS + 1, dtype=np.float32)
    ang = (np.float32(2.0 * np.pi) * t[:, None]) * bands
    z = np.concatenate([t[:, None], np.cos(ang), np.sin(ang)], axis=-1).astype(np.float32)
    z = np.pad(z, ((0, 0), (0, 32 - HY_EMB)))
    deltas = np.abs(np.linspace(HY_MIN_DECAY, HY_MAX_DECAY, HY_CH, dtype=np.float32))[None, :]
    return jnp.asarray(z), jnp.asarray(deltas)


def _rope_tables(seq_len):
    rows = seq_len // GRID_W
    row = np.repeat(np.arange(rows, dtype=np.float32), GRID_W)
    col = np.tile(np.arange(GRID_W, dtype=np.float32), rows)
    n_freq = A_HEAD_DIM // 4
    inv = (np.float32(ROPE_THETA) ** (-np.arange(n_freq, dtype=np.float32) / n_freq)).astype(np.float32)
    ang = np.concatenate([row[:, None] * inv, col[:, None] * inv], axis=-1).astype(np.float32)
    cos, sin = np.cos(ang), np.sin(ang)
    return (jnp.asarray(np.concatenate([cos, cos], axis=-1), F32),
            jnp.asarray(np.concatenate([-sin, sin], axis=-1), F32))


def _pad_rows(a, rows=8):
    return jnp.pad(a, ((0, rows - a.shape[0]), (0, 0)))


def kernel(x_prompt, x_sample, cache_attn_k, cache_attn_v, state_mlstm_C, state_mlstm_n, state_mlstm_m, c, c_ctx, w_mod, b_mod, norm_mix_pre, norm_mix_post, norm_ffn_pre, norm_ffn_post, ffn_w_up, ffn_conv_w, ffn_conv_b, ffn_w_down, ah_w_in, ah_w_out, attn_q_norm, attn_k_norm, hy_conv_w, hy_conv_b, hy_w1, hy_b1, hy_w2, hy_b2, hy_w3, hy_b3, hy_sin_freq, hy_skip, ml_w_in, ml_b_gates, ml_conv_w, ml_conv_b, ml_head_norm, ml_w_out):
    bp, lp, _ = x_prompt.shape
    bs, ls, _ = x_sample.shape
    past = cache_attn_k.shape[2]
    xp = x_prompt.reshape(bp * lp, D_MODEL)
    xs = x_sample.reshape(bs * ls, D_MODEL)
    groups = {
        "p": dict(batch=bp, seq_len=lp, tm=1024, rows_per_mod=bp * lp, ffn_tm=512, ffn_tf=D_FF,
                  seq_bb=8, hy_tc=HY_CH, hy_tk=256, ml_cw=COL_TILE),
        "s": dict(batch=bs, seq_len=ls, tm=ls, rows_per_mod=ls, ffn_tm=512, ffn_tf=D_FF,
                  seq_bb=1, hy_tc=HY_CH, hy_tk=512, ml_cw=512),
    }
    ffn_up16 = ffn_w_up.astype(BF16)
    ffn_down16 = ffn_w_down.astype(BF16)

    cc = jnp.concatenate([c, c_ctx[None, :], jnp.zeros((8 - bs - 1, D_MODEL), F32)], axis=0)
    mod_all = _mod_call(cc, w_mod, b_mod).reshape(DEPTH, 8, 6, D_MODEL)
    mod_all = jnp.pad(mod_all, ((0, 0), (0, 0), (0, 2), (0, 0)))
    rope = _rope_tables(ls)

    new_k = new_v = new_c = new_n = new_m = None
    for l in range(DEPTH):
        j = l // 2
        mods = {"s": mod_all[l, :bs], "p": mod_all[l, bs:bs + 1]}
        gains = _pad_rows(jnp.stack([norm_mix_pre[l], norm_mix_post[l], norm_ffn_pre[l], norm_ffn_post[l]]))
        xin = {"p": xp, "s": xs}
        xmid = {}
        if l % 2 == 0:
            w_in = ah_w_in[j].astype(BF16)
            w_out = ah_w_out[j].astype(BF16)
            qkn = _pad_rows(jnp.stack([attn_q_norm[j], attn_k_norm[j]]))
            conv = jnp.pad(_pad_rows(jnp.concatenate([hy_conv_w[j], hy_conv_b[j][None, :]], axis=0)),
                           ((0, 0), (A_Q + 2 * A_KV, 0)))
            w1 = jnp.pad(hy_w1[j], ((0, 32 - HY_EMB), (0, 0)))
            sf = _pad_rows(hy_sin_freq[j])
            for g, cfg in groups.items():
                batch, seq_len, tm, rpm = cfg["batch"], cfg["seq_len"], cfg["tm"], cfg["rows_per_mod"]
                outs = _ah_inproj_call(xin[g], mods[g], gains, w_in, qkn, conv, rope if g == "s" else None,
                                       tm=512, cw=cfg["ml_cw"], seq_len=seq_len, rows_per_mod=rpm)
                q, k, v, u = outs[:4]
                if g == "s":
                    ctx_k = cache_attn_k[:, j].reshape(bs, past, A_KV).astype(BF16)
                    ctx_v = cache_attn_v[:, j].reshape(bs, past, A_KV).astype(BF16)
                else:
                    ctx_k = ctx_v = None
                    new_k = outs[4].reshape(bp, 1, lp, A_KV_HEADS, A_HEAD_DIM)
                    new_v = outs[5].reshape(bp, 1, lp, A_KV_HEADS, A_HEAD_DIM)
                attn = _attn_call(q, k, v, ctx_k, ctx_v, batch=batch, seq_len=seq_len, tq=256,
                                  bb=cfg["seq_bb"])
                z, deltas = _hy_features(seq_len)
                dft = _dft_tables(seq_len)
                tk = cfg["hy_tk"]
                gr, gi = _hy_filter_call(z, w1, hy_b1[j][None, :], hy_w2[j], hy_b2[j][None, :], hy_w3[j],
                                         hy_b3[j][None, :], sf, deltas, dft[0], dft[1], seq_len=seq_len, tk=tk)
                hyo = _hy_conv_call(u, hy_skip[j][None, :], dft, gr, gi, batch=batch, seq_len=seq_len,
                                    tc=cfg["hy_tc"], tk=tk, bb=cfg["seq_bb"])
                xmid[g] = [attn.reshape(batch * seq_len, A_Q), hyo.reshape(batch * seq_len, HY_CH)]
            head_norm = None
        else:
            w_in = ml_w_in[j].astype(BF16)
            w_g = jnp.pad(ml_w_in[j][:, 4 * ML_W:], ((0, 0), (0, LANES - 4 * ML_HEADS))).astype(BF16)
            b_g = jnp.pad(ml_b_gates[j], (0, LANES - 4 * ML_HEADS))[None, :]
            w_out = ml_w_out[j].astype(BF16)
            conv = jnp.pad(_pad_rows(jnp.concatenate([ml_conv_w[j], ml_conv_b[j][None, :]], axis=0)),
                           ((0, 0), (0, 2 * ML_W)))
            for g, cfg in groups.items():
                batch, seq_len, tm, rpm = cfg["batch"], cfg["seq_len"], cfg["tm"], cfg["rows_per_mod"]
                q, kt, v, og, gates = _ml_inproj_call(xin[g], mods[g], gains, w_in, w_g, b_g, conv, tm=512,
                                                      cw=cfg["ml_cw"], seq_len=seq_len, rows_per_mod=rpm)
                if g == "s":
                    init, m0 = (state_mlstm_C[:, j], state_mlstm_n[:, j]), state_mlstm_m[:, j]
                else:
                    init = m0 = None
                cols, rows, m_new = _ml_gate_call(gates, m0, batch=batch, seq_len=seq_len, chunk=ML_SCAN_CHUNK)
                hf, hb, c_new, n_new = _ml_scan_call(q, kt, v, cols, rows, init, batch=batch, seq_len=seq_len,
                                                     chunk=ML_SCAN_CHUNK, cps=4 if g == "s" else 2)
                if g == "p":
                    new_c, new_n, new_m = c_new[:, None], n_new[:, None], m_new[:, None, :, :, 0]
                xmid[g] = [hf.reshape(batch * seq_len, ML_W), hb.reshape(batch * seq_len, ML_W), og]
            head_norm = ml_head_norm[j][None, :]
        conv = _pad_rows(jnp.concatenate([ffn_conv_w[l], ffn_conv_b[l][None, :]], axis=0))
        xout = {}
        for g, cfg in groups.items():
            xout[g] = _mix_ffn_call(xin[g], xmid[g], w_out, head_norm, mods[g], gains, ffn_up16, conv, ffn_down16,
                                    l, tm=cfg["ffn_tm"], tf=cfg["ffn_tf"], seq_len=cfg["seq_len"],
                                    rows_per_mod=cfg["rows_per_mod"])
        xp, xs = xout["p"], xout["s"]

    return (xp.reshape(bp, lp, D_MODEL), xs.reshape(bs, ls, D_MODEL), new_k, new_v, new_c, new_n, new_m)
```

```python
import functools
import math

import numpy as np
import jax
import jax.numpy as jnp
from jax import lax
from jax.experimental import pallas as pl
from jax.experimental.pallas import tpu as pltpu

F32 = jnp.float32
BF16 = jnp.bfloat16

D_MODEL = 1024
DEPTH = 2
GRID_W = 64
A_HEADS = 4
A_KV_HEADS = 2
A_HEAD_DIM = 128
A_Q = A_HEADS * A_HEAD_DIM
A_KV = A_KV_HEADS * A_HEAD_DIM
ROPE_THETA = 10000.0
HY_CH = D_MODEL // 2
HY_BANDS = 8
HY_EMB = 1 + 2 * HY_BANDS
HY_W = 64
HY_TARGET = 1e-2
HY_FAST_PCT = 0.3
HY_SLOW_PCT = 1.5
HY_MAX_DECAY = math.log(HY_TARGET) / HY_FAST_PCT
HY_MIN_DECAY = math.log(HY_TARGET) / HY_SLOW_PCT
AH_IN = A_Q + 2 * A_KV + 3 * HY_CH
ML_HEADS = 8
ML_HEAD_DIM = D_MODEL // ML_HEADS
ML_W = ML_HEADS * ML_HEAD_DIM
D_FF = 2816
NORM_EPS = 1e-6
NEG_BIG = -1e30

LANES = 128
VMEM_LIMIT = 56 * 1024 * 1024
COL_TILE = 256
ML_SCAN_CHUNK = 128
INV_SQRT2 = 1.0 / math.sqrt(2.0)


def _cparams(*sem):
    return pltpu.CompilerParams(dimension_semantics=sem, vmem_limit_bytes=VMEM_LIMIT)


def _rms(x, gain):
    return x * lax.rsqrt(jnp.mean(x * x, axis=-1, keepdims=True) + NORM_EPS) * gain


def _modulated(x, gain, shift, scale):
    return _rms(x, gain * (1.0 + scale)) + shift


def _dot(a, b):
    return jnp.dot(a, b, preferred_element_type=F32)


def _dot_nt(a, b):
    return lax.dot_general(a, b, (((1,), (1,)), ((), ())), preferred_element_type=F32)


def _dot_tn(a, b):
    return lax.dot_general(a, b, (((0,), (0,)), ((), ())), preferred_element_type=F32)


def _split3(x):
    hi = x.astype(BF16)
    r1 = x - hi.astype(F32)
    mid = r1.astype(BF16)
    lo = (r1 - mid.astype(F32)).astype(BF16)
    return hi, mid, lo


def _dot_x3(a, b):
    a0 = a.astype(BF16)
    a1 = (a - a0.astype(F32)).astype(BF16)
    b0 = b.astype(BF16)
    b1 = (b - b0.astype(F32)).astype(BF16)
    return _dot(a0, b0) + (_dot(a0, b1) + _dot(a1, b0))


def _dwconv3(y, conv_ref, seq_len, edge=None):
    rows, cols = y.shape
    pos = lax.broadcasted_iota(jnp.int32, (rows, LANES), 0)
    if edge is None:
        pos = pos % seq_len
        first = pos == 0
        last = pos == seq_len - 1
    else:
        first = pos == 0
        last = pos == rows - 1
    outs = []
    for c0 in range(0, cols, LANES):
        yc = y[:, c0:c0 + LANES]
        before = 0.0 if edge is None else edge[0][:, c0:c0 + LANES]
        after = 0.0 if edge is None else edge[1][:, c0:c0 + LANES]
        prev = jnp.where(first, before, pltpu.roll(yc, 1, 0))
        nxt = jnp.where(last, after, pltpu.roll(yc, rows - 1, 0))
        outs.append(prev * conv_ref[0:1, c0:c0 + LANES] + yc * conv_ref[1:2, c0:c0 + LANES]
                    + nxt * conv_ref[2:3, c0:c0 + LANES] + conv_ref[3:4, c0:c0 + LANES])
    return outs[0] if len(outs) == 1 else jnp.concatenate(outs, axis=1)


def _mod_kernel(c_ref, w_ref, b_ref, o_ref):
    a = c_ref[...]
    a = a * jax.nn.sigmoid(a)
    o_ref[0] = _dot(a.astype(BF16), w_ref[0].astype(BF16)) + b_ref[0]


def _mod_call(cc, w_mod, b_mod):
    tn = 3072
    n = 6 * D_MODEL
    return pl.pallas_call(
        _mod_kernel,
        grid=(DEPTH, n // tn),
        in_specs=[pl.BlockSpec((8, D_MODEL), lambda l, j: (0, 0)),
                  pl.BlockSpec((1, D_MODEL, tn), lambda l, j: (l, 0, j)),
                  pl.BlockSpec((1, 1, tn), lambda l, j: (l, 0, j))],
        out_specs=pl.BlockSpec((1, 8, tn), lambda l, j: (l, 0, j)),
        out_shape=jax.ShapeDtypeStruct((DEPTH, 8, n), F32),
        compiler_params=_cparams("arbitrary", "arbitrary"),
        name="mod",
    )(cc, w_mod, b_mod.reshape(DEPTH, 1, n))


def _ah_inproj_kernel(*refs, seq_len, rope, halo, cw):
    it = iter(refs)
    x_ref = next(it)
    xb_ref, xa_ref = (next(it), next(it)) if halo else (None, None)
    mod_ref, gain_ref, w_ref, qkn_ref, conv_ref = (next(it) for _ in range(5))
    cc_ref, ss_ref = (next(it), next(it)) if rope else (None, None)
    q_ref, k_ref, v_ref, u_ref = (next(it) for _ in range(4))
    kf_ref, vf_ref = (None, None) if rope else (next(it), next(it))
    tm = x_ref.shape[0]
    dh = A_HEAD_DIM

    def modulated(x):
        return _modulated(x, gain_ref[0:1, :], mod_ref[0, 0:1, :], mod_ref[0, 1:2, :]).astype(BF16)

    def head(yh, g):
        yh = _rms(yh, g)
        if rope:
            yh = yh * cc_ref[...] + pltpu.roll(yh, dh // 2, 1) * ss_ref[...]
        return yh

    xn = modulated(x_ref[...])
    yq = _dot(xn, w_ref[:, 0:A_Q])
    for h in range(A_HEADS):
        sl = slice(h * dh, (h + 1) * dh)
        q_ref[:, sl] = (head(yq[:, sl], qkn_ref[0:1, :]) * (dh ** -0.5)).astype(BF16)
    ykv = _dot(xn, w_ref[:, A_Q:A_Q + 2 * A_KV])
    for h in range(A_KV_HEADS):
        sl = slice(h * dh, (h + 1) * dh)
        kh = head(ykv[:, sl], qkn_ref[1:2, :])
        k_ref[:, sl] = kh.astype(BF16)
        if kf_ref is not None:
            kf_ref[pl.ds(h, tm, stride=A_KV_HEADS), :] = kh
    yv = ykv[:, A_KV:2 * A_KV]
    v_ref[...] = yv.astype(BF16)
    if vf_ref is not None:
        for h in range(A_KV_HEADS):
            vf_ref[pl.ds(h, tm, stride=A_KV_HEADS), :] = yv[:, h * dh:(h + 1) * dh]

    xe = xn
    if halo:
        xe = jnp.concatenate([xn, modulated(jnp.concatenate([xb_ref[...], xa_ref[...]], axis=0))], axis=0)
    u0 = A_Q + 2 * A_KV
    for t in range(3 * HY_CH // cw):
        cols = slice(u0 + t * cw, u0 + (t + 1) * cw)
        y = _dot(xe, w_ref[:, cols])
        edge = None
        if halo:
            y, edge = _halo_edges(y, tm, seq_len)
        u_ref[:, t * cw:(t + 1) * cw] = _dwconv3(y, conv_ref.at[:, cols], seq_len, edge).astype(BF16)


def _ah_inproj_call(x, mod, gains, w, qkn, conv, rope_tabs, *, tm, cw, seq_len, rows_per_mod):
    rows = x.shape[0]
    rope = rope_tabs is not None
    halo = tm < seq_len
    assert (seq_len % tm == 0) if halo else (tm % seq_len == 0)
    row_blk = lambda cols: pl.BlockSpec((tm, cols), lambda i: (i, 0))
    out_specs = [row_blk(A_Q), row_blk(A_KV), row_blk(A_KV), row_blk(3 * HY_CH)]
    out_shape = [jax.ShapeDtypeStruct((rows, A_Q), BF16), jax.ShapeDtypeStruct((rows, A_KV), BF16),
                 jax.ShapeDtypeStruct((rows, A_KV), BF16), jax.ShapeDtypeStruct((rows, 3 * HY_CH), BF16)]
    if not rope:
        out_specs += [pl.BlockSpec((tm * A_KV_HEADS, A_HEAD_DIM), lambda i: (i, 0))] * 2
        out_shape += [jax.ShapeDtypeStruct((rows * A_KV_HEADS, A_HEAD_DIM), F32)] * 2
    in_specs = [row_blk(D_MODEL)] + (_halo_specs(tm, rows) if halo else []) + [
        pl.BlockSpec((1, 8, D_MODEL), lambda i: (i * tm // rows_per_mod, 0, 0)),
        pl.BlockSpec((8, D_MODEL), lambda i: (0, 0)),
        pl.BlockSpec((D_MODEL, AH_IN), lambda i: (0, 0)),
        pl.BlockSpec((8, A_HEAD_DIM), lambda i: (0, 0)),
        pl.BlockSpec((8, AH_IN), lambda i: (0, 0))]
    args = ([x, x, x] if halo else [x]) + [mod, gains, w, qkn, conv]
    if rope:
        per_seq = max(seq_len // tm, 1)
        in_specs += [pl.BlockSpec((tm, A_HEAD_DIM), lambda i: (i % per_seq, 0))] * 2
        args += list(rope_tabs)
    return pl.pallas_call(
        functools.partial(_ah_inproj_kernel, seq_len=seq_len, rope=rope, halo=halo, cw=cw),
        grid=(rows // tm,),
        in_specs=in_specs,
        out_specs=out_specs,
        out_shape=out_shape,
        compiler_params=_cparams("arbitrary"),
        name="ah_inproj",
    )(*args)


def _attn_kernel(*refs, ctx, bb):
    if ctx:
        q_ref, k_ref, v_ref, kc_ref, vc_ref, o_ref = refs
    else:
        q_ref, k_ref, v_ref, o_ref = refs
    dh = A_HEAD_DIM
    for bi in range(bb):
        k = k_ref[bi]
        if ctx:
            kc = kc_ref[bi]
        else:
            vaug = jnp.concatenate([v_ref[bi], jnp.ones((k.shape[0], dh), BF16)], axis=1)
        for g in range(A_HEADS // A_KV_HEADS):
            sl = slice(g * dh, (g + 1) * dh)
            q = q_ref[bi, :, sl]
            s = _dot_nt(q, k)
            m = jnp.max(s, axis=-1, keepdims=True)
            if ctx:
                sc = _dot_nt(q, kc)
                m = jnp.maximum(m, jnp.max(sc, axis=-1, keepdims=True))
            if ctx:
                p = jnp.exp(s - m)
                pc = jnp.exp(sc - m)
                den = jnp.sum(p, axis=-1, keepdims=True) + jnp.sum(pc, axis=-1, keepdims=True)
                o = _dot(p.astype(BF16), v_ref[bi]) + _dot(pc.astype(BF16), vc_ref[bi])
                o_ref[bi, :, sl] = (o / den).astype(BF16)
            else:
                o = _dot(jnp.exp(s - m).astype(BF16), vaug)
                o_ref[bi, :, sl] = (o[:, :dh] / o[:, dh:]).astype(BF16)


def _attn_call(q, k, v, ctx_k, ctx_v, *, batch, seq_len, tq, bb):
    ctx = ctx_k is not None
    qw = A_Q // A_KV_HEADS
    in_specs = [pl.BlockSpec((bb, tq, qw), lambda b, h, i: (b, i, h)),
                pl.BlockSpec((bb, seq_len, A_HEAD_DIM), lambda b, h, i: (b, 0, h)),
                pl.BlockSpec((bb, seq_len, A_HEAD_DIM), lambda b, h, i: (b, 0, h))]
    args = [q.reshape(batch, seq_len, A_Q), k.reshape(batch, seq_len, A_KV), v.reshape(batch, seq_len, A_KV)]
    if ctx:
        past = ctx_k.shape[1]
        in_specs += [pl.BlockSpec((bb, past, A_HEAD_DIM), lambda b, h, i: (b, 0, h))] * 2
        args += [ctx_k, ctx_v]
    return pl.pallas_call(
        functools.partial(_attn_kernel, ctx=ctx, bb=bb),
        grid=(batch // bb, A_KV_HEADS, seq_len // tq),
        in_specs=in_specs,
        out_specs=pl.BlockSpec((bb, tq, qw), lambda b, h, i: (b, i, h)),
        out_shape=jax.ShapeDtypeStruct((batch, seq_len, A_Q), BF16),
        compiler_params=_cparams("arbitrary", "arbitrary", "arbitrary"),
        name="attn",
    )(*args)


def _hy_filter_kernel(z_ref, w1_ref, b1_ref, w2_ref, b2_ref, w3_ref, b3_ref, sf_ref, dl_ref,
                      cm_ref, sm_ref, gr_ref, gi_ref, hs_ref, hd_ref):
    @pl.when(pl.program_id(0) == 0)
    def _():
        z = z_ref[...]
        h = jnp.sin(sf_ref[0:1, :] * (_dot_x3(z, w1_ref[...]) + b1_ref[...]))
        h = jnp.sin(sf_ref[1:2, :] * (_dot_x3(h, w2_ref[...]) + b2_ref[...]))
        filt = _dot_x3(h, w3_ref[...]) + b3_ref[...]
        win = jnp.exp(-z[:, 0:1] * dl_ref[...])
        hf = filt[:, :HY_CH] * win
        hb = filt[:, HY_CH:] * win
        hs_ref[...] = (hf + hb).astype(BF16)
        hd_ref[...] = (hb - hf).astype(BF16)

    gr_ref[...] = _dot(cm_ref[...], hs_ref[...])
    gi_ref[...] = _dot(sm_ref[...], hd_ref[...])


def _hy_filter_call(z, w1, b1, w2, b2, w3, b3, sf, deltas, cm, sm, *, seq_len, tk):
    full = lambda a: pl.BlockSpec(a.shape, lambda k: (0,) * a.ndim)
    small = [z, w1, b1, w2, b2, w3, b3, sf, deltas]
    return pl.pallas_call(
        _hy_filter_kernel,
        grid=(seq_len // tk,),
        in_specs=[full(a) for a in small] + [pl.BlockSpec((tk, seq_len), lambda k: (k, 0))] * 2,
        out_specs=[pl.BlockSpec((tk, HY_CH), lambda k: (k, 0))] * 2,
        out_shape=[jax.ShapeDtypeStruct((seq_len, HY_CH), F32)] * 2,
        scratch_shapes=[pltpu.VMEM((seq_len, HY_CH), BF16)] * 2,
        compiler_params=_cparams("arbitrary"),
        name="hy_filter",
    )(*small, cm, sm)


def _hy_conv_kernel(x0_ref, x1_ref, v_ref, skip_ref, cm_ref, sm_ref, cmt_ref, smt_ref, gr_ref, gi_ref,
                    o_ref, vv_ref, acc_ref, *, seq_len, bb):
    kb = pl.program_id(2)
    single = seq_len == cm_ref.shape[0]
    gr = gr_ref[...]
    gi = gi_ref[...]
    for bi in range(bb):
        def gated():
            return v_ref[bi].astype(F32) * x1_ref[bi].astype(F32)

        if single:
            vv = gated().astype(BF16)
        else:
            @pl.when(kb == 0)
            def _():
                vv_ref[bi] = gated().astype(BF16)
                acc_ref[bi] = jnp.zeros_like(acc_ref[bi])

            vv = vv_ref[bi]
        vr = _dot(cm_ref[...], vv)
        wi = _dot(sm_ref[...], vv)
        pr = gr * vr + gi * wi
        qi = gr * wi - gi * vr
        y = _dot(cmt_ref[...], pr.astype(BF16)) + _dot(smt_ref[...], qi.astype(BF16))

        def finish(total):
            o_ref[bi] = ((total * (1.0 / seq_len) + skip_ref[...] * gated())
                         * x0_ref[bi].astype(F32)).astype(BF16)

        if single:
            finish(y)
        else:
            acc_ref[bi] += y

            @pl.when(kb == pl.num_programs(2) - 1)
            def _():
                finish(acc_ref[bi])


def _hy_conv_call(u, skip, dft, gr, gi, *, batch, seq_len, tc, tk, bb):
    proj3 = u.reshape(batch, seq_len, 3 * HY_CH)
    cm, sm, cmt, smt = dft
    nch = HY_CH // tc
    return pl.pallas_call(
        functools.partial(_hy_conv_kernel, seq_len=seq_len, bb=bb),
        grid=(batch // bb, nch, seq_len // tk),
        in_specs=[pl.BlockSpec((bb, seq_len, tc), lambda b, c, k: (b, 0, c)),
                  pl.BlockSpec((bb, seq_len, tc), lambda b, c, k: (b, 0, nch + c)),
                  pl.BlockSpec((bb, seq_len, tc), lambda b, c, k: (b, 0, 2 * nch + c)),
                  pl.BlockSpec((1, tc), lambda b, c, k: (0, c)),
                  pl.BlockSpec((tk, seq_len), lambda b, c, k: (k, 0)),
                  pl.BlockSpec((tk, seq_len), lambda b, c, k: (k, 0)),
                  pl.BlockSpec((seq_len, tk), lambda b, c, k: (0, k)),
                  pl.BlockSpec((seq_len, tk), lambda b, c, k: (0, k)),
                  pl.BlockSpec((tk, tc), lambda b, c, k: (k, c)),
                  pl.BlockSpec((tk, tc), lambda b, c, k: (k, c))],
        out_specs=pl.BlockSpec((bb, seq_len, tc), lambda b, c, k: (b, 0, c)),
        out_shape=jax.ShapeDtypeStruct((batch, seq_len, HY_CH), BF16),
        scratch_shapes=[pltpu.VMEM((bb, seq_len, tc), BF16), pltpu.VMEM((bb, seq_len, tc), F32)],
        compiler_params=_cparams("arbitrary", "arbitrary", "arbitrary"),
        name="hy_conv",
    )(proj3, proj3, proj3, skip, cm, sm, cmt, smt, gr, gi)


HALO_ROWS = 16


def _mix_ffn_kernel(*refs, seq_len, halo, tf, mixer):
    it = iter(refs)

    def rows_of():
        parts = [next(it)[...] for _ in range(3 if halo else 1)]
        return parts[0] if len(parts) == 1 else jnp.concatenate(parts, axis=0)

    x = rows_of()
    mix_in = [rows_of() for _ in range(2 if mixer == "ah" else 3)]
    mod_ref, gain_ref, wo_ref = next(it), next(it), next(it)
    hn_ref = next(it) if mixer == "ml" else None
    wu_ref, conv_ref, wd_ref, o_ref = next(it), next(it), next(it), next(it)
    tm = o_ref.shape[0]

    if mixer == "ah":
        mixed = _dot(mix_in[0], wo_ref[0:A_Q, :]) + _dot(mix_in[1], wo_ref[A_Q:, :])
    else:
        h = mix_in[0].astype(F32) + mix_in[1].astype(F32)
        parts = []
        for hd in range(ML_HEADS):
            sl = slice(hd * ML_HEAD_DIM, (hd + 1) * ML_HEAD_DIM)
            parts.append(_rms(h[:, sl], hn_ref[0:1, sl]))
        gated = jnp.concatenate(parts, axis=1) * jax.nn.sigmoid(mix_in[2].astype(F32))
        mixed = _dot(gated.astype(BF16), wo_ref[...])
    x_mid = x + _rms(mixed, gain_ref[1:2, :] * mod_ref[0, 2:3, :])

    xn = _modulated(x_mid, gain_ref[2:3, :], mod_ref[0, 3:4, :], mod_ref[0, 4:5, :]).astype(BF16)
    out = None
    for t in range(D_FF // tf):
        cg = slice(t * tf, (t + 1) * tf)
        cl = slice(D_FF + t * tf, D_FF + (t + 1) * tf)
        yg = _dot(xn, wu_ref[0, :, cg])
        yl = _dot(xn, wu_ref[0, :, cl])
        edge_g = edge_l = None
        if halo:
            yg, edge_g = _halo_edges(yg, tm, seq_len, HALO_ROWS)
            yl, edge_l = _halo_edges(yl, tm, seq_len, HALO_ROWS)
        hg = _dwconv3(yg, conv_ref.at[:, cg], seq_len, edge_g)
        hl = _dwconv3(yl, conv_ref.at[:, cl], seq_len, edge_l)
        act = (0.5 * hg * (1.0 + lax.erf(hg * INV_SQRT2))) * hl
        part = _dot(act.astype(BF16), wd_ref[0, cg, :])
        out = part if out is None else out + part
    o_ref[...] = x_mid[0:tm, :] + _rms(out, gain_ref[3:4, :] * mod_ref[0, 5:6, :])


def _mix_ffn_call(x, mix_in, w_out, head_norm, mod, gains, w_up, conv, w_down, layer, *,
                  tm, tf, seq_len, rows_per_mod):
    rows = x.shape[0]
    mixer = "ah" if head_norm is None else "ml"
    halo = tm < seq_len
    assert (seq_len % tm == 0) if halo else (tm % seq_len == 0)
    row_specs, row_args = [], []
    for a in [x] + list(mix_in):
        cols = a.shape[1]
        row_specs.append(pl.BlockSpec((tm, cols), lambda i: (i, 0)))
        row_args.append(a)
        if halo:
            row_specs += _halo_specs(tm, rows, cols, HALO_ROWS)
            row_args += [a, a]
    whole = lambda a: pl.BlockSpec(a.shape, lambda i: (0,) * a.ndim)
    resident = pl.Buffered(1)
    extra = [] if head_norm is None else [head_norm]
    return pl.pallas_call(
        functools.partial(_mix_ffn_kernel, seq_len=seq_len, halo=halo, tf=tf, mixer=mixer),
        grid=(rows // tm,),
        in_specs=row_specs + [
            pl.BlockSpec((1, 8, D_MODEL), lambda i: (i * tm // rows_per_mod, 0, 0)),
            whole(gains), whole(w_out)] + [whole(a) for a in extra] + [
            pl.BlockSpec((1, D_MODEL, 2 * D_FF), lambda i: (layer, 0, 0), pipeline_mode=resident),
            whole(conv),
            pl.BlockSpec((1, D_FF, D_MODEL), lambda i: (layer, 0, 0), pipeline_mode=resident)],
        out_specs=pl.BlockSpec((tm, D_MODEL), lambda i: (i, 0)),
        out_shape=jax.ShapeDtypeStruct((rows, D_MODEL), F32),
        compiler_params=_cparams("arbitrary"),
        name="mix_ffn",
    )(*row_args, mod, gains, w_out, *extra, w_up, conv, w_down)


def _halo_specs(tm, rows, cols=D_MODEL, sub=8):
    per = tm // sub
    return [pl.BlockSpec((sub, cols), lambda i, *_: (jnp.maximum(i * per - 1, 0), 0)),
            pl.BlockSpec((sub, cols), lambda i, *_: (jnp.minimum((i + 1) * per, rows // sub - 1), 0))]


def _halo_edges(y, tm, seq_len, sub=8):
    i = pl.program_id(0)
    at_start = (i * tm) % seq_len == 0
    at_end = ((i + 1) * tm) % seq_len == 0
    edge = (jnp.where(at_start, 0.0, y[tm + sub - 1:tm + sub, :]),
            jnp.where(at_end, 0.0, y[tm + sub:tm + sub + 1, :]))
    return y[0:tm, :], edge


def _ml_inproj_kernel(*refs, seq_len, cw, halo):
    if halo:
        x_ref, xb_ref, xa_ref, mod_ref, gain_ref, w_ref, wg_ref, bg_ref, conv_ref = refs[:9]
    else:
        x_ref, mod_ref, gain_ref, w_ref, wg_ref, bg_ref, conv_ref = refs[:7]
    q_ref, kt_ref, v_ref, o_ref, g_ref = refs[-5:]
    tm = x_ref.shape[0]

    def modulated(x):
        return _modulated(x, gain_ref[0:1, :], mod_ref[0, 0:1, :], mod_ref[0, 1:2, :]).astype(BF16)

    xn = modulated(x_ref[...])
    g_ref[...] = (_dot(xn, wg_ref[...]) + bg_ref[...]).T
    xe = xn
    if halo:
        xe = jnp.concatenate([xn, modulated(jnp.concatenate([xb_ref[...], xa_ref[...]], axis=0))], axis=0)

    for t in range(2 * ML_W // cw):
        cols = slice(t * cw, (t + 1) * cw)
        y = _dot(xe, w_ref[:, cols])
        edge = None
        if halo:
            y, edge = _halo_edges(y, tm, seq_len)
        z = _dwconv3(y, conv_ref.at[:, cols], seq_len, edge)
        z = z * jax.nn.sigmoid(z)
        if t * cw < ML_W:
            q_ref[:, cols] = z.astype(BF16)
        else:
            kt_ref[t * cw - ML_W:(t + 1) * cw - ML_W, :] = (z * (ML_HEAD_DIM ** -0.5)).T.astype(BF16)
        y = _dot(xn, w_ref[:, 2 * ML_W + t * cw:2 * ML_W + (t + 1) * cw]).astype(BF16)
        if t * cw < ML_W:
            v_ref[:, t * cw:(t + 1) * cw] = y
        else:
            o_ref[:, t * cw - ML_W:(t + 1) * cw - ML_W] = y


def _ml_inproj_call(x, mod, gains, w, wg, bg, conv, *, tm, cw, seq_len, rows_per_mod):
    rows = x.shape[0]
    n = 4 * ML_W
    halo = tm < seq_len
    assert (seq_len % tm == 0) if halo else (tm % seq_len == 0)
    x_specs = [pl.BlockSpec((tm, D_MODEL), lambda i: (i, 0))] + (_halo_specs(tm, rows) if halo else [])
    return pl.pallas_call(
        functools.partial(_ml_inproj_kernel, seq_len=seq_len, cw=cw, halo=halo),
        grid=(rows // tm,),
        in_specs=x_specs + [
            pl.BlockSpec((1, 8, D_MODEL), lambda i: (i * tm // rows_per_mod, 0, 0)),
            pl.BlockSpec((8, D_MODEL), lambda i: (0, 0)),
            pl.BlockSpec((D_MODEL, n), lambda i: (0, 0)),
            pl.BlockSpec((D_MODEL, LANES), lambda i: (0, 0)),
            pl.BlockSpec((1, LANES), lambda i: (0, 0)),
            pl.BlockSpec((8, n), lambda i: (0, 0))],
        out_specs=[pl.BlockSpec((tm, ML_W), lambda i: (i, 0)),
                   pl.BlockSpec((ML_W, tm), lambda i: (0, i)),
                   pl.BlockSpec((tm, ML_W), lambda i: (i, 0)),
                   pl.BlockSpec((tm, ML_W), lambda i: (i, 0)),
                   pl.BlockSpec((LANES, tm), lambda i: (0, i))],
        out_shape=[jax.ShapeDtypeStruct((rows, ML_W), BF16), jax.ShapeDtypeStruct((ML_W, rows), BF16),
                   jax.ShapeDtypeStruct((rows, ML_W), BF16), jax.ShapeDtypeStruct((rows, ML_W), BF16),
                   jax.ShapeDtypeStruct((LANES, rows), F32)],
        compiler_params=_cparams("arbitrary"),
        name="ml_inproj",
    )(*([x, x, x] if halo else [x]), mod, gains, w, wg, bg, conv)


def _log_sigmoid(x):
    return jnp.minimum(x, 0.0) - jnp.log1p(jnp.exp(-jnp.abs(x)))


def _lane_scan(x, op, reverse, fill, seg):
    n = x.shape[1]
    pos = lax.broadcasted_iota(jnp.int32, x.shape, 1) % seg
    sh = 1
    while sh < seg:
        if reverse:
            x = op(x, jnp.where(pos < seg - sh, pltpu.roll(x, n - sh, 1), fill))
        else:
            x = op(x, jnp.where(pos >= sh, pltpu.roll(x, sh, 1), fill))
        sh *= 2
    return x


def _bcast_selectors(chunk):
    sel = np.zeros((ML_HEADS, LANES, chunk + LANES), np.float32)
    for h in range(ML_HEADS):
        for g in range(3):
            sel[h, g * ML_HEADS + h, :chunk] = 1.0
            sel[h, (3 + g) * ML_HEADS + h, chunk:] = 1.0
    return jnp.asarray(sel, BF16)


def _ml_scan_kernel(*refs, chunk, cps, has_init, whole_seq):
    if has_init:
        (qf, ktf, vf, qb, ktb, vb, colf, colb, rowf, rowb, sel_ref, c0_ref, n0_ref,
         hf_ref, hb_ref, c_ref, n_ref, s_sc, p16_sc, e_sc) = refs
    else:
        (qf, ktf, vf, qb, ktb, vb, colf, colb, rowf, rowb, sel_ref,
         hf_ref, hb_ref, c_ref, n_ref, s_sc, p16_sc, e_sc) = refs
    nh, dh = ML_HEADS, ML_HEAD_DIM
    step = pl.program_id(1)
    zero_start = (not has_init) and whole_seq

    if not zero_start:
        @pl.when(step == 0)
        def _():
            if has_init:
                for d in range(2):
                    for h in range(nh):
                        s_sc[d, h, :, 0:dh] = c0_ref[0, d, h].T
                        s_sc[d, h, :, dh:2 * dh] = jnp.broadcast_to(n0_ref[0, d, h:h + 1, :], (dh, dh)).T
            else:
                s_sc[...] = jnp.zeros_like(s_sc)

    row = lax.broadcasted_iota(jnp.int32, (chunk, chunk), 0)
    col = lax.broadcasted_iota(jnp.int32, (chunk, chunk), 1)
    ones_cols = jnp.ones((chunk, dh), BF16)

    dirs = ((qf, ktf, vf, colf, rowf, hf_ref), (qb, ktb, vb, colb, rowb, hb_ref))
    for sub in range(cps):
        for d, (q_ref, kt_ref, _, col_ref, row_ref, _) in enumerate(dirs):
            reverse = d == 1
            lc = cps - 1 - sub if reverse else sub
            tok = slice(lc * chunk, (lc + 1) * chunk)
            cols = col_ref[0, lc, 0]
            r = row_ref[0, lc, 0, 0:nh, :]
            m_prev = row_ref[0, lc, 0, 3 * nh:4 * nh, :]
            mask = (col >= row) if reverse else (col <= row)
            for h in range(nh):
                sl = slice(h * dh, (h + 1) * dh)
                bc = _dot(cols, sel_ref[h])
                p_bc = bc[:, :chunk]
                w = jnp.exp(jnp.where(mask, p_bc + r[h:h + 1, :], NEG_BIG))
                p16_sc[sub, d, h] = (_dot(q_ref[0, tok, sl], kt_ref[sl, tok]) * w).astype(BF16)
                if not (zero_start and sub == 0):
                    e_sc[sub, d, h, :, 0:dh] = jnp.exp(p_bc[:, :dh] + m_prev[h:h + 1, 0:dh])
                e_sc[sub, d, h, :, dh:2 * dh] = jnp.exp(bc[:, chunk:])
    for sub in range(cps):
        for d, (q_ref, kt_ref, v_ref, _, row_ref, h_ref) in enumerate(dirs):
            lc = cps - 1 - sub if d == 1 else sub
            tok = slice(lc * chunk, (lc + 1) * chunk)
            w_tok = row_ref[0, lc, 0, nh:2 * nh, :]
            w_state = row_ref[0, lc, 0, 2 * nh:3 * nh, :]
            for h in range(nh):
                sl = slice(h * dh, (h + 1) * dh)
                q16 = q_ref[0, tok, sl]
                kt16 = kt_ref[sl, tok]
                vaug = jnp.concatenate([v_ref[0, tok, sl], ones_cols], axis=1)
                tot = _dot(p16_sc[sub, d, h], vaug)
                kts = (kt16.astype(F32) * w_tok[h:h + 1, :]).astype(BF16)
                s_new = _dot(kts, vaug)
                if not (zero_start and sub == 0):
                    s_old = s_sc[d, h]
                    w_inter = e_sc[sub, d, h, :, 0:dh]
                    tot = tot + jnp.concatenate([w_inter, w_inter], axis=1) * _dot(q16, s_old.astype(BF16))
                    ws = w_state[h:h + 1, 0:dh]
                    s_new = jnp.concatenate([ws, ws], axis=1) * s_old + s_new
                h_ref[0, tok, sl] = (tot[:, :dh] / jnp.maximum(jnp.abs(tot[:, dh:]),
                                                               e_sc[sub, d, h, :, dh:2 * dh])).astype(BF16)
                s_sc[d, h] = s_new

    @pl.when(step == pl.num_programs(1) - 1)
    def _():
        for d in range(2):
            for h in range(nh):
                s_fin = s_sc[d, h]
                c_ref[0, d, h] = s_fin[:, 0:dh].T
                n_ref[0, d, h:h + 1, :] = s_fin[:, dh:2 * dh].T[0:1, :]


def _ml_gate_kernel(*refs, nc, chunk, bb, has_init):
    if has_init:
        gt_ref, m0_ref, cols_ref, rows_ref, m_ref = refs
    else:
        gt_ref, cols_ref, rows_ref, m_ref = refs
    nh = ML_HEADS
    pad = jnp.zeros((LANES - 6 * nh, chunk), F32)
    for d in range(2):
        reverse = d == 1
        last = 0 if reverse else chunk - 1
        b_all = _lane_scan(_log_sigmoid(gt_ref[nh * (2 + d):nh * (3 + d), :]), jnp.add, reverse, 0.0, chunk)
        r_all = gt_ref[nh * d:nh * (d + 1), :] - b_all
        g_all = _lane_scan(r_all, jnp.maximum, reverse, -jnp.inf, chunk)
        for bi in range(bb):
            m = m0_ref[bi, d] if has_init else jnp.zeros((nh, chunk), F32)
            for c in (range(nc - 1, -1, -1) if reverse else range(nc)):
                c0 = (bi * nc + c) * chunk
                b = b_all[:, c0:c0 + chunk]
                r = r_all[:, c0:c0 + chunk]
                mx = jnp.maximum(m, g_all[:, c0:c0 + chunk])
                mt = b + mx
                m_new = jnp.broadcast_to(mt[:, last:last + 1], (nh, chunk))
                b_last = jnp.broadcast_to(b[:, last:last + 1], (nh, chunk))
                rows_ref[bi, c, d, 0:nh, :] = r
                rows_ref[bi, c, d, nh:2 * nh, :] = jnp.exp(r + b_last - m_new)
                rows_ref[bi, c, d, 2 * nh:3 * nh, :] = jnp.exp(b_last + m - m_new)
                rows_ref[bi, c, d, 3 * nh:4 * nh, :] = m
                stack = []
                for x in (-mx, -mt):
                    stack += [t.astype(F32) for t in _split3(x)]
                stack.append(pad)
                cols_ref[bi, c, d] = jnp.concatenate(stack, axis=0).T.astype(BF16)
                m = m_new
            m_ref[bi, d] = m[:, 0:LANES]


def _ml_gate_call(gates, m0, *, batch, seq_len, chunk):
    nc = seq_len // chunk
    bb = max(1, 16 // nc)
    nh = ML_HEADS
    has_init = m0 is not None
    in_specs = [pl.BlockSpec((LANES, bb * seq_len), lambda i: (0, i))]
    args = [gates]
    if has_init:
        in_specs.append(pl.BlockSpec((bb, 2, nh, chunk), lambda i: (i, 0, 0, 0)))
        args.append(jnp.broadcast_to(m0[..., None], m0.shape + (chunk,)))
    return pl.pallas_call(
        functools.partial(_ml_gate_kernel, nc=nc, chunk=chunk, bb=bb, has_init=has_init),
        grid=(batch // bb,),
        in_specs=in_specs,
        out_specs=[pl.BlockSpec((bb, nc, 2, chunk, LANES), lambda i: (i, 0, 0, 0, 0)),
                   pl.BlockSpec((bb, nc, 2, 4 * nh, chunk), lambda i: (i, 0, 0, 0, 0)),
                   pl.BlockSpec((bb, 2, nh, LANES), lambda i: (i, 0, 0, 0))],
        out_shape=[jax.ShapeDtypeStruct((batch, nc, 2, chunk, LANES), BF16),
                   jax.ShapeDtypeStruct((batch, nc, 2, 4 * nh, chunk), F32),
                   jax.ShapeDtypeStruct((batch, 2, nh, LANES), F32)],
        compiler_params=_cparams("arbitrary"),
        name="ml_gate",
    )(*args)


def _ml_scan_call(q, kt, v, cols, rows, init, *, batch, seq_len, chunk, cps):
    nc = seq_len // (chunk * cps)
    nh = ML_HEADS
    blk = chunk * cps
    q3 = q.reshape(batch, seq_len, ML_W)
    v3 = v.reshape(batch, seq_len, ML_W)
    sel = _bcast_selectors(chunk)
    has_init = init is not None

    fwd3 = lambda b, i: (b, i, 0)
    bwd3 = lambda b, i: (b, nc - 1 - i, 0)
    in_specs = [pl.BlockSpec((1, blk, ML_W), fwd3),
                pl.BlockSpec((ML_W, blk), lambda b, i: (0, b * nc + i)),
                pl.BlockSpec((1, blk, ML_W), fwd3),
                pl.BlockSpec((1, blk, ML_W), bwd3),
                pl.BlockSpec((ML_W, blk), lambda b, i: (0, b * nc + nc - 1 - i)),
                pl.BlockSpec((1, blk, ML_W), bwd3),
                pl.BlockSpec((1, cps, 1, chunk, LANES), lambda b, i: (b, i, 0, 0, 0)),
                pl.BlockSpec((1, cps, 1, chunk, LANES), lambda b, i: (b, nc - 1 - i, 1, 0, 0)),
                pl.BlockSpec((1, cps, 1, 4 * nh, chunk), lambda b, i: (b, i, 0, 0, 0)),
                pl.BlockSpec((1, cps, 1, 4 * nh, chunk), lambda b, i: (b, nc - 1 - i, 1, 0, 0)),
                pl.BlockSpec(sel.shape, lambda b, i: (0, 0, 0))]
    args = [q3, kt, v3, q3, kt, v3, cols, cols, rows, rows, sel]
    c_spec = pl.BlockSpec((1, 2, nh, ML_HEAD_DIM, ML_HEAD_DIM), lambda b, i: (b, 0, 0, 0, 0))
    n_spec = pl.BlockSpec((1, 2, nh, ML_HEAD_DIM), lambda b, i: (b, 0, 0, 0))
    if has_init:
        in_specs += [c_spec, n_spec]
        args += list(init)
    return pl.pallas_call(
        functools.partial(_ml_scan_kernel, chunk=chunk, cps=cps, has_init=has_init, whole_seq=nc == 1),
        grid=(batch, nc),
        in_specs=in_specs,
        out_specs=[pl.BlockSpec((1, blk, ML_W), fwd3),
                   pl.BlockSpec((1, blk, ML_W), bwd3),
                   c_spec, n_spec],
        out_shape=[jax.ShapeDtypeStruct((batch, seq_len, ML_W), BF16),
                   jax.ShapeDtypeStruct((batch, seq_len, ML_W), BF16),
                   jax.ShapeDtypeStruct((batch, 2, nh, ML_HEAD_DIM, ML_HEAD_DIM), F32),
                   jax.ShapeDtypeStruct((batch, 2, nh, ML_HEAD_DIM), F32)],
        scratch_shapes=[pltpu.VMEM((2, nh, ML_HEAD_DIM, 2 * ML_HEAD_DIM), F32),
                        pltpu.VMEM((cps, 2, nh, chunk, chunk), BF16),
                        pltpu.VMEM((cps, 2, nh, chunk, 2 * ML_HEAD_DIM), F32)],
        compiler_params=_cparams("arbitrary", "arbitrary"),
        name="ml_scan",
    )(*args)


@functools.lru_cache(maxsize=None)
def _dft_tables_np(seq_len):
    n4 = 4 * seq_len
    ar = np.arange(seq_len, dtype=np.int64)
    idx = ((2 * ar + 1)[:, None] * ar[None, :]) % n4
    ang = (2.0 * np.pi / n4) * idx.astype(np.float64)
    cm = np.cos(ang).astype(np.float32)
    sm = np.sin(ang).astype(np.float32)
    return tuple(np.ascontiguousarray(a) for a in (cm, sm, cm.T, sm.T))


def _dft_tables(seq_len):
    return tuple(jnp.asarray(a).astype(BF16) for a in _dft_tables_np(seq_len))


def _hy_features(seq_len):
    t = np.linspace(0.0, 1.0, seq_len, dtype=np.float32)
    bands = np.arange(1, HY_BANDS + 1, dtype=np.float32)
    ang = (np.float32(2.0 * np.pi) * t[:, None]) * bands
    z = np.concatenate([t[:, None], np.cos(ang), np.sin(ang)], axis=-1).astype(np.float32)
    z = np.pad(z, ((0, 0), (0, 32 - HY_EMB)))
    deltas = np.abs(np.linspace(HY_MIN_DECAY, HY_MAX_DECAY, HY_CH, dtype=np.float32))[None, :]
    return jnp.asarray(z), jnp.asarray(deltas)


def _rope_tables(seq_len):
    rows = seq_len // GRID_W
    row = np.repeat(np.arange(rows, dtype=np.float32), GRID_W)
    col = np.tile(np.arange(GRID_W, dtype=np.float32), rows)
    n_freq = A_HEAD_DIM // 4
    inv = (np.float32(ROPE_THETA) ** (-np.arange(n_freq, dtype=np.float32) / n_freq)).astype(np.float32)
    ang = np.concatenate([row[:, None] * inv, col[:, None] * inv], axis=-1).astype(np.float32)
    cos, sin = np.cos(ang), np.sin(ang)
    return (jnp.asarray(np.concatenate([cos, cos], axis=-1), F32),
            jnp.asarray(np.concatenate([-sin, sin], axis=-1), F32))


def _pad_rows(a, rows=8):
    return jnp.pad(a, ((0, rows - a.shape[0]), (0, 0)))


def kernel(x_prompt, x_sample, cache_attn_k, cache_attn_v, state_mlstm_C, state_mlstm_n, state_mlstm_m, c, c_ctx, w_mod, b_mod, norm_mix_pre, norm_mix_post, norm_ffn_pre, norm_ffn_post, ffn_w_up, ffn_conv_w, ffn_conv_b, ffn_w_down, ah_w_in, ah_w_out, attn_q_norm, attn_k_norm, hy_conv_w, hy_conv_b, hy_w1, hy_b1, hy_w2, hy_b2, hy_w3, hy_b3, hy_sin_freq, hy_skip, ml_w_in, ml_b_gates, ml_conv_w, ml_conv_b, ml_head_norm, ml_w_out):
    bp, lp, _ = x_prompt.shape
    bs, ls, _ = x_sample.shape
    past = cache_attn_k.shape[2]
    xp = x_prompt.reshape(bp * lp, D_MODEL)
    xs = x_sample.reshape(bs * ls, D_MODEL)
    groups = {
        "p": dict(batch=bp, seq_len=lp, tm=1024, rows_per_mod=bp * lp, ffn_tm=512, ffn_tf=D_FF,
                  seq_bb=8, attn_bb=8, hy_tc=HY_CH, hy_tk=256, ml_cw=COL_TILE),
        "s": dict(batch=bs, seq_len=ls, tm=ls, rows_per_mod=ls, ffn_tm=512, ffn_tf=D_FF,
                  seq_bb=1, attn_bb=2, hy_tc=HY_CH, hy_tk=512, ml_cw=512),
    }
    ffn_up16 = ffn_w_up.astype(BF16)
    ffn_down16 = ffn_w_down.astype(BF16)

    cc = jnp.concatenate([c, c_ctx[None, :], jnp.zeros((8 - bs - 1, D_MODEL), F32)], axis=0)
    mod_all = _mod_call(cc, w_mod, b_mod).reshape(DEPTH, 8, 6, D_MODEL)
    mod_all = jnp.pad(mod_all, ((0, 0), (0, 0), (0, 2), (0, 0)))
    rope = _rope_tables(ls)

    new_k = new_v = new_c = new_n = new_m = None
    for l in range(DEPTH):
        j = l // 2
        mods = {"s": mod_all[l, :bs], "p": mod_all[l, bs:bs + 1]}
        gains = _pad_rows(jnp.stack([norm_mix_pre[l], norm_mix_post[l], norm_ffn_pre[l], norm_ffn_post[l]]))
        xin = {"p": xp, "s": xs}
        xmid = {}
        if l % 2 == 0:
            w_in = ah_w_in[j].astype(BF16)
            w_out = ah_w_out[j].astype(BF16)
            qkn = _pad_rows(jnp.stack([attn_q_norm[j], attn_k_norm[j]]))
            conv = jnp.pad(_pad_rows(jnp.concatenate([hy_conv_w[j], hy_conv_b[j][None, :]], axis=0)),
                           ((0, 0), (A_Q + 2 * A_KV, 0)))
            w1 = jnp.pad(hy_w1[j], ((0, 32 - HY_EMB), (0, 0)))
            sf = _pad_rows(hy_sin_freq[j])
            for g, cfg in groups.items():
                batch, seq_len, tm, rpm = cfg["batch"], cfg["seq_len"], cfg["tm"], cfg["rows_per_mod"]
                outs = _ah_inproj_call(xin[g], mods[g], gains, w_in, qkn, conv, rope if g == "s" else None,
                                       tm=512, cw=cfg["ml_cw"], seq_len=seq_len, rows_per_mod=rpm)
                q, k, v, u = outs[:4]
                if g == "s":
                    ctx_k = cache_attn_k[:, j].reshape(bs, past, A_KV).astype(BF16)
                    ctx_v = cache_attn_v[:, j].reshape(bs, past, A_KV).astype(BF16)
                else:
                    ctx_k = ctx_v = None
                    new_k = outs[4].reshape(bp, 1, lp, A_KV_HEADS, A_HEAD_DIM)
                    new_v = outs[5].reshape(bp, 1, lp, A_KV_HEADS, A_HEAD_DIM)
                attn = _attn_call(q, k, v, ctx_k, ctx_v, batch=batch, seq_len=seq_len, tq=256,
                                  bb=cfg["attn_bb"])
                z, deltas = _hy_features(seq_len)
                dft = _dft_tables(seq_len)
                tk = cfg["hy_tk"]
                gr, gi = _hy_filter_call(z, w1, hy_b1[j][None, :], hy_w2[j], hy_b2[j][None, :], hy_w3[j],
                                         hy_b3[j][None, :], sf, deltas, dft[0], dft[1], seq_len=seq_len, tk=tk)
                hyo = _hy_conv_call(u, hy_skip[j][None, :], dft, gr, gi, batch=batch, seq_len=seq_len,
                                    tc=cfg["hy_tc"], tk=tk, bb=cfg["seq_bb"])
                xmid[g] = [attn.reshape(batch * seq_len, A_Q), hyo.reshape(batch * seq_len, HY_CH)]
            head_norm = None
        else:
            w_in = ml_w_in[j].astype(BF16)
            w_g = jnp.pad(ml_w_in[j][:, 4 * ML_W:], ((0, 0), (0, LANES - 4 * ML_HEADS))).astype(BF16)
            b_g = jnp.pad(ml_b_gates[j], (0, LANES - 4 * ML_HEADS))[None, :]
            w_out = ml_w_out[j].astype(BF16)
            conv = jnp.pad(_pad_rows(jnp.concatenate([ml_conv_w[j], ml_conv_b[j][None, :]], axis=0)),
                           ((0, 0), (0, 2 * ML_W)))
            for g, cfg in groups.items():
                batch, seq_len, tm, rpm = cfg["batch"], cfg["seq_len"], cfg["tm"], cfg["rows_per_mod"]
                q, kt, v, og, gates = _ml_inproj_call(xin[g], mods[g], gains, w_in, w_g, b_g, conv, tm=512,
                                                      cw=cfg["ml_cw"], seq_len=seq_len, rows_per_mod=rpm)
                if g == "s":
                    init, m0 = (state_mlstm_C[:, j], state_mlstm_n[:, j]), state_mlstm_m[:, j]
                else:
                    init = m0 = None
                cols, rows, m_new = _ml_gate_call(gates, m0, batch=batch, seq_len=seq_len, chunk=ML_SCAN_CHUNK)
                hf, hb, c_new, n_new = _ml_scan_call(q, kt, v, cols, rows, init, batch=batch, seq_len=seq_len,
                                                     chunk=ML_SCAN_CHUNK, cps=4 if g == "s" else 2)
                if g == "p":
                    new_c, new_n, new_m = c_new[:, None], n_new[:, None], m_new[:, None, :, :, 0]
                xmid[g] = [hf.reshape(batch * seq_len, ML_W), hb.reshape(batch * seq_len, ML_W), og]
            head_norm = ml_head_norm[j][None, :]
        conv = _pad_rows(jnp.concatenate([ffn_conv_w[l], ffn_conv_b[l][None, :]], axis=0))
        xout = {}
        for g, cfg in groups.items():
            xout[g] = _mix_ffn_call(xin[g], xmid[g], w_out, head_norm, mods[g], gains, ffn_up16, conv, ffn_down16,
                                    l, tm=cfg["ffn_tm"], tf=cfg["ffn_tf"], seq_len=cfg["seq_len"],
                                    rows_per_mod=cfg["rows_per_mod"])
        xp, xs = xout["p"], xout["s"]

    return (xp.reshape(bp, lp, D_MODEL), xs.reshape(bs, ls, D_MODEL), new_k, new_v, new_c, new_n, new_m)
```

```python
import functools
import math

import numpy as np
import jax
import jax.numpy as jnp
from jax import lax
from jax.experimental import pallas as pl
from jax.experimental.pallas import tpu as pltpu

F32 = jnp.float32
BF16 = jnp.bfloat16

D_MODEL = 1024
DEPTH = 2
GRID_W = 64
A_HEADS = 4
A_KV_HEADS = 2
A_HEAD_DIM = 128
A_Q = A_HEADS * A_HEAD_DIM
A_KV = A_KV_HEADS * A_HEAD_DIM
ROPE_THETA = 10000.0
HY_CH = D_MODEL // 2
HY_BANDS = 8
HY_EMB = 1 + 2 * HY_BANDS
HY_W = 64
HY_TARGET = 1e-2
HY_FAST_PCT = 0.3
HY_SLOW_PCT = 1.5
HY_MAX_DECAY = math.log(HY_TARGET) / HY_FAST_PCT
HY_MIN_DECAY = math.log(HY_TARGET) / HY_SLOW_PCT
AH_IN = A_Q + 2 * A_KV + 3 * HY_CH
ML_HEADS = 8
ML_HEAD_DIM = D_MODEL // ML_HEADS
ML_W = ML_HEADS * ML_HEAD_DIM
D_FF = 2816
NORM_EPS = 1e-6
NEG_BIG = -1e30

LANES = 128
VMEM_LIMIT = 56 * 1024 * 1024
COL_TILE = 256
ML_SCAN_CHUNK = 128
INV_SQRT2 = 1.0 / math.sqrt(2.0)


def _cparams(*sem):
    return pltpu.CompilerParams(dimension_semantics=sem, vmem_limit_bytes=VMEM_LIMIT)


def _rms(x, gain):
    return x * lax.rsqrt(jnp.mean(x * x, axis=-1, keepdims=True) + NORM_EPS) * gain


def _modulated(x, gain, shift, scale):
    return _rms(x, gain * (1.0 + scale)) + shift


def _dot(a, b):
    return jnp.dot(a, b, preferred_element_type=F32)


def _dot_nt(a, b):
    return lax.dot_general(a, b, (((1,), (1,)), ((), ())), preferred_element_type=F32)


def _dot_tn(a, b):
    return lax.dot_general(a, b, (((0,), (0,)), ((), ())), preferred_element_type=F32)


def _split3(x):
    hi = x.astype(BF16)
    r1 = x - hi.astype(F32)
    mid = r1.astype(BF16)
    lo = (r1 - mid.astype(F32)).astype(BF16)
    return hi, mid, lo


def _dot_x3(a, b):
    a0 = a.astype(BF16)
    a1 = (a - a0.astype(F32)).astype(BF16)
    b0 = b.astype(BF16)
    b1 = (b - b0.astype(F32)).astype(BF16)
    return _dot(a0, b0) + (_dot(a0, b1) + _dot(a1, b0))


def _dwconv3(y, conv_ref, seq_len, edge=None):
    rows, cols = y.shape
    pos = lax.broadcasted_iota(jnp.int32, (rows, LANES), 0)
    if edge is None:
        pos = pos % seq_len
        first = pos == 0
        last = pos == seq_len - 1
    else:
        first = pos == 0
        last = pos == rows - 1
    outs = []
    for c0 in range(0, cols, LANES):
        yc = y[:, c0:c0 + LANES]
        before = 0.0 if edge is None else edge[0][:, c0:c0 + LANES]
        after = 0.0 if edge is None else edge[1][:, c0:c0 + LANES]
        prev = jnp.where(first, before, pltpu.roll(yc, 1, 0))
        nxt = jnp.where(last, after, pltpu.roll(yc, rows - 1, 0))
        outs.append(prev * conv_ref[0:1, c0:c0 + LANES] + yc * conv_ref[1:2, c0:c0 + LANES]
                    + nxt * conv_ref[2:3, c0:c0 + LANES] + conv_ref[3:4, c0:c0 + LANES])
    return outs[0] if len(outs) == 1 else jnp.concatenate(outs, axis=1)


def _mod_kernel(c_ref, w_ref, b_ref, o_ref):
    a = c_ref[...]
    a = a * jax.nn.sigmoid(a)
    o_ref[0] = _dot(a.astype(BF16), w_ref[0].astype(BF16)) + b_ref[0]


def _mod_call(cc, w_mod, b_mod):
    tn = 3072
    n = 6 * D_MODEL
    return pl.pallas_call(
        _mod_kernel,
        grid=(DEPTH, n // tn),
        in_specs=[pl.BlockSpec((8, D_MODEL), lambda l, j: (0, 0)),
                  pl.BlockSpec((1, D_MODEL, tn), lambda l, j: (l, 0, j)),
                  pl.BlockSpec((1, 1, tn), lambda l, j: (l, 0, j))],
        out_specs=pl.BlockSpec((1, 8, tn), lambda l, j: (l, 0, j)),
        out_shape=jax.ShapeDtypeStruct((DEPTH, 8, n), F32),
        compiler_params=_cparams("arbitrary", "arbitrary"),
        name="mod",
    )(cc, w_mod, b_mod.reshape(DEPTH, 1, n))


def _ah_inproj_kernel(*refs, seq_len, rope, halo, cw):
    it = iter(refs)
    x_ref = next(it)
    xb_ref, xa_ref = (next(it), next(it)) if halo else (None, None)
    mod_ref, gain_ref, w_ref, qkn_ref, conv_ref = (next(it) for _ in range(5))
    cc_ref, ss_ref = (next(it), next(it)) if rope else (None, None)
    q_ref, k_ref, v_ref, u_ref = (next(it) for _ in range(4))
    kf_ref, vf_ref = (None, None) if rope else (next(it), next(it))
    tm = x_ref.shape[0]
    dh = A_HEAD_DIM

    def modulated(x):
        return _modulated(x, gain_ref[0:1, :], mod_ref[0, 0:1, :], mod_ref[0, 1:2, :]).astype(BF16)

    def head(yh, g):
        yh = _rms(yh, g)
        if rope:
            yh = yh * cc_ref[...] + pltpu.roll(yh, dh // 2, 1) * ss_ref[...]
        return yh

    xn = modulated(x_ref[...])
    yq = _dot(xn, w_ref[:, 0:A_Q])
    for h in range(A_HEADS):
        sl = slice(h * dh, (h + 1) * dh)
        q_ref[:, sl] = (head(yq[:, sl], qkn_ref[0:1, :]) * (dh ** -0.5)).astype(BF16)
    ykv = _dot(xn, w_ref[:, A_Q:A_Q + 2 * A_KV])
    for h in range(A_KV_HEADS):
        sl = slice(h * dh, (h + 1) * dh)
        kh = head(ykv[:, sl], qkn_ref[1:2, :])
        k_ref[:, sl] = kh.astype(BF16)
        if kf_ref is not None:
            kf_ref[pl.ds(h, tm, stride=A_KV_HEADS), :] = kh
    yv = ykv[:, A_KV:2 * A_KV]
    v_ref[...] = yv.astype(BF16)
    if vf_ref is not None:
        for h in range(A_KV_HEADS):
            vf_ref[pl.ds(h, tm, stride=A_KV_HEADS), :] = yv[:, h * dh:(h + 1) * dh]

    xe = xn
    if halo:
        xe = jnp.concatenate([xn, modulated(jnp.concatenate([xb_ref[...], xa_ref[...]], axis=0))], axis=0)
    u0 = A_Q + 2 * A_KV
    for t in range(3 * HY_CH // cw):
        cols = slice(u0 + t * cw, u0 + (t + 1) * cw)
        y = _dot(xe, w_ref[:, cols])
        edge = None
        if halo:
            y, edge = _halo_edges(y, tm, seq_len)
        u_ref[:, t * cw:(t + 1) * cw] = _dwconv3(y, conv_ref.at[:, cols], seq_len, edge).astype(BF16)


def _ah_inproj_call(x, mod, gains, w, qkn, conv, rope_tabs, *, tm, cw, seq_len, rows_per_mod):
    rows = x.shape[0]
    rope = rope_tabs is not None
    halo = tm < seq_len
    assert (seq_len % tm == 0) if halo else (tm % seq_len == 0)
    row_blk = lambda cols: pl.BlockSpec((tm, cols), lambda i: (i, 0))
    out_specs = [row_blk(A_Q), row_blk(A_KV), row_blk(A_KV), row_blk(3 * HY_CH)]
    out_shape = [jax.ShapeDtypeStruct((rows, A_Q), BF16), jax.ShapeDtypeStruct((rows, A_KV), BF16),
                 jax.ShapeDtypeStruct((rows, A_KV), BF16), jax.ShapeDtypeStruct((rows, 3 * HY_CH), BF16)]
    if not rope:
        out_specs += [pl.BlockSpec((tm * A_KV_HEADS, A_HEAD_DIM), lambda i: (i, 0))] * 2
        out_shape += [jax.ShapeDtypeStruct((rows * A_KV_HEADS, A_HEAD_DIM), F32)] * 2
    in_specs = [row_blk(D_MODEL)] + (_halo_specs(tm, rows) if halo else []) + [
        pl.BlockSpec((1, 8, D_MODEL), lambda i: (i * tm // rows_per_mod, 0, 0)),
        pl.BlockSpec((8, D_MODEL), lambda i: (0, 0)),
        pl.BlockSpec((D_MODEL, AH_IN), lambda i: (0, 0)),
        pl.BlockSpec((8, A_HEAD_DIM), lambda i: (0, 0)),
        pl.BlockSpec((8, AH_IN), lambda i: (0, 0))]
    args = ([x, x, x] if halo else [x]) + [mod, gains, w, qkn, conv]
    if rope:
        per_seq = max(seq_len // tm, 1)
        in_specs += [pl.BlockSpec((tm, A_HEAD_DIM), lambda i: (i % per_seq, 0))] * 2
        args += list(rope_tabs)
    return pl.pallas_call(
        functools.partial(_ah_inproj_kernel, seq_len=seq_len, rope=rope, halo=halo, cw=cw),
        grid=(rows // tm,),
        in_specs=in_specs,
        out_specs=out_specs,
        out_shape=out_shape,
        compiler_params=_cparams("arbitrary"),
        name="ah_inproj",
    )(*args)


def _attn_kernel(*refs, ctx, bb):
    if ctx:
        q_ref, k_ref, v_ref, kc_ref, vc_ref, o_ref = refs
    else:
        q_ref, k_ref, v_ref, o_ref = refs
    dh = A_HEAD_DIM
    for bi in range(bb):
        k = k_ref[bi]
        if ctx:
            kc = kc_ref[bi]
        else:
            vaug = jnp.concatenate([v_ref[bi], jnp.ones((k.shape[0], dh), BF16)], axis=1)
        for g in range(A_HEADS // A_KV_HEADS):
            sl = slice(g * dh, (g + 1) * dh)
            q = q_ref[bi, :, sl]
            if ctx:
                kb = 1024
                blocks = [(k_ref, v_ref, j * kb, kb) for j in range(k.shape[0] // kb)]
                blocks.append((kc_ref, vc_ref, 0, kc.shape[0]))
                parts = []
                for kr, vr, s0, n in blocks:
                    s = _dot_nt(q, kr[bi, s0:s0 + n, :])
                    mb = jnp.max(s, axis=-1, keepdims=True)
                    p = jnp.exp(s - mb)
                    parts.append((mb, jnp.sum(p, axis=-1, keepdims=True),
                                  _dot(p.astype(BF16), vr[bi, s0:s0 + n, :])))
                m = parts[0][0]
                for mb, _, _ in parts[1:]:
                    m = jnp.maximum(m, mb)
                den = acc = None
                for mb, ps, pv in parts:
                    wgt = jnp.exp(mb - m)
                    den = ps * wgt if den is None else den + ps * wgt
                    acc = pv * wgt if acc is None else acc + pv * wgt
                o_ref[bi, :, sl] = (acc / den).astype(BF16)
            else:
                s = _dot_nt(q, k)
                m = jnp.max(s, axis=-1, keepdims=True)
                o = _dot(jnp.exp(s - m).astype(BF16), vaug)
                o_ref[bi, :, sl] = (o[:, :dh] / o[:, dh:]).astype(BF16)


def _attn_call(q, k, v, ctx_k, ctx_v, *, batch, seq_len, tq, bb):
    ctx = ctx_k is not None
    qw = A_Q // A_KV_HEADS
    in_specs = [pl.BlockSpec((bb, tq, qw), lambda b, h, i: (b, i, h)),
                pl.BlockSpec((bb, seq_len, A_HEAD_DIM), lambda b, h, i: (b, 0, h)),
                pl.BlockSpec((bb, seq_len, A_HEAD_DIM), lambda b, h, i: (b, 0, h))]
    args = [q.reshape(batch, seq_len, A_Q), k.reshape(batch, seq_len, A_KV), v.reshape(batch, seq_len, A_KV)]
    if ctx:
        past = ctx_k.shape[1]
        in_specs += [pl.BlockSpec((bb, past, A_HEAD_DIM), lambda b, h, i: (b, 0, h))] * 2
        args += [ctx_k, ctx_v]
    return pl.pallas_call(
        functools.partial(_attn_kernel, ctx=ctx, bb=bb),
        grid=(batch // bb, A_KV_HEADS, seq_len // tq),
        in_specs=in_specs,
        out_specs=pl.BlockSpec((bb, tq, qw), lambda b, h, i: (b, i, h)),
        out_shape=jax.ShapeDtypeStruct((batch, seq_len, A_Q), BF16),
        compiler_params=_cparams("arbitrary", "arbitrary", "arbitrary"),
        name="attn",
    )(*args)


def _hy_filter_kernel(z_ref, w1_ref, b1_ref, w2_ref, b2_ref, w3_ref, b3_ref, sf_ref, dl_ref,
                      cm_ref, sm_ref, gr_ref, gi_ref, hs_ref, hd_ref):
    @pl.when(pl.program_id(0) == 0)
    def _():
        z = z_ref[...]
        h = jnp.sin(sf_ref[0:1, :] * (_dot_x3(z, w1_ref[...]) + b1_ref[...]))
        h = jnp.sin(sf_ref[1:2, :] * (_dot_x3(h, w2_ref[...]) + b2_ref[...]))
        filt = _dot_x3(h, w3_ref[...]) + b3_ref[...]
        win = jnp.exp(-z[:, 0:1] * dl_ref[...])
        hf = filt[:, :HY_CH] * win
        hb = filt[:, HY_CH:] * win
        hs_ref[...] = (hf + hb).astype(BF16)
        hd_ref[...] = (hb - hf).astype(BF16)

    gr_ref[...] = _dot(cm_ref[...], hs_ref[...])
    gi_ref[...] = _dot(sm_ref[...], hd_ref[...])


def _hy_filter_call(z, w1, b1, w2, b2, w3, b3, sf, deltas, cm, sm, *, seq_len, tk):
    full = lambda a: pl.BlockSpec(a.shape, lambda k: (0,) * a.ndim)
    small = [z, w1, b1, w2, b2, w3, b3, sf, deltas]
    return pl.pallas_call(
        _hy_filter_kernel,
        grid=(seq_len // tk,),
        in_specs=[full(a) for a in small] + [pl.BlockSpec((tk, seq_len), lambda k: (k, 0))] * 2,
        out_specs=[pl.BlockSpec((tk, HY_CH), lambda k: (k, 0))] * 2,
        out_shape=[jax.ShapeDtypeStruct((seq_len, HY_CH), F32)] * 2,
        scratch_shapes=[pltpu.VMEM((seq_len, HY_CH), BF16)] * 2,
        compiler_params=_cparams("arbitrary"),
        name="hy_filter",
    )(*small, cm, sm)


def _hy_conv_kernel(x0_ref, x1_ref, v_ref, skip_ref, cm_ref, sm_ref, cmt_ref, smt_ref, gr_ref, gi_ref,
                    o_ref, vv_ref, acc_ref, *, seq_len, bb):
    kb = pl.program_id(2)
    single = seq_len == cm_ref.shape[0]
    gr = gr_ref[...]
    gi = gi_ref[...]
    for bi in range(bb):
        def gated():
            return v_ref[bi].astype(F32) * x1_ref[bi].astype(F32)

        if single:
            vv = gated().astype(BF16)
        else:
            @pl.when(kb == 0)
            def _():
                vv_ref[bi] = gated().astype(BF16)
                acc_ref[bi] = jnp.zeros_like(acc_ref[bi])

            vv = vv_ref[bi]
        vr = _dot(cm_ref[...], vv)
        wi = _dot(sm_ref[...], vv)
        pr = gr * vr + gi * wi
        qi = gr * wi - gi * vr
        y = _dot(cmt_ref[...], pr.astype(BF16)) + _dot(smt_ref[...], qi.astype(BF16))

        def finish(total):
            o_ref[bi] = ((total * (1.0 / seq_len) + skip_ref[...] * gated())
                         * x0_ref[bi].astype(F32)).astype(BF16)

        if single:
            finish(y)
        else:
            acc_ref[bi] += y

            @pl.when(kb == pl.num_programs(2) - 1)
            def _():
                finish(acc_ref[bi])


def _hy_conv_call(u, skip, dft, gr, gi, *, batch, seq_len, tc, tk, bb):
    proj3 = u.reshape(batch, seq_len, 3 * HY_CH)
    cm, sm, cmt, smt = dft
    nch = HY_CH // tc
    return pl.pallas_call(
        functools.partial(_hy_conv_kernel, seq_len=seq_len, bb=bb),
        grid=(batch // bb, nch, seq_len // tk),
        in_specs=[pl.BlockSpec((bb, seq_len, tc), lambda b, c, k: (b, 0, c)),
                  pl.BlockSpec((bb, seq_len, tc), lambda b, c, k: (b, 0, nch + c)),
                  pl.BlockSpec((bb, seq_len, tc), lambda b, c, k: (b, 0, 2 * nch + c)),
                  pl.BlockSpec((1, tc), lambda b, c, k: (0, c)),
                  pl.BlockSpec((tk, seq_len), lambda b, c, k: (k, 0)),
                  pl.BlockSpec((tk, seq_len), lambda b, c, k: (k, 0)),
                  pl.BlockSpec((seq_len, tk), lambda b, c, k: (0, k)),
                  pl.BlockSpec((seq_len, tk), lambda b, c, k: (0, k)),
                  pl.BlockSpec((tk, tc), lambda b, c, k: (k, c)),
                  pl.BlockSpec((tk, tc), lambda b, c, k: (k, c))],
        out_specs=pl.BlockSpec((bb, seq_len, tc), lambda b, c, k: (b, 0, c)),
        out_shape=jax.ShapeDtypeStruct((batch, seq_len, HY_CH), BF16),
        scratch_shapes=[pltpu.VMEM((bb, seq_len, tc), BF16), pltpu.VMEM((bb, seq_len, tc), F32)],
        compiler_params=_cparams("arbitrary", "arbitrary", "arbitrary"),
        name="hy_conv",
    )(proj3, proj3, proj3, skip, cm, sm, cmt, smt, gr, gi)


HALO_ROWS = 16


def _mix_ffn_kernel(*refs, seq_len, halo, tf, mixer):
    it = iter(refs)

    def rows_of():
        parts = [next(it)[...] for _ in range(3 if halo else 1)]
        return parts[0] if len(parts) == 1 else jnp.concatenate(parts, axis=0)

    x = rows_of()
    mix_in = [rows_of() for _ in range(2 if mixer == "ah" else 3)]
    mod_ref, gain_ref, wo_ref = next(it), next(it), next(it)
    hn_ref = next(it) if mixer == "ml" else None
    wu_ref, conv_ref, wd_ref, o_ref = next(it), next(it), next(it), next(it)
    tm = o_ref.shape[0]

    if mixer == "ah":
        mixed = _dot(mix_in[0], wo_ref[0:A_Q, :]) + _dot(mix_in[1], wo_ref[A_Q:, :])
    else:
        h = mix_in[0].astype(F32) + mix_in[1].astype(F32)
        parts = []
        for hd in range(ML_HEADS):
            sl = slice(hd * ML_HEAD_DIM, (hd + 1) * ML_HEAD_DIM)
            parts.append(_rms(h[:, sl], hn_ref[0:1, sl]))
        gated = jnp.concatenate(parts, axis=1) * jax.nn.sigmoid(mix_in[2].astype(F32))
        mixed = _dot(gated.astype(BF16), wo_ref[...])
    x_mid = x + _rms(mixed, gain_ref[1:2, :] * mod_ref[0, 2:3, :])

    xn = _modulated(x_mid, gain_ref[2:3, :], mod_ref[0, 3:4, :], mod_ref[0, 4:5, :]).astype(BF16)
    out = None
    for t in range(D_FF // tf):
        cg = slice(t * tf, (t + 1) * tf)
        cl = slice(D_FF + t * tf, D_FF + (t + 1) * tf)
        yg = _dot(xn, wu_ref[0, :, cg])
        yl = _dot(xn, wu_ref[0, :, cl])
        edge_g = edge_l = None
        if halo:
            yg, edge_g = _halo_edges(yg, tm, seq_len, HALO_ROWS)
            yl, edge_l = _halo_edges(yl, tm, seq_len, HALO_ROWS)
        hg = _dwconv3(yg, conv_ref.at[:, cg], seq_len, edge_g)
        hl = _dwconv3(yl, conv_ref.at[:, cl], seq_len, edge_l)
        act = (0.5 * hg * (1.0 + lax.erf(hg * INV_SQRT2))) * hl
        part = _dot(act.astype(BF16), wd_ref[0, cg, :])
        out = part if out is None else out + part
    o_ref[...] = x_mid[0:tm, :] + _rms(out, gain_ref[3:4, :] * mod_ref[0, 5:6, :])


def _mix_ffn_call(x, mix_in, w_out, head_norm, mod, gains, w_up, conv, w_down, layer, *,
                  tm, tf, seq_len, rows_per_mod):
    rows = x.shape[0]
    mixer = "ah" if head_norm is None else "ml"
    halo = tm < seq_len
    assert (seq_len % tm == 0) if halo else (tm % seq_len == 0)
    row_specs, row_args = [], []
    for a in [x] + list(mix_in):
        cols = a.shape[1]
        row_specs.append(pl.BlockSpec((tm, cols), lambda i: (i, 0)))
        row_args.append(a)
        if halo:
            row_specs += _halo_specs(tm, rows, cols, HALO_ROWS)
            row_args += [a, a]
    whole = lambda a: pl.BlockSpec(a.shape, lambda i: (0,) * a.ndim)
    resident = pl.Buffered(1)
    extra = [] if head_norm is None else [head_norm]
    return pl.pallas_call(
        functools.partial(_mix_ffn_kernel, seq_len=seq_len, halo=halo, tf=tf, mixer=mixer),
        grid=(rows // tm,),
        in_specs=row_specs + [
            pl.BlockSpec((1, 8, D_MODEL), lambda i: (i * tm // rows_per_mod, 0, 0)),
            whole(gains), whole(w_out)] + [whole(a) for a in extra] + [
            pl.BlockSpec((1, D_MODEL, 2 * D_FF), lambda i: (layer, 0, 0), pipeline_mode=resident),
            whole(conv),
            pl.BlockSpec((1, D_FF, D_MODEL), lambda i: (layer, 0, 0), pipeline_mode=resident)],
        out_specs=pl.BlockSpec((tm, D_MODEL), lambda i: (i, 0)),
        out_shape=jax.ShapeDtypeStruct((rows, D_MODEL), F32),
        compiler_params=_cparams("arbitrary"),
        name="mix_ffn",
    )(*row_args, mod, gains, w_out, *extra, w_up, conv, w_down)


def _halo_specs(tm, rows, cols=D_MODEL, sub=8):
    per = tm // sub
    return [pl.BlockSpec((sub, cols), lambda i, *_: (jnp.maximum(i * per - 1, 0), 0)),
            pl.BlockSpec((sub, cols), lambda i, *_: (jnp.minimum((i + 1) * per, rows // sub - 1), 0))]


def _halo_edges(y, tm, seq_len, sub=8):
    i = pl.program_id(0)
    at_start = (i * tm) % seq_len == 0
    at_end = ((i + 1) * tm) % seq_len == 0
    edge = (jnp.where(at_start, 0.0, y[tm + sub - 1:tm + sub, :]),
            jnp.where(at_end, 0.0, y[tm + sub:tm + sub + 1, :]))
    return y[0:tm, :], edge


def _ml_inproj_kernel(*refs, seq_len, cw, halo):
    if halo:
        x_ref, xb_ref, xa_ref, mod_ref, gain_ref, w_ref, wg_ref, bg_ref, conv_ref = refs[:9]
    else:
        x_ref, mod_ref, gain_ref, w_ref, wg_ref, bg_ref, conv_ref = refs[:7]
    q_ref, kt_ref, v_ref, o_ref, g_ref = refs[-5:]
    tm = x_ref.shape[0]

    def modulated(x):
        return _modulated(x, gain_ref[0:1, :], mod_ref[0, 0:1, :], mod_ref[0, 1:2, :]).astype(BF16)

    xn = modulated(x_ref[...])
    g_ref[...] = (_dot(xn, wg_ref[...]) + bg_ref[...]).T
    xe = xn
    if halo:
        xe = jnp.concatenate([xn, modulated(jnp.concatenate([xb_ref[...], xa_ref[...]], axis=0))], axis=0)

    for t in range(2 * ML_W // cw):
        cols = slice(t * cw, (t + 1) * cw)
        y = _dot(xe, w_ref[:, cols])
        edge = None
        if halo:
            y, edge = _halo_edges(y, tm, seq_len)
        z = _dwconv3(y, conv_ref.at[:, cols], seq_len, edge)
        z = z * jax.nn.sigmoid(z)
        if t * cw < ML_W:
            q_ref[:, cols] = z.astype(BF16)
        else:
            kt_ref[t * cw - ML_W:(t + 1) * cw - ML_W, :] = (z * (ML_HEAD_DIM ** -0.5)).T.astype(BF16)
        y = _dot(xn, w_ref[:, 2 * ML_W + t * cw:2 * ML_W + (t + 1) * cw]).astype(BF16)
        if t * cw < ML_W:
            v_ref[:, t * cw:(t + 1) * cw] = y
        else:
            o_ref[:, t * cw - ML_W:(t + 1) * cw - ML_W] = y


def _ml_inproj_call(x, mod, gains, w, wg, bg, conv, *, tm, cw, seq_len, rows_per_mod):
    rows = x.shape[0]
    n = 4 * ML_W
    halo = tm < seq_len
    assert (seq_len % tm == 0) if halo else (tm % seq_len == 0)
    x_specs = [pl.BlockSpec((tm, D_MODEL), lambda i: (i, 0))] + (_halo_specs(tm, rows) if halo else [])
    return pl.pallas_call(
        functools.partial(_ml_inproj_kernel, seq_len=seq_len, cw=cw, halo=halo),
        grid=(rows // tm,),
        in_specs=x_specs + [
            pl.BlockSpec((1, 8, D_MODEL), lambda i: (i * tm // rows_per_mod, 0, 0)),
            pl.BlockSpec((8, D_MODEL), lambda i: (0, 0)),
            pl.BlockSpec((D_MODEL, n), lambda i: (0, 0)),
            pl.BlockSpec((D_MODEL, LANES), lambda i: (0, 0)),
            pl.BlockSpec((1, LANES), lambda i: (0, 0)),
            pl.BlockSpec((8, n), lambda i: (0, 0))],
        out_specs=[pl.BlockSpec((tm, ML_W), lambda i: (i, 0)),
                   pl.BlockSpec((ML_W, tm), lambda i: (0, i)),
                   pl.BlockSpec((tm, ML_W), lambda i: (i, 0)),
                   pl.BlockSpec((tm, ML_W), lambda i: (i, 0)),
                   pl.BlockSpec((LANES, tm), lambda i: (0, i))],
        out_shape=[jax.ShapeDtypeStruct((rows, ML_W), BF16), jax.ShapeDtypeStruct((ML_W, rows), BF16),
                   jax.ShapeDtypeStruct((rows, ML_W), BF16), jax.ShapeDtypeStruct((rows, ML_W), BF16),
                   jax.ShapeDtypeStruct((LANES, rows), F32)],
        compiler_params=_cparams("arbitrary"),
        name="ml_inproj",
    )(*([x, x, x] if halo else [x]), mod, gains, w, wg, bg, conv)


def _log_sigmoid(x):
    return jnp.minimum(x, 0.0) - jnp.log1p(jnp.exp(-jnp.abs(x)))


def _lane_scan(x, op, reverse, fill, seg):
    n = x.shape[1]
    pos = lax.broadcasted_iota(jnp.int32, x.shape, 1) % seg
    sh = 1
    while sh < seg:
        if reverse:
            x = op(x, jnp.where(pos < seg - sh, pltpu.roll(x, n - sh, 1), fill))
        else:
            x = op(x, jnp.where(pos >= sh, pltpu.roll(x, sh, 1), fill))
        sh *= 2
    return x


def _bcast_selectors(chunk):
    sel = np.zeros((ML_HEADS, LANES, chunk + LANES), np.float32)
    for h in range(ML_HEADS):
        for g in range(3):
            sel[h, g * ML_HEADS + h, :chunk] = 1.0
            sel[h, (3 + g) * ML_HEADS + h, chunk:] = 1.0
    return jnp.asarray(sel, BF16)


def _ml_scan_kernel(*refs, chunk, cps, has_init, whole_seq):
    if has_init:
        (qf, ktf, vf, qb, ktb, vb, colf, colb, rowf, rowb, sel_ref, c0_ref, n0_ref,
         hf_ref, hb_ref, c_ref, n_ref, s_sc, p16_sc, e_sc) = refs
    else:
        (qf, ktf, vf, qb, ktb, vb, colf, colb, rowf, rowb, sel_ref,
         hf_ref, hb_ref, c_ref, n_ref, s_sc, p16_sc, e_sc) = refs
    nh, dh = ML_HEADS, ML_HEAD_DIM
    step = pl.program_id(1)
    zero_start = (not has_init) and whole_seq

    if not zero_start:
        @pl.when(step == 0)
        def _():
            if has_init:
                for d in range(2):
                    for h in range(nh):
                        s_sc[d, h, :, 0:dh] = c0_ref[0, d, h].T
                        s_sc[d, h, :, dh:2 * dh] = jnp.broadcast_to(n0_ref[0, d, h:h + 1, :], (dh, dh)).T
            else:
                s_sc[...] = jnp.zeros_like(s_sc)

    row = lax.broadcasted_iota(jnp.int32, (chunk, chunk), 0)
    col = lax.broadcasted_iota(jnp.int32, (chunk, chunk), 1)
    ones_cols = jnp.ones((chunk, dh), BF16)

    dirs = ((qf, ktf, vf, colf, rowf, hf_ref), (qb, ktb, vb, colb, rowb, hb_ref))
    for sub in range(cps):
        for d, (q_ref, kt_ref, _, col_ref, row_ref, _) in enumerate(dirs):
            reverse = d == 1
            lc = cps - 1 - sub if reverse else sub
            tok = slice(lc * chunk, (lc + 1) * chunk)
            cols = col_ref[0, lc, 0]
            r = row_ref[0, lc, 0, 0:nh, :]
            m_prev = row_ref[0, lc, 0, 3 * nh:4 * nh, :]
            mask = (col >= row) if reverse else (col <= row)
            for h in range(nh):
                sl = slice(h * dh, (h + 1) * dh)
                bc = _dot(cols, sel_ref[h])
                p_bc = bc[:, :chunk]
                w = jnp.exp(jnp.where(mask, p_bc + r[h:h + 1, :], NEG_BIG))
                p16_sc[sub, d, h] = (_dot(q_ref[0, tok, sl], kt_ref[sl, tok]) * w).astype(BF16)
                if not (zero_start and sub == 0):
                    e_sc[sub, d, h, :, 0:dh] = jnp.exp(p_bc[:, :dh] + m_prev[h:h + 1, 0:dh])
                e_sc[sub, d, h, :, dh:2 * dh] = jnp.exp(bc[:, chunk:])
    for sub in range(cps):
        for d, (q_ref, kt_ref, v_ref, _, row_ref, h_ref) in enumerate(dirs):
            lc = cps - 1 - sub if d == 1 else sub
            tok = slice(lc * chunk, (lc + 1) * chunk)
            w_tok = row_ref[0, lc, 0, nh:2 * nh, :]
            w_state = row_ref[0, lc, 0, 2 * nh:3 * nh, :]
            for h in range(nh):
                sl = slice(h * dh, (h + 1) * dh)
                q16 = q_ref[0, tok, sl]
                kt16 = kt_ref[sl, tok]
                vaug = jnp.concatenate([v_ref[0, tok, sl], ones_cols], axis=1)
                tot = _dot(p16_sc[sub, d, h], vaug)
                kts = (kt16.astype(F32) * w_tok[h:h + 1, :]).astype(BF16)
                s_new = _dot(kts, vaug)
                if not (zero_start and sub == 0):
                    s_old = s_sc[d, h]
                    w_inter = e_sc[sub, d, h, :, 0:dh]
                    tot = tot + jnp.concatenate([w_inter, w_inter], axis=1) * _dot(q16, s_old.astype(BF16))
                    ws = w_state[h:h + 1, 0:dh]
                    s_new = jnp.concatenate([ws, ws], axis=1) * s_old + s_new
                h_ref[0, tok, sl] = (tot[:, :dh] / jnp.maximum(jnp.abs(tot[:, dh:]),
                                                               e_sc[sub, d, h, :, dh:2 * dh])).astype(BF16)
                s_sc[d, h] = s_new

    @pl.when(step == pl.num_programs(1) - 1)
    def _():
        for d in range(2):
            for h in range(nh):
                s_fin = s_sc[d, h]
                c_ref[0, d, h] = s_fin[:, 0:dh].T
                n_ref[0, d, h:h + 1, :] = s_fin[:, dh:2 * dh].T[0:1, :]


def _ml_gate_kernel(*refs, nc, chunk, bb, has_init):
    if has_init:
        gt_ref, m0_ref, cols_ref, rows_ref, m_ref = refs
    else:
        gt_ref, cols_ref, rows_ref, m_ref = refs
    nh = ML_HEADS
    pad = jnp.zeros((LANES - 6 * nh, chunk), F32)
    for d in range(2):
        reverse = d == 1
        last = 0 if reverse else chunk - 1
        b_all = _lane_scan(_log_sigmoid(gt_ref[nh * (2 + d):nh * (3 + d), :]), jnp.add, reverse, 0.0, chunk)
        r_all = gt_ref[nh * d:nh * (d + 1), :] - b_all
        g_all = _lane_scan(r_all, jnp.maximum, reverse, -jnp.inf, chunk)
        for bi in range(bb):
            m = m0_ref[bi, d] if has_init else jnp.zeros((nh, chunk), F32)
            for c in (range(nc - 1, -1, -1) if reverse else range(nc)):
                c0 = (bi * nc + c) * chunk
                b = b_all[:, c0:c0 + chunk]
                r = r_all[:, c0:c0 + chunk]
                mx = jnp.maximum(m, g_all[:, c0:c0 + chunk])
                mt = b + mx
                m_new = jnp.broadcast_to(mt[:, last:last + 1], (nh, chunk))
                b_last = jnp.broadcast_to(b[:, last:last + 1], (nh, chunk))
                rows_ref[bi, c, d, 0:nh, :] = r
                rows_ref[bi, c, d, nh:2 * nh, :] = jnp.exp(r + b_last - m_new)
                rows_ref[bi, c, d, 2 * nh:3 * nh, :] = jnp.exp(b_last + m - m_new)
                rows_ref[bi, c, d, 3 * nh:4 * nh, :] = m
                stack = []
                for x in (-mx, -mt):
                    stack += [t.astype(F32) for t in _split3(x)]
                stack.append(pad)
                cols_ref[bi, c, d] = jnp.concatenate(stack, axis=0).T.astype(BF16)
                m = m_new
            m_ref[bi, d] = m[:, 0:LANES]


def _ml_gate_call(gates, m0, *, batch, seq_len, chunk):
    nc = seq_len // chunk
    bb = max(1, 16 // nc)
    nh = ML_HEADS
    has_init = m0 is not None
    in_specs = [pl.BlockSpec((LANES, bb * seq_len), lambda i: (0, i))]
    args = [gates]
    if has_init:
        in_specs.append(pl.BlockSpec((bb, 2, nh, chunk), lambda i: (i, 0, 0, 0)))
        args.append(jnp.broadcast_to(m0[..., None], m0.shape + (chunk,)))
    return pl.pallas_call(
        functools.partial(_ml_gate_kernel, nc=nc, chunk=chunk, bb=bb, has_init=has_init),
        grid=(batch // bb,),
        in_specs=in_specs,
        out_specs=[pl.BlockSpec((bb, nc, 2, chunk, LANES), lambda i: (i, 0, 0, 0, 0)),
                   pl.BlockSpec((bb, nc, 2, 4 * nh, chunk), lambda i: (i, 0, 0, 0, 0)),
                   pl.BlockSpec((bb, 2, nh, LANES), lambda i: (i, 0, 0, 0))],
        out_shape=[jax.ShapeDtypeStruct((batch, nc, 2, chunk, LANES), BF16),
                   jax.ShapeDtypeStruct((batch, nc, 2, 4 * nh, chunk), F32),
                   jax.ShapeDtypeStruct((batch, 2, nh, LANES), F32)],
        compiler_params=_cparams("arbitrary"),
        name="ml_gate",
    )(*args)


def _ml_scan_call(q, kt, v, cols, rows, init, *, batch, seq_len, chunk, cps):
    nc = seq_len // (chunk * cps)
    nh = ML_HEADS
    blk = chunk * cps
    q3 = q.reshape(batch, seq_len, ML_W)
    v3 = v.reshape(batch, seq_len, ML_W)
    sel = _bcast_selectors(chunk)
    has_init = init is not None

    fwd3 = lambda b, i: (b, i, 0)
    bwd3 = lambda b, i: (b, nc - 1 - i, 0)
    in_specs = [pl.BlockSpec((1, blk, ML_W), fwd3),
                pl.BlockSpec((ML_W, blk), lambda b, i: (0, b * nc + i)),
                pl.BlockSpec((1, blk, ML_W), fwd3),
                pl.BlockSpec((1, blk, ML_W), bwd3),
                pl.BlockSpec((ML_W, blk), lambda b, i: (0, b * nc + nc - 1 - i)),
                pl.BlockSpec((1, blk, ML_W), bwd3),
                pl.BlockSpec((1, cps, 1, chunk, LANES), lambda b, i: (b, i, 0, 0, 0)),
                pl.BlockSpec((1, cps, 1, chunk, LANES), lambda b, i: (b, nc - 1 - i, 1, 0, 0)),
                pl.BlockSpec((1, cps, 1, 4 * nh, chunk), lambda b, i: (b, i, 0, 0, 0)),
                pl.BlockSpec((1, cps, 1, 4 * nh, chunk), lambda b, i: (b, nc - 1 - i, 1, 0, 0)),
                pl.BlockSpec(sel.shape, lambda b, i: (0, 0, 0))]
    args = [q3, kt, v3, q3, kt, v3, cols, cols, rows, rows, sel]
    c_spec = pl.BlockSpec((1, 2, nh, ML_HEAD_DIM, ML_HEAD_DIM), lambda b, i: (b, 0, 0, 0, 0))
    n_spec = pl.BlockSpec((1, 2, nh, ML_HEAD_DIM), lambda b, i: (b, 0, 0, 0))
    if has_init:
        in_specs += [c_spec, n_spec]
        args += list(init)
    return pl.pallas_call(
        functools.partial(_ml_scan_kernel, chunk=chunk, cps=cps, has_init=has_init, whole_seq=nc == 1),
        grid=(batch, nc),
        in_specs=in_specs,
        out_specs=[pl.BlockSpec((1, blk, ML_W), fwd3),
                   pl.BlockSpec((1, blk, ML_W), bwd3),
                   c_spec, n_spec],
        out_shape=[jax.ShapeDtypeStruct((batch, seq_len, ML_W), BF16),
                   jax.ShapeDtypeStruct((batch, seq_len, ML_W), BF16),
                   jax.ShapeDtypeStruct((batch, 2, nh, ML_HEAD_DIM, ML_HEAD_DIM), F32),
                   jax.ShapeDtypeStruct((batch, 2, nh, ML_HEAD_DIM), F32)],
        scratch_shapes=[pltpu.VMEM((2, nh, ML_HEAD_DIM, 2 * ML_HEAD_DIM), F32),
                        pltpu.VMEM((cps, 2, nh, chunk, chunk), BF16),
                        pltpu.VMEM((cps, 2, nh, chunk, 2 * ML_HEAD_DIM), F32)],
        compiler_params=_cparams("arbitrary", "arbitrary"),
        name="ml_scan",
    )(*args)


@functools.lru_cache(maxsize=None)
def _dft_tables_np(seq_len):
    n4 = 4 * seq_len
    ar = np.arange(seq_len, dtype=np.int64)
    idx = ((2 * ar + 1)[:, None] * ar[None, :]) % n4
    ang = (2.0 * np.pi / n4) * idx.astype(np.float64)
    cm = np.cos(ang).astype(np.float32)
    sm = np.sin(ang).astype(np.float32)
    return tuple(np.ascontiguousarray(a) for a in (cm, sm, cm.T, sm.T))


def _dft_tables(seq_len):
    return tuple(jnp.asarray(a).astype(BF16) for a in _dft_tables_np(seq_len))


def _hy_features(seq_len):
    t = np.linspace(0.0, 1.0, seq_len, dtype=np.float32)
    bands = np.arange(1, HY_BANDS + 1, dtype=np.float32)
    ang = (np.float32(2.0 * np.pi) * t[:, None]) * bands
    z = np.concatenate([t[:, None], np.cos(ang), np.sin(ang)], axis=-1).astype(np.float32)
    z = np.pad(z, ((0, 0), (0, 32 - HY_EMB)))
    deltas = np.abs(np.linspace(HY_MIN_DECAY, HY_MAX_DECAY, HY_CH, dtype=np.float32))[None, :]
    return jnp.asarray(z), jnp.asarray(deltas)


def _rope_tables(seq_len):
    rows = seq_len // GRID_W
    row = np.repeat(np.arange(rows, dtype=np.float32), GRID_W)
    col = np.tile(np.arange(GRID_W, dtype=np.float32), rows)
    n_freq = A_HEAD_DIM // 4
    inv = (np.float32(ROPE_THETA) ** (-np.arange(n_freq, dtype=np.float32) / n_freq)).astype(np.float32)
    ang = np.concatenate([row[:, None] * inv, col[:, None] * inv], axis=-1).astype(np.float32)
    cos, sin = np.cos(ang), np.sin(ang)
    return (jnp.asarray(np.concatenate([cos, cos], axis=-1), F32),
            jnp.asarray(np.concatenate([-sin, sin], axis=-1), F32))


def _pad_rows(a, rows=8):
    return jnp.pad(a, ((0, rows - a.shape[0]), (0, 0)))


def kernel(x_prompt, x_sample, cache_attn_k, cache_attn_v, state_mlstm_C, state_mlstm_n, state_mlstm_m, c, c_ctx, w_mod, b_mod, norm_mix_pre, norm_mix_post, norm_ffn_pre, norm_ffn_post, ffn_w_up, ffn_conv_w, ffn_conv_b, ffn_w_down, ah_w_in, ah_w_out, attn_q_norm, attn_k_norm, hy_conv_w, hy_conv_b, hy_w1, hy_b1, hy_w2, hy_b2, hy_w3, hy_b3, hy_sin_freq, hy_skip, ml_w_in, ml_b_gates, ml_conv_w, ml_conv_b, ml_head_norm, ml_w_out):
    bp, lp, _ = x_prompt.shape
    bs, ls, _ = x_sample.shape
    past = cache_attn_k.shape[2]
    xp = x_prompt.reshape(bp * lp, D_MODEL)
    xs = x_sample.reshape(bs * ls, D_MODEL)
    groups = {
        "p": dict(batch=bp, seq_len=lp, tm=1024, rows_per_mod=bp * lp, ffn_tm=512, ffn_tf=D_FF,
                  seq_bb=8, hy_tc=HY_CH, hy_tk=256, ml_cw=COL_TILE),
        "s": dict(batch=bs, seq_len=ls, tm=ls, rows_per_mod=ls, ffn_tm=512, ffn_tf=D_FF,
                  seq_bb=1, hy_tc=HY_CH, hy_tk=512, ml_cw=512),
    }
    ffn_up16 = ffn_w_up.astype(BF16)
    ffn_down16 = ffn_w_down.astype(BF16)

    cc = jnp.concatenate([c, c_ctx[None, :], jnp.zeros((8 - bs - 1, D_MODEL), F32)], axis=0)
    mod_all = _mod_call(cc, w_mod, b_mod).reshape(DEPTH, 8, 6, D_MODEL)
    mod_all = jnp.pad(mod_all, ((0, 0), (0, 0), (0, 2), (0, 0)))
    rope = _rope_tables(ls)

    new_k = new_v = new_c = new_n = new_m = None
    for l in range(DEPTH):
        j = l // 2
        mods = {"s": mod_all[l, :bs], "p": mod_all[l, bs:bs + 1]}
        gains = _pad_rows(jnp.stack([norm_mix_pre[l], norm_mix_post[l], norm_ffn_pre[l], norm_ffn_post[l]]))
        xin = {"p": xp, "s": xs}
        xmid = {}
        if l % 2 == 0:
            w_in = ah_w_in[j].astype(BF16)
            w_out = ah_w_out[j].astype(BF16)
            qkn = _pad_rows(jnp.stack([attn_q_norm[j], attn_k_norm[j]]))
            conv = jnp.pad(_pad_rows(jnp.concatenate([hy_conv_w[j], hy_conv_b[j][None, :]], axis=0)),
                           ((0, 0), (A_Q + 2 * A_KV, 0)))
            w1 = jnp.pad(hy_w1[j], ((0, 32 - HY_EMB), (0, 0)))
            sf = _pad_rows(hy_sin_freq[j])
            for g, cfg in groups.items():
                batch, seq_len, tm, rpm = cfg["batch"], cfg["seq_len"], cfg["tm"], cfg["rows_per_mod"]
                outs = _ah_inproj_call(xin[g], mods[g], gains, w_in, qkn, conv, rope if g == "s" else None,
                                       tm=512, cw=cfg["ml_cw"], seq_len=seq_len, rows_per_mod=rpm)
                q, k, v, u = outs[:4]
                if g == "s":
                    ctx_k = cache_attn_k[:, j].reshape(bs, past, A_KV).astype(BF16)
                    ctx_v = cache_attn_v[:, j].reshape(bs, past, A_KV).astype(BF16)
                else:
                    ctx_k = ctx_v = None
                    new_k = outs[4].reshape(bp, 1, lp, A_KV_HEADS, A_HEAD_DIM)
                    new_v = outs[5].reshape(bp, 1, lp, A_KV_HEADS, A_HEAD_DIM)
                attn = _attn_call(q, k, v, ctx_k, ctx_v, batch=batch, seq_len=seq_len, tq=256,
                                  bb=cfg["seq_bb"])
                z, deltas = _hy_features(seq_len)
                dft = _dft_tables(seq_len)
                tk = cfg["hy_tk"]
                gr, gi = _hy_filter_call(z, w1, hy_b1[j][None, :], hy_w2[j], hy_b2[j][None, :], hy_w3[j],
                                         hy_b3[j][None, :], sf, deltas, dft[0], dft[1], seq_len=seq_len, tk=tk)
                hyo = _hy_conv_call(u, hy_skip[j][None, :], dft, gr, gi, batch=batch, seq_len=seq_len,
                                    tc=cfg["hy_tc"], tk=tk, bb=cfg["seq_bb"])
                xmid[g] = [attn.reshape(batch * seq_len, A_Q), hyo.reshape(batch * seq_len, HY_CH)]
            head_norm = None
        else:
            w_in = ml_w_in[j].astype(BF16)
            w_g = jnp.pad(ml_w_in[j][:, 4 * ML_W:], ((0, 0), (0, LANES - 4 * ML_HEADS))).astype(BF16)
            b_g = jnp.pad(ml_b_gates[j], (0, LANES - 4 * ML_HEADS))[None, :]
            w_out = ml_w_out[j].astype(BF16)
            conv = jnp.pad(_pad_rows(jnp.concatenate([ml_conv_w[j], ml_conv_b[j][None, :]], axis=0)),
                           ((0, 0), (0, 2 * ML_W)))
            for g, cfg in groups.items():
                batch, seq_len, tm, rpm = cfg["batch"], cfg["seq_len"], cfg["tm"], cfg["rows_per_mod"]
                q, kt, v, og, gates = _ml_inproj_call(xin[g], mods[g], gains, w_in, w_g, b_g, conv, tm=512,
                                                      cw=cfg["ml_cw"], seq_len=seq_len, rows_per_mod=rpm)
                if g == "s":
                    init, m0 = (state_mlstm_C[:, j], state_mlstm_n[:, j]), state_mlstm_m[:, j]
                else:
                    init = m0 = None
                cols, rows, m_new = _ml_gate_call(gates, m0, batch=batch, seq_len=seq_len, chunk=ML_SCAN_CHUNK)
                hf, hb, c_new, n_new = _ml_scan_call(q, kt, v, cols, rows, init, batch=batch, seq_len=seq_len,
                                                     chunk=ML_SCAN_CHUNK, cps=4 if g == "s" else 2)
                if g == "p":
                    new_c, new_n, new_m = c_new[:, None], n_new[:, None], m_new[:, None, :, :, 0]
                xmid[g] = [hf.reshape(batch * seq_len, ML_W), hb.reshape(batch * seq_len, ML_W), og]
            head_norm = ml_head_norm[j][None, :]
        conv = _pad_rows(jnp.concatenate([ffn_conv_w[l], ffn_conv_b[l][None, :]], axis=0))
        xout = {}
        for g, cfg in groups.items():
            xout[g] = _mix_ffn_call(xin[g], xmid[g], w_out, head_norm, mods[g], gains, ffn_up16, conv, ffn_down16,
                                    l, tm=cfg["ffn_tm"], tf=cfg["ffn_tf"], seq_len=cfg["seq_len"],
                                    rows_per_mod=cfg["rows_per_mod"])
        xp, xs = xout["p"], xout["s"]

    return (xp.reshape(bp, lp, D_MODEL), xs.reshape(bs, ls, D_MODEL), new_k, new_v, new_c, new_n, new_m)
```
